```python
import math
import jax, jax.numpy as jnp
from jax import lax
import numpy as np

D_MODEL = 1024
BATCH = 4
SEQ = 4096
DEPTH = 2

RNN_WIDTH = D_MODEL // 2
RNN_BLOCKS = 8
RNN_BLOCK = RNN_WIDTH // RNN_BLOCKS
CONV_WIDTH = 4
RGLRU_C = 8.0
ATTN_HEADS = 4
ATTN_HEAD_DIM = 64
ATTN_V_DIM = 2 * ATTN_HEAD_DIM
QK_WIDTH = ATTN_HEADS * 2 * ATTN_HEAD_DIM
ATTN_WIDTH = ATTN_HEADS * ATTN_V_DIM
ROPE_THETA = 10000.0
Q_BLOCK = 128
NEG_INF = -1e30
SSM_WIDTH = D_MODEL // 2
SSM_GROUP = 16
SSM_GROUPS = SSM_WIDTH // SSM_GROUP
SSM_STATE = 64
N_BRANCH = 3
IN_SPLITS = (RNN_WIDTH, 2 * RNN_WIDTH, 2 * RNN_WIDTH + QK_WIDTH, 2 * RNN_WIDTH + 2 * QK_WIDTH, 2 * RNN_WIDTH + 2 * QK_WIDTH + ATTN_WIDTH, 2 * RNN_WIDTH + 2 * QK_WIDTH + ATTN_WIDTH + SSM_WIDTH)
IN_COLS = IN_SPLITS[-1] + N_BRANCH * D_MODEL
N_GROUPS = 4
EXPERTS_PER_GROUP = 8
TOP_K_FINE = 2
EXPERT_FF = D_MODEL // 2

kernel_name = 'hybrid_rglru_diffattn_s5_hmoe'


def rmsnorm(x, g, eps=1e-6):
    xf = x.astype(jnp.float32)
    y = xf * lax.rsqrt(jnp.mean(xf * xf, axis=-1, keepdims=True) + eps)
    return (y * g.astype(jnp.float32)).astype(x.dtype)


def rope(x, positions):
    d = x.shape[-1]
    inv_freq = ROPE_THETA ** (-jnp.arange(0, d, 2, dtype=jnp.float32) / d)
    ang = positions.astype(jnp.float32)[..., None] * inv_freq
    ang = jnp.concatenate([ang, ang], axis=-1)[:, :, None, None, :]
    xf = x.astype(jnp.float32)
    rot = jnp.concatenate([-xf[..., d // 2:], xf[..., :d // 2]], axis=-1)
    return (xf * jnp.cos(ang) + rot * jnp.sin(ang)).astype(x.dtype)


def causal_depthwise_conv(x, w, b):
    K = w.shape[0]
    L = x.shape[1]
    xp = jnp.pad(x, ((0, 0), (K - 1, 0), (0, 0)))
    y = b
    for k in range(K):
        y = y + w[k] * xp[:, K - 1 - k:K - 1 - k + L]
    return y


def _real_combine(e1, e2):
    a1, b1 = e1
    a2, b2 = e2
    return (a1 * a2, a2 * b1 + b2)


def _complex_combine(e1, e2):
    a1r, a1i, b1r, b1i = e1
    a2r, a2i, b2r, b2i = e2
    return (a2r * a1r - a2i * a1i, a2r * a1i + a2i * a1r,
            a2r * b1r - a2i * b1i + b2r, a2r * b1i + a2i * b1r + b2i)


def rglru_branch(x_rnn, g_rnn, reset, conv_w, conv_b, rg_wa, rg_ba, rg_wx, rg_bx, rg_lambda):
    B_, L_, _ = x_rnn.shape
    xc = causal_depthwise_conv(x_rnn, conv_w, conv_b).astype(jnp.float32)
    xb = xc.reshape(B_, L_, RNN_BLOCKS, RNN_BLOCK)
    r = jax.nn.sigmoid(jnp.einsum('blhi,hij->blhj', xb, rg_wa.astype(jnp.float32)).reshape(B_, L_, RNN_WIDTH) + rg_ba)
    i = jax.nn.sigmoid(jnp.einsum('blhi,hij->blhj', xb, rg_wx.astype(jnp.float32)).reshape(B_, L_, RNN_WIDTH) + rg_bx)
    log_a = -RGLRU_C * r * jax.nn.softplus(-rg_lambda.astype(jnp.float32))
    a = jnp.exp(log_a)
    mult = jnp.sqrt(-jnp.expm1(2.0 * log_a))
    rs = reset[..., None]
    mult = jnp.where(rs, 1.0, mult)
    a = jnp.where(rs, 0.0, a)
    _, h = lax.associative_scan(_real_combine, (a, mult * i * xc), axis=1)
    return (h * jax.nn.gelu(g_rnn.astype(jnp.float32))).astype(x_rnn.dtype)


def diff_attention(q, k, v, lam, lambda_init, subln_g):
    B_, L_, H, _, d = q.shape
    nb = L_ // Q_BLOCK
    qf = q.astype(jnp.float32).transpose(0, 2, 3, 1, 4)
    kf = k.astype(jnp.float32).transpose(0, 2, 3, 1, 4)
    vf = v.astype(jnp.float32).transpose(0, 2, 1, 3)
    q_blocks = qf.reshape(B_, H, 2, nb, Q_BLOCK, d).transpose(3, 0, 1, 2, 4, 5)
    key_idx = jnp.arange(L_)
    scale = d ** -0.5

    def attend_block(args):
        qb, bi = args
        s = jnp.einsum('bhcqd,bhckd->bhcqk', qb, kf) * scale
        q_idx = bi * Q_BLOCK + jnp.arange(Q_BLOCK)
        causal = key_idx[None, :] <= q_idx[:, None]
        p = jax.nn.softmax(jnp.where(causal, s, NEG_INF), axis=-1)
        w = p[:, :, 0] - lam * p[:, :, 1]
        return jnp.einsum('bhqk,bhkv->bhqv', w, vf)

    o = lax.map(attend_block, (q_blocks, jnp.arange(nb)))
    o = o.transpose(1, 0, 3, 2, 4).reshape(B_, L_, H, 2 * d)
    o = rmsnorm(o, subln_g, eps=1e-5) * (1.0 - lambda_init)
    return o.reshape(B_, L_, H * 2 * d).astype(q.dtype)


def s5_branch(u, lam_re, lam_im, b_re, b_im, c_re, c_im, d_skip, log_dt, glu_w, glu_b):
    B_, L_, _ = u.shape
    uf = u.astype(jnp.float32).reshape(B_, L_, SSM_GROUPS, SSM_GROUP)
    lr = lam_re.astype(jnp.float32)
    li = lam_im.astype(jnp.float32)
    dt = jnp.exp(log_dt.astype(jnp.float32))[:, None]
    mag = jnp.exp(lr * dt)
    ar = mag * jnp.cos(li * dt)
    ai = mag * jnp.sin(li * dt)
    den = lr * lr + li * li
    cr = ((ar - 1.0) * lr + ai * li) / den
    ci = (ai * lr - (ar - 1.0) * li) / den
    br = b_re.astype(jnp.float32)
    bi = b_im.astype(jnp.float32)
    bbar_re = cr[..., None] * br - ci[..., None] * bi
    bbar_im = cr[..., None] * bi + ci[..., None] * br
    bu_re = jnp.einsum('blgi,gni->blgn', uf, bbar_re)
    bu_im = jnp.einsum('blgi,gni->blgn', uf, bbar_im)
    a_re = jnp.broadcast_to(ar, (1, L_) + ar.shape)
    a_im = jnp.broadcast_to(ai, (1, L_) + ai.shape)
    _, _, s_re, s_im = lax.associative_scan(_complex_combine, (a_re, a_im, bu_re, bu_im), axis=1)
    y = (jnp.einsum('blgn,gin->blgi', s_re, c_re.astype(jnp.float32))
         - jnp.einsum('blgn,gin->blgi', s_im, c_im.astype(jnp.float32))
         + d_skip.astype(jnp.float32) * uf)
    z = jax.nn.gelu(y.reshape(B_, L_, SSM_WIDTH))
    out = z * jax.nn.sigmoid(z @ glu_w.astype(jnp.float32) + glu_b.astype(jnp.float32))
    return out.astype(u.dtype)


def token_mixer(xn, positions, layer_idx, w_in, conv_w, conv_b, rg_wa, rg_ba, rg_wx, rg_bx, rg_lambda,
                lam_q1, lam_k1, lam_q2, lam_k2, subln_g,
                ssm_lambda_re, ssm_lambda_im, ssm_b_re, ssm_b_im, ssm_c_re, ssm_c_im, ssm_d, ssm_log_dt,
                ssm_glu_w, ssm_glu_b, proj_rnn, proj_attn, proj_ssm, w_out):
    B_, L_, _ = xn.shape
    h = xn @ w_in
    x_rnn, g_rnn, q, k, v, u, gate_logits = jnp.split(h, IN_SPLITS, axis=-1)
    y_rnn = rglru_branch(x_rnn, g_rnn, positions == 0, conv_w, conv_b, rg_wa, rg_ba, rg_wx, rg_bx, rg_lambda)
    q = rope(q.reshape(B_, L_, ATTN_HEADS, 2, ATTN_HEAD_DIM), positions)
    k = rope(k.reshape(B_, L_, ATTN_HEADS, 2, ATTN_HEAD_DIM), positions)
    lambda_init = 0.8 - 0.6 * math.exp(-0.3 * layer_idx)
    lam = (jnp.exp(jnp.sum(lam_q1.astype(jnp.float32) * lam_k1.astype(jnp.float32)))
           - jnp.exp(jnp.sum(lam_q2.astype(jnp.float32) * lam_k2.astype(jnp.float32))) + lambda_init)
    y_attn = diff_attention(q, k, v.reshape(B_, L_, ATTN_HEADS, ATTN_V_DIM), lam, lambda_init, subln_g)
    y_ssm = s5_branch(u, ssm_lambda_re, ssm_lambda_im, ssm_b_re, ssm_b_im, ssm_c_re, ssm_c_im, ssm_d,
                      ssm_log_dt, ssm_glu_w, ssm_glu_b)
    gates = jax.nn.sigmoid(gate_logits.astype(jnp.float32)).astype(xn.dtype).reshape(B_, L_, N_BRANCH, D_MODEL)
    merged = (gates[:, :, 0] * (y_rnn @ proj_rnn)
              + gates[:, :, 1] * (y_attn @ proj_attn)
              + gates[:, :, 2] * (y_ssm @ proj_ssm))
    return merged @ w_out


def hierarchical_moe(xn, rc_w, rc_b, rf_w, rf_b, w1, w3, w2):
    B_, L_, D = xn.shape
    xt = xn.reshape(B_ * L_, D)
    coarse = (xt @ rc_w + rc_b).astype(jnp.float32)
    p_group = jax.nn.softmax(coarse, axis=-1)
    _, g_idx = lax.top_k(coarse, 1)
    p_sel = jnp.take_along_axis(p_group, g_idx, axis=-1)
    fine = (xt @ rf_w + rf_b).astype(jnp.float32).reshape(-1, N_GROUPS, EXPERTS_PER_GROUP)
    fine_sel = jnp.take_along_axis(fine, g_idx[:, :, None], axis=1)[:, 0]
    top_vals, top_idx = lax.top_k(fine_sel, TOP_K_FINE)
    top_w = jax.nn.softmax(top_vals, axis=-1)
    fine_gate = jnp.einsum('nk,nke->ne', top_w, jax.nn.one_hot(top_idx, EXPERTS_PER_GROUP, dtype=jnp.float32))
    gate = ((p_sel * jax.nn.one_hot(g_idx[:, 0], N_GROUPS, dtype=jnp.float32))[:, :, None]
            * fine_gate[:, None, :]).astype(xn.dtype)
    out = jnp.zeros_like(xt)
    for gi in range(N_GROUPS):
        hid = jax.nn.silu(jnp.einsum('nd,edf->nef', xt, w1[gi])) * jnp.einsum('nd,edf->nef', xt, w3[gi])
        out = out + jnp.einsum('nef,ne,efd->nd', hid, gate[:, gi], w2[gi])
    return out.reshape(B_, L_, D)


def setup_inputs(seed: int = 0) -> dict:
    key = jax.random.key(seed)
    ks = iter(jax.random.split(key, 64))
    f32 = jnp.float32

    def nrm(shape, scale):
        return scale * jax.random.normal(next(ks), shape, f32)

    x = jax.random.normal(next(ks), (BATCH, SEQ, D_MODEL), f32)
    positions = jnp.broadcast_to(jnp.arange(SEQ, dtype=jnp.int32), (BATCH, SEQ))
    a0 = jnp.sqrt(jax.random.uniform(next(ks), (DEPTH, RNN_WIDTH), f32, 0.81, 0.998))
    n_idx = jnp.arange(SSM_STATE, dtype=f32)
    return {
        'x': x,
        'positions': positions,
        'mix_norm_g': 1.0 + nrm((DEPTH, D_MODEL), 0.02),
        'w_in': nrm((DEPTH, D_MODEL, IN_COLS), D_MODEL ** -0.5),
        'conv_w': nrm((DEPTH, CONV_WIDTH, RNN_WIDTH), CONV_WIDTH ** -0.5),
        'conv_b': nrm((DEPTH, RNN_WIDTH), 0.01),
        'rg_wa': nrm((DEPTH, RNN_BLOCKS, RNN_BLOCK, RNN_BLOCK), RNN_BLOCK ** -0.5),
        'rg_ba': nrm((DEPTH, RNN_WIDTH), 0.01),
        'rg_wx': nrm((DEPTH, RNN_BLOCKS, RNN_BLOCK, RNN_BLOCK), RNN_BLOCK ** -0.5),
        'rg_bx': nrm((DEPTH, RNN_WIDTH), 0.01),
        'rg_lambda': jnp.log(a0) - jnp.log1p(-a0),
        'lam_q1': nrm((DEPTH, ATTN_HEAD_DIM), 0.1),
        'lam_k1': nrm((DEPTH, ATTN_HEAD_DIM), 0.1),
        'lam_q2': nrm((DEPTH, ATTN_HEAD_DIM), 0.1),
        'lam_k2': nrm((DEPTH, ATTN_HEAD_DIM), 0.1),
        'subln_g': 1.0 + nrm((DEPTH, ATTN_V_DIM), 0.02),
        'ssm_lambda_re': -0.5 + nrm((DEPTH, SSM_GROUPS, SSM_STATE), 0.01),
        'ssm_lambda_im': math.pi * n_idx + nrm((DEPTH, SSM_GROUPS, SSM_STATE), 0.01),
        'ssm_b_re': nrm((DEPTH, SSM_GROUPS, SSM_STATE, SSM_GROUP), (2 * SSM_GROUP) ** -0.5),
        'ssm_b_im': nrm((DEPTH, SSM_GROUPS, SSM_STATE, SSM_GROUP), (2 * SSM_GROUP) ** -0.5),
        'ssm_c_re': nrm((DEPTH, SSM_GROUPS, SSM_GROUP, SSM_STATE), SSM_STATE ** -0.5),
        'ssm_c_im': nrm((DEPTH, SSM_GROUPS, SSM_GROUP, SSM_STATE), SSM_STATE ** -0.5),
        'ssm_d': nrm((DEPTH, SSM_GROUPS, SSM_GROUP), 1.0),
        'ssm_log_dt': jax.random.uniform(next(ks), (DEPTH, SSM_GROUPS), f32, math.log(1e-3), math.log(1e-1)),
        'ssm_glu_w': nrm((DEPTH, SSM_WIDTH, SSM_WIDTH), SSM_WIDTH ** -0.5),
        'ssm_glu_b': nrm((DEPTH, SSM_WIDTH), 0.01),
        'proj_rnn': nrm((DEPTH, RNN_WIDTH, D_MODEL), RNN_WIDTH ** -0.5),
        'proj_attn': nrm((DEPTH, ATTN_WIDTH, D_MODEL), ATTN_WIDTH ** -0.5),
        'proj_ssm': nrm((DEPTH, SSM_WIDTH, D_MODEL), SSM_WIDTH ** -0.5),
        'w_out': nrm((DEPTH, D_MODEL, D_MODEL), D_MODEL ** -0.5),
        'ffn_norm_g': 1.0 + nrm((DEPTH, D_MODEL), 0.02),
        'router_coarse_w': nrm((DEPTH, D_MODEL, N_GROUPS), D_MODEL ** -0.5),
        'router_coarse_b': nrm((DEPTH, N_GROUPS), 0.01),
        'router_fine_w': nrm((DEPTH, D_MODEL, N_GROUPS * EXPERTS_PER_GROUP), D_MODEL ** -0.5),
        'router_fine_b': nrm((DEPTH, N_GROUPS * EXPERTS_PER_GROUP), 0.01),
        'expert_w1': nrm((DEPTH, N_GROUPS, EXPERTS_PER_GROUP, D_MODEL, EXPERT_FF), D_MODEL ** -0.5),
        'expert_w3': nrm((DEPTH, N_GROUPS, EXPERTS_PER_GROUP, D_MODEL, EXPERT_FF), D_MODEL ** -0.5),
        'expert_w2': nrm((DEPTH, N_GROUPS, EXPERTS_PER_GROUP, EXPERT_FF, D_MODEL), EXPERT_FF ** -0.5),
        'final_norm_g': 1.0 + nrm((D_MODEL,), 0.02),
    }


def reference(x, positions, mix_norm_g, w_in, conv_w, conv_b, rg_wa, rg_ba, rg_wx, rg_bx, rg_lambda,
              lam_q1, lam_k1, lam_q2, lam_k2, subln_g,
              ssm_lambda_re, ssm_lambda_im, ssm_b_re, ssm_b_im, ssm_c_re, ssm_c_im, ssm_d, ssm_log_dt,
              ssm_glu_w, ssm_glu_b, proj_rnn, proj_attn, proj_ssm, w_out,
              ffn_norm_g, router_coarse_w, router_coarse_b, router_fine_w, router_fine_b,
              expert_w1, expert_w3, expert_w2, final_norm_g):
    for l in range(DEPTH):
        xn = rmsnorm(x, mix_norm_g[l])
        x = x + token_mixer(xn, positions, l, w_in[l], conv_w[l], conv_b[l], rg_wa[l], rg_ba[l],
                            rg_wx[l], rg_bx[l], rg_lambda[l], lam_q1[l], lam_k1[l], lam_q2[l], lam_k2[l],
                            subln_g[l], ssm_lambda_re[l], ssm_lambda_im[l], ssm_b_re[l], ssm_b_im[l],
                            ssm_c_re[l], ssm_c_im[l], ssm_d[l], ssm_log_dt[l], ssm_glu_w[l], ssm_glu_b[l],
                            proj_rnn[l], proj_attn[l], proj_ssm[l], w_out[l])
        xn = rmsnorm(x, ffn_norm_g[l])
        x = x + hierarchical_moe(xn, router_coarse_w[l], router_coarse_b[l], router_fine_w[l],
                                 router_fine_b[l], expert_w1[l], expert_w3[l], expert_w2[l])
    return rmsnorm(x, final_norm_g)
```

```python
import functools
import math

import jax
import jax.numpy as jnp
from jax import lax
from jax.experimental import pallas as pl
from jax.experimental.pallas import tpu as pltpu

F32 = jnp.float32
BF16 = jnp.bfloat16

RGLRU_C = 8.0
ROPE_THETA = 10000.0
TOP_K_FINE = 2
NEG_INF = -1e30
MIX_EPS = 1e-6
SUBLN_EPS = 1e-5

LANES = 128
SUBLANES = 8
VMEM_LIMIT_BYTES = 56 * 1024 * 1024

S5_CHUNK = 16


def _cparams(*sem):
    return pltpu.CompilerParams(dimension_semantics=sem, vmem_limit_bytes=VMEM_LIMIT_BYTES)


def _const_spec(shape):
    nd = len(shape)
    return pl.BlockSpec(shape, lambda *_: (0,) * nd, pipeline_mode=pl.Buffered(1))


def _gelu_tanh(x):
    return 0.5 * x * (1.0 + jnp.tanh(math.sqrt(2.0 / math.pi) * (x + 0.044715 * (x * x * x))))


def _sigmoid(x):
    return 1.0 / (1.0 + jnp.exp(-x))


def _rmsnorm(x, g, eps):
    return x * lax.rsqrt(jnp.mean(x * x, axis=-1, keepdims=True) + eps) * g


def _inproj_kernel(x_ref, g_ref, pos_ref, invf_ref, w_ref, xr_ref, gr_ref, q_ref, k_ref, v_ref, u_ref,
                   *, splits, head_dim, q_scale):
    x = x_ref[...]
    xn = _rmsnorm(x, g_ref[...], MIX_EPS)
    h = jnp.dot(xn.astype(BF16), w_ref[...], preferred_element_type=F32)
    s0, s1, s2, s3, s4, s5 = splits
    xr_ref[...] = h[:, :s0].astype(xr_ref.dtype)
    gr_ref[...] = h[:, s0:s1].astype(gr_ref.dtype)
    v_ref[...] = h[:, s3:s4].astype(v_ref.dtype)
    u_ref[...] = h[:, s4:s5].astype(u_ref.dtype)

    ang = pos_ref[...] * invf_ref[...]
    cos = jnp.cos(ang)
    sin = jnp.sin(ang)
    lane = lax.broadcasted_iota(jnp.int32, ang.shape, 1)
    first_half = (lane % head_dim) < (head_dim // 2)
    sin_signed = jnp.where(first_half, -sin, sin)

    def rope(t, scale):
        outs = []
        for a in range(t.shape[1] // LANES):
            xs = t[:, a * LANES:(a + 1) * LANES]
            fwd = pltpu.roll(xs, LANES - head_dim // 2, 1)
            bwd = pltpu.roll(xs, head_dim // 2, 1)
            rot = jnp.where(first_half, fwd, bwd)
            outs.append((xs * cos + rot * sin_signed) * scale)
        return jnp.concatenate(outs, axis=1)

    q_ref[...] = rope(h[:, s1:s2], q_scale).astype(q_ref.dtype)
    k_ref[...] = rope(h[:, s2:s3], 1.0).astype(k_ref.dtype)


def _inproj(x2d, g, posf, invf, w, splits, head_dim, tm):
    n, d = x2d.shape
    c = w.shape[1]
    widths = [splits[0]] + [splits[i] - splits[i - 1] for i in range(1, 6)]
    dts = [F32, BF16, BF16, BF16, BF16, F32]
    kern = functools.partial(_inproj_kernel, splits=splits, head_dim=head_dim, q_scale=head_dim ** -0.5)
    row = lambda i: (i, 0)
    return pl.pallas_call(
        kern,
        grid=(n // tm,),
        in_specs=[pl.BlockSpec((tm, d), row), _const_spec((1, d)), pl.BlockSpec((tm, 1), row),
                  _const_spec((1, LANES)), _const_spec((d, c))],
        out_specs=[pl.BlockSpec((tm, wd), row) for wd in widths],
        out_shape=[jax.ShapeDtypeStruct((n, wd), dt) for wd, dt in zip(widths, dts)],
        compiler_params=_cparams("parallel"),
        name="inproj",
    )(x2d, g, posf, invf, w)


def _rglru_kernel(x_ref, g_ref, pos_ref, cw_ref, cb_ref, w_ref, b_ref, lam_ref, o_ref, halo_ref, h_ref):
    j = pl.program_id(1)

    @pl.when(j == 0)
    def _():
        halo_ref[...] = jnp.zeros_like(halo_ref)
        h_ref[...] = jnp.zeros_like(h_ref)

    x = x_ref[0].astype(F32)
    t, r = x.shape
    halo = halo_ref[...]
    row8 = lax.broadcasted_iota(jnp.int32, (SUBLANES, r), 0)
    cw = cw_ref[...]
    xc = cb_ref[...] + cw[0:1] * x
    for k in range(1, cw.shape[0]):
        rolled = pltpu.roll(x, k, 0)
        first = jnp.where(row8 < k, pltpu.roll(halo, k, 0), rolled[:SUBLANES])
        xc = xc + cw[k:k + 1] * jnp.concatenate([first, rolled[SUBLANES:]], axis=0)
    halo_ref[...] = x[t - SUBLANES:]

    gates = jnp.dot(xc.astype(BF16), w_ref[...], preferred_element_type=F32) + b_ref[...]
    rg = _sigmoid(gates[:, :r])
    ig = _sigmoid(gates[:, r:])
    z = -lam_ref[...]
    softplus = jnp.maximum(z, 0.0) + jnp.log(1.0 + jnp.exp(-jnp.abs(z)))
    log_a = (-RGLRU_C) * rg * softplus
    a = jnp.exp(log_a)
    mult = jnp.sqrt(1.0 - jnp.exp(2.0 * log_a))
    reset = pos_ref[0] == 0.0
    a = jnp.where(reset, 0.0, a)
    mult = jnp.where(reset, 1.0, mult)
    b = mult * ig * xc

    rows = lax.broadcasted_iota(jnp.int32, (t, r), 0)
    d = 1
    while d < t:
        keep = rows >= d
        a_sh = jnp.where(keep, pltpu.roll(a, d, 0), 1.0)
        b_sh = jnp.where(keep, pltpu.roll(b, d, 0), 0.0)
        b = b + a * b_sh
        a = a * a_sh
        d *= 2
    h = b + a * h_ref[...]
    h_ref[...] = h[t - 1:t]
    o_ref[0] = (h * _gelu_tanh(g_ref[0].astype(F32))).astype(o_ref.dtype)


def _rglru(x_rnn, g_rnn, posf, conv_w, conv_b, w_gates, b_gates, lam, tt):
    bsz, seq, r = x_rnn.shape
    kw = conv_w.shape[0]
    blk = lambda b, j: (b, j, 0)
    return pl.pallas_call(
        _rglru_kernel,
        grid=(bsz, seq // tt),
        in_specs=[pl.BlockSpec((1, tt, r), blk), pl.BlockSpec((1, tt, r), blk), pl.BlockSpec((1, tt, 1), blk),
                  _const_spec((kw, r)), _const_spec((1, r)), _const_spec((r, 2 * r)), _const_spec((1, 2 * r)),
                  _const_spec((1, r))],
        out_specs=pl.BlockSpec((1, tt, r), blk),
        out_shape=jax.ShapeDtypeStruct((bsz, seq, r), BF16),
        scratch_shapes=[pltpu.VMEM((SUBLANES, r), F32), pltpu.VMEM((1, r), F32)],
        compiler_params=_cparams("parallel", "arbitrary"),
        name="rglru",
    )(x_rnn, g_rnn, posf, conv_w, conv_b, w_gates, b_gates, lam)


def _attn_kernel(q_ref, k_ref, v_ref, lamv_ref, sg_ref, o_ref, *, tq, head_dim, lambda_init):
    i = pl.program_id(2)
    q = q_ref[0]
    lane = lax.broadcasted_iota(jnp.int32, q.shape, 1)
    zero = jnp.zeros_like(q)
    qq = jnp.concatenate([jnp.where(lane < head_dim, q, zero), jnp.where(lane >= head_dim, q, zero)], axis=0)
    vdim = v_ref.shape[2]

    def step(j, carry, masked):
        m, l, acc = carry
        kb = k_ref[0, pl.ds(j * tq, tq), :]
        vb = v_ref[0, pl.ds(j * tq, tq), :]
        s = lax.dot_general(qq, kb, (((1,), (1,)), ((), ())), preferred_element_type=F32)
        if masked:
            rows = lax.broadcasted_iota(jnp.int32, s.shape, 0)
            cols = lax.broadcasted_iota(jnp.int32, s.shape, 1)
            rows = jnp.where(rows >= tq, rows - tq, rows)
            s = jnp.where(cols <= rows, s, NEG_INF)
        m_new = jnp.maximum(m, jnp.max(s, axis=1, keepdims=True))
        p = jnp.exp(s - m_new)
        alpha = jnp.exp(m - m_new)
        l = alpha * l + jnp.sum(p, axis=1, keepdims=True)
        acc = alpha * acc + jnp.dot(p.astype(BF16), vb, preferred_element_type=F32)
        return m_new, l, acc

    init = (jnp.full((2 * tq, 1), NEG_INF, F32), jnp.zeros((2 * tq, 1), F32), jnp.zeros((2 * tq, vdim), F32))
    carry = lax.fori_loop(0, i, lambda j, c: step(j, c, False), init)
    m, l, acc = step(i, carry, True)

    lamv = lamv_ref[...]
    lam = (jnp.exp(jnp.sum(lamv[0:1] * lamv[1:2], axis=1, keepdims=True))
           - jnp.exp(jnp.sum(lamv[2:3] * lamv[3:4], axis=1, keepdims=True)) + lambda_init)
    o = acc / l
    o = o[:tq] - lam * o[tq:]
    o = _rmsnorm(o, sg_ref[...], SUBLN_EPS) * (1.0 - lambda_init)
    o_ref[0] = o.astype(o_ref.dtype)


def _diff_attention(q, k, v, lamv, subln_g, heads, head_dim, lambda_init, tq):
    bsz, seq, _ = q.shape
    vdim = v.shape[2] // heads
    kern = functools.partial(_attn_kernel, tq=tq, head_dim=head_dim, lambda_init=lambda_init)
    return pl.pallas_call(
        kern,
        grid=(bsz, heads, seq // tq),
        in_specs=[pl.BlockSpec((1, tq, 2 * head_dim), lambda b, h, i: (b, i, h)),
                  pl.BlockSpec((1, seq, 2 * head_dim), lambda b, h, i: (b, 0, h)),
                  pl.BlockSpec((1, seq, vdim), lambda b, h, i: (b, 0, h)),
                  _const_spec(lamv.shape), _const_spec(subln_g.shape)],
        out_specs=pl.BlockSpec((1, tq, vdim), lambda b, h, i: (b, i, h)),
        out_shape=jax.ShapeDtypeStruct(v.shape, BF16),
        compiler_params=_cparams("parallel", "parallel", "arbitrary"),
        name="diff_attn",
    )(q, k, v, lamv, subln_g)


def _s5_tables(lam_re, lam_im, b_re, b_im, c_re, c_im, d_skip, log_dt, n_steps):
    tc = S5_CHUNK
    g, n, p = b_re.shape
    lr = lam_re.astype(F32)
    li = lam_im.astype(F32)
    dt = jnp.exp(log_dt.astype(F32))[:, None]
    mag = jnp.exp(lr * dt)
    ar = mag * jnp.cos(li * dt)
    ai = mag * jnp.sin(li * dt)
    den = lr * lr + li * li
    cr = ((ar - 1.0) * lr + ai * li) / den
    ci = (ai * lr - (ar - 1.0) * li) / den
    bb_re = cr[..., None] * b_re - ci[..., None] * b_im
    bb_im = cr[..., None] * b_im + ci[..., None] * b_re
    hi = lax.Precision.HIGHEST

    def apow(e):
        e = jnp.asarray(e, F32)[None, None, :]
        m = jnp.exp(e * (lr * dt)[..., None])
        ph = e * (li * dt)[..., None]
        return m * jnp.cos(ph), m * jnp.sin(ph)

    lags = jnp.arange(tc)
    pw_re, pw_im = apow(lags)
    ab_re = pw_re[..., None] * bb_re[:, :, None, :] - pw_im[..., None] * bb_im[:, :, None, :]
    ab_im = pw_re[..., None] * bb_im[:, :, None, :] + pw_im[..., None] * bb_re[:, :, None, :]
    kl = (jnp.einsum('gon,gnji->gjoi', c_re, ab_re, precision=hi)
          - jnp.einsum('gon,gnji->gjoi', c_im, ab_im, precision=hi))
    lag = lags[:, None] - lags[None, :]
    ktoe = jnp.where((lag >= 0)[None, :, :, None, None], kl[:, jnp.clip(lag, 0, tc - 1)], 0.0)
    skip = (jnp.eye(tc, dtype=F32)[None, :, :, None, None] * jnp.eye(p, dtype=F32)[None, None, None]
            * d_skip[:, None, None, None, :])
    kt = (ktoe + skip).transpose(0, 1, 3, 2, 4).reshape(g, tc * p, tc * p)
    bt = jnp.concatenate([ab_re[:, :, ::-1, :].reshape(g, n, tc * p), ab_im[:, :, ::-1, :].reshape(g, n, tc * p)],
                         axis=1)
    p1_re, p1_im = apow(lags + 1)
    ca_re = jnp.einsum('gon,gnt->gton', c_re, p1_re) - jnp.einsum('gon,gnt->gton', c_im, p1_im)
    ca_im = jnp.einsum('gon,gnt->gton', c_re, p1_im) + jnp.einsum('gon,gnt->gton', c_im, p1_re)
    ct = jnp.concatenate([ca_re.reshape(g, tc * p, n), -ca_im.reshape(g, tc * p, n)], axis=2)
    st_re, st_im = apow(tc * (2 ** jnp.arange(n_steps)))
    sr = st_re.transpose(0, 2, 1)
    si = st_im.transpose(0, 2, 1)
    tab = jnp.stack([jnp.concatenate([sr, sr], axis=2), jnp.concatenate([-si, si], axis=2)], axis=2)
    tab = tab.reshape(g, 2 * n_steps, 2 * n)
    tab = jnp.pad(tab, ((0, 0), (0, 2 * n - 2 * n_steps), (0, 0)))
    return kt.astype(BF16), bt.astype(BF16), ct.astype(BF16), tab.astype(F32)


def _s5_kernel(u_ref, kt_ref, bt_ref, ct_ref, tab_ref, o_ref, ut_ref, y_ref, carry_ref, *, n_groups, p, width):
    tc = S5_CHUNK
    c = u_ref.shape[1]
    n2 = bt_ref.shape[1]
    half = n2 // 2
    n_steps = int(math.log2(c))

    @pl.when(pl.program_id(1) == 0)
    def _():
        carry_ref[...] = jnp.zeros_like(carry_ref)

    for k in range(tc):
        ut_ref[k] = jnp.transpose(u_ref[0, :, k * width:(k + 1) * width]).astype(BF16)

    lane = lax.broadcasted_iota(jnp.int32, (n2, c), 1)

    def swap(s):
        return jnp.concatenate([s[half:], s[:half]], axis=0)

    def group(g, _):
        row0 = pl.multiple_of(g * p, p)
        ug = jnp.concatenate([ut_ref[k, pl.ds(row0, p), :] for k in range(tc)], axis=0)
        yi = jnp.dot(kt_ref[g], ug, preferred_element_type=F32)
        st = jnp.dot(bt_ref[g], ug, preferred_element_type=F32)
        tabt = jnp.transpose(tab_ref[g])
        cin = carry_ref[g]
        inj = tabt[:, 0:1] * cin + tabt[:, 1:2] * swap(cin)
        st = st + jnp.where(lane == 0, inj, 0.0)
        for s in range(n_steps):
            d = 1 << s
            sh = jnp.where(lane >= d, pltpu.roll(st, d, 1), 0.0)
            st = st + tabt[:, 2 * s:2 * s + 1] * sh + tabt[:, 2 * s + 1:2 * s + 2] * swap(sh)
        prev = jnp.where(lane >= 1, pltpu.roll(st, 1, 1), cin)
        carry_ref[g] = jnp.broadcast_to(st[:, c - 1:c], (n2, c))
        y_ref[g] = yi + jnp.dot(ct_ref[g], prev.astype(BF16), preferred_element_type=F32)
        return 0

    lax.fori_loop(0, n_groups, group, 0)

    for t in range(tc):
        yt = jnp.concatenate([y_ref[g, t * p:(t + 1) * p, :] for g in range(n_groups)], axis=0)
        o_ref[0, :, t * width:(t + 1) * width] = jnp.transpose(yt).astype(o_ref.dtype)


def _s5(u, tables, c_lanes):
    kt, bt, ct, tab = tables
    bsz, seq, width = u.shape
    tc = S5_CHUNK
    n_groups, n2 = bt.shape[0], bt.shape[1]
    p = width // n_groups
    nchunk = seq // tc
    uv = u.reshape(bsz, nchunk, tc * width)
    kern = functools.partial(_s5_kernel, n_groups=n_groups, p=p, width=width)
    blk = lambda b, j: (b, j, 0)
    out = pl.pallas_call(
        kern,
        grid=(bsz, nchunk // c_lanes),
        in_specs=[pl.BlockSpec((1, c_lanes, tc * width), blk), _const_spec(kt.shape), _const_spec(bt.shape),
                  _const_spec(ct.shape), _const_spec(tab.shape)],
        out_specs=pl.BlockSpec((1, c_lanes, tc * width), blk),
        out_shape=jax.ShapeDtypeStruct(uv.shape, BF16),
        scratch_shapes=[pltpu.VMEM((tc, width, c_lanes), BF16), pltpu.VMEM((n_groups, tc * p, c_lanes), F32),
                        pltpu.VMEM((n_groups, n2, c_lanes), F32)],
        compiler_params=_cparams("parallel", "arbitrary"),
        name="s5",
    )(uv, kt, bt, ct, tab)
    return out.reshape(bsz, seq, width)


def _merge_kernel(x_ref, yr_ref, ya_ref, ys_ref, mg_ref, wg_ref, gw_ref, gb_ref, pr_ref, pa_ref, ps_ref, wo_ref,
                  fg_ref, rw_ref, rb_ref, x1_ref, xn_ref, route_ref, cnt_ref, run_ref,
                  *, n_groups, n_experts, d_model):
    i = pl.program_id(0)

    @pl.when(i == 0)
    def _():
        run_ref[...] = jnp.zeros_like(run_ref)

    x = x_ref[...]
    tm = x.shape[0]
    xn = _rmsnorm(x, mg_ref[...], MIX_EPS)
    gates = _sigmoid(jnp.dot(xn.astype(BF16), wg_ref[...], preferred_element_type=F32))
    z = _gelu_tanh(ys_ref[...].astype(F32))
    ys = z * _sigmoid(jnp.dot(z.astype(BF16), gw_ref[...], preferred_element_type=F32) + gb_ref[...])
    merged = (gates[:, :d_model] * jnp.dot(yr_ref[...], pr_ref[...], preferred_element_type=F32)
              + gates[:, d_model:2 * d_model] * jnp.dot(ya_ref[...], pa_ref[...], preferred_element_type=F32)
              + gates[:, 2 * d_model:] * jnp.dot(ys.astype(BF16), ps_ref[...], preferred_element_type=F32))
    x1 = x + jnp.dot(merged.astype(BF16), wo_ref[...], preferred_element_type=F32)
    x1_ref[...] = x1
    xn2 = _rmsnorm(x1, fg_ref[...], MIX_EPS)
    xn_ref[...] = xn2.astype(xn_ref.dtype)

    logits = jnp.dot(xn2, rw_ref[...], preferred_element_type=F32, precision=lax.Precision.HIGHEST) + rb_ref[...]
    lane = lax.broadcasted_iota(jnp.int32, logits.shape, 1).astype(F32)
    big = float(LANES)
    coarse = jnp.where(lane < n_groups, logits, NEG_INF)
    cmax = jnp.max(coarse, axis=1, keepdims=True)
    gsel = jnp.min(jnp.where(coarse == cmax, lane, big), axis=1, keepdims=True)
    p_sel = 1.0 / jnp.sum(jnp.where(lane < n_groups, jnp.exp(logits - cmax), 0.0), axis=1, keepdims=True)
    lo = n_groups + gsel * n_experts
    fine = jnp.where((lane >= lo) & (lane < lo + n_experts), logits, NEG_INF)
    m1 = jnp.max(fine, axis=1, keepdims=True)
    i1 = jnp.min(jnp.where(fine == m1, lane, big), axis=1, keepdims=True)
    fine2 = jnp.where(lane == i1, NEG_INF, fine)
    m2 = jnp.max(fine2, axis=1, keepdims=True)
    i2 = jnp.min(jnp.where(fine2 == m2, lane, big), axis=1, keepdims=True)
    e21 = jnp.exp(m2 - m1)
    w1 = p_sel / (1.0 + e21)
    w2 = p_sel * e21 / (1.0 + e21)
    oh1 = lane == i1
    oh2 = lane == i2
    onehot = jnp.where(oh1 | oh2, 1.0, 0.0)
    r_i = lax.broadcasted_iota(jnp.int32, (tm, tm), 0)
    c_i = lax.broadcasted_iota(jnp.int32, (tm, tm), 1)
    earlier = jnp.where(c_i < r_i, 1.0, 0.0).astype(BF16)
    before = jnp.dot(earlier, onehot.astype(BF16), preferred_element_type=F32) + run_ref[...]
    rank1 = jnp.sum(jnp.where(oh1, before, 0.0), axis=1, keepdims=True)
    rank2 = jnp.sum(jnp.where(oh2, before, 0.0), axis=1, keepdims=True)
    run_ref[...] = run_ref[...] + jnp.sum(onehot, axis=0, keepdims=True)
    cnt_ref[...] = run_ref[...]
    e1 = i1 - n_groups
    e2 = i2 - n_groups
    route = jnp.where(lane == 0, e1, 0.0)
    route = jnp.where(lane == 1, e2, route)
    route = jnp.where(lane == 2, w1, route)
    route = jnp.where(lane == 3, w2, route)
    route = jnp.where(lane == 4, rank1, route)
    route = jnp.where(lane == 5, rank2, route)
    route_ref[...] = route


def _merge(x2d, y_rnn, y_attn, y_s5, mix_g, w_gate, glu_w, glu_b, p_rnn, p_attn, p_ssm, w_out, ffn_g, rw, rb,
           n_groups, n_experts, tm):
    n, d = x2d.shape
    row = lambda i: (i, 0)
    kern = functools.partial(_merge_kernel, n_groups=n_groups, n_experts=n_experts, d_model=d)
    consts = [mix_g, w_gate, glu_w, glu_b, p_rnn, p_attn, p_ssm, w_out, ffn_g, rw, rb]
    return pl.pallas_call(
        kern,
        grid=(n // tm,),
        in_specs=[pl.BlockSpec((tm, d), row), pl.BlockSpec((tm, y_rnn.shape[1]), row),
                  pl.BlockSpec((tm, y_attn.shape[1]), row), pl.BlockSpec((tm, y_s5.shape[1]), row)]
                 + [_const_spec(a.shape) for a in consts],
        out_specs=[pl.BlockSpec((tm, d), row), pl.BlockSpec((tm, d), row), pl.BlockSpec((tm, LANES), row),
                   pl.BlockSpec((1, LANES), lambda i: (0, 0))],
        out_shape=[jax.ShapeDtypeStruct((n, d), F32), jax.ShapeDtypeStruct((n, d), F32),
                   jax.ShapeDtypeStruct((n, LANES), F32), jax.ShapeDtypeStruct((1, LANES), F32)],
        scratch_shapes=[pltpu.VMEM((1, LANES), F32)],
        compiler_params=_cparams("arbitrary"),
        name="merge_router",
    )(x2d, y_rnn, y_attn, y_s5, *consts)


def _dispatch_kernel(dest_ref, x_ref, xs_in_ref, xs_ref, sem):
    del xs_in_ref
    tm = x_ref.shape[0]

    def copy(r, slot):
        return pltpu.make_async_copy(x_ref.at[pl.ds(r, 1)], xs_ref.at[pl.ds(dest_ref[TOP_K_FINE * r + slot], 1)], sem)

    def start(r, _):
        for slot in range(TOP_K_FINE):
            copy(r, slot).start()
        return 0

    def wait(r, _):
        for slot in range(TOP_K_FINE):
            copy(r, slot).wait()
        return 0

    lax.fori_loop(0, tm, start, 0)
    lax.fori_loop(0, tm, wait, 0)


def _dispatch(dest_flat, xn2, n_rows, tm):
    n, d = xn2.shape
    xs0 = jnp.zeros((n_rows, d), xn2.dtype)
    return pl.pallas_call(
        _dispatch_kernel,
        grid=(n // tm,),
        in_specs=[pl.BlockSpec((TOP_K_FINE * tm,), lambda i: (i,), memory_space=pltpu.SMEM),
                  pl.BlockSpec((tm, d), lambda i: (i, 0)),
                  pl.BlockSpec(memory_space=pl.ANY)],
        out_specs=pl.BlockSpec(memory_space=pl.ANY),
        out_shape=jax.ShapeDtypeStruct((n_rows, d), xn2.dtype),
        scratch_shapes=[pltpu.SemaphoreType.DMA(())],
        input_output_aliases={2: 0},
        compiler_params=_cparams("arbitrary"),
        name="moe_dispatch",
    )(dest_flat, xn2, xs0)


def _experts_kernel(te_ref, nact_ref, x_ref, w1_ref, w3_ref, w2_ref, o_ref, w1b_ref, w3b_ref, w2b_ref):
    i = pl.program_id(0)
    fresh = jnp.logical_or(i == 0, te_ref[i] != te_ref[jnp.maximum(i - 1, 0)])

    @pl.when(jnp.logical_and(i < nact_ref[0], fresh))
    def _():
        w1b_ref[...] = w1_ref[0].astype(BF16)
        w3b_ref[...] = w3_ref[0].astype(BF16)
        w2b_ref[...] = w2_ref[0].astype(BF16)

    @pl.when(i < nact_ref[0])
    def _():
        xb = x_ref[...].astype(BF16)
        h1 = jnp.dot(xb, w1b_ref[...], preferred_element_type=F32)
        h3 = jnp.dot(xb, w3b_ref[...], preferred_element_type=F32)
        hid = h1 * _sigmoid(h1) * h3
        o_ref[...] = jnp.dot(hid.astype(BF16), w2b_ref[...], preferred_element_type=F32).astype(o_ref.dtype)

    @pl.when(i >= nact_ref[0])
    def _():
        o_ref[...] = jnp.zeros_like(o_ref)


def _experts(tile_expert, n_active, xs, w1, w3, w2, tm):
    n_rows, d = xs.shape
    f = w1.shape[2]
    row = lambda i, te, na: (jnp.minimum(i, na[0] - 1), 0)
    wsel = lambda i, te, na: (te[i], 0, 0)
    grid_spec = pltpu.PrefetchScalarGridSpec(
        num_scalar_prefetch=2,
        grid=(n_rows // tm,),
        in_specs=[pl.BlockSpec((tm, d), row), pl.BlockSpec((1, d, f), wsel), pl.BlockSpec((1, d, f), wsel),
                  pl.BlockSpec((1, f, d), wsel)],
        out_specs=pl.BlockSpec((tm, d), lambda i, te, na: (i, 0)),
        scratch_shapes=[pltpu.VMEM((d, f), BF16), pltpu.VMEM((d, f), BF16), pltpu.VMEM((f, d), BF16)],
    )
    return pl.pallas_call(
        _experts_kernel,
        grid_spec=grid_spec,
        out_shape=jax.ShapeDtypeStruct((n_rows, d), F32),
        compiler_params=_cparams("arbitrary"),
        name="moe_experts",
    )(tile_expert, n_active, xs, w1, w3, w2)


def _combine_kernel(dest_ref, x_ref, route_ref, fg_ref, ys_ref, o_ref, buf_ref, sem, *, final_norm):
    tm = x_ref.shape[0]

    def copy(r, slot):
        return pltpu.make_async_copy(ys_ref.at[pl.ds(dest_ref[TOP_K_FINE * r + slot], 1)],
                                     buf_ref.at[slot, pl.ds(r, 1)], sem)

    def start(r, _):
        for slot in range(TOP_K_FINE):
            copy(r, slot).start()
        return 0

    def wait(r, _):
        for slot in range(TOP_K_FINE):
            copy(r, slot).wait()
        return 0

    lax.fori_loop(0, tm, start, 0)
    lax.fori_loop(0, tm, wait, 0)
    route = route_ref[...]
    out = x_ref[...] + route[:, 2:3] * buf_ref[0] + route[:, 3:4] * buf_ref[1]
    if final_norm:
        out = _rmsnorm(out, fg_ref[...], MIX_EPS)
    o_ref[...] = out


def _combine(dest_flat, x1, route, final_g, ys, tm, final_norm):
    n, d = x1.shape
    kern = functools.partial(_combine_kernel, final_norm=final_norm)
    return pl.pallas_call(
        kern,
        grid=(n // tm,),
        in_specs=[pl.BlockSpec((TOP_K_FINE * tm,), lambda i: (i,), memory_space=pltpu.SMEM),
                  pl.BlockSpec((tm, d), lambda i: (i, 0)),
                  pl.BlockSpec((tm, LANES), lambda i: (i, 0)),
                  _const_spec((1, d)),
                  pl.BlockSpec(memory_space=pl.ANY)],
        out_specs=pl.BlockSpec((tm, d), lambda i: (i, 0)),
        out_shape=jax.ShapeDtypeStruct((n, d), F32),
        scratch_shapes=[pltpu.VMEM((TOP_K_FINE, tm, d), F32), pltpu.SemaphoreType.DMA(())],
        compiler_params=_cparams("arbitrary"),
        name="moe_combine",
    )(dest_flat, x1, route, final_g, ys)


def _tile_plan(n, seq):
    return dict(inproj=min(512, n), rglru=min(256, seq), attn=min(256, seq),
                s5_lanes=min(LANES, seq // S5_CHUNK), merge=min(256, n), moe=min(256, n))


def kernel(x, positions, mix_norm_g, w_in, conv_w, conv_b, rg_wa, rg_ba, rg_wx, rg_bx, rg_lambda,
           lam_q1, lam_k1, lam_q2, lam_k2, subln_g,
           ssm_lambda_re, ssm_lambda_im, ssm_b_re, ssm_b_im, ssm_c_re, ssm_c_im, ssm_d, ssm_log_dt,
           ssm_glu_w, ssm_glu_b, proj_rnn, proj_attn, proj_ssm, w_out,
           ffn_norm_g, router_coarse_w, router_coarse_b, router_fine_w, router_fine_b,
           expert_w1, expert_w3, expert_w2, final_norm_g):
    return _forward(_tile_plan(x.shape[0] * x.shape[1], x.shape[1]),
                    x, positions, mix_norm_g, w_in, conv_w, conv_b, rg_wa, rg_ba, rg_wx, rg_bx, rg_lambda,
                    lam_q1, lam_k1, lam_q2, lam_k2, subln_g,
                    ssm_lambda_re, ssm_lambda_im, ssm_b_re, ssm_b_im, ssm_c_re, ssm_c_im, ssm_d, ssm_log_dt,
                    ssm_glu_w, ssm_glu_b, proj_rnn, proj_attn, proj_ssm, w_out,
                    ffn_norm_g, router_coarse_w, router_coarse_b, router_fine_w, router_fine_b,
                    expert_w1, expert_w3, expert_w2, final_norm_g)


def _forward(tiles, x, positions, mix_norm_g, w_in, conv_w, conv_b, rg_wa, rg_ba, rg_wx, rg_bx, rg_lambda,
             lam_q1, lam_k1, lam_q2, lam_k2, subln_g,
             ssm_lambda_re, ssm_lambda_im, ssm_b_re, ssm_b_im, ssm_c_re, ssm_c_im, ssm_d, ssm_log_dt,
             ssm_glu_w, ssm_glu_b, proj_rnn, proj_attn, proj_ssm, w_out,
             ffn_norm_g, router_coarse_w, router_coarse_b, router_fine_w, router_fine_b,
             expert_w1, expert_w3, expert_w2, final_norm_g):
    bsz, seq, d_model = x.shape
    depth = w_in.shape[0]
    n = bsz * seq
    r = conv_w.shape[2]
    sw = ssm_glu_w.shape[1]
    vdim = subln_g.shape[1]
    head_dim = vdim // 2
    in_cols = w_in.shape[2]
    qk = (in_cols - 2 * r - sw - 3 * d_model) // 3
    heads = qk // (2 * head_dim)
    splits = (r, 2 * r, 2 * r + qk, 2 * r + 2 * qk, 2 * r + 3 * qk, 2 * r + 3 * qk + sw)
    mix_cols = splits[-1]
    n_groups = router_coarse_w.shape[2]
    n_experts = expert_w1.shape[2]
    n_total = n_groups * n_experts
    rnn_blocks = rg_wa.shape[1]

    tm_in, tt_rnn, tq = tiles["inproj"], tiles["rglru"], tiles["attn"]
    s5_lanes, tm_merge, tm_moe = tiles["s5_lanes"], tiles["merge"], tiles["moe"]
    n_rows = TOP_K_FINE * n + n_total * tm_moe

    posf = positions.astype(F32)
    pos_col = posf.reshape(n, 1)
    pos_blk = posf.reshape(bsz, seq, 1)
    inv_freq = ROPE_THETA ** (-jnp.arange(0, head_dim, 2, dtype=F32) / head_dim)
    invf = jnp.tile(inv_freq, LANES // (head_dim // 2)).reshape(1, LANES)
    eye_blocks = jnp.eye(rnn_blocks, dtype=F32)

    x2d = x.reshape(n, d_model)
    for l in range(depth):
        lambda_init = 0.8 - 0.6 * math.exp(-0.3 * l)
        w_mix = w_in[l, :, :mix_cols].astype(BF16)
        w_gate = w_in[l, :, mix_cols:].astype(BF16)
        x_rnn, g_rnn, q, k, v, u = _inproj(x2d, mix_norm_g[l].reshape(1, d_model), pos_col, invf, w_mix,
                                           splits, head_dim, tm_in)

        def block_diag(w):
            return jnp.einsum('hij,hk->hikj', w, eye_blocks).reshape(r, r)

        w_gates = jnp.concatenate([block_diag(rg_wa[l]), block_diag(rg_wx[l])], axis=1).astype(BF16)
        b_gates = jnp.concatenate([rg_ba[l], rg_bx[l]]).reshape(1, 2 * r)
        y_rnn = _rglru(x_rnn.reshape(bsz, seq, r), g_rnn.reshape(bsz, seq, r), pos_blk, conv_w[l],
                       conv_b[l].reshape(1, r), w_gates, b_gates, rg_lambda[l].reshape(1, r), tt_rnn)

        lamv = jnp.stack([lam_q1[l], lam_k1[l], lam_q2[l], lam_k2[l]])
        y_attn = _diff_attention(q.reshape(bsz, seq, qk), k.reshape(bsz, seq, qk), v.reshape(bsz, seq, qk),
                                 lamv, subln_g[l].reshape(1, vdim), heads, head_dim, lambda_init, tq)

        tables = _s5_tables(ssm_lambda_re[l], ssm_lambda_im[l], ssm_b_re[l], ssm_b_im[l], ssm_c_re[l],
                            ssm_c_im[l], ssm_d[l], ssm_log_dt[l], int(math.log2(s5_lanes)))
        y_s5 = _s5(u.reshape(bsz, seq, sw), tables, s5_lanes)

        rw = jnp.concatenate([router_coarse_w[l], router_fine_w[l]], axis=1)
        rw = jnp.pad(rw, ((0, 0), (0, LANES - rw.shape[1])))
        rb = jnp.concatenate([router_coarse_b[l], router_fine_b[l]])
        rb = jnp.pad(rb, (0, LANES - rb.shape[0])).reshape(1, LANES)
        x1, xn2, route, counts = _merge(
            x2d, y_rnn.reshape(n, r), y_attn.reshape(n, qk), y_s5.reshape(n, sw),
            mix_norm_g[l].reshape(1, d_model), w_gate, ssm_glu_w[l].astype(BF16), ssm_glu_b[l].reshape(1, sw),
            proj_rnn[l].astype(BF16), proj_attn[l].astype(BF16), proj_ssm[l].astype(BF16), w_out[l].astype(BF16),
            ffn_norm_g[l].reshape(1, d_model), rw, rb, n_groups, n_experts, tm_merge)

        cnt = counts[0, n_groups:n_groups + n_total].astype(jnp.int32)
        tiles = (cnt + tm_moe - 1) // tm_moe
        tile_end = jnp.cumsum(tiles)
        offsets = (tile_end - tiles) * tm_moe
        eid = route[:, 0:TOP_K_FINE].astype(jnp.int32)
        rank = route[:, 4:4 + TOP_K_FINE].astype(jnp.int32)
        dest = (offsets[eid] + rank).reshape(TOP_K_FINE * n)
        n_active = tile_end[-1:]
        tile_ids = jnp.minimum(jnp.arange(n_rows // tm_moe, dtype=jnp.int32), n_active[0] - 1)
        tile_expert = jnp.sum((tile_ids[:, None] >= tile_end[None, :]).astype(jnp.int32), axis=1)

        xs = _dispatch(dest, xn2, n_rows, tm_moe)
        ys = _experts(tile_expert, n_active.astype(jnp.int32), xs,
                      expert_w1[l].reshape(n_total, d_model, -1), expert_w3[l].reshape(n_total, d_model, -1),
                      expert_w2[l].reshape(n_total, -1, d_model), tm_moe)
        x2d = _combine(dest, x1, route, final_norm_g.reshape(1, d_model), ys, tm_moe, l == depth - 1)
    return x2d.reshape(bsz, seq, d_model)
```

```python
import functools
import math

import jax
import jax.numpy as jnp
from jax import lax
from jax.experimental import pallas as pl
from jax.experimental.pallas import tpu as pltpu

F32 = jnp.float32
BF16 = jnp.bfloat16

RGLRU_C = 8.0
ROPE_THETA = 10000.0
TOP_K_FINE = 2
NEG_INF = -1e30
MIX_EPS = 1e-6
SUBLN_EPS = 1e-5

LANES = 128
SUBLANES = 8
VMEM_LIMIT_BYTES = 56 * 1024 * 1024

S5_CHUNK = 16


def _cparams(*sem):
    return pltpu.CompilerParams(dimension_semantics=sem, vmem_limit_bytes=VMEM_LIMIT_BYTES)


def _const_spec(shape):
    nd = len(shape)
    return pl.BlockSpec(shape, lambda *_: (0,) * nd, pipeline_mode=pl.Buffered(1))


def _gelu_tanh(x):
    return 0.5 * x * (1.0 + jnp.tanh(math.sqrt(2.0 / math.pi) * (x + 0.044715 * (x * x * x))))


def _sigmoid(x):
    return 1.0 / (1.0 + jnp.exp(-x))


def _rmsnorm(x, g, eps):
    return x * lax.rsqrt(jnp.mean(x * x, axis=-1, keepdims=True) + eps) * g


def _inproj_kernel(x_ref, g_ref, pos_ref, invf_ref, w_ref, xr_ref, gr_ref, q_ref, k_ref, vt_ref, *u_refs,
                   splits, head_dim, q_scale):
    x = x_ref[...]
    xn = _rmsnorm(x, g_ref[...], MIX_EPS)
    h = jnp.dot(xn.astype(BF16), w_ref[...], preferred_element_type=F32)
    s0, s1, s2, s3, s4, s5 = splits
    xr_ref[...] = h[:, :s0].astype(xr_ref.dtype)
    gr_ref[...] = h[:, s0:s1].astype(gr_ref.dtype)
    for a, u_ref in enumerate(u_refs):
        u_ref[...] = h[:, s4 + a * LANES:s4 + (a + 1) * LANES]
    tkv = vt_ref.shape[2]
    for c in range(vt_ref.shape[0]):
        vt_ref[c] = jnp.transpose(h[c * tkv:(c + 1) * tkv, s3:s4]).astype(vt_ref.dtype)

    ang = pos_ref[...] * invf_ref[...]
    cos = jnp.cos(ang)
    sin = jnp.sin(ang)
    lane = lax.broadcasted_iota(jnp.int32, ang.shape, 1)
    first_half = (lane % head_dim) < (head_dim // 2)
    sin_signed = jnp.where(first_half, -sin, sin)

    def rope(t, scale):
        outs = []
        for a in range(t.shape[1] // LANES):
            xs = t[:, a * LANES:(a + 1) * LANES]
            fwd = pltpu.roll(xs, LANES - head_dim // 2, 1)
            bwd = pltpu.roll(xs, head_dim // 2, 1)
            rot = jnp.where(first_half, fwd, bwd)
            outs.append((xs * cos + rot * sin_signed) * scale)
        return jnp.concatenate(outs, axis=1)

    q_ref[...] = rope(h[:, s1:s2], q_scale).astype(q_ref.dtype)
    k_ref[...] = rope(h[:, s2:s3], 1.0).astype(k_ref.dtype)


def _inproj(x2d, g, posf, invf, w, splits, head_dim, tm, tkv):
    n, d = x2d.shape
    c = w.shape[1]
    widths = [splits[0]] + [splits[i] - splits[i - 1] for i in range(1, 6)]
    n_slab = widths[5] // LANES
    kern = functools.partial(_inproj_kernel, splits=splits, head_dim=head_dim,
                             q_scale=head_dim ** -0.5 * math.log2(math.e))
    row = lambda i: (i, 0)
    rows = lambda wd: pl.BlockSpec((tm, wd), row)
    out = lambda wd, dt: jax.ShapeDtypeStruct((n, wd), dt)
    return pl.pallas_call(
        kern,
        grid=(n // tm,),
        in_specs=[pl.BlockSpec((tm, d), row), _const_spec((1, d)), pl.BlockSpec((tm, 1), row),
                  _const_spec((1, LANES)), _const_spec((d, c))],
        out_specs=[rows(widths[0]), rows(widths[1]), rows(widths[2]), rows(widths[3]),
                   pl.BlockSpec((tm // tkv, widths[4], tkv), lambda i: (i, 0, 0))] + [rows(LANES)] * n_slab,
        out_shape=[out(widths[0], F32), out(widths[1], BF16), out(widths[2], BF16), out(widths[3], BF16),
                   jax.ShapeDtypeStruct((n // tkv, widths[4], tkv), BF16)] + [out(LANES, F32)] * n_slab,
        compiler_params=_cparams("parallel"),
        name="inproj",
    )(x2d, g, posf, invf, w)


def _rglru_kernel(x_ref, g_ref, pos_ref, cw_ref, cb_ref, w_ref, b_ref, lam_ref, o_ref, halo_ref, h_ref):
    j = pl.program_id(1)

    @pl.when(j == 0)
    def _():
        halo_ref[...] = jnp.zeros_like(halo_ref)
        h_ref[...] = jnp.zeros_like(h_ref)

    x = x_ref[0].astype(F32)
    t, r = x.shape
    halo = halo_ref[...]
    row8 = lax.broadcasted_iota(jnp.int32, (SUBLANES, r), 0)
    cw = cw_ref[...]
    xc = cb_ref[...] + cw[0:1] * x
    for k in range(1, cw.shape[0]):
        rolled = pltpu.roll(x, k, 0)
        first = jnp.where(row8 < k, pltpu.roll(halo, k, 0), rolled[:SUBLANES])
        xc = xc + cw[k:k + 1] * jnp.concatenate([first, rolled[SUBLANES:]], axis=0)
    halo_ref[...] = x[t - SUBLANES:]

    gates = jnp.dot(xc.astype(BF16), w_ref[...], preferred_element_type=F32) + b_ref[...]
    rg = _sigmoid(gates[:, :r])
    ig = _sigmoid(gates[:, r:])
    z = -lam_ref[...]
    softplus = jnp.maximum(z, 0.0) + jnp.log(1.0 + jnp.exp(-jnp.abs(z)))
    log_a = (-RGLRU_C) * rg * softplus
    a = jnp.exp(log_a)
    mult = jnp.sqrt(1.0 - jnp.exp(2.0 * log_a))
    reset = pos_ref[0] == 0.0
    a = jnp.where(reset, 0.0, a)
    mult = jnp.where(reset, 1.0, mult)
    b = mult * ig * xc

    rows = lax.broadcasted_iota(jnp.int32, (t, r), 0)
    d = 1
    while d < t:
        keep = rows >= d
        a_sh = jnp.where(keep, pltpu.roll(a, d, 0), 1.0)
        b_sh = jnp.where(keep, pltpu.roll(b, d, 0), 0.0)
        b = b + a * b_sh
        a = a * a_sh
        d *= 2
    h = b + a * h_ref[...]
    h_ref[...] = h[t - 1:t]
    o_ref[0] = (h * _gelu_tanh(g_ref[0].astype(F32))).astype(o_ref.dtype)


def _rglru(x_rnn, g_rnn, posf, conv_w, conv_b, w_gates, b_gates, lam, tt):
    bsz, seq, r = x_rnn.shape
    kw = conv_w.shape[0]
    blk = lambda b, j: (b, j, 0)
    return pl.pallas_call(
        _rglru_kernel,
        grid=(bsz, seq // tt),
        in_specs=[pl.BlockSpec((1, tt, r), blk), pl.BlockSpec((1, tt, r), blk), pl.BlockSpec((1, tt, 1), blk),
                  _const_spec((kw, r)), _const_spec((1, r)), _const_spec((r, 2 * r)), _const_spec((1, 2 * r)),
                  _const_spec((1, r))],
        out_specs=pl.BlockSpec((1, tt, r), blk),
        out_shape=jax.ShapeDtypeStruct((bsz, seq, r), BF16),
        scratch_shapes=[pltpu.VMEM((SUBLANES, r), F32), pltpu.VMEM((1, r), F32)],
        compiler_params=_cparams("parallel", "arbitrary"),
        name="rglru",
    )(x_rnn, g_rnn, posf, conv_w, conv_b, w_gates, b_gates, lam)


def _attn_kernel(q_ref, k_ref, vt_ref, lamv_ref, sg_ref, o_ref, *, tq, head_dim, lambda_init):
    i = pl.program_id(2)
    hw = 2 * head_dim
    hp = q_ref.shape[2] // hw
    vdim = vt_ref.shape[1] // hp
    lane = lax.broadcasted_iota(jnp.int32, (tq, hw), 1)
    zero = jnp.zeros((tq, hw), q_ref.dtype)
    ones = jnp.ones((2 * SUBLANES, tq), BF16)
    qqs = []
    for a in range(hp):
        q = q_ref[0, :, a * hw:(a + 1) * hw]
        qqs.append(jnp.concatenate([jnp.where(lane < head_dim, q, zero), jnp.where(lane >= head_dim, q, zero)],
                                   axis=0))

    def step(j, carry, masked):
        row0 = pl.multiple_of(j * tq, tq)
        out = []
        for a in range(hp):
            m, acc = carry[2 * a], carry[2 * a + 1]
            kb = k_ref[0, pl.ds(row0, tq), a * hw:(a + 1) * hw]
            s = lax.dot_general(kb, qqs[a], (((1,), (1,)), ((), ())), preferred_element_type=F32)
            if masked:
                key = lax.broadcasted_iota(jnp.int32, s.shape, 0)
                qry = lax.broadcasted_iota(jnp.int32, s.shape, 1)
                qry = jnp.where(qry >= tq, qry - tq, qry)
                s = jnp.where(key <= qry, s, NEG_INF)
            m_new = jnp.maximum(m, jnp.max(s, axis=0, keepdims=True))
            p = jnp.exp2(s - m_new)
            alpha = jnp.exp2(m - m_new)
            vt = jnp.concatenate([vt_ref[j, a * vdim:(a + 1) * vdim, :], ones], axis=0)
            out += [m_new, alpha * acc + jnp.dot(vt, p.astype(BF16), preferred_element_type=F32)]
        return tuple(out)

    init = (jnp.full((1, 2 * tq), NEG_INF, F32), jnp.zeros((vdim + 2 * SUBLANES, 2 * tq), F32)) * hp
    carry = lax.fori_loop(0, i, lambda j, c: step(j, c, False), init)
    carry = step(i, carry, True)

    lamv = lamv_ref[...]
    lam = (jnp.exp(jnp.sum(lamv[0:1] * lamv[1:2], axis=1, keepdims=True))
           - jnp.exp(jnp.sum(lamv[2:3] * lamv[3:4], axis=1, keepdims=True)) + lambda_init)
    for a in range(hp):
        acc = carry[2 * a + 1]
        ot = acc[:vdim] / acc[vdim:vdim + 1]
        o = jnp.transpose(ot[:, :tq] - lam * ot[:, tq:])
        o = _rmsnorm(o, sg_ref[...], SUBLN_EPS) * (1.0 - lambda_init)
        o_ref[0, :, a * vdim:(a + 1) * vdim] = o.astype(o_ref.dtype)


def _diff_attention(q, k, vt, lamv, subln_g, heads, head_dim, lambda_init, tq, hp):
    bsz, seq, _ = q.shape
    vdim = vt.shape[1] // heads
    nkv = seq // tq
    kern = functools.partial(_attn_kernel, tq=tq, head_dim=head_dim, lambda_init=lambda_init)
    return pl.pallas_call(
        kern,
        grid=(bsz, heads // hp, seq // tq),
        in_specs=[pl.BlockSpec((1, tq, hp * 2 * head_dim), lambda b, h, i: (b, i, h)),
                  pl.BlockSpec((1, seq, hp * 2 * head_dim), lambda b, h, i: (b, 0, h)),
                  pl.BlockSpec((nkv, hp * vdim, tq), lambda b, h, i: (b, h, 0)),
                  _const_spec(lamv.shape), _const_spec(subln_g.shape)],
        out_specs=pl.BlockSpec((1, tq, hp * vdim), lambda b, h, i: (b, i, h)),
        out_shape=jax.ShapeDtypeStruct((bsz, seq, heads * vdim), BF16),
        compiler_params=_cparams("parallel", "parallel", "arbitrary"),
        name="diff_attn",
    )(q, k, vt, lamv, subln_g)


def _s5_tables(lam_re, lam_im, b_re, b_im, c_re, c_im, d_skip, log_dt, n_steps):
    tc = S5_CHUNK
    g, n, p = b_re.shape
    lr = lam_re.astype(F32)
    li = lam_im.astype(F32)
    dt = jnp.exp(log_dt.astype(F32))[:, None]
    mag = jnp.exp(lr * dt)
    ar = mag * jnp.cos(li * dt)
    ai = mag * jnp.sin(li * dt)
    den = lr * lr + li * li
    cr = ((ar - 1.0) * lr + ai * li) / den
    ci = (ai * lr - (ar - 1.0) * li) / den
    bb_re = cr[..., None] * b_re - ci[..., None] * b_im
    bb_im = cr[..., None] * b_im + ci[..., None] * b_re
    hi = lax.Precision.HIGHEST

    def apow(e):
        e = jnp.asarray(e, F32)[None, None, :]
        m = jnp.exp(e * (lr * dt)[..., None])
        ph = e * (li * dt)[..., None]
        return m * jnp.cos(ph), m * jnp.sin(ph)

    lags = jnp.arange(tc)
    pw_re, pw_im = apow(lags)
    ab_re = pw_re[..., None] * bb_re[:, :, None, :] - pw_im[..., None] * bb_im[:, :, None, :]
    ab_im = pw_re[..., None] * bb_im[:, :, None, :] + pw_im[..., None] * bb_re[:, :, None, :]
    kl = (jnp.einsum('gon,gnji->gjoi', c_re, ab_re, precision=hi)
          - jnp.einsum('gon,gnji->gjoi', c_im, ab_im, precision=hi))
    lag = lags[:, None] - lags[None, :]
    ktoe = jnp.where((lag >= 0)[None, :, :, None, None], kl[:, jnp.clip(lag, 0, tc - 1)], 0.0)
    skip = (jnp.eye(tc, dtype=F32)[None, :, :, None, None] * jnp.eye(p, dtype=F32)[None, None, None]
            * d_skip[:, None, None, None, :])
    kt = (ktoe + skip).transpose(0, 1, 3, 2, 4).reshape(g, tc * p, tc * p)
    bt = jnp.concatenate([ab_re[:, :, ::-1, :].reshape(g, n, tc * p), ab_im[:, :, ::-1, :].reshape(g, n, tc * p)],
                         axis=1)
    p1_re, p1_im = apow(lags + 1)
    ca_re = jnp.einsum('gon,gnt->gton', c_re, p1_re) - jnp.einsum('gon,gnt->gton', c_im, p1_im)
    ca_im = jnp.einsum('gon,gnt->gton', c_re, p1_im) + jnp.einsum('gon,gnt->gton', c_im, p1_re)
    ct = jnp.concatenate([ca_re.reshape(g, tc * p, n), -ca_im.reshape(g, tc * p, n)], axis=2)
    st_re, st_im = apow(tc * (2 ** jnp.arange(n_steps)))
    tab = jnp.stack([st_re.transpose(0, 2, 1), st_im.transpose(0, 2, 1)], axis=2)
    tab = tab.reshape(g // 2, 2, 2 * n_steps, n).transpose(0, 2, 1, 3).reshape(g // 2, 2 * n_steps, 2 * n)
    tab = jnp.pad(tab, ((0, 0), (0, -(2 * n_steps) % SUBLANES), (0, 0)))
    return kt.astype(BF16), bt.astype(BF16), ct.astype(BF16), tab.astype(F32)


def _s5_kernel(*refs, n_groups, p, n_slab):
    u_refs = refs[:n_slab]
    kt_ref, bt_ref, ct_ref, tab_ref = refs[n_slab:n_slab + 4]
    o_refs = refs[n_slab + 4:2 * n_slab + 4]
    ut_ref, y_ref, carry_ref = refs[2 * n_slab + 4:]
    tc = S5_CHUNK
    c = u_refs[0].shape[1] // tc
    n2 = bt_ref.shape[1]
    half = n2 // 2
    n_steps = int(math.log2(c))

    @pl.when(pl.program_id(1) == 0)
    def _():
        carry_ref[...] = jnp.zeros_like(carry_ref)

    for k in range(tc):
        for a in range(n_slab):
            ut_ref[k, a * LANES:(a + 1) * LANES, :] = jnp.transpose(
                u_refs[a][0, pl.ds(k, c, stride=tc), :]).astype(BF16)

    row = lax.broadcasted_iota(jnp.int32, (c, n2), 0)

    def shift(x, d, fill):
        return jnp.where(row >= d, pltpu.roll(x, d, 0), fill)

    def pair(gp, _):
        gs = (2 * gp, 2 * gp + 1)
        local = []
        for g in gs:
            ug = jnp.concatenate([ut_ref[k, pl.ds(pl.multiple_of(g * p, p), p), :] for k in range(tc)], axis=0)
            y_ref[g] = jnp.dot(kt_ref[g], ug, preferred_element_type=F32)
            local.append(jnp.dot(bt_ref[g], ug, preferred_element_type=F32))
        sr = jnp.transpose(jnp.concatenate([local[0][:half], local[1][:half]], axis=0))
        si = jnp.transpose(jnp.concatenate([local[0][half:], local[1][half:]], axis=0))
        tab = tab_ref[gp]
        cin_r = carry_ref[gp, 0:1, :]
        cin_i = carry_ref[gp, 1:2, :]
        sr = sr + jnp.where(row == 0, tab[0:1] * cin_r - tab[1:2] * cin_i, 0.0)
        si = si + jnp.where(row == 0, tab[0:1] * cin_i + tab[1:2] * cin_r, 0.0)
        for s in range(n_steps):
            d = 1 << s
            ar, ai = tab[2 * s:2 * s + 1], tab[2 * s + 1:2 * s + 2]
            hr, hi = shift(sr, d, 0.0), shift(si, d, 0.0)
            sr, si = sr + ar * hr - ai * hi, si + ar * hi + ai * hr
        carry_ref[gp, 0:1, :] = sr[c - 1:c]
        carry_ref[gp, 1:2, :] = si[c - 1:c]
        pr = jnp.transpose(shift(sr, 1, cin_r))
        pi = jnp.transpose(shift(si, 1, cin_i))
        for idx, g in enumerate(gs):
            prev = jnp.concatenate([pr[idx * half:(idx + 1) * half], pi[idx * half:(idx + 1) * half]], axis=0)
            y_ref[g] = y_ref[g] + jnp.dot(ct_ref[g], prev.astype(BF16), preferred_element_type=F32)
        return 0

    lax.fori_loop(0, n_groups // 2, pair, 0, unroll=2)

    groups_per_slab = LANES // p
    for t in range(tc):
        for a in range(n_slab):
            yt = jnp.concatenate([y_ref[g, t * p:(t + 1) * p, :]
                                  for g in range(a * groups_per_slab, (a + 1) * groups_per_slab)], axis=0)
            o_refs[a][0, pl.ds(t, c, stride=tc), :] = jnp.transpose(yt)


def _s5(u_slabs, tables, c_lanes):
    kt, bt, ct, tab = tables
    n_slab = len(u_slabs)
    bsz, seq, _ = u_slabs[0].shape
    width = n_slab * LANES
    tc = S5_CHUNK
    n_groups, n2 = bt.shape[0], bt.shape[1]
    p = width // n_groups
    rows = c_lanes * tc
    kern = functools.partial(_s5_kernel, n_groups=n_groups, p=p, n_slab=n_slab)
    slab = pl.BlockSpec((1, rows, LANES), lambda b, j: (b, j, 0))
    return pl.pallas_call(
        kern,
        grid=(bsz, seq // rows),
        in_specs=[slab] * n_slab + [_const_spec(kt.shape), _const_spec(bt.shape), _const_spec(ct.shape),
                                    _const_spec(tab.shape)],
        out_specs=[slab] * n_slab,
        out_shape=[jax.ShapeDtypeStruct((bsz, seq, LANES), F32)] * n_slab,
        scratch_shapes=[pltpu.VMEM((tc, width, c_lanes), BF16), pltpu.VMEM((n_groups, tc * p, c_lanes), F32),
                        pltpu.VMEM((n_groups // 2, SUBLANES, n2), F32)],
        compiler_params=_cparams("parallel", "arbitrary"),
        name="s5",
    )(*u_slabs, kt, bt, ct, tab)


def _merge_kernel(*refs, n_groups, n_experts, d_model, n_slab):
    x_ref, yr_ref, ya_ref = refs[:3]
    ys_refs = refs[3:3 + n_slab]
    (mg_ref, wg_ref, gw_ref, gb_ref, pr_ref, pa_ref, ps_ref, wo_ref, fg_ref, rw_ref, rb_ref,
     x1_ref, xn_ref, route_ref, cnt_ref, run_ref) = refs[3 + n_slab:]
    i = pl.program_id(0)

    @pl.when(i == 0)
    def _():
        run_ref[...] = jnp.zeros_like(run_ref)

    x = x_ref[...]
    tm = x.shape[0]
    xn = _rmsnorm(x, mg_ref[...], MIX_EPS)
    gates = _sigmoid(jnp.dot(xn.astype(BF16), wg_ref[...], preferred_element_type=F32))
    z = _gelu_tanh(jnp.concatenate([ref[...] for ref in ys_refs], axis=1))
    ys = z * _sigmoid(jnp.dot(z.astype(BF16), gw_ref[...], preferred_element_type=F32) + gb_ref[...])
    merged = (gates[:, :d_model] * jnp.dot(yr_ref[...], pr_ref[...], preferred_element_type=F32)
              + gates[:, d_model:2 * d_model] * jnp.dot(ya_ref[...], pa_ref[...], preferred_element_type=F32)
              + gates[:, 2 * d_model:] * jnp.dot(ys.astype(BF16), ps_ref[...], preferred_element_type=F32))
    x1 = x + jnp.dot(merged.astype(BF16), wo_ref[...], preferred_element_type=F32)
    x1_ref[...] = x1
    xn2 = _rmsnorm(x1, fg_ref[...], MIX_EPS)
    xn_ref[...] = xn2.astype(xn_ref.dtype)

    logits = jnp.dot(xn2, rw_ref[...], preferred_element_type=F32, precision=lax.Precision.HIGHEST) + rb_ref[...]
    lane = lax.broadcasted_iota(jnp.int32, logits.shape, 1).astype(F32)
    big = float(LANES)
    coarse = jnp.where(lane < n_groups, logits, NEG_INF)
    cmax = jnp.max(coarse, axis=1, keepdims=True)
    gsel = jnp.min(jnp.where(coarse == cmax, lane, big), axis=1, keepdims=True)
    p_sel = 1.0 / jnp.sum(jnp.where(lane < n_groups, jnp.exp(logits - cmax), 0.0), axis=1, keepdims=True)
    lo = n_groups + gsel * n_experts
    fine = jnp.where((lane >= lo) & (lane < lo + n_experts), logits, NEG_INF)
    m1 = jnp.max(fine, axis=1, keepdims=True)
    i1 = jnp.min(jnp.where(fine == m1, lane, big), axis=1, keepdims=True)
    fine2 = jnp.where(lane == i1, NEG_INF, fine)
    m2 = jnp.max(fine2, axis=1, keepdims=True)
    i2 = jnp.min(jnp.where(fine2 == m2, lane, big), axis=1, keepdims=True)
    e21 = jnp.exp(m2 - m1)
    w1 = p_sel / (1.0 + e21)
    w2 = p_sel * e21 / (1.0 + e21)
    oh1 = lane == i1
    oh2 = lane == i2
    onehot = jnp.where(oh1 | oh2, 1.0, 0.0)
    r_i = lax.broadcasted_iota(jnp.int32, (tm, tm), 0)
    c_i = lax.broadcasted_iota(jnp.int32, (tm, tm), 1)
    earlier = jnp.where(c_i < r_i, 1.0, 0.0).astype(BF16)
    before = jnp.dot(earlier, onehot.astype(BF16), preferred_element_type=F32) + run_ref[...]
    rank1 = jnp.sum(jnp.where(oh1, before, 0.0), axis=1, keepdims=True)
    rank2 = jnp.sum(jnp.where(oh2, before, 0.0), axis=1, keepdims=True)
    run_ref[...] = run_ref[...] + jnp.sum(onehot, axis=0, keepdims=True)
    cnt_ref[...] = run_ref[...]
    e1 = i1 - n_groups
    e2 = i2 - n_groups
    route = jnp.where(lane == 0, e1, 0.0)
    route = jnp.where(lane == 1, e2, route)
    route = jnp.where(lane == 2, w1, route)
    route = jnp.where(lane == 3, w2, route)
    route = jnp.where(lane == 4, rank1, route)
    route = jnp.where(lane == 5, rank2, route)
    route_ref[...] = route


def _merge(x2d, y_rnn, y_attn, y_s5_slabs, mix_g, w_gate, glu_w, glu_b, p_rnn, p_attn, p_ssm, w_out, ffn_g, rw, rb,
           n_groups, n_experts, tm):
    n, d = x2d.shape
    row = lambda i: (i, 0)
    n_slab = len(y_s5_slabs)
    kern = functools.partial(_merge_kernel, n_groups=n_groups, n_experts=n_experts, d_model=d, n_slab=n_slab)
    consts = [mix_g, w_gate, glu_w, glu_b, p_rnn, p_attn, p_ssm, w_out, ffn_g, rw, rb]
    return pl.pallas_call(
        kern,
        grid=(n // tm,),
        in_specs=[pl.BlockSpec((tm, d), row), pl.BlockSpec((tm, y_rnn.shape[1]), row),
                  pl.BlockSpec((tm, y_attn.shape[1]), row)] + [pl.BlockSpec((tm, LANES), row)] * n_slab
                 + [_const_spec(a.shape) for a in consts],
        out_specs=[pl.BlockSpec((tm, d), row), pl.BlockSpec((tm, d), row), pl.BlockSpec((tm, LANES), row),
                   pl.BlockSpec((1, LANES), lambda i: (0, 0))],
        out_shape=[jax.ShapeDtypeStruct((n, d), F32), jax.ShapeDtypeStruct((n, d), F32),
                   jax.ShapeDtypeStruct((n, LANES), F32), jax.ShapeDtypeStruct((1, LANES), F32)],
        scratch_shapes=[pltpu.VMEM((1, LANES), F32)],
        compiler_params=_cparams("arbitrary"),
        name="merge_router",
    )(x2d, y_rnn, y_attn, *y_s5_slabs, *consts)


def _dispatch_kernel(dest_ref, x_ref, xs_in_ref, xs_ref, sem):
    del xs_in_ref
    tm = x_ref.shape[0]

    def copy(r, slot):
        return pltpu.make_async_copy(x_ref.at[pl.ds(r, 1)], xs_ref.at[pl.ds(dest_ref[TOP_K_FINE * r + slot], 1)], sem)

    def start(r, _):
        for slot in range(TOP_K_FINE):
            copy(r, slot).start(priority=slot % 2)
        return 0

    def wait(r, _):
        for slot in range(TOP_K_FINE):
            copy(r, slot).wait()
        return 0

    lax.fori_loop(0, tm, start, 0)
    lax.fori_loop(0, tm, wait, 0)


def _dispatch(dest_flat, xn2, n_rows, tm):
    n, d = xn2.shape
    xs0 = jnp.zeros((n_rows, d), xn2.dtype)
    return pl.pallas_call(
        _dispatch_kernel,
        grid=(n // tm,),
        in_specs=[pl.BlockSpec((TOP_K_FINE * tm,), lambda i: (i,), memory_space=pltpu.SMEM),
                  pl.BlockSpec((tm, d), lambda i: (i, 0)),
                  pl.BlockSpec(memory_space=pl.ANY)],
        out_specs=pl.BlockSpec(memory_space=pl.ANY),
        out_shape=jax.ShapeDtypeStruct((n_rows, d), xn2.dtype),
        scratch_shapes=[pltpu.SemaphoreType.DMA(())],
        input_output_aliases={2: 0},
        compiler_params=_cparams("arbitrary"),
        name="moe_dispatch",
    )(dest_flat, xn2, xs0)


def _experts_kernel(te_ref, nact_ref, x_ref, w1_ref, w3_ref, w2_ref, o_ref, w1b_ref, w3b_ref, w2b_ref):
    i = pl.program_id(0)
    fresh = jnp.logical_or(i == 0, te_ref[i] != te_ref[jnp.maximum(i - 1, 0)])

    @pl.when(jnp.logical_and(i < nact_ref[0], fresh))
    def _():
        w1b_ref[...] = w1_ref[0].astype(BF16)
        w3b_ref[...] = w3_ref[0].astype(BF16)
        w2b_ref[...] = w2_ref[0].astype(BF16)

    @pl.when(i < nact_ref[0])
    def _():
        xb = x_ref[...].astype(BF16)
        h1 = jnp.dot(xb, w1b_ref[...], preferred_element_type=F32)
        h3 = jnp.dot(xb, w3b_ref[...], preferred_element_type=F32)
        hid = h1 * _sigmoid(h1) * h3
        o_ref[...] = jnp.dot(hid.astype(BF16), w2b_ref[...], preferred_element_type=F32).astype(o_ref.dtype)

    @pl.when(i >= nact_ref[0])
    def _():
        o_ref[...] = jnp.zeros_like(o_ref)


def _experts(tile_expert, n_active, xs, w1, w3, w2, tm):
    n_rows, d = xs.shape
    f = w1.shape[2]
    row = lambda i, te, na: (jnp.minimum(i, na[0] - 1), 0)
    wsel = lambda i, te, na: (te[i], 0, 0)
    grid_spec = pltpu.PrefetchScalarGridSpec(
        num_scalar_prefetch=2,
        grid=(n_rows // tm,),
        in_specs=[pl.BlockSpec((tm, d), row), pl.BlockSpec((1, d, f), wsel), pl.BlockSpec((1, d, f), wsel),
                  pl.BlockSpec((1, f, d), wsel)],
        out_specs=pl.BlockSpec((tm, d), lambda i, te, na: (i, 0)),
        scratch_shapes=[pltpu.VMEM((d, f), BF16), pltpu.VMEM((d, f), BF16), pltpu.VMEM((f, d), BF16)],
    )
    return pl.pallas_call(
        _experts_kernel,
        grid_spec=grid_spec,
        out_shape=jax.ShapeDtypeStruct((n_rows, d), F32),
        compiler_params=_cparams("arbitrary"),
        name="moe_experts",
    )(tile_expert, n_active, xs, w1, w3, w2)


def _combine_kernel(dest_ref, x_ref, route_ref, fg_ref, ys_ref, o_ref, buf_ref, sem, *, final_norm):
    tm = x_ref.shape[0]

    def copy(r, slot):
        return pltpu.make_async_copy(ys_ref.at[pl.ds(dest_ref[TOP_K_FINE * r + slot], 1)],
                                     buf_ref.at[slot, pl.ds(r, 1)], sem)

    def start(r, _):
        for slot in range(TOP_K_FINE):
            copy(r, slot).start(priority=slot % 2)
        return 0

    def wait(r, _):
        for slot in range(TOP_K_FINE):
            copy(r, slot).wait()
        return 0

    lax.fori_loop(0, tm, start, 0)
    lax.fori_loop(0, tm, wait, 0)
    route = route_ref[...]
    out = x_ref[...] + route[:, 2:3] * buf_ref[0] + route[:, 3:4] * buf_ref[1]
    if final_norm:
        out = _rmsnorm(out, fg_ref[...], MIX_EPS)
    o_ref[...] = out


def _combine(dest_flat, x1, route, final_g, ys, tm, final_norm):
    n, d = x1.shape
    kern = functools.partial(_combine_kernel, final_norm=final_norm)
    return pl.pallas_call(
        kern,
        grid=(n // tm,),
        in_specs=[pl.BlockSpec((TOP_K_FINE * tm,), lambda i: (i,), memory_space=pltpu.SMEM),
                  pl.BlockSpec((tm, d), lambda i: (i, 0)),
                  pl.BlockSpec((tm, LANES), lambda i: (i, 0)),
                  _const_spec((1, d)),
                  pl.BlockSpec(memory_space=pl.ANY)],
        out_specs=pl.BlockSpec((tm, d), lambda i: (i, 0)),
        out_shape=jax.ShapeDtypeStruct((n, d), F32),
        scratch_shapes=[pltpu.VMEM((TOP_K_FINE, tm, d), F32), pltpu.SemaphoreType.DMA(())],
        compiler_params=_cparams("arbitrary"),
        name="moe_combine",
    )(dest_flat, x1, route, final_g, ys)


def _tile_plan(n, seq):
    return dict(inproj=min(512, n), rglru=min(256, seq), attn=min(512, seq), attn_heads=2,
                s5_lanes=min(LANES, seq // S5_CHUNK), merge=min(512, n), moe=min(256, n))


def kernel(x, positions, mix_norm_g, w_in, conv_w, conv_b, rg_wa, rg_ba, rg_wx, rg_bx, rg_lambda,
           lam_q1, lam_k1, lam_q2, lam_k2, subln_g,
           ssm_lambda_re, ssm_lambda_im, ssm_b_re, ssm_b_im, ssm_c_re, ssm_c_im, ssm_d, ssm_log_dt,
           ssm_glu_w, ssm_glu_b, proj_rnn, proj_attn, proj_ssm, w_out,
           ffn_norm_g, router_coarse_w, router_coarse_b, router_fine_w, router_fine_b,
           expert_w1, expert_w3, expert_w2, final_norm_g):
    return _forward(_tile_plan(x.shape[0] * x.shape[1], x.shape[1]),
                    x, positions, mix_norm_g, w_in, conv_w, conv_b, rg_wa, rg_ba, rg_wx, rg_bx, rg_lambda,
                    lam_q1, lam_k1, lam_q2, lam_k2, subln_g,
                    ssm_lambda_re, ssm_lambda_im, ssm_b_re, ssm_b_im, ssm_c_re, ssm_c_im, ssm_d, ssm_log_dt,
                    ssm_glu_w, ssm_glu_b, proj_rnn, proj_attn, proj_ssm, w_out,
                    ffn_norm_g, router_coarse_w, router_coarse_b, router_fine_w, router_fine_b,
                    expert_w1, expert_w3, expert_w2, final_norm_g)


def _forward(tiles, x, positions, mix_norm_g, w_in, conv_w, conv_b, rg_wa, rg_ba, rg_wx, rg_bx, rg_lambda,
             lam_q1, lam_k1, lam_q2, lam_k2, subln_g,
             ssm_lambda_re, ssm_lambda_im, ssm_b_re, ssm_b_im, ssm_c_re, ssm_c_im, ssm_d, ssm_log_dt,
             ssm_glu_w, ssm_glu_b, proj_rnn, proj_attn, proj_ssm, w_out,
             ffn_norm_g, router_coarse_w, router_coarse_b, router_fine_w, router_fine_b,
             expert_w1, expert_w3, expert_w2, final_norm_g):
    bsz, seq, d_model = x.shape
    depth = w_in.shape[0]
    n = bsz * seq
    r = conv_w.shape[2]
    sw = ssm_glu_w.shape[1]
    vdim = subln_g.shape[1]
    head_dim = vdim // 2
    in_cols = w_in.shape[2]
    qk = (in_cols - 2 * r - sw - 3 * d_model) // 3
    heads = qk // (2 * head_dim)
    splits = (r, 2 * r, 2 * r + qk, 2 * r + 2 * qk, 2 * r + 3 * qk, 2 * r + 3 * qk + sw)
    mix_cols = splits[-1]
    n_groups = router_coarse_w.shape[2]
    n_experts = expert_w1.shape[2]
    n_total = n_groups * n_experts
    rnn_blocks = rg_wa.shape[1]

    tm_in, tt_rnn, tq = tiles["inproj"], tiles["rglru"], tiles["attn"]
    s5_lanes, tm_merge, tm_moe = tiles["s5_lanes"], tiles["merge"], tiles["moe"]
    n_rows = TOP_K_FINE * n + n_total * tm_moe

    posf = positions.astype(F32)
    pos_col = posf.reshape(n, 1)
    pos_blk = posf.reshape(bsz, seq, 1)
    inv_freq = ROPE_THETA ** (-jnp.arange(0, head_dim, 2, dtype=F32) / head_dim)
    invf = jnp.tile(inv_freq, LANES // (head_dim // 2)).reshape(1, LANES)
    eye_blocks = jnp.eye(rnn_blocks, dtype=F32)

    x2d = x.reshape(n, d_model)
    for l in range(depth):
        lambda_init = 0.8 - 0.6 * math.exp(-0.3 * l)
        w_mix = w_in[l, :, :mix_cols].astype(BF16)
        w_gate = w_in[l, :, mix_cols:].astype(BF16)
        x_rnn, g_rnn, q, k, vt, *u_slabs = _inproj(x2d, mix_norm_g[l].reshape(1, d_model), pos_col, invf, w_mix,
                                                   splits, head_dim, tm_in, tq)

        def block_diag(w):
            return jnp.einsum('hij,hk->hikj', w, eye_blocks).reshape(r, r)

        w_gates = jnp.concatenate([block_diag(rg_wa[l]), block_diag(rg_wx[l])], axis=1).astype(BF16)
        b_gates = jnp.concatenate([rg_ba[l], rg_bx[l]]).reshape(1, 2 * r)
        y_rnn = _rglru(x_rnn.reshape(bsz, seq, r), g_rnn.reshape(bsz, seq, r), pos_blk, conv_w[l],
                       conv_b[l].reshape(1, r), w_gates, b_gates, rg_lambda[l].reshape(1, r), tt_rnn)

        lamv = jnp.stack([lam_q1[l], lam_k1[l], lam_q2[l], lam_k2[l]])
        y_attn = _diff_attention(q.reshape(bsz, seq, qk), k.reshape(bsz, seq, qk), vt,
                                 lamv, subln_g[l].reshape(1, vdim), heads, head_dim, lambda_init, tq,
                                 min(tiles["attn_heads"], heads))

        tables = _s5_tables(ssm_lambda_re[l], ssm_lambda_im[l], ssm_b_re[l], ssm_b_im[l], ssm_c_re[l],
                            ssm_c_im[l], ssm_d[l], ssm_log_dt[l], int(math.log2(s5_lanes)))
        y_s5 = _s5([u.reshape(bsz, seq, LANES) for u in u_slabs], tables, s5_lanes)

        rw = jnp.concatenate([router_coarse_w[l], router_fine_w[l]], axis=1)
        rw = jnp.pad(rw, ((0, 0), (0, LANES - rw.shape[1])))
        rb = jnp.concatenate([router_coarse_b[l], router_fine_b[l]])
        rb = jnp.pad(rb, (0, LANES - rb.shape[0])).reshape(1, LANES)
        x1, xn2, route, counts = _merge(
            x2d, y_rnn.reshape(n, r), y_attn.reshape(n, qk), [y.reshape(n, LANES) for y in y_s5],
            mix_norm_g[l].reshape(1, d_model), w_gate, ssm_glu_w[l].astype(BF16), ssm_glu_b[l].reshape(1, sw),
            proj_rnn[l].astype(BF16), proj_attn[l].astype(BF16), proj_ssm[l].astype(BF16), w_out[l].astype(BF16),
            ffn_norm_g[l].reshape(1, d_model), rw, rb, n_groups, n_experts, tm_merge)

        cnt = counts[0, n_groups:n_groups + n_total].astype(jnp.int32)
        n_tiles = (cnt + tm_moe - 1) // tm_moe
        tile_end = jnp.cumsum(n_tiles)
        offsets = (tile_end - n_tiles) * tm_moe
        eid = route[:, 0:TOP_K_FINE].astype(jnp.int32)
        rank = route[:, 4:4 + TOP_K_FINE].astype(jnp.int32)
        dest = (offsets[eid] + rank).reshape(TOP_K_FINE * n)
        n_active = tile_end[-1:]
        tile_ids = jnp.minimum(jnp.arange(n_rows // tm_moe, dtype=jnp.int32), n_active[0] - 1)
        tile_expert = jnp.sum((tile_ids[:, None] >= tile_end[None, :]).astype(jnp.int32), axis=1)

        xs = _dispatch(dest, xn2, n_rows, tm_moe)
        ys = _experts(tile_expert + l * n_total, n_active.astype(jnp.int32), xs,
                      expert_w1.reshape(depth * n_total, d_model, -1), expert_w3.reshape(depth * n_total, d_model, -1),
                      expert_w2.reshape(depth * n_total, -1, d_model), tm_moe)
        x2d = _combine(dest, x1, route, final_norm_g.reshape(1, d_model), ys, tm_moe, l == depth - 1)
    return x2d.reshape(bsz, seq, d_model)
```

```python
import functools
import math

import jax
import jax.numpy as jnp
from jax import lax
from jax.experimental import pallas as pl
from jax.experimental.pallas import tpu as pltpu

F32 = jnp.float32
BF16 = jnp.bfloat16

RGLRU_C = 8.0
ROPE_THETA = 10000.0
TOP_K_FINE = 2
NEG_INF = -1e30
MIX_EPS = 1e-6
SUBLN_EPS = 1e-5

LANES = 128
SUBLANES = 8
VMEM_LIMIT_BYTES = 56 * 1024 * 1024

S5_CHUNK = 16


def _cparams(*sem):
    return pltpu.CompilerParams(dimension_semantics=sem, vmem_limit_bytes=VMEM_LIMIT_BYTES)


def _const_spec(shape):
    nd = len(shape)
    return pl.BlockSpec(shape, lambda *_: (0,) * nd, pipeline_mode=pl.Buffered(1))


def _gelu_tanh(x):
    return 0.5 * x * (1.0 + jnp.tanh(math.sqrt(2.0 / math.pi) * (x + 0.044715 * (x * x * x))))


def _sigmoid(x):
    return 1.0 / (1.0 + jnp.exp(-x))


def _rmsnorm(x, g, eps):
    return x * lax.rsqrt(jnp.mean(x * x, axis=-1, keepdims=True) + eps) * g


def _inproj_kernel(x_ref, g_ref, pos_ref, invf_ref, w_ref, xr_ref, gr_ref, q_ref, k_ref, vt_ref, *u_refs,
                   splits, head_dim, q_scale):
    x = x_ref[...]
    xn = _rmsnorm(x, g_ref[...], MIX_EPS)
    h = jnp.dot(xn.astype(BF16), w_ref[...], preferred_element_type=F32)
    s0, s1, s2, s3, s4, s5 = splits
    xr_ref[...] = h[:, :s0].astype(xr_ref.dtype)
    gr_ref[...] = h[:, s0:s1].astype(gr_ref.dtype)
    for a, u_ref in enumerate(u_refs):
        u_ref[...] = h[:, s4 + a * LANES:s4 + (a + 1) * LANES]
    tkv = vt_ref.shape[2]
    for c in range(vt_ref.shape[0]):
        vt_ref[c] = jnp.transpose(h[c * tkv:(c + 1) * tkv, s3:s4]).astype(vt_ref.dtype)

    ang = pos_ref[...] * invf_ref[...]
    cos = jnp.cos(ang)
    sin = jnp.sin(ang)
    lane = lax.broadcasted_iota(jnp.int32, ang.shape, 1)
    first_half = (lane % head_dim) < (head_dim // 2)
    sin_signed = jnp.where(first_half, -sin, sin)

    def rope(t, scale):
        outs = []
        for a in range(t.shape[1] // LANES):
            xs = t[:, a * LANES:(a + 1) * LANES]
            fwd = pltpu.roll(xs, LANES - head_dim // 2, 1)
            bwd = pltpu.roll(xs, head_dim // 2, 1)
            rot = jnp.where(first_half, fwd, bwd)
            outs.append((xs * cos + rot * sin_signed) * scale)
        return jnp.concatenate(outs, axis=1)

    q_ref[...] = rope(h[:, s1:s2], q_scale).astype(q_ref.dtype)
    k_ref[...] = rope(h[:, s2:s3], 1.0).astype(k_ref.dtype)


def _inproj(x2d, g, posf, invf, w, splits, head_dim, tm, tkv):
    n, d = x2d.shape
    c = w.shape[1]
    widths = [splits[0]] + [splits[i] - splits[i - 1] for i in range(1, 6)]
    n_slab = widths[5] // LANES
    kern = functools.partial(_inproj_kernel, splits=splits, head_dim=head_dim,
                             q_scale=head_dim ** -0.5 * math.log2(math.e))
    row = lambda i: (i, 0)
    rows = lambda wd: pl.BlockSpec((tm, wd), row)
    out = lambda wd, dt: jax.ShapeDtypeStruct((n, wd), dt)
    return pl.pallas_call(
        kern,
        grid=(n // tm,),
        in_specs=[pl.BlockSpec((tm, d), row), _const_spec((1, d)), pl.BlockSpec((tm, 1), row),
                  _const_spec((1, LANES)), _const_spec((d, c))],
        out_specs=[rows(widths[0]), rows(widths[1]), rows(widths[2]), rows(widths[3]),
                   pl.BlockSpec((tm // tkv, widths[4], tkv), lambda i: (i, 0, 0))] + [rows(LANES)] * n_slab,
        out_shape=[out(widths[0], F32), out(widths[1], BF16), out(widths[2], BF16), out(widths[3], BF16),
                   jax.ShapeDtypeStruct((n // tkv, widths[4], tkv), BF16)] + [out(LANES, F32)] * n_slab,
        compiler_params=_cparams("parallel"),
        name="inproj",
    )(x2d, g, posf, invf, w)


def _rglru_kernel(x_ref, g_ref, pos_ref, cw_ref, cb_ref, w_ref, b_ref, lam_ref, o_ref, halo_ref, h_ref):
    j = pl.program_id(1)

    @pl.when(j == 0)
    def _():
        halo_ref[...] = jnp.zeros_like(halo_ref)
        h_ref[...] = jnp.zeros_like(h_ref)

    x = x_ref[0].astype(F32)
    t, r = x.shape
    halo = halo_ref[...]
    row8 = lax.broadcasted_iota(jnp.int32, (SUBLANES, r), 0)
    cw = cw_ref[...]
    xc = cb_ref[...] + cw[0:1] * x
    for k in range(1, cw.shape[0]):
        rolled = pltpu.roll(x, k, 0)
        first = jnp.where(row8 < k, pltpu.roll(halo, k, 0), rolled[:SUBLANES])
        xc = xc + cw[k:k + 1] * jnp.concatenate([first, rolled[SUBLANES:]], axis=0)
    halo_ref[...] = x[t - SUBLANES:]

    gates = jnp.dot(xc.astype(BF16), w_ref[...], preferred_element_type=F32) + b_ref[...]
    rg = _sigmoid(gates[:, :r])
    ig = _sigmoid(gates[:, r:])
    z = -lam_ref[...]
    softplus = jnp.maximum(z, 0.0) + jnp.log(1.0 + jnp.exp(-jnp.abs(z)))
    log_a = (-RGLRU_C) * rg * softplus
    a = jnp.exp(log_a)
    mult = jnp.sqrt(1.0 - jnp.exp(2.0 * log_a))
    reset = pos_ref[0] == 0.0
    a = jnp.where(reset, 0.0, a)
    mult = jnp.where(reset, 1.0, mult)
    b = mult * ig * xc

    rows = lax.broadcasted_iota(jnp.int32, (t, r), 0)
    d = 1
    while d < t:
        keep = rows >= d
        a_sh = jnp.where(keep, pltpu.roll(a, d, 0), 1.0)
        b_sh = jnp.where(keep, pltpu.roll(b, d, 0), 0.0)
        b = b + a * b_sh
        a = a * a_sh
        d *= 2
    h = b + a * h_ref[...]
    h_ref[...] = h[t - 1:t]
    o_ref[0] = (h * _gelu_tanh(g_ref[0].astype(F32))).astype(o_ref.dtype)


def _rglru(x_rnn, g_rnn, posf, conv_w, conv_b, w_gates, b_gates, lam, tt):
    bsz, seq, r = x_rnn.shape
    kw = conv_w.shape[0]
    blk = lambda b, j: (b, j, 0)
    return pl.pallas_call(
        _rglru_kernel,
        grid=(bsz, seq // tt),
        in_specs=[pl.BlockSpec((1, tt, r), blk), pl.BlockSpec((1, tt, r), blk), pl.BlockSpec((1, tt, 1), blk),
                  _const_spec((kw, r)), _const_spec((1, r)), _const_spec((r, 2 * r)), _const_spec((1, 2 * r)),
                  _const_spec((1, r))],
        out_specs=pl.BlockSpec((1, tt, r), blk),
        out_shape=jax.ShapeDtypeStruct((bsz, seq, r), BF16),
        scratch_shapes=[pltpu.VMEM((SUBLANES, r), F32), pltpu.VMEM((1, r), F32)],
        compiler_params=_cparams("parallel", "arbitrary"),
        name="rglru",
    )(x_rnn, g_rnn, posf, conv_w, conv_b, w_gates, b_gates, lam)


def _attn_kernel(q_ref, k_ref, vt_ref, lamv_ref, sg_ref, o_ref, *, tq, head_dim, lambda_init):
    i = pl.program_id(2)
    hw = 2 * head_dim
    hp = q_ref.shape[2] // hw
    vdim = vt_ref.shape[1] // hp
    lane = lax.broadcasted_iota(jnp.int32, (tq, hw), 1)
    zero = jnp.zeros((tq, hw), q_ref.dtype)
    ones = jnp.ones((2 * SUBLANES, tq), BF16)
    qqs = []
    for a in range(hp):
        q = q_ref[0, :, a * hw:(a + 1) * hw]
        qqs.append(jnp.concatenate([jnp.where(lane < head_dim, q, zero), jnp.where(lane >= head_dim, q, zero)],
                                   axis=0))

    def step(j, carry, masked):
        row0 = pl.multiple_of(j * tq, tq)
        out = []
        for a in range(hp):
            m, acc = carry[2 * a], carry[2 * a + 1]
            kb = k_ref[0, pl.ds(row0, tq), a * hw:(a + 1) * hw]
            s = lax.dot_general(kb, qqs[a], (((1,), (1,)), ((), ())), preferred_element_type=F32)
            if masked:
                key = lax.broadcasted_iota(jnp.int32, s.shape, 0)
                qry = lax.broadcasted_iota(jnp.int32, s.shape, 1)
                qry = jnp.where(qry >= tq, qry - tq, qry)
                s = jnp.where(key <= qry, s, NEG_INF)
            m_new = jnp.maximum(m, jnp.max(s, axis=0, keepdims=True))
            p = jnp.exp2(s - m_new)
            alpha = jnp.exp2(m - m_new)
            vt = jnp.concatenate([vt_ref[j, a * vdim:(a + 1) * vdim, :], ones], axis=0)
            out += [m_new, alpha * acc + jnp.dot(vt, p.astype(BF16), preferred_element_type=F32)]
        return tuple(out)

    init = (jnp.full((1, 2 * tq), NEG_INF, F32), jnp.zeros((vdim + 2 * SUBLANES, 2 * tq), F32)) * hp
    carry = lax.fori_loop(0, i, lambda j, c: step(j, c, False), init)
    carry = step(i, carry, True)

    lamv = lamv_ref[...]
    lam = (jnp.exp(jnp.sum(lamv[0:1] * lamv[1:2], axis=1, keepdims=True))
           - jnp.exp(jnp.sum(lamv[2:3] * lamv[3:4], axis=1, keepdims=True)) + lambda_init)
    for a in range(hp):
        acc = carry[2 * a + 1]
        ot = acc[:vdim] / acc[vdim:vdim + 1]
        o = jnp.transpose(ot[:, :tq] - lam * ot[:, tq:])
        o = _rmsnorm(o, sg_ref[...], SUBLN_EPS) * (1.0 - lambda_init)
        o_ref[0, :, a * vdim:(a + 1) * vdim] = o.astype(o_ref.dtype)


def _diff_attention(q, k, vt, lamv, subln_g, heads, head_dim, lambda_init, tq, hp):
    bsz, seq, _ = q.shape
    vdim = vt.shape[1] // heads
    nkv = seq // tq
    kern = functools.partial(_attn_kernel, tq=tq, head_dim=head_dim, lambda_init=lambda_init)
    return pl.pallas_call(
        kern,
        grid=(bsz, heads // hp, seq // tq),
        in_specs=[pl.BlockSpec((1, tq, hp * 2 * head_dim), lambda b, h, i: (b, i, h)),
                  pl.BlockSpec((1, seq, hp * 2 * head_dim), lambda b, h, i: (b, 0, h)),
                  pl.BlockSpec((nkv, hp * vdim, tq), lambda b, h, i: (b, h, 0)),
                  _const_spec(lamv.shape), _const_spec(subln_g.shape)],
        out_specs=pl.BlockSpec((1, tq, hp * vdim), lambda b, h, i: (b, i, h)),
        out_shape=jax.ShapeDtypeStruct((bsz, seq, heads * vdim), BF16),
        compiler_params=_cparams("parallel", "parallel", "arbitrary"),
        name="diff_attn",
    )(q, k, vt, lamv, subln_g)


def _s5_tables(lam_re, lam_im, b_re, b_im, c_re, c_im, d_skip, log_dt, n_steps):
    tc = S5_CHUNK
    g, n, p = b_re.shape
    lr = lam_re.astype(F32)
    li = lam_im.astype(F32)
    dt = jnp.exp(log_dt.astype(F32))[:, None]
    mag = jnp.exp(lr * dt)
    ar = mag * jnp.cos(li * dt)
    ai = mag * jnp.sin(li * dt)
    den = lr * lr + li * li
    cr = ((ar - 1.0) * lr + ai * li) / den
    ci = (ai * lr - (ar - 1.0) * li) / den
    bb_re = cr[..., None] * b_re - ci[..., None] * b_im
    bb_im = cr[..., None] * b_im + ci[..., None] * b_re
    hi = lax.Precision.HIGHEST

    def apow(e):
        e = jnp.asarray(e, F32)[None, None, :]
        m = jnp.exp(e * (lr * dt)[..., None])
        ph = e * (li * dt)[..., None]
        return m * jnp.cos(ph), m * jnp.sin(ph)

    lags = jnp.arange(tc)
    pw_re, pw_im = apow(lags)
    ab_re = pw_re[..., None] * bb_re[:, :, None, :] - pw_im[..., None] * bb_im[:, :, None, :]
    ab_im = pw_re[..., None] * bb_im[:, :, None, :] + pw_im[..., None] * bb_re[:, :, None, :]
    kl = (jnp.einsum('gon,gnji->gjoi', c_re, ab_re, precision=hi)
          - jnp.einsum('gon,gnji->gjoi', c_im, ab_im, precision=hi))
    lag = lags[:, None] - lags[None, :]
    ktoe = jnp.where((lag >= 0)[None, :, :, None, None], kl[:, jnp.clip(lag, 0, tc - 1)], 0.0)
    skip = (jnp.eye(tc, dtype=F32)[None, :, :, None, None] * jnp.eye(p, dtype=F32)[None, None, None]
            * d_skip[:, None, None, None, :])
    kt = (ktoe + skip).transpose(0, 1, 3, 2, 4).reshape(g, tc * p, tc * p)
    bt = jnp.concatenate([ab_re[:, :, ::-1, :].reshape(g, n, tc * p), ab_im[:, :, ::-1, :].reshape(g, n, tc * p)],
                         axis=1)
    p1_re, p1_im = apow(lags + 1)
    ca_re = jnp.einsum('gon,gnt->gton', c_re, p1_re) - jnp.einsum('gon,gnt->gton', c_im, p1_im)
    ca_im = jnp.einsum('gon,gnt->gton', c_re, p1_im) + jnp.einsum('gon,gnt->gton', c_im, p1_re)
    ct = jnp.concatenate([ca_re.reshape(g, tc * p, n), -ca_im.reshape(g, tc * p, n)], axis=2)
    st_re, st_im = apow(tc * (2 ** jnp.arange(n_steps)))
    tab = jnp.stack([st_re.transpose(0, 2, 1), st_im.transpose(0, 2, 1)], axis=2)
    tab = tab.reshape(g // 2, 2, 2 * n_steps, n).transpose(0, 2, 1, 3).reshape(g // 2, 2 * n_steps, 2 * n)
    tab = jnp.pad(tab, ((0, 0), (0, -(2 * n_steps) % SUBLANES), (0, 0)))
    return kt.astype(BF16), bt.astype(BF16), ct.astype(BF16), tab.astype(F32)


def _s5_kernel(*refs, n_groups, p, n_slab):
    u_refs = refs[:n_slab]
    kt_ref, bt_ref, ct_ref, tab_ref = refs[n_slab:n_slab + 4]
    o_refs = refs[n_slab + 4:2 * n_slab + 4]
    ut_ref, y_ref, carry_ref = refs[2 * n_slab + 4:]
    tc = S5_CHUNK
    c = u_refs[0].shape[1] // tc
    n2 = bt_ref.shape[1]
    half = n2 // 2
    n_steps = int(math.log2(c))

    @pl.when(pl.program_id(1) == 0)
    def _():
        carry_ref[...] = jnp.zeros_like(carry_ref)

    for k in range(tc):
        for a in range(n_slab):
            ut_ref[k, a * LANES:(a + 1) * LANES, :] = jnp.transpose(
                u_refs[a][0, pl.ds(k, c, stride=tc), :]).astype(BF16)

    row = lax.broadcasted_iota(jnp.int32, (c, n2), 0)

    def shift(x, d, fill):
        return jnp.where(row >= d, pltpu.roll(x, d, 0), fill)

    def pair(gp, _):
        gs = (2 * gp, 2 * gp + 1)
        local = []
        for g in gs:
            ug = jnp.concatenate([ut_ref[k, pl.ds(pl.multiple_of(g * p, p), p), :] for k in range(tc)], axis=0)
            y_ref[g] = jnp.dot(kt_ref[g], ug, preferred_element_type=F32)
            local.append(jnp.dot(bt_ref[g], ug, preferred_element_type=F32))
        sr = jnp.transpose(jnp.concatenate([local[0][:half], local[1][:half]], axis=0))
        si = jnp.transpose(jnp.concatenate([local[0][half:], local[1][half:]], axis=0))
        tab = tab_ref[gp]
        cin_r = carry_ref[gp, 0:1, :]
        cin_i = carry_ref[gp, 1:2, :]
        sr = sr + jnp.where(row == 0, tab[0:1] * cin_r - tab[1:2] * cin_i, 0.0)
        si = si + jnp.where(row == 0, tab[0:1] * cin_i + tab[1:2] * cin_r, 0.0)
        for s in range(n_steps):
            d = 1 << s
            ar, ai = tab[2 * s:2 * s + 1], tab[2 * s + 1:2 * s + 2]
            hr, hi = shift(sr, d, 0.0), shift(si, d, 0.0)
            sr, si = sr + ar * hr - ai * hi, si + ar * hi + ai * hr
        carry_ref[gp, 0:1, :] = sr[c - 1:c]
        carry_ref[gp, 1:2, :] = si[c - 1:c]
        pr = jnp.transpose(shift(sr, 1, cin_r))
        pi = jnp.transpose(shift(si, 1, cin_i))
        for idx, g in enumerate(gs):
            prev = jnp.concatenate([pr[idx * half:(idx + 1) * half], pi[idx * half:(idx + 1) * half]], axis=0)
            y_ref[g] = y_ref[g] + jnp.dot(ct_ref[g], prev.astype(BF16), preferred_element_type=F32)
        return 0

    lax.fori_loop(0, n_groups // 2, pair, 0, unroll=2)

    groups_per_slab = LANES // p
    for t in range(tc):
        for a in range(n_slab):
            yt = jnp.concatenate([y_ref[g, t * p:(t + 1) * p, :]
                                  for g in range(a * groups_per_slab, (a + 1) * groups_per_slab)], axis=0)
            o_refs[a][0, pl.ds(t, c, stride=tc), :] = jnp.transpose(yt)


def _s5(u_slabs, tables, c_lanes):
    kt, bt, ct, tab = tables
    n_slab = len(u_slabs)
    bsz, seq, _ = u_slabs[0].shape
    width = n_slab * LANES
    tc = S5_CHUNK
    n_groups, n2 = bt.shape[0], bt.shape[1]
    p = width // n_groups
    rows = c_lanes * tc
    kern = functools.partial(_s5_kernel, n_groups=n_groups, p=p, n_slab=n_slab)
    slab = pl.BlockSpec((1, rows, LANES), lambda b, j: (b, j, 0))
    return pl.pallas_call(
        kern,
        grid=(bsz, seq // rows),
        in_specs=[slab] * n_slab + [_const_spec(kt.shape), _const_spec(bt.shape), _const_spec(ct.shape),
                                    _const_spec(tab.shape)],
        out_specs=[slab] * n_slab,
        out_shape=[jax.ShapeDtypeStruct((bsz, seq, LANES), F32)] * n_slab,
        scratch_shapes=[pltpu.VMEM((tc, width, c_lanes), BF16), pltpu.VMEM((n_groups, tc * p, c_lanes), F32),
                        pltpu.VMEM((n_groups // 2, SUBLANES, n2), F32)],
        compiler_params=_cparams("parallel", "arbitrary"),
        name="s5",
    )(*u_slabs, kt, bt, ct, tab)


def _merge_kernel(*refs, n_groups, n_experts, d_model, n_slab):
    x_ref, yr_ref, ya_ref = refs[:3]
    ys_refs = refs[3:3 + n_slab]
    (mg_ref, wg_ref, gw_ref, gb_ref, pr_ref, pa_ref, ps_ref, wo_ref, fg_ref, rw_ref, rb_ref,
     x1_ref, xn_ref, route_ref, routet_ref, tab_ref, cnt_ref, run_ref) = refs[3 + n_slab:]
    i = pl.program_id(0)

    @pl.when(i == 0)
    def _():
        run_ref[...] = jnp.zeros_like(run_ref)

    x = x_ref[...]
    tm = x.shape[0]
    xn = _rmsnorm(x, mg_ref[...], MIX_EPS)
    gates = _sigmoid(jnp.dot(xn.astype(BF16), wg_ref[...], preferred_element_type=F32))
    z = _gelu_tanh(jnp.concatenate([ref[...] for ref in ys_refs], axis=1))
    ys = z * _sigmoid(jnp.dot(z.astype(BF16), gw_ref[...], preferred_element_type=F32) + gb_ref[...])
    merged = (gates[:, :d_model] * jnp.dot(yr_ref[...], pr_ref[...], preferred_element_type=F32)
              + gates[:, d_model:2 * d_model] * jnp.dot(ya_ref[...], pa_ref[...], preferred_element_type=F32)
              + gates[:, 2 * d_model:] * jnp.dot(ys.astype(BF16), ps_ref[...], preferred_element_type=F32))
    x1 = x + jnp.dot(merged.astype(BF16), wo_ref[...], preferred_element_type=F32)
    x1_ref[...] = x1
    xn2 = _rmsnorm(x1, fg_ref[...], MIX_EPS)
    xn_ref[...] = xn2.astype(xn_ref.dtype)

    logits = jnp.dot(xn2, rw_ref[...], preferred_element_type=F32, precision=lax.Precision.HIGHEST) + rb_ref[...]
    lane = lax.broadcasted_iota(jnp.int32, logits.shape, 1).astype(F32)
    big = float(LANES)
    coarse = jnp.where(lane < n_groups, logits, NEG_INF)
    cmax = jnp.max(coarse, axis=1, keepdims=True)
    gsel = jnp.min(jnp.where(coarse == cmax, lane, big), axis=1, keepdims=True)
    p_sel = 1.0 / jnp.sum(jnp.where(lane < n_groups, jnp.exp(logits - cmax), 0.0), axis=1, keepdims=True)
    lo = n_groups + gsel * n_experts
    fine = jnp.where((lane >= lo) & (lane < lo + n_experts), logits, NEG_INF)
    m1 = jnp.max(fine, axis=1, keepdims=True)
    i1 = jnp.min(jnp.where(fine == m1, lane, big), axis=1, keepdims=True)
    fine2 = jnp.where(lane == i1, NEG_INF, fine)
    m2 = jnp.max(fine2, axis=1, keepdims=True)
    i2 = jnp.min(jnp.where(fine2 == m2, lane, big), axis=1, keepdims=True)
    e21 = jnp.exp(m2 - m1)
    w1 = p_sel / (1.0 + e21)
    w2 = p_sel * e21 / (1.0 + e21)
    oh1 = lane == i1
    oh2 = lane == i2
    onehot = jnp.where(oh1 | oh2, 1.0, 0.0)
    r_i = lax.broadcasted_iota(jnp.int32, (tm, tm), 0)
    c_i = lax.broadcasted_iota(jnp.int32, (tm, tm), 1)
    earlier = jnp.where(c_i < r_i, 1.0, 0.0).astype(BF16)
    rank = jnp.dot(earlier, onehot.astype(BF16), preferred_element_type=F32)
    cnt = jnp.sum(onehot, axis=0, keepdims=True)
    cnt = jnp.floor((cnt + (SUBLANES - 1)) * (1.0 / SUBLANES)) * SUBLANES
    k_i = lax.broadcasted_iota(jnp.int32, (LANES, LANES), 0)
    l_i = lax.broadcasted_iota(jnp.int32, (LANES, LANES), 1)
    lower = jnp.where(k_i < l_i, 1.0, 0.0)
    start = jnp.dot(jnp.broadcast_to(cnt, (SUBLANES, LANES)), lower, preferred_element_type=F32,
                    precision=lax.Precision.HIGHEST)[0:1]
    pos = rank + start
    lp1 = jnp.sum(jnp.where(oh1, pos, 0.0), axis=1, keepdims=True)
    lp2 = jnp.sum(jnp.where(oh2, pos, 0.0), axis=1, keepdims=True)
    route = jnp.where(lane == 0, w1, 0.0)
    route = jnp.where(lane == 1, w2, route)
    route = jnp.where(lane == 2, lp1, route)
    route = jnp.where(lane == 3, lp2, route)
    route_ref[...] = route
    routet_ref[...] = jnp.transpose(route)[:SUBLANES]
    sub = lax.broadcasted_iota(jnp.int32, (SUBLANES, LANES), 0)
    tab_ref[...] = jnp.where(sub == 0, cnt, jnp.where(sub == 1, start, jnp.where(sub == 2, run_ref[...], 0.0)))
    run_ref[...] = run_ref[...] + cnt
    cnt_ref[...] = run_ref[...]


def _merge(x2d, y_rnn, y_attn, y_s5_slabs, mix_g, w_gate, glu_w, glu_b, p_rnn, p_attn, p_ssm, w_out, ffn_g, rw, rb,
           n_groups, n_experts, tm):
    n, d = x2d.shape
    row = lambda i: (i, 0)
    n_slab = len(y_s5_slabs)
    kern = functools.partial(_merge_kernel, n_groups=n_groups, n_experts=n_experts, d_model=d, n_slab=n_slab)
    consts = [mix_g, w_gate, glu_w, glu_b, p_rnn, p_attn, p_ssm, w_out, ffn_g, rw, rb]
    return pl.pallas_call(
        kern,
        grid=(n // tm,),
        in_specs=[pl.BlockSpec((tm, d), row), pl.BlockSpec((tm, y_rnn.shape[1]), row),
                  pl.BlockSpec((tm, y_attn.shape[1]), row)] + [pl.BlockSpec((tm, LANES), row)] * n_slab
                 + [_const_spec(a.shape) for a in consts],
        out_specs=[pl.BlockSpec((tm, d), row), pl.BlockSpec((tm, d), row), pl.BlockSpec((tm, LANES), row),
                   pl.BlockSpec((SUBLANES, tm), lambda i: (0, i)), pl.BlockSpec((SUBLANES, LANES), row),
                   pl.BlockSpec((1, LANES), lambda i: (0, 0))],
        out_shape=[jax.ShapeDtypeStruct((n, d), F32), jax.ShapeDtypeStruct((n, d), BF16),
                   jax.ShapeDtypeStruct((n, LANES), F32), jax.ShapeDtypeStruct((SUBLANES, n), F32),
                   jax.ShapeDtypeStruct((n // tm * SUBLANES, LANES), F32), jax.ShapeDtypeStruct((1, LANES), F32)],
        scratch_shapes=[pltpu.VMEM((1, LANES), F32)],
        compiler_params=_cparams("arbitrary"),
        name="merge_router",
    )(x2d, y_rnn, y_attn, *y_s5_slabs, *consts)


def _sorted_rows(tm, n_total):
    rows = TOP_K_FINE * tm + n_total * (SUBLANES - 1)
    return -(-rows // LANES) * LANES


RUN_BITS = 6
TILE_BITS = 3


def _run_copies(base, n_total, cnt_ref, loc_ref, dst_ref, local_ref, remote_ref, sem, to_remote, wait):
    def copy(e, off, size):
        local = local_ref.at[pl.ds(pl.multiple_of(loc_ref[base + e] + off, SUBLANES), size)]
        remote = remote_ref.at[pl.ds(pl.multiple_of(dst_ref[base + e] + off, SUBLANES), size)]
        desc = pltpu.make_async_copy(local, remote, sem) if to_remote else pltpu.make_async_copy(remote, local, sem)
        if wait:
            desc.wait()
        else:
            desc.start()

    def expert(e, _):
        cnt = cnt_ref[base + e]
        big = 1 << RUN_BITS

        def chunk(c, _):
            copy(e,c * big, big)
            return 0

        n_big = cnt >> RUN_BITS
        lax.fori_loop(0, n_big, chunk, 0)
        off = n_big * big
        for b in reversed(range(TILE_BITS, RUN_BITS)):
            size = 1 << b

            @pl.when((cnt & size) != 0)
            def _(off=off, size=size):
                copy(e,off, size)

            off = off + (cnt & size)
        return 0

    lax.fori_loop(0, n_total, expert, 0)


def _dispatch_kernel(cnt_ref, loc_ref, dst_ref, x_ref, rt_ref, xs_in_ref, xs_ref, buf_ref, sem, *, n_total):
    del xs_in_ref
    tm = x_ref.shape[0]
    rows = buf_ref.shape[0]
    pos = rt_ref[...]
    j = lax.broadcasted_iota(jnp.int32, (rows, tm), 0).astype(F32)
    sel = jnp.where((j == pos[2:3]) | (j == pos[3:4]), 1.0, 0.0).astype(BF16)
    buf_ref[...] = jnp.dot(sel, x_ref[...], preferred_element_type=F32)
    base = pl.program_id(0) * n_total
    for wait in (False, True):
        _run_copies(base, n_total, cnt_ref, loc_ref, dst_ref, buf_ref, xs_ref, sem, True, wait)


def _dispatch(tile_cnt, tile_loc, tile_dst, xn2, route_t, n_rows, n_total, tm):
    n, d = xn2.shape
    xs0 = jnp.zeros((n_rows, d), F32)
    grid_spec = pltpu.PrefetchScalarGridSpec(
        num_scalar_prefetch=3,
        grid=(n // tm,),
        in_specs=[pl.BlockSpec((tm, d), lambda i, *_: (i, 0)),
                  pl.BlockSpec((SUBLANES, tm), lambda i, *_: (0, i)),
                  pl.BlockSpec(memory_space=pl.ANY)],
        out_specs=pl.BlockSpec(memory_space=pl.ANY),
        scratch_shapes=[pltpu.VMEM((_sorted_rows(tm, n_total), d), F32), pltpu.SemaphoreType.DMA(())],
    )
    return pl.pallas_call(
        functools.partial(_dispatch_kernel, n_total=n_total),
        grid_spec=grid_spec,
        out_shape=jax.ShapeDtypeStruct((n_rows, d), F32),
        input_output_aliases={5: 0},
        compiler_params=_cparams("arbitrary"),
        name="moe_dispatch",
    )(tile_cnt, tile_loc, tile_dst, xn2, route_t, xs0)


def _experts_kernel(te_ref, nact_ref, x_ref, w1_ref, w3_ref, w2_ref, o_ref, w1b_ref, w3b_ref, w2b_ref):
    i = pl.program_id(0)
    fresh = jnp.logical_or(i == 0, te_ref[i] != te_ref[jnp.maximum(i - 1, 0)])

    @pl.when(jnp.logical_and(i < nact_ref[0], fresh))
    def _():
        w1b_ref[...] = w1_ref[0].astype(BF16)
        w3b_ref[...] = w3_ref[0].astype(BF16)
        w2b_ref[...] = w2_ref[0].astype(BF16)

    @pl.when(i < nact_ref[0])
    def _():
        xb = x_ref[...].astype(BF16)
        h1 = jnp.dot(xb, w1b_ref[...], preferred_element_type=F32)
        h3 = jnp.dot(xb, w3b_ref[...], preferred_element_type=F32)
        hid = h1 * _sigmoid(h1) * h3
        o_ref[...] = jnp.dot(hid.astype(BF16), w2b_ref[...], preferred_element_type=F32).astype(o_ref.dtype)

    @pl.when(i >= nact_ref[0])
    def _():
        o_ref[...] = jnp.zeros_like(o_ref)


def _experts(tile_expert, n_active, xs, w1, w3, w2, tm):
    n_rows, d = xs.shape
    f = w1.shape[2]
    row = lambda i, te, na: (jnp.minimum(i, na[0] - 1), 0)
    wsel = lambda i, te, na: (te[i], 0, 0)
    grid_spec = pltpu.PrefetchScalarGridSpec(
        num_scalar_prefetch=2,
        grid=(n_rows // tm,),
        in_specs=[pl.BlockSpec((tm, d), row), pl.BlockSpec((1, d, f), wsel), pl.BlockSpec((1, d, f), wsel),
                  pl.BlockSpec((1, f, d), wsel)],
        out_specs=pl.BlockSpec((tm, d), lambda i, te, na: (i, 0)),
        scratch_shapes=[pltpu.VMEM((d, f), BF16), pltpu.VMEM((d, f), BF16), pltpu.VMEM((f, d), BF16)],
    )
    return pl.pallas_call(
        _experts_kernel,
        grid_spec=grid_spec,
        out_shape=jax.ShapeDtypeStruct((n_rows, d), F32),
        compiler_params=_cparams("arbitrary"),
        name="moe_experts",
    )(tile_expert, n_active, xs, w1, w3, w2)


def _combine_kernel(cnt_ref, loc_ref, dst_ref, x_ref, route_ref, fg_ref, ys_ref, o_ref, buf_ref, sem,
                    *, n_total, final_norm):
    tm = x_ref.shape[0]
    rows = buf_ref.shape[0]
    base = pl.program_id(0) * n_total

    @pl.when(pl.program_id(0) == 0)
    def _():
        buf_ref[...] = jnp.zeros_like(buf_ref)

    for wait in (False, True):
        _run_copies(base, n_total, cnt_ref, loc_ref, dst_ref, buf_ref, ys_ref, sem, False, wait)
    route = route_ref[...]
    yb = buf_ref[...].astype(BF16)
    j = lax.broadcasted_iota(jnp.int32, (tm, rows), 1).astype(F32)
    out = x_ref[...]
    for slot in range(TOP_K_FINE):
        pick = jnp.where(j == route[:, 2 + slot:3 + slot], 1.0, 0.0).astype(BF16)
        out = out + route[:, slot:slot + 1] * jnp.dot(pick, yb, preferred_element_type=F32)
    if final_norm:
        out = _rmsnorm(out, fg_ref[...], MIX_EPS)
    o_ref[...] = out


def _combine(tile_cnt, tile_loc, tile_dst, x1, route, final_g, ys, n_total, tm, final_norm):
    n, d = x1.shape
    kern = functools.partial(_combine_kernel, n_total=n_total, final_norm=final_norm)
    grid_spec = pltpu.PrefetchScalarGridSpec(
        num_scalar_prefetch=3,
        grid=(n // tm,),
        in_specs=[pl.BlockSpec((tm, d), lambda i, *_: (i, 0)),
                  pl.BlockSpec((tm, LANES), lambda i, *_: (i, 0)),
                  pl.BlockSpec((1, d), lambda i, *_: (0, 0)),
                  pl.BlockSpec(memory_space=pl.ANY)],
        out_specs=pl.BlockSpec((tm, d), lambda i, *_: (i, 0)),
        scratch_shapes=[pltpu.VMEM((_sorted_rows(tm, n_total), d), F32), pltpu.SemaphoreType.DMA(())],
    )
    return pl.pallas_call(
        kern,
        grid_spec=grid_spec,
        out_shape=jax.ShapeDtypeStruct((n, d), F32),
        compiler_params=_cparams("arbitrary"),
        name="moe_combine",
    )(tile_cnt, tile_loc, tile_dst, x1, route, final_g, ys)


def _tile_plan(n, seq):
    return dict(inproj=min(512, n), rglru=min(256, seq), attn=min(512, seq), attn_heads=2,
                s5_lanes=min(LANES, seq // S5_CHUNK), merge=min(512, n), moe=min(256, n))


def kernel(x, positions, mix_norm_g, w_in, conv_w, conv_b, rg_wa, rg_ba, rg_wx, rg_bx, rg_lambda,
           lam_q1, lam_k1, lam_q2, lam_k2, subln_g,
           ssm_lambda_re, ssm_lambda_im, ssm_b_re, ssm_b_im, ssm_c_re, ssm_c_im, ssm_d, ssm_log_dt,
           ssm_glu_w, ssm_glu_b, proj_rnn, proj_attn, proj_ssm, w_out,
           ffn_norm_g, router_coarse_w, router_coarse_b, router_fine_w, router_fine_b,
           expert_w1, expert_w3, expert_w2, final_norm_g):
    return _forward(_tile_plan(x.shape[0] * x.shape[1], x.shape[1]),
                    x, positions, mix_norm_g, w_in, conv_w, conv_b, rg_wa, rg_ba, rg_wx, rg_bx, rg_lambda,
                    lam_q1, lam_k1, lam_q2, lam_k2, subln_g,
                    ssm_lambda_re, ssm_lambda_im, ssm_b_re, ssm_b_im, ssm_c_re, ssm_c_im, ssm_d, ssm_log_dt,
                    ssm_glu_w, ssm_glu_b, proj_rnn, proj_attn, proj_ssm, w_out,
                    ffn_norm_g, router_coarse_w, router_coarse_b, router_fine_w, router_fine_b,
                    expert_w1, expert_w3, expert_w2, final_norm_g)


def _forward(tiles, x, positions, mix_norm_g, w_in, conv_w, conv_b, rg_wa, rg_ba, rg_wx, rg_bx, rg_lambda,
             lam_q1, lam_k1, lam_q2, lam_k2, subln_g,
             ssm_lambda_re, ssm_lambda_im, ssm_b_re, ssm_b_im, ssm_c_re, ssm_c_im, ssm_d, ssm_log_dt,
             ssm_glu_w, ssm_glu_b, proj_rnn, proj_attn, proj_ssm, w_out,
             ffn_norm_g, router_coarse_w, router_coarse_b, router_fine_w, router_fine_b,
             expert_w1, expert_w3, expert_w2, final_norm_g):
    bsz, seq, d_model = x.shape
    depth = w_in.shape[0]
    n = bsz * seq
    r = conv_w.shape[2]
    sw = ssm_glu_w.shape[1]
    vdim = subln_g.shape[1]
    head_dim = vdim // 2
    in_cols = w_in.shape[2]
    qk = (in_cols - 2 * r - sw - 3 * d_model) // 3
    heads = qk // (2 * head_dim)
    splits = (r, 2 * r, 2 * r + qk, 2 * r + 2 * qk, 2 * r + 3 * qk, 2 * r + 3 * qk + sw)
    mix_cols = splits[-1]
    n_groups = router_coarse_w.shape[2]
    n_experts = expert_w1.shape[2]
    n_total = n_groups * n_experts
    rnn_blocks = rg_wa.shape[1]

    tm_in, tt_rnn, tq = tiles["inproj"], tiles["rglru"], tiles["attn"]
    s5_lanes, tm_merge, tm_moe = tiles["s5_lanes"], tiles["merge"], tiles["moe"]
    n_rows = TOP_K_FINE * n + (n // tm_merge) * n_total * (SUBLANES - 1) + n_total * tm_moe
    n_rows = -(-n_rows // tm_moe) * tm_moe

    posf = positions.astype(F32)
    pos_col = posf.reshape(n, 1)
    pos_blk = posf.reshape(bsz, seq, 1)
    inv_freq = ROPE_THETA ** (-jnp.arange(0, head_dim, 2, dtype=F32) / head_dim)
    invf = jnp.tile(inv_freq, LANES // (head_dim // 2)).reshape(1, LANES)
    eye_blocks = jnp.eye(rnn_blocks, dtype=F32)

    x2d = x.reshape(n, d_model)
    for l in range(depth):
        lambda_init = 0.8 - 0.6 * math.exp(-0.3 * l)
        w_mix = w_in[l, :, :mix_cols].astype(BF16)
        w_gate = w_in[l, :, mix_cols:].astype(BF16)
        x_rnn, g_rnn, q, k, vt, *u_slabs = _inproj(x2d, mix_norm_g[l].reshape(1, d_model), pos_col, invf, w_mix,
                                                   splits, head_dim, tm_in, tq)

        def block_diag(w):
            return jnp.einsum('hij,hk->hikj', w, eye_blocks).reshape(r, r)

        w_gates = jnp.concatenate([block_diag(rg_wa[l]), block_diag(rg_wx[l])], axis=1).astype(BF16)
        b_gates = jnp.concatenate([rg_ba[l], rg_bx[l]]).reshape(1, 2 * r)
        y_rnn = _rglru(x_rnn.reshape(bsz, seq, r), g_rnn.reshape(bsz, seq, r), pos_blk, conv_w[l],
                       conv_b[l].reshape(1, r), w_gates, b_gates, rg_lambda[l].reshape(1, r), tt_rnn)

        lamv = jnp.stack([lam_q1[l], lam_k1[l], lam_q2[l], lam_k2[l]])
        y_attn = _diff_attention(q.reshape(bsz, seq, qk), k.reshape(bsz, seq, qk), vt,
                                 lamv, subln_g[l].reshape(1, vdim), heads, head_dim, lambda_init, tq,
                                 min(tiles["attn_heads"], heads))

        tables = _s5_tables(ssm_lambda_re[l], ssm_lambda_im[l], ssm_b_re[l], ssm_b_im[l], ssm_c_re[l],
                            ssm_c_im[l], ssm_d[l], ssm_log_dt[l], int(math.log2(s5_lanes)))
        y_s5 = _s5([u.reshape(bsz, seq, LANES) for u in u_slabs], tables, s5_lanes)

        rw = jnp.concatenate([router_coarse_w[l], router_fine_w[l]], axis=1)
        rw = jnp.pad(rw, ((0, 0), (0, LANES - rw.shape[1])))
        rb = jnp.concatenate([router_coarse_b[l], router_fine_b[l]])
        rb = jnp.pad(rb, (0, LANES - rb.shape[0])).reshape(1, LANES)
        x1, xn2, route, route_t, tile_tab, counts = _merge(
            x2d, y_rnn.reshape(n, r), y_attn.reshape(n, qk), [y.reshape(n, LANES) for y in y_s5],
            mix_norm_g[l].reshape(1, d_model), w_gate, ssm_glu_w[l].astype(BF16), ssm_glu_b[l].reshape(1, sw),
            proj_rnn[l].astype(BF16), proj_attn[l].astype(BF16), proj_ssm[l].astype(BF16), w_out[l].astype(BF16),
            ffn_norm_g[l].reshape(1, d_model), rw, rb, n_groups, n_experts, tm_merge)

        cnt = counts[0, n_groups:n_groups + n_total].astype(jnp.int32)
        n_tiles = (cnt + tm_moe - 1) // tm_moe
        tile_end = jnp.cumsum(n_tiles)
        offsets = (tile_end - n_tiles) * tm_moe
        tab = tile_tab.reshape(n // tm_merge, SUBLANES, LANES)[:, :, n_groups:n_groups + n_total].astype(jnp.int32)
        tile_cnt = tab[:, 0].reshape(-1)
        tile_loc = tab[:, 1].reshape(-1)
        tile_dst = (tab[:, 2] + offsets[None, :]).reshape(-1)
        n_active = tile_end[-1:]
        tile_ids = jnp.minimum(jnp.arange(n_rows // tm_moe, dtype=jnp.int32), n_active[0] - 1)
        tile_expert = jnp.sum((tile_ids[:, None] >= tile_end[None, :]).astype(jnp.int32), axis=1)

        xs = _dispatch(tile_cnt, tile_loc, tile_dst, xn2, route_t, n_rows, n_total, tm_merge)
        ys = _experts(tile_expert + l * n_total, n_active.astype(jnp.int32), xs,
                      expert_w1.reshape(depth * n_total, d_model, -1), expert_w3.reshape(depth * n_total, d_model, -1),
                      expert_w2.reshape(depth * n_total, -1, d_model), tm_moe)
        x2d = _combine(tile_cnt, tile_loc, tile_dst, x1, route, final_norm_g.reshape(1, d_model), ys, n_total,
                       tm_merge, l == depth - 1)
    return x2d.reshape(bsz, seq, d_model)
```

```python
import functools
import math

import jax
import jax.numpy as jnp
from jax import lax
from jax.experimental import pallas as pl
from jax.experimental.pallas import tpu as pltpu

F32 = jnp.float32
BF16 = jnp.bfloat16

RGLRU_C = 8.0
ROPE_THETA = 10000.0
TOP_K_FINE = 2
NEG_INF = -1e30
MIX_EPS = 1e-6
SUBLN_EPS = 1e-5

LANES = 128
SUBLANES = 8
VMEM_LIMIT_BYTES = 56 * 1024 * 1024

S5_CHUNK = 16


def _cparams(*sem):
    return pltpu.CompilerParams(dimension_semantics=sem, vmem_limit_bytes=VMEM_LIMIT_BYTES)


def _const_spec(shape):
    nd = len(shape)
    return pl.BlockSpec(shape, lambda *_: (0,) * nd, pipeline_mode=pl.Buffered(1))


def _gelu_tanh(x):
    return 0.5 * x * (1.0 + jnp.tanh(math.sqrt(2.0 / math.pi) * (x + 0.044715 * (x * x * x))))


def _sigmoid(x):
    return 0.5 + 0.5 * jnp.tanh(0.5 * x)


def _rmsnorm(x, g, eps):
    return x * lax.rsqrt(jnp.mean(x * x, axis=-1, keepdims=True) + eps) * g


def _inproj_kernel(x_ref, g_ref, pos_ref, invf_ref, w_ref, xr_ref, gr_ref, q_ref, k_ref, vt_ref, *u_refs,
                   splits, head_dim, q_scale):
    x = x_ref[...]
    xn = _rmsnorm(x, g_ref[...], MIX_EPS)
    h = jnp.dot(xn.astype(BF16), w_ref[...], preferred_element_type=F32)
    s0, s1, s2, s3, s4, s5 = splits
    xr_ref[...] = h[:, :s0].astype(xr_ref.dtype)
    gr_ref[...] = h[:, s0:s1].astype(gr_ref.dtype)
    for a, u_ref in enumerate(u_refs):
        u_ref[...] = h[:, s4 + a * LANES:s4 + (a + 1) * LANES]
    tkv = vt_ref.shape[2]
    for c in range(vt_ref.shape[0]):
        vt_ref[c] = jnp.transpose(h[c * tkv:(c + 1) * tkv, s3:s4]).astype(vt_ref.dtype)

    ang = pos_ref[...] * invf_ref[...]
    cos = jnp.cos(ang)
    sin = jnp.sin(ang)
    lane = lax.broadcasted_iota(jnp.int32, ang.shape, 1)
    first_half = (lane % head_dim) < (head_dim // 2)
    sin_signed = jnp.where(first_half, -sin, sin)

    def rope(t, scale):
        outs = []
        for a in range(t.shape[1] // LANES):
            xs = t[:, a * LANES:(a + 1) * LANES]
            fwd = pltpu.roll(xs, LANES - head_dim // 2, 1)
            bwd = pltpu.roll(xs, head_dim // 2, 1)
            rot = jnp.where(first_half, fwd, bwd)
            outs.append((xs * cos + rot * sin_signed) * scale)
        return jnp.concatenate(outs, axis=1)

    q_ref[...] = rope(h[:, s1:s2], q_scale).astype(q_ref.dtype)
    k_ref[...] = rope(h[:, s2:s3], 1.0).astype(k_ref.dtype)


def _inproj(x2d, g, posf, invf, w, splits, head_dim, tm, tkv):
    n, d = x2d.shape
    c = w.shape[1]
    widths = [splits[0]] + [splits[i] - splits[i - 1] for i in range(1, 6)]
    n_slab = widths[5] // LANES
    kern = functools.partial(_inproj_kernel, splits=splits, head_dim=head_dim,
                             q_scale=head_dim ** -0.5 * math.log2(math.e))
    row = lambda i: (i, 0)
    rows = lambda wd: pl.BlockSpec((tm, wd), row)
    out = lambda wd, dt: jax.ShapeDtypeStruct((n, wd), dt)
    return pl.pallas_call(
        kern,
        grid=(n // tm,),
        in_specs=[pl.BlockSpec((tm, d), row), _const_spec((1, d)), pl.BlockSpec((tm, 1), row),
                  _const_spec((1, LANES)), _const_spec((d, c))],
        out_specs=[rows(widths[0]), rows(widths[1]), rows(widths[2]), rows(widths[3]),
                   pl.BlockSpec((tm // tkv, widths[4], tkv), lambda i: (i, 0, 0))] + [rows(LANES)] * n_slab,
        out_shape=[out(widths[0], F32), out(widths[1], BF16), out(widths[2], BF16), out(widths[3], BF16),
                   jax.ShapeDtypeStruct((n // tkv, widths[4], tkv), BF16)] + [out(LANES, F32)] * n_slab,
        compiler_params=_cparams("parallel"),
        name="inproj",
    )(x2d, g, posf, invf, w)


def _rglru_kernel(x_ref, g_ref, pos_ref, cw_ref, cb_ref, w_ref, b_ref, lam_ref, o_ref, halo_ref, h_ref):
    j = pl.program_id(1)

    @pl.when(j == 0)
    def _():
        halo_ref[...] = jnp.zeros_like(halo_ref)
        h_ref[...] = jnp.zeros_like(h_ref)

    x = x_ref[0].astype(F32)
    t, r = x.shape
    halo = halo_ref[...]
    row8 = lax.broadcasted_iota(jnp.int32, (SUBLANES, r), 0)
    cw = cw_ref[...]
    xc = cb_ref[...] + cw[0:1] * x
    for k in range(1, cw.shape[0]):
        rolled = pltpu.roll(x, k, 0)
        first = jnp.where(row8 < k, pltpu.roll(halo, k, 0), rolled[:SUBLANES])
        xc = xc + cw[k:k + 1] * jnp.concatenate([first, rolled[SUBLANES:]], axis=0)
    halo_ref[...] = x[t - SUBLANES:]

    gates = jnp.dot(xc.astype(BF16), w_ref[...], preferred_element_type=F32) + b_ref[...]
    rg = _sigmoid(gates[:, :r])
    ig = _sigmoid(gates[:, r:])
    z = -lam_ref[...]
    softplus = jnp.maximum(z, 0.0) + jnp.log(1.0 + jnp.exp(-jnp.abs(z)))
    log_a = (-RGLRU_C) * rg * softplus
    a = jnp.exp(log_a)
    mult = jnp.sqrt(1.0 - jnp.exp(2.0 * log_a))
    reset = pos_ref[0] == 0.0
    a = jnp.where(reset, 0.0, a)
    mult = jnp.where(reset, 1.0, mult)
    b = mult * ig * xc

    rows = lax.broadcasted_iota(jnp.int32, (t, r), 0)
    d = 1
    while d < t:
        keep = rows >= d
        a_sh = jnp.where(keep, pltpu.roll(a, d, 0), 1.0)
        b_sh = jnp.where(keep, pltpu.roll(b, d, 0), 0.0)
        b = b + a * b_sh
        a = a * a_sh
        d *= 2
    h = b + a * h_ref[...]
    h_ref[...] = h[t - 1:t]
    o_ref[0] = (h * _gelu_tanh(g_ref[0].astype(F32))).astype(o_ref.dtype)


def _rglru(x_rnn, g_rnn, posf, conv_w, conv_b, w_gates, b_gates, lam, tt):
    bsz, seq, r = x_rnn.shape
    kw = conv_w.shape[0]
    blk = lambda b, j: (b, j, 0)
    return pl.pallas_call(
        _rglru_kernel,
        grid=(bsz, seq // tt),
        in_specs=[pl.BlockSpec((1, tt, r), blk), pl.BlockSpec((1, tt, r), blk), pl.BlockSpec((1, tt, 1), blk),
                  _const_spec((kw, r)), _const_spec((1, r)), _const_spec((r, 2 * r)), _const_spec((1, 2 * r)),
                  _const_spec((1, r))],
        out_specs=pl.BlockSpec((1, tt, r), blk),
        out_shape=jax.ShapeDtypeStruct((bsz, seq, r), BF16),
        scratch_shapes=[pltpu.VMEM((SUBLANES, r), F32), pltpu.VMEM((1, r), F32)],
        compiler_params=_cparams("parallel", "arbitrary"),
        name="rglru",
    )(x_rnn, g_rnn, posf, conv_w, conv_b, w_gates, b_gates, lam)


def _attn_kernel(q_ref, k_ref, vt_ref, lamv_ref, sg_ref, o_ref, *, tq, head_dim, lambda_init):
    i = pl.program_id(2)
    hw = 2 * head_dim
    hp = q_ref.shape[2] // hw
    vdim = vt_ref.shape[1] // hp
    lane = lax.broadcasted_iota(jnp.int32, (tq, hw), 1)
    zero = jnp.zeros((tq, hw), q_ref.dtype)
    ones = jnp.ones((2 * SUBLANES, tq), BF16)
    qqs = []
    for a in range(hp):
        q = q_ref[0, :, a * hw:(a + 1) * hw]
        qqs.append(jnp.concatenate([jnp.where(lane < head_dim, q, zero), jnp.where(lane >= head_dim, q, zero)],
                                   axis=0))

    def step(j, carry, masked):
        row0 = pl.multiple_of(j * tq, tq)
        out = []
        for a in range(hp):
            m, acc = carry[2 * a], carry[2 * a + 1]
            kb = k_ref[0, pl.ds(row0, tq), a * hw:(a + 1) * hw]
            s = lax.dot_general(kb, qqs[a], (((1,), (1,)), ((), ())), preferred_element_type=F32)
            if masked:
                key = lax.broadcasted_iota(jnp.int32, s.shape, 0)
                qry = lax.broadcasted_iota(jnp.int32, s.shape, 1)
                qry = jnp.where(qry >= tq, qry - tq, qry)
                s = jnp.where(key <= qry, s, NEG_INF)
            m_new = jnp.maximum(m, jnp.max(s, axis=0, keepdims=True))
            p = jnp.exp2(s - m_new)
            alpha = jnp.exp2(m - m_new)
            vt = jnp.concatenate([vt_ref[j, a * vdim:(a + 1) * vdim, :], ones], axis=0)
            out += [m_new, alpha * acc + jnp.dot(vt, p.astype(BF16), preferred_element_type=F32)]
        return tuple(out)

    init = (jnp.full((1, 2 * tq), NEG_INF, F32), jnp.zeros((vdim + 2 * SUBLANES, 2 * tq), F32)) * hp
    carry = lax.fori_loop(0, i, lambda j, c: step(j, c, False), init)
    carry = step(i, carry, True)

    lamv = lamv_ref[...]
    lam = (jnp.exp(jnp.sum(lamv[0:1] * lamv[1:2], axis=1, keepdims=True))
           - jnp.exp(jnp.sum(lamv[2:3] * lamv[3:4], axis=1, keepdims=True)) + lambda_init)
    for a in range(hp):
        acc = carry[2 * a + 1]
        ot = acc[:vdim] / acc[vdim:vdim + 1]
        o = jnp.transpose(ot[:, :tq] - lam * ot[:, tq:])
        o = _rmsnorm(o, sg_ref[...], SUBLN_EPS) * (1.0 - lambda_init)
        o_ref[0, :, a * vdim:(a + 1) * vdim] = o.astype(o_ref.dtype)


def _diff_attention(q, k, vt, lamv, subln_g, heads, head_dim, lambda_init, tq, hp):
    bsz, seq, _ = q.shape
    vdim = vt.shape[1] // heads
    nkv = seq // tq
    kern = functools.partial(_attn_kernel, tq=tq, head_dim=head_dim, lambda_init=lambda_init)
    return pl.pallas_call(
        kern,
        grid=(bsz, heads // hp, seq // tq),
        in_specs=[pl.BlockSpec((1, tq, hp * 2 * head_dim), lambda b, h, i: (b, i, h)),
                  pl.BlockSpec((1, seq, hp * 2 * head_dim), lambda b, h, i: (b, 0, h)),
                  pl.BlockSpec((nkv, hp * vdim, tq), lambda b, h, i: (b, h, 0)),
                  _const_spec(lamv.shape), _const_spec(subln_g.shape)],
        out_specs=pl.BlockSpec((1, tq, hp * vdim), lambda b, h, i: (b, i, h)),
        out_shape=jax.ShapeDtypeStruct((bsz, seq, heads * vdim), BF16),
        compiler_params=_cparams("parallel", "parallel", "arbitrary"),
        name="diff_attn",
    )(q, k, vt, lamv, subln_g)


def _s5_tables(lam_re, lam_im, b_re, b_im, c_re, c_im, d_skip, log_dt, n_steps):
    tc = S5_CHUNK
    g, n, p = b_re.shape
    lr = lam_re.astype(F32)
    li = lam_im.astype(F32)
    dt = jnp.exp(log_dt.astype(F32))[:, None]
    mag = jnp.exp(lr * dt)
    ar = mag * jnp.cos(li * dt)
    ai = mag * jnp.sin(li * dt)
    den = lr * lr + li * li
    cr = ((ar - 1.0) * lr + ai * li) / den
    ci = (ai * lr - (ar - 1.0) * li) / den
    bb_re = cr[..., None] * b_re - ci[..., None] * b_im
    bb_im = cr[..., None] * b_im + ci[..., None] * b_re
    hi = lax.Precision.HIGHEST

    def apow(e):
        e = jnp.asarray(e, F32)[None, None, :]
        m = jnp.exp(e * (lr * dt)[..., None])
        ph = e * (li * dt)[..., None]
        return m * jnp.cos(ph), m * jnp.sin(ph)

    lags = jnp.arange(tc)
    pw_re, pw_im = apow(lags)
    ab_re = pw_re[..., None] * bb_re[:, :, None, :] - pw_im[..., None] * bb_im[:, :, None, :]
    ab_im = pw_re[..., None] * bb_im[:, :, None, :] + pw_im[..., None] * bb_re[:, :, None, :]
    kl = (jnp.einsum('gon,gnji->gjoi', c_re, ab_re, precision=hi)
          - jnp.einsum('gon,gnji->gjoi', c_im, ab_im, precision=hi))
    skip = jnp.eye(p, dtype=F32)[None] * d_skip[:, None, :]
    kl = jnp.concatenate([kl[:, :1] + skip[:, None], kl[:, 1:]], axis=1)
    kl_oji = kl.transpose(0, 2, 1, 3)
    block_rows = [jnp.pad(kl_oji[:, :, t::-1, :].reshape(g, p, (t + 1) * p), ((0, 0), (0, 0), (0, (tc - 1 - t) * p)))
                  for t in range(tc)]
    kt = jnp.stack(block_rows, axis=1).reshape(g, tc * p, tc * p)
    bt = jnp.concatenate([ab_re[:, :, ::-1, :].reshape(g, n, tc * p), ab_im[:, :, ::-1, :].reshape(g, n, tc * p)],
                         axis=1)
    p1_re, p1_im = apow(lags + 1)
    ca_re = jnp.einsum('gon,gnt->gton', c_re, p1_re) - jnp.einsum('gon,gnt->gton', c_im, p1_im)
    ca_im = jnp.einsum('gon,gnt->gton', c_re, p1_im) + jnp.einsum('gon,gnt->gton', c_im, p1_re)
    ct = jnp.concatenate([ca_re.reshape(g, tc * p, n), -ca_im.reshape(g, tc * p, n)], axis=2)
    st_re, st_im = apow(tc * (2 ** jnp.arange(n_steps)))
    tab = jnp.stack([st_re.transpose(0, 2, 1), st_im.transpose(0, 2, 1)], axis=2)
    tab = tab.reshape(g // 2, 2, 2 * n_steps, n).transpose(0, 2, 1, 3).reshape(g // 2, 2 * n_steps, 2 * n)
    tab = jnp.pad(tab, ((0, 0), (0, -(2 * n_steps) % SUBLANES), (0, 0)))
    return kt.astype(BF16), bt.astype(BF16), ct.astype(BF16), tab.astype(F32)


def _s5_kernel(*refs, n_groups, p, n_slab):
    u_refs = refs[:n_slab]
    kt_ref, bt_ref, ct_ref, tab_ref = refs[n_slab:n_slab + 4]
    o_refs = refs[n_slab + 4:2 * n_slab + 4]
    ut_ref, y_ref, carry_ref = refs[2 * n_slab + 4:]
    tc = S5_CHUNK
    c = u_refs[0].shape[1] // tc
    n2 = bt_ref.shape[1]
    half = n2 // 2
    n_steps = int(math.log2(c))

    @pl.when(pl.program_id(1) == 0)
    def _():
        carry_ref[...] = jnp.zeros_like(carry_ref)

    for k in range(tc):
        for a in range(n_slab):
            ut_ref[k, a * LANES:(a + 1) * LANES, :] = jnp.transpose(
                u_refs[a][0, pl.ds(k, c, stride=tc), :]).astype(BF16)

    row = lax.broadcasted_iota(jnp.int32, (c, n2), 0)

    def shift(x, d, fill):
        return jnp.where(row >= d, pltpu.roll(x, d, 0), fill)

    def pair(gp, _):
        gs = (2 * gp, 2 * gp + 1)
        local = []
        for g in gs:
            ug = jnp.concatenate([ut_ref[k, pl.ds(pl.multiple_of(g * p, p), p), :] for k in range(tc)], axis=0)
            y_ref[g] = jnp.dot(kt_ref[g], ug, preferred_element_type=F32)
            local.append(jnp.dot(bt_ref[g], ug, preferred_element_type=F32))
        sr = jnp.transpose(jnp.concatenate([local[0][:half], local[1][:half]], axis=0))
        si = jnp.transpose(jnp.concatenate([local[0][half:], local[1][half:]], axis=0))
        tab = tab_ref[gp]
        cin_r = carry_ref[gp, 0:1, :]
        cin_i = carry_ref[gp, 1:2, :]
        sr = sr + jnp.where(row == 0, tab[0:1] * cin_r - tab[1:2] * cin_i, 0.0)
        si = si + jnp.where(row == 0, tab[0:1] * cin_i + tab[1:2] * cin_r, 0.0)
        for s in range(n_steps):
            d = 1 << s
            ar, ai = tab[2 * s:2 * s + 1], tab[2 * s + 1:2 * s + 2]
            hr, hi = shift(sr, d, 0.0), shift(si, d, 0.0)
            sr, si = sr + ar * hr - ai * hi, si + ar * hi + ai * hr
        carry_ref[gp, 0:1, :] = sr[c - 1:c]
        carry_ref[gp, 1:2, :] = si[c - 1:c]
        pr = jnp.transpose(shift(sr, 1, cin_r))
        pi = jnp.transpose(shift(si, 1, cin_i))
        for idx, g in enumerate(gs):
            prev = jnp.concatenate([pr[idx * half:(idx + 1) * half], pi[idx * half:(idx + 1) * half]], axis=0)
            y_ref[g] = y_ref[g] + jnp.dot(ct_ref[g], prev.astype(BF16), preferred_element_type=F32)
        return 0

    lax.fori_loop(0, n_groups // 2, pair, 0, unroll=2)

    groups_per_slab = LANES // p
    for t in range(tc):
        for a in range(n_slab):
            yt = jnp.concatenate([y_ref[g, t * p:(t + 1) * p, :]
                                  for g in range(a * groups_per_slab, (a + 1) * groups_per_slab)], axis=0)
            o_refs[a][0, pl.ds(t, c, stride=tc), :] = jnp.transpose(yt)


def _s5(u_slabs, tables, c_lanes):
    kt, bt, ct, tab = tables
    n_slab = len(u_slabs)
    bsz, seq, _ = u_slabs[0].shape
    width = n_slab * LANES
    tc = S5_CHUNK
    n_groups, n2 = bt.shape[0], bt.shape[1]
    p = width // n_groups
    rows = c_lanes * tc
    kern = functools.partial(_s5_kernel, n_groups=n_groups, p=p, n_slab=n_slab)
    slab = pl.BlockSpec((1, rows, LANES), lambda b, j: (b, j, 0))
    return pl.pallas_call(
        kern,
        grid=(bsz, seq // rows),
        in_specs=[slab] * n_slab + [_const_spec(kt.shape), _const_spec(bt.shape), _const_spec(ct.shape),
                                    _const_spec(tab.shape)],
        out_specs=[slab] * n_slab,
        out_shape=[jax.ShapeDtypeStruct((bsz, seq, LANES), F32)] * n_slab,
        scratch_shapes=[pltpu.VMEM((tc, width, c_lanes), BF16), pltpu.VMEM((n_groups, tc * p, c_lanes), F32),
                        pltpu.VMEM((n_groups // 2, SUBLANES, n2), F32)],
        compiler_params=_cparams("parallel", "arbitrary"),
        name="s5",
    )(*u_slabs, kt, bt, ct, tab)


def _merge_kernel(*refs, n_groups, n_experts, d_model, n_slab):
    x_ref, yr_ref, ya_ref = refs[:3]
    ys_refs = refs[3:3 + n_slab]
    (mg_ref, wg_ref, gw_ref, gb_ref, pr_ref, pa_ref, ps_ref, wo_ref, fg_ref, rw_ref, rb_ref,
     x1_ref, xn_ref, route_ref, routet_ref, tab_ref, cnt_ref, run_ref) = refs[3 + n_slab:]
    i = pl.program_id(0)

    @pl.when(i == 0)
    def _():
        run_ref[...] = jnp.zeros_like(run_ref)

    x = x_ref[...]
    tm = x.shape[0]
    xn = _rmsnorm(x, mg_ref[...], MIX_EPS)
    gates = _sigmoid(jnp.dot(xn.astype(BF16), wg_ref[...], preferred_element_type=F32))
    z = _gelu_tanh(jnp.concatenate([ref[...] for ref in ys_refs], axis=1))
    ys = z * _sigmoid(jnp.dot(z.astype(BF16), gw_ref[...], preferred_element_type=F32) + gb_ref[...])
    merged = (gates[:, :d_model] * jnp.dot(yr_ref[...], pr_ref[...], preferred_element_type=F32)
              + gates[:, d_model:2 * d_model] * jnp.dot(ya_ref[...], pa_ref[...], preferred_element_type=F32)
              + gates[:, 2 * d_model:] * jnp.dot(ys.astype(BF16), ps_ref[...], preferred_element_type=F32))
    x1 = x + jnp.dot(merged.astype(BF16), wo_ref[...], preferred_element_type=F32)
    x1_ref[...] = x1
    xn2 = _rmsnorm(x1, fg_ref[...], MIX_EPS)
    xn_ref[...] = xn2.astype(xn_ref.dtype)

    rw = rw_ref[...]
    x_hi = xn2.astype(BF16)
    x_lo = (xn2 - x_hi.astype(F32)).astype(BF16)
    w_hi = rw.astype(BF16)
    w_lo = (rw - w_hi.astype(F32)).astype(BF16)
    logits = (jnp.dot(x_hi, w_hi, preferred_element_type=F32) + jnp.dot(x_lo, w_hi, preferred_element_type=F32)
              + jnp.dot(x_hi, w_lo, preferred_element_type=F32) + rb_ref[...])
    lane = lax.broadcasted_iota(jnp.int32, logits.shape, 1).astype(F32)
    big = float(LANES)
    coarse = jnp.where(lane < n_groups, logits, NEG_INF)
    cmax = jnp.max(coarse, axis=1, keepdims=True)
    gsel = jnp.min(jnp.where(coarse == cmax, lane, big), axis=1, keepdims=True)
    p_sel = 1.0 / jnp.sum(jnp.where(lane < n_groups, jnp.exp(logits - cmax), 0.0), axis=1, keepdims=True)
    lo = n_groups + gsel * n_experts
    fine = jnp.where((lane >= lo) & (lane < lo + n_experts), logits, NEG_INF)
    m1 = jnp.max(fine, axis=1, keepdims=True)
    i1 = jnp.min(jnp.where(fine == m1, lane, big), axis=1, keepdims=True)
    fine2 = jnp.where(lane == i1, NEG_INF, fine)
    m2 = jnp.max(fine2, axis=1, keepdims=True)
    i2 = jnp.min(jnp.where(fine2 == m2, lane, big), axis=1, keepdims=True)
    e21 = jnp.exp(m2 - m1)
    w1 = p_sel / (1.0 + e21)
    w2 = p_sel * e21 / (1.0 + e21)
    oh1 = lane == i1
    oh2 = lane == i2
    onehot = jnp.where(oh1 | oh2, 1.0, 0.0)
    r_i = lax.broadcasted_iota(jnp.int32, (tm, tm), 0)
    c_i = lax.broadcasted_iota(jnp.int32, (tm, tm), 1)
    earlier = jnp.where(c_i < r_i, 1.0, 0.0).astype(BF16)
    rank = jnp.dot(earlier, onehot.astype(BF16), preferred_element_type=F32)
    cnt = jnp.sum(onehot, axis=0, keepdims=True)
    cnt = jnp.floor((cnt + (SUBLANES - 1)) * (1.0 / SUBLANES)) * SUBLANES
    k_i = lax.broadcasted_iota(jnp.int32, (LANES, LANES), 0)
    l_i = lax.broadcasted_iota(jnp.int32, (LANES, LANES), 1)
    lower = jnp.where(k_i < l_i, 1.0, 0.0)
    start = jnp.dot(jnp.broadcast_to(cnt, (SUBLANES, LANES)), lower, preferred_element_type=F32,
                    precision=lax.Precision.HIGHEST)[0:1]
    pos = rank + start
    lp1 = jnp.sum(jnp.where(oh1, pos, 0.0), axis=1, keepdims=True)
    lp2 = jnp.sum(jnp.where(oh2, pos, 0.0), axis=1, keepdims=True)
    route = jnp.where(lane == 0, w1, 0.0)
    route = jnp.where(lane == 1, w2, route)
    route = jnp.where(lane == 2, lp1, route)
    route = jnp.where(lane == 3, lp2, route)
    route_ref[...] = route
    routet_ref[...] = jnp.transpose(route)[:SUBLANES]
    sub = lax.broadcasted_iota(jnp.int32, (SUBLANES, LANES), 0)
    tab_ref[...] = jnp.where(sub == 0, cnt, jnp.where(sub == 1, start, jnp.where(sub == 2, run_ref[...], 0.0)))
    run_ref[...] = run_ref[...] + cnt
    cnt_ref[...] = run_ref[...]


def _merge(x2d, y_rnn, y_attn, y_s5_slabs, mix_g, w_gate, glu_w, glu_b, p_rnn, p_attn, p_ssm, w_out, ffn_g, rw, rb,
           n_groups, n_experts, tm):
    n, d = x2d.shape
    row = lambda i: (i, 0)
    n_slab = len(y_s5_slabs)
    kern = functools.partial(_merge_kernel, n_groups=n_groups, n_experts=n_experts, d_model=d, n_slab=n_slab)
    consts = [mix_g, w_gate, glu_w, glu_b, p_rnn, p_attn, p_ssm, w_out, ffn_g, rw, rb]
    return pl.pallas_call(
        kern,
        grid=(n // tm,),
        in_specs=[pl.BlockSpec((tm, d), row), pl.BlockSpec((tm, y_rnn.shape[1]), row),
                  pl.BlockSpec((tm, y_attn.shape[1]), row)] + [pl.BlockSpec((tm, LANES), row)] * n_slab
                 + [_const_spec(a.shape) for a in consts],
        out_specs=[pl.BlockSpec((tm, d), row), pl.BlockSpec((tm, d), row), pl.BlockSpec((tm, LANES), row),
                   pl.BlockSpec((SUBLANES, tm), lambda i: (0, i)), pl.BlockSpec((SUBLANES, LANES), row),
                   pl.BlockSpec((1, LANES), lambda i: (0, 0))],
        out_shape=[jax.ShapeDtypeStruct((n, d), F32), jax.ShapeDtypeStruct((n, d), BF16),
                   jax.ShapeDtypeStruct((n, LANES), F32), jax.ShapeDtypeStruct((SUBLANES, n), F32),
                   jax.ShapeDtypeStruct((n // tm * SUBLANES, LANES), F32), jax.ShapeDtypeStruct((1, LANES), F32)],
        scratch_shapes=[pltpu.VMEM((1, LANES), F32)],
        compiler_params=_cparams("arbitrary"),
        name="merge_router",
    )(x2d, y_rnn, y_attn, *y_s5_slabs, *consts)


def _sorted_rows(tm, n_total):
    rows = TOP_K_FINE * tm + n_total * (SUBLANES - 1)
    return -(-rows // LANES) * LANES


RUN_BITS = 6
TILE_BITS = 3


def _run_copies(base, n_total, cnt_ref, loc_ref, dst_ref, local_ref, remote_ref, sem, to_remote, wait):
    def copy(e, off, size):
        lstart = 0 if loc_ref is None else pl.multiple_of(loc_ref[base + e] + off, SUBLANES)
        local = local_ref.at[pl.ds(lstart, size)]
        remote = remote_ref.at[pl.ds(pl.multiple_of(dst_ref[base + e] + off, SUBLANES), size)]
        desc = pltpu.make_async_copy(local, remote, sem) if to_remote else pltpu.make_async_copy(remote, local, sem)
        if wait:
            desc.wait()
        else:
            desc.start()

    def expert(e, _):
        cnt = cnt_ref[base + e]
        big = 1 << RUN_BITS

        def chunk(c, _):
            copy(e,c * big, big)
            return 0

        n_big = cnt >> RUN_BITS
        lax.fori_loop(0, n_big, chunk, 0)
        off = n_big * big
        for b in reversed(range(TILE_BITS, RUN_BITS)):
            size = 1 << b

            @pl.when((cnt & size) != 0)
            def _(off=off, size=size):
                copy(e,off, size)

            off = off + (cnt & size)
        return 0

    lax.fori_loop(0, n_total, expert, 0)


def _dispatch_kernel(cnt_ref, loc_ref, dst_ref, gap_cnt_ref, gap_dst_ref, x_ref, rt_ref, xs_ref, buf_ref, zero_ref,
                     sem, *, n_total):
    tm = x_ref.shape[0]
    rows = buf_ref.shape[0]
    pos = rt_ref[...]
    j = lax.broadcasted_iota(jnp.int32, (rows, tm), 0).astype(F32)
    sel = jnp.where((j == pos[2:3]) | (j == pos[3:4]), 1.0, 0.0).astype(BF16)
    buf_ref[...] = jnp.dot(sel, x_ref[...], preferred_element_type=F32)
    base = pl.program_id(0) * n_total
    for wait in (False, True):
        _run_copies(base, n_total, cnt_ref, loc_ref, dst_ref, buf_ref, xs_ref, sem, True, wait)

    @pl.when(pl.program_id(0) == pl.num_programs(0) - 1)
    def _():
        zero_ref[...] = jnp.zeros_like(zero_ref)
        for wait in (False, True):
            _run_copies(0, gap_cnt_ref.shape[0], gap_cnt_ref, None, gap_dst_ref, zero_ref, xs_ref, sem, True, wait)


def _dispatch(tile_cnt, tile_loc, tile_dst, gap_cnt, gap_dst, xn2, route_t, n_rows, n_total, tm):
    n, d = xn2.shape
    grid_spec = pltpu.PrefetchScalarGridSpec(
        num_scalar_prefetch=5,
        grid=(n // tm,),
        in_specs=[pl.BlockSpec((tm, d), lambda i, *_: (i, 0)),
                  pl.BlockSpec((SUBLANES, tm), lambda i, *_: (0, i))],
        out_specs=pl.BlockSpec(memory_space=pl.ANY),
        scratch_shapes=[pltpu.VMEM((_sorted_rows(tm, n_total), d), F32), pltpu.VMEM((1 << RUN_BITS, d), F32),
                        pltpu.SemaphoreType.DMA(())],
    )
    return pl.pallas_call(
        functools.partial(_dispatch_kernel, n_total=n_total),
        grid_spec=grid_spec,
        out_shape=jax.ShapeDtypeStruct((n_rows, d), F32),
        compiler_params=_cparams("arbitrary"),
        name="moe_dispatch",
    )(tile_cnt, tile_loc, tile_dst, gap_cnt, gap_dst, xn2, route_t)


def _experts_kernel(te_ref, nact_ref, x_ref, w1_ref, w3_ref, w2_ref, o_ref, w1b_ref, w3b_ref, w2b_ref):
    i = pl.program_id(0)
    fresh = jnp.logical_or(i == 0, te_ref[i] != te_ref[jnp.maximum(i - 1, 0)])

    @pl.when(jnp.logical_and(i < nact_ref[0], fresh))
    def _():
        w1b_ref[...] = w1_ref[0].astype(BF16)
        w3b_ref[...] = w3_ref[0].astype(BF16)
        w2b_ref[...] = w2_ref[0].astype(BF16)

    @pl.when(i < nact_ref[0])
    def _():
        xb = x_ref[...].astype(BF16)
        h1 = jnp.dot(xb, w1b_ref[...], preferred_element_type=F32)
        h3 = jnp.dot(xb, w3b_ref[...], preferred_element_type=F32)
        hid = h1 * _sigmoid(h1) * h3
        o_ref[...] = jnp.dot(hid.astype(BF16), w2b_ref[...], preferred_element_type=F32).astype(o_ref.dtype)

    @pl.when(i >= nact_ref[0])
    def _():
        o_ref[...] = jnp.zeros_like(o_ref)


def _experts(tile_expert, n_active, xs, w1, w3, w2, tm):
    n_rows, d = xs.shape
    f = w1.shape[2]
    row = lambda i, te, na: (jnp.minimum(i, na[0] - 1), 0)
    wsel = lambda i, te, na: (te[i], 0, 0)
    grid_spec = pltpu.PrefetchScalarGridSpec(
        num_scalar_prefetch=2,
        grid=(n_rows // tm,),
        in_specs=[pl.BlockSpec((tm, d), row), pl.BlockSpec((1, d, f), wsel), pl.BlockSpec((1, d, f), wsel),
                  pl.BlockSpec((1, f, d), wsel)],
        out_specs=pl.BlockSpec((tm, d), lambda i, te, na: (i, 0)),
        scratch_shapes=[pltpu.VMEM((d, f), BF16), pltpu.VMEM((d, f), BF16), pltpu.VMEM((f, d), BF16)],
    )
    return pl.pallas_call(
        _experts_kernel,
        grid_spec=grid_spec,
        out_shape=jax.ShapeDtypeStruct((n_rows, d), F32),
        compiler_params=_cparams("arbitrary"),
        name="moe_experts",
    )(tile_expert, n_active, xs, w1, w3, w2)


def _combine_kernel(cnt_ref, loc_ref, dst_ref, x_ref, route_ref, rt_ref, fg_ref, ys_ref, o_ref, buf_ref, sem,
                    *, n_total, final_norm):
    tm = x_ref.shape[0]
    rows = buf_ref.shape[0]
    base = pl.program_id(0) * n_total

    @pl.when(pl.program_id(0) == 0)
    def _():
        buf_ref[...] = jnp.zeros_like(buf_ref)

    for wait in (False, True):
        _run_copies(base, n_total, cnt_ref, loc_ref, dst_ref, buf_ref, ys_ref, sem, False, wait)
    rt = rt_ref[...]
    jr = lax.broadcasted_iota(jnp.int32, (rows, tm), 0).astype(F32)
    gate = jnp.sum(jnp.where(jr == rt[2:3], rt[0:1], 0.0) + jnp.where(jr == rt[3:4], rt[1:2], 0.0),
                   axis=1, keepdims=True)
    yb = (buf_ref[...] * gate).astype(BF16)
    route = route_ref[...]
    jc = lax.broadcasted_iota(jnp.int32, (tm, rows), 1).astype(F32)
    pick = jnp.where((jc == route[:, 2:3]) | (jc == route[:, 3:4]), 1.0, 0.0).astype(BF16)
    out = x_ref[...] + jnp.dot(pick, yb, preferred_element_type=F32)
    if final_norm:
        out = _rmsnorm(out, fg_ref[...], MIX_EPS)
    o_ref[...] = out


def _combine(tile_cnt, tile_loc, tile_dst, x1, route, route_t, final_g, ys, n_total, tm, final_norm):
    n, d = x1.shape
    kern = functools.partial(_combine_kernel, n_total=n_total, final_norm=final_norm)
    grid_spec = pltpu.PrefetchScalarGridSpec(
        num_scalar_prefetch=3,
        grid=(n // tm,),
        in_specs=[pl.BlockSpec((tm, d), lambda i, *_: (i, 0)),
                  pl.BlockSpec((tm, LANES), lambda i, *_: (i, 0)),
                  pl.BlockSpec((SUBLANES, tm), lambda i, *_: (0, i)),
                  pl.BlockSpec((1, d), lambda i, *_: (0, 0)),
                  pl.BlockSpec(memory_space=pl.ANY)],
        out_specs=pl.BlockSpec((tm, d), lambda i, *_: (i, 0)),
        scratch_shapes=[pltpu.VMEM((_sorted_rows(tm, n_total), d), F32), pltpu.SemaphoreType.DMA(())],
    )
    return pl.pallas_call(
        kern,
        grid_spec=grid_spec,
        out_shape=jax.ShapeDtypeStruct((n, d), F32),
        compiler_params=_cparams("arbitrary"),
        name="moe_combine",
    )(tile_cnt, tile_loc, tile_dst, x1, route, route_t, final_g, ys)


def _tile_plan(n, seq):
    return dict(inproj=min(512, n), rglru=min(256, seq), attn=min(512, seq), attn_heads=2,
                s5_lanes=min(LANES, seq // S5_CHUNK), merge=min(512, n), moe=min(384, n))


def kernel(x, positions, mix_norm_g, w_in, conv_w, conv_b, rg_wa, rg_ba, rg_wx, rg_bx, rg_lambda,
           lam_q1, lam_k1, lam_q2, lam_k2, subln_g,
           ssm_lambda_re, ssm_lambda_im, ssm_b_re, ssm_b_im, ssm_c_re, ssm_c_im, ssm_d, ssm_log_dt,
           ssm_glu_w, ssm_glu_b, proj_rnn, proj_attn, proj_ssm, w_out,
           ffn_norm_g, router_coarse_w, router_coarse_b, router_fine_w, router_fine_b,
           expert_w1, expert_w3, expert_w2, final_norm_g):
    return _forward(_tile_plan(x.shape[0] * x.shape[1], x.shape[1]),
                    x, positions, mix_norm_g, w_in, conv_w, conv_b, rg_wa, rg_ba, rg_wx, rg_bx, rg_lambda,
                    lam_q1, lam_k1, lam_q2, lam_k2, subln_g,
                    ssm_lambda_re, ssm_lambda_im, ssm_b_re, ssm_b_im, ssm_c_re, ssm_c_im, ssm_d, ssm_log_dt,
                    ssm_glu_w, ssm_glu_b, proj_rnn, proj_attn, proj_ssm, w_out,
                    ffn_norm_g, router_coarse_w, router_coarse_b, router_fine_w, router_fine_b,
                    expert_w1, expert_w3, expert_w2, final_norm_g)


def _forward(tiles, x, positions, mix_norm_g, w_in, conv_w, conv_b, rg_wa, rg_ba, rg_wx, rg_bx, rg_lambda,
             lam_q1, lam_k1, lam_q2, lam_k2, subln_g,
             ssm_lambda_re, ssm_lambda_im, ssm_b_re, ssm_b_im, ssm_c_re, ssm_c_im, ssm_d, ssm_log_dt,
             ssm_glu_w, ssm_glu_b, proj_rnn, proj_attn, proj_ssm, w_out,
             ffn_norm_g, router_coarse_w, router_coarse_b, router_fine_w, router_fine_b,
             expert_w1, expert_w3, expert_w2, final_norm_g):
    bsz, seq, d_model = x.shape
    depth = w_in.shape[0]
    n = bsz * seq
    r = conv_w.shape[2]
    sw = ssm_glu_w.shape[1]
    vdim = subln_g.shape[1]
    head_dim = vdim // 2
    in_cols = w_in.shape[2]
    qk = (in_cols - 2 * r - sw - 3 * d_model) // 3
    heads = qk // (2 * head_dim)
    splits = (r, 2 * r, 2 * r + qk, 2 * r + 2 * qk, 2 * r + 3 * qk, 2 * r + 3 * qk + sw)
    mix_cols = splits[-1]
    n_groups = router_coarse_w.shape[2]
    n_experts = expert_w1.shape[2]
    n_total = n_groups * n_experts
    rnn_blocks = rg_wa.shape[1]

    tm_in, tt_rnn, tq = tiles["inproj"], tiles["rglru"], tiles["attn"]
    s5_lanes, tm_merge, tm_moe = tiles["s5_lanes"], tiles["merge"], tiles["moe"]
    n_rows = TOP_K_FINE * n + (n // tm_merge) * n_total * (SUBLANES - 1) + n_total * tm_moe
    n_rows = -(-n_rows // tm_moe) * tm_moe

    posf = positions.astype(F32)
    pos_col = posf.reshape(n, 1)
    pos_blk = posf.reshape(bsz, seq, 1)
    inv_freq = ROPE_THETA ** (-jnp.arange(0, head_dim, 2, dtype=F32) / head_dim)
    invf = jnp.tile(inv_freq, LANES // (head_dim // 2)).reshape(1, LANES)
    eye_blocks = jnp.eye(rnn_blocks, dtype=F32)

    x2d = x.reshape(n, d_model)
    for l in range(depth):
        lambda_init = 0.8 - 0.6 * math.exp(-0.3 * l)
        w_mix = w_in[l, :, :mix_cols].astype(BF16)
        w_gate = w_in[l, :, mix_cols:].astype(BF16)
        x_rnn, g_rnn, q, k, vt, *u_slabs = _inproj(x2d, mix_norm_g[l].reshape(1, d_model), pos_col, invf, w_mix,
                                                   splits, head_dim, tm_in, tq)

        def block_diag(w):
            return jnp.einsum('hij,hk->hikj', w, eye_blocks).reshape(r, r)

        w_gates = jnp.concatenate([block_diag(rg_wa[l]), block_diag(rg_wx[l])], axis=1).astype(BF16)
        b_gates = jnp.concatenate([rg_ba[l], rg_bx[l]]).reshape(1, 2 * r)
        y_rnn = _rglru(x_rnn.reshape(bsz, seq, r), g_rnn.reshape(bsz, seq, r), pos_blk, conv_w[l],
                       conv_b[l].reshape(1, r), w_gates, b_gates, rg_lambda[l].reshape(1, r), tt_rnn)

        lamv = jnp.stack([lam_q1[l], lam_k1[l], lam_q2[l], lam_k2[l]])
        y_attn = _diff_attention(q.reshape(bsz, seq, qk), k.reshape(bsz, seq, qk), vt,
                                 lamv, subln_g[l].reshape(1, vdim), heads, head_dim, lambda_init, tq,
                                 min(tiles["attn_heads"], heads))

        tables = _s5_tables(ssm_lambda_re[l], ssm_lambda_im[l], ssm_b_re[l], ssm_b_im[l], ssm_c_re[l],
                            ssm_c_im[l], ssm_d[l], ssm_log_dt[l], int(math.log2(s5_lanes)))
        y_s5 = _s5([u.reshape(bsz, seq, LANES) for u in u_slabs], tables, s5_lanes)

        rw = jnp.concatenate([router_coarse_w[l], router_fine_w[l]], axis=1)
        rw = jnp.pad(rw, ((0, 0), (0, LANES - rw.shape[1])))
        rb = jnp.concatenate([router_coarse_b[l], router_fine_b[l]])
        rb = jnp.pad(rb, (0, LANES - rb.shape[0])).reshape(1, LANES)
        x1, xn2, route, route_t, tile_tab, counts = _merge(
            x2d, y_rnn.reshape(n, r), y_attn.reshape(n, qk), [y.reshape(n, LANES) for y in y_s5],
            mix_norm_g[l].reshape(1, d_model), w_gate, ssm_glu_w[l].astype(BF16), ssm_glu_b[l].reshape(1, sw),
            proj_rnn[l].astype(BF16), proj_attn[l].astype(BF16), proj_ssm[l].astype(BF16), w_out[l].astype(BF16),
            ffn_norm_g[l].reshape(1, d_model), rw, rb, n_groups, n_experts, tm_merge)

        cnt = counts[0, n_groups:n_groups + n_total].astype(jnp.int32)
        n_tiles = (cnt + tm_moe - 1) // tm_moe
        tile_end = jnp.cumsum(n_tiles)
        offsets = (tile_end - n_tiles) * tm_moe
        tab = tile_tab.reshape(n // tm_merge, SUBLANES, LANES)[:, :, n_groups:n_groups + n_total].astype(jnp.int32)
        tile_cnt = tab[:, 0].reshape(-1)
        tile_loc = tab[:, 1].reshape(-1)
        tile_dst = (tab[:, 2] + offsets[None, :]).reshape(-1)
        n_active = tile_end[-1:]
        gap_dst = jnp.concatenate([offsets + cnt, n_active * tm_moe])
        gap_cnt = jnp.concatenate([tile_end * tm_moe, jnp.full((1,), n_rows, jnp.int32)]) - gap_dst
        tile_ids = jnp.minimum(jnp.arange(n_rows // tm_moe, dtype=jnp.int32), n_active[0] - 1)
        tile_expert = jnp.sum((tile_ids[:, None] >= tile_end[None, :]).astype(jnp.int32), axis=1)

        xs = _dispatch(tile_cnt, tile_loc, tile_dst, gap_cnt, gap_dst, xn2, route_t, n_rows, n_total, tm_merge)
        ys = _experts(tile_expert + l * n_total, n_active.astype(jnp.int32), xs,
                      expert_w1.reshape(depth * n_total, d_model, -1), expert_w3.reshape(depth * n_total, d_model, -1),
                      expert_w2.reshape(depth * n_total, -1, d_model), tm_moe)
        x2d = _combine(tile_cnt, tile_loc, tile_dst, x1, route, route_t, final_norm_g.reshape(1, d_model), ys,
                       n_total, tm_merge, l == depth - 1)
    return x2d.reshape(bsz, seq, d_model)
```

```python
import functools
import math

import jax
import jax.numpy as jnp
from jax import lax
from jax.experimental import pallas as pl
from jax.experimental.pallas import tpu as pltpu

F32 = jnp.float32
BF16 = jnp.bfloat16

RGLRU_C = 8.0
ROPE_THETA = 10000.0
TOP_K_FINE = 2
NEG_INF = -1e30
MIX_EPS = 1e-6
SUBLN_EPS = 1e-5

LANES = 128
SUBLANES = 8
VMEM_LIMIT_BYTES = 56 * 1024 * 1024

S5_CHUNK = 16


def _cparams(*sem):
    return pltpu.CompilerParams(dimension_semantics=sem, vmem_limit_bytes=VMEM_LIMIT_BYTES)


def _const_spec(shape):
    nd = len(shape)
    return pl.BlockSpec(shape, lambda *_: (0,) * nd, pipeline_mode=pl.Buffered(1))


def _layer_spec(shape, layer):
    nd = len(shape)
    return pl.BlockSpec((1,) + tuple(shape[1:]), lambda *_: (layer,) + (0,) * (nd - 1), pipeline_mode=pl.Buffered(1))


def _gelu_tanh(x):
    return 0.5 * x * (1.0 + jnp.tanh(math.sqrt(2.0 / math.pi) * (x + 0.044715 * (x * x * x))))


def _sigmoid(x):
    return 0.5 + 0.5 * jnp.tanh(0.5 * x)


def _rmsnorm(x, g, eps):
    return x * lax.rsqrt(jnp.mean(x * x, axis=-1, keepdims=True) + eps) * g


def _inproj_kernel(x_ref, g_ref, pos_ref, invf_ref, w_ref, xr_ref, gr_ref, q_ref, k_ref, vt_ref, *u_refs,
                   splits, head_dim, q_scale):
    x = x_ref[...]
    xn = _rmsnorm(x, g_ref[0], MIX_EPS)
    h = jnp.dot(xn.astype(BF16), w_ref[0], preferred_element_type=F32)
    s0, s1, s2, s3, s4, s5 = splits
    xr_ref[...] = h[:, :s0].astype(xr_ref.dtype)
    gr_ref[...] = h[:, s0:s1].astype(gr_ref.dtype)
    for a, u_ref in enumerate(u_refs):
        u_ref[...] = h[:, s4 + a * LANES:s4 + (a + 1) * LANES]
    tkv = vt_ref.shape[2]
    for c in range(vt_ref.shape[0]):
        vt_ref[c] = jnp.transpose(h[c * tkv:(c + 1) * tkv, s3:s4]).astype(vt_ref.dtype)

    ang = pos_ref[...] * invf_ref[...]
    cos = jnp.cos(ang)
    sin = jnp.sin(ang)
    lane = lax.broadcasted_iota(jnp.int32, ang.shape, 1)
    first_half = (lane % head_dim) < (head_dim // 2)
    sin_signed = jnp.where(first_half, -sin, sin)

    def rope(t, scale):
        outs = []
        for a in range(t.shape[1] // LANES):
            xs = t[:, a * LANES:(a + 1) * LANES]
            fwd = pltpu.roll(xs, LANES - head_dim // 2, 1)
            bwd = pltpu.roll(xs, head_dim // 2, 1)
            rot = jnp.where(first_half, fwd, bwd)
            outs.append((xs * cos + rot * sin_signed) * scale)
        return jnp.concatenate(outs, axis=1)

    q_ref[...] = rope(h[:, s1:s2], q_scale).astype(q_ref.dtype)
    k_ref[...] = rope(h[:, s2:s3], 1.0).astype(k_ref.dtype)


def _inproj(x2d, g, posf, invf, w, layer, splits, head_dim, tm, tkv):
    n, d = x2d.shape
    widths = [splits[0]] + [splits[i] - splits[i - 1] for i in range(1, 6)]
    n_slab = widths[5] // LANES
    kern = functools.partial(_inproj_kernel, splits=splits, head_dim=head_dim,
                             q_scale=head_dim ** -0.5 * math.log2(math.e))
    row = lambda i: (i, 0)
    rows = lambda wd: pl.BlockSpec((tm, wd), row)
    out = lambda wd, dt: jax.ShapeDtypeStruct((n, wd), dt)
    return pl.pallas_call(
        kern,
        grid=(n // tm,),
        in_specs=[pl.BlockSpec((tm, d), row), _layer_spec(g.shape, layer), pl.BlockSpec((tm, 1), row),
                  _const_spec((1, LANES)), _layer_spec(w.shape, layer)],
        out_specs=[rows(widths[0]), rows(widths[1]), rows(widths[2]), rows(widths[3]),
                   pl.BlockSpec((tm // tkv, widths[4], tkv), lambda i: (i, 0, 0))] + [rows(LANES)] * n_slab,
        out_shape=[out(widths[0], F32), out(widths[1], BF16), out(widths[2], BF16), out(widths[3], BF16),
                   jax.ShapeDtypeStruct((n // tkv, widths[4], tkv), BF16)] + [out(LANES, F32)] * n_slab,
        compiler_params=_cparams("parallel"),
        name="inproj",
    )(x2d, g, posf, invf, w)


def _rglru_kernel(x_ref, g_ref, pos_ref, cw_ref, cb_ref, w_ref, b_ref, lam_ref, o_ref, halo_ref, h_ref):
    j = pl.program_id(1)

    @pl.when(j == 0)
    def _():
        halo_ref[...] = jnp.zeros_like(halo_ref)
        h_ref[...] = jnp.zeros_like(h_ref)

    x = x_ref[0].astype(F32)
    t, r = x.shape
    halo = halo_ref[...]
    row8 = lax.broadcasted_iota(jnp.int32, (SUBLANES, r), 0)
    cw = cw_ref[0]
    xc = cb_ref[0] + cw[0:1] * x
    for k in range(1, cw.shape[0]):
        rolled = pltpu.roll(x, k, 0)
        first = jnp.where(row8 < k, pltpu.roll(halo, k, 0), rolled[:SUBLANES])
        xc = xc + cw[k:k + 1] * jnp.concatenate([first, rolled[SUBLANES:]], axis=0)
    halo_ref[...] = x[t - SUBLANES:]

    gates = jnp.dot(xc.astype(BF16), w_ref[0], preferred_element_type=F32) + b_ref[0]
    rg = _sigmoid(gates[:, :r])
    ig = _sigmoid(gates[:, r:])
    z = -lam_ref[0]
    softplus = jnp.maximum(z, 0.0) + jnp.log(1.0 + jnp.exp(-jnp.abs(z)))
    log_a = (-RGLRU_C) * rg * softplus
    a = jnp.exp(log_a)
    mult = jnp.sqrt(1.0 - jnp.exp(2.0 * log_a))
    reset = pos_ref[0] == 0.0
    a = jnp.where(reset, 0.0, a)
    mult = jnp.where(reset, 1.0, mult)
    b = mult * ig * xc

    rows = lax.broadcasted_iota(jnp.int32, (t, r), 0)
    d = 1
    while d < t:
        keep = rows >= d
        a_sh = jnp.where(keep, pltpu.roll(a, d, 0), 1.0)
        b_sh = jnp.where(keep, pltpu.roll(b, d, 0), 0.0)
        b = b + a * b_sh
        a = a * a_sh
        d *= 2
    h = b + a * h_ref[...]
    h_ref[...] = h[t - 1:t]
    o_ref[0] = (h * _gelu_tanh(g_ref[0].astype(F32))).astype(o_ref.dtype)


def _rglru(x_rnn, g_rnn, posf, conv_w, conv_b, w_gates, b_gates, lam, layer, tt):
    bsz, seq, r = x_rnn.shape
    blk = lambda b, j: (b, j, 0)
    return pl.pallas_call(
        _rglru_kernel,
        grid=(bsz, seq // tt),
        in_specs=[pl.BlockSpec((1, tt, r), blk), pl.BlockSpec((1, tt, r), blk), pl.BlockSpec((1, tt, 1), blk)]
                 + [_layer_spec(a.shape, layer) for a in (conv_w, conv_b, w_gates, b_gates, lam)],
        out_specs=pl.BlockSpec((1, tt, r), blk),
        out_shape=jax.ShapeDtypeStruct((bsz, seq, r), BF16),
        scratch_shapes=[pltpu.VMEM((SUBLANES, r), F32), pltpu.VMEM((1, r), F32)],
        compiler_params=_cparams("parallel", "arbitrary"),
        name="rglru",
    )(x_rnn, g_rnn, posf, conv_w, conv_b, w_gates, b_gates, lam)


def _attn_kernel(q_ref, k_ref, vt_ref, lamv_ref, sg_ref, o_ref, *, tq, head_dim, lambda_init):
    i = pl.program_id(2)
    hw = 2 * head_dim
    hp = q_ref.shape[2] // hw
    vdim = vt_ref.shape[1] // hp
    lane = lax.broadcasted_iota(jnp.int32, (tq, hw), 1)
    zero = jnp.zeros((tq, hw), q_ref.dtype)
    ones = jnp.ones((2 * SUBLANES, tq), BF16)
    qqs = []
    for a in range(hp):
        q = q_ref[0, :, a * hw:(a + 1) * hw]
        qqs.append(jnp.concatenate([jnp.where(lane < head_dim, q, zero), jnp.where(lane >= head_dim, q, zero)],
                                   axis=0))

    def step(j, carry, masked):
        row0 = pl.multiple_of(j * tq, tq)
        out = []
        for a in range(hp):
            m, acc = carry[2 * a], carry[2 * a + 1]
            kb = k_ref[0, pl.ds(row0, tq), a * hw:(a + 1) * hw]
            s = lax.dot_general(kb, qqs[a], (((1,), (1,)), ((), ())), preferred_element_type=F32)
            if masked:
                key = lax.broadcasted_iota(jnp.int32, s.shape, 0)
                qry = lax.broadcasted_iota(jnp.int32, s.shape, 1)
                qry = jnp.where(qry >= tq, qry - tq, qry)
                s = jnp.where(key <= qry, s, NEG_INF)
            m_new = jnp.maximum(m, jnp.max(s, axis=0, keepdims=True))
            p = jnp.exp2(s - m_new)
            alpha = jnp.exp2(m - m_new)
            vt = jnp.concatenate([vt_ref[j, a * vdim:(a + 1) * vdim, :], ones], axis=0)
            out += [m_new, alpha * acc + jnp.dot(vt, p.astype(BF16), preferred_element_type=F32)]
        return tuple(out)

    init = (jnp.full((1, 2 * tq), NEG_INF, F32), jnp.zeros((vdim + 2 * SUBLANES, 2 * tq), F32)) * hp
    carry = lax.fori_loop(0, i, lambda j, c: step(j, c, False), init)
    carry = step(i, carry, True)

    lamv = lamv_ref[0]
    lam = (jnp.exp(jnp.sum(lamv[0:1] * lamv[1:2], axis=1, keepdims=True))
           - jnp.exp(jnp.sum(lamv[2:3] * lamv[3:4], axis=1, keepdims=True)) + lambda_init)
    for a in range(hp):
        acc = carry[2 * a + 1]
        ot = acc[:vdim] / acc[vdim:vdim + 1]
        o = jnp.transpose(ot[:, :tq] - lam * ot[:, tq:])
        o = _rmsnorm(o, sg_ref[0], SUBLN_EPS) * (1.0 - lambda_init)
        o_ref[0, :, a * vdim:(a + 1) * vdim] = o.astype(o_ref.dtype)


def _diff_attention(q, k, vt, lamv, subln_g, layer, heads, head_dim, lambda_init, tq, hp):
    bsz, seq, _ = q.shape
    vdim = vt.shape[1] // heads
    nkv = seq // tq
    kern = functools.partial(_attn_kernel, tq=tq, head_dim=head_dim, lambda_init=lambda_init)
    return pl.pallas_call(
        kern,
        grid=(bsz, heads // hp, seq // tq),
        in_specs=[pl.BlockSpec((1, tq, hp * 2 * head_dim), lambda b, h, i: (b, i, h)),
                  pl.BlockSpec((1, seq, hp * 2 * head_dim), lambda b, h, i: (b, 0, h)),
                  pl.BlockSpec((nkv, hp * vdim, tq), lambda b, h, i: (b, h, 0)),
                  _layer_spec(lamv.shape, layer), _layer_spec(subln_g.shape, layer)],
        out_specs=pl.BlockSpec((1, tq, hp * vdim), lambda b, h, i: (b, i, h)),
        out_shape=jax.ShapeDtypeStruct((bsz, seq, heads * vdim), BF16),
        compiler_params=_cparams("parallel", "parallel", "arbitrary"),
        name="diff_attn",
    )(q, k, vt, lamv, subln_g)


def _s5_tables(lam_re, lam_im, b_re, b_im, c_re, c_im, d_skip, log_dt, n_steps):
    tc = S5_CHUNK
    g, n, p = b_re.shape
    lr = lam_re.astype(F32)
    li = lam_im.astype(F32)
    dt = jnp.exp(log_dt.astype(F32))[:, None]
    mag = jnp.exp(lr * dt)
    ar = mag * jnp.cos(li * dt)
    ai = mag * jnp.sin(li * dt)
    den = lr * lr + li * li
    cr = ((ar - 1.0) * lr + ai * li) / den
    ci = (ai * lr - (ar - 1.0) * li) / den
    bb_re = cr[..., None] * b_re - ci[..., None] * b_im
    bb_im = cr[..., None] * b_im + ci[..., None] * b_re

    def apow(e):
        e = jnp.asarray(e, F32)[None, None, :]
        m = jnp.exp(e * (lr * dt)[..., None])
        ph = e * (li * dt)[..., None]
        return m * jnp.cos(ph), m * jnp.sin(ph)

    lags = jnp.arange(tc)
    pw_re, pw_im = apow(lags)
    ab_re = pw_re[..., None] * bb_re[:, :, None, :] - pw_im[..., None] * bb_im[:, :, None, :]
    ab_im = pw_re[..., None] * bb_im[:, :, None, :] + pw_im[..., None] * bb_re[:, :, None, :]
    abt_re = ab_re.transpose(0, 1, 3, 2).reshape(g, n, 1, p * tc)
    abt_im = ab_im.transpose(0, 1, 3, 2).reshape(g, n, 1, p * tc)
    ct_re = c_re.transpose(0, 2, 1)[..., None]
    ct_im = c_im.transpose(0, 2, 1)[..., None]
    kl = jnp.sum(ct_re * abt_re - ct_im * abt_im, axis=1).reshape(g, p, p, tc)
    skip = jnp.eye(p, dtype=F32)[None] * d_skip[:, None, :]
    kl = jnp.concatenate([kl[..., :1] + skip[..., None], kl[..., 1:]], axis=-1)
    seq = kl.transpose(0, 1, 3, 2)[:, :, ::-1, :].reshape(g, p, tc * p)
    seq = jnp.concatenate([seq, jnp.zeros((g, p, (tc - 1) * p), F32)], axis=-1)
    rows = [seq[:, :, (tc - 1 - t) * p:(tc - 1 - t) * p + tc * p] for t in range(tc)]
    kt = jnp.stack(rows, axis=1).reshape(g, tc * p, tc * p)
    bt = jnp.concatenate([ab_re[:, :, ::-1, :].reshape(g, n, tc * p), ab_im[:, :, ::-1, :].reshape(g, n, tc * p)],
                         axis=1)
    p1_re, p1_im = apow(lags + 1)
    ca_re = jnp.einsum('gon,gnt->gton', c_re, p1_re) - jnp.einsum('gon,gnt->gton', c_im, p1_im)
    ca_im = jnp.einsum('gon,gnt->gton', c_re, p1_im) + jnp.einsum('gon,gnt->gton', c_im, p1_re)
    ct = jnp.concatenate([ca_re.reshape(g, tc * p, n), -ca_im.reshape(g, tc * p, n)], axis=2)
    st_re, st_im = apow(tc * (2 ** jnp.arange(n_steps)))
    tab = jnp.stack([st_re.transpose(0, 2, 1), st_im.transpose(0, 2, 1)], axis=2)
    tab = tab.reshape(g // 2, 2, 2 * n_steps, n).transpose(0, 2, 1, 3).reshape(g // 2, 2 * n_steps, 2 * n)
    tab = jnp.pad(tab, ((0, 0), (0, -(2 * n_steps) % SUBLANES), (0, 0)))
    return kt.astype(BF16), bt.astype(BF16), ct.astype(BF16), tab.astype(F32)


def _s5_kernel(*refs, n_groups, p, n_slab):
    u_refs = refs[:n_slab]
    kt_ref, bt_ref, ct_ref, tab_ref = refs[n_slab:n_slab + 4]
    o_refs = refs[n_slab + 4:2 * n_slab + 4]
    ut_ref, y_ref, carry_ref = refs[2 * n_slab + 4:]
    tc = S5_CHUNK
    c = u_refs[0].shape[1] // tc
    n2 = bt_ref.shape[2]
    half = n2 // 2
    n_steps = int(math.log2(c))

    @pl.when(pl.program_id(1) == 0)
    def _():
        carry_ref[...] = jnp.zeros_like(carry_ref)

    for k in range(tc):
        for a in range(n_slab):
            ut_ref[k, a * LANES:(a + 1) * LANES, :] = jnp.transpose(
                u_refs[a][0, pl.ds(k, c, stride=tc), :]).astype(BF16)

    row = lax.broadcasted_iota(jnp.int32, (c, n2), 0)

    def shift(x, d, fill):
        return jnp.where(row >= d, pltpu.roll(x, d, 0), fill)

    def pair(gp, _):
        gs = (2 * gp, 2 * gp + 1)
        local = []
        for g in gs:
            ug = jnp.concatenate([ut_ref[k, pl.ds(pl.multiple_of(g * p, p), p), :] for k in range(tc)], axis=0)
            y_ref[g] = jnp.dot(kt_ref[0, g], ug, preferred_element_type=F32)
            local.append(jnp.dot(bt_ref[0, g], ug, preferred_element_type=F32))
        sr = jnp.transpose(jnp.concatenate([local[0][:half], local[1][:half]], axis=0))
        si = jnp.transpose(jnp.concatenate([local[0][half:], local[1][half:]], axis=0))
        tab = tab_ref[0, gp]
        cin_r = carry_ref[gp, 0:1, :]
        cin_i = carry_ref[gp, 1:2, :]
        sr = sr + jnp.where(row == 0, tab[0:1] * cin_r - tab[1:2] * cin_i, 0.0)
        si = si + jnp.where(row == 0, tab[0:1] * cin_i + tab[1:2] * cin_r, 0.0)
        for s in range(n_steps):
            d = 1 << s
            ar, ai = tab[2 * s:2 * s + 1], tab[2 * s + 1:2 * s + 2]
            hr, hi = shift(sr, d, 0.0), shift(si, d, 0.0)
            sr, si = sr + ar * hr - ai * hi, si + ar * hi + ai * hr
        carry_ref[gp, 0:1, :] = sr[c - 1:c]
        carry_ref[gp, 1:2, :] = si[c - 1:c]
        pr = jnp.transpose(shift(sr, 1, cin_r))
        pi = jnp.transpose(shift(si, 1, cin_i))
        for idx, g in enumerate(gs):
            prev = jnp.concatenate([pr[idx * half:(idx + 1) * half], pi[idx * half:(idx + 1) * half]], axis=0)
            y_ref[g] = y_ref[g] + jnp.dot(ct_ref[0, g], prev.astype(BF16), preferred_element_type=F32)
        return 0

    lax.fori_loop(0, n_groups // 2, pair, 0, unroll=2)

    groups_per_slab = LANES // p
    for t in range(tc):
        for a in range(n_slab):
            yt = jnp.concatenate([y_ref[g, t * p:(t + 1) * p, :]
                                  for g in range(a * groups_per_slab, (a + 1) * groups_per_slab)], axis=0)
            o_refs[a][0, pl.ds(t, c, stride=tc), :] = jnp.transpose(yt)


def _s5(u_slabs, tables, layer, c_lanes):
    kt, bt, ct, tab = tables
    n_slab = len(u_slabs)
    bsz, seq, _ = u_slabs[0].shape
    width = n_slab * LANES
    tc = S5_CHUNK
    n_groups, n2 = bt.shape[1], bt.shape[2]
    p = width // n_groups
    rows = c_lanes * tc
    kern = functools.partial(_s5_kernel, n_groups=n_groups, p=p, n_slab=n_slab)
    slab = pl.BlockSpec((1, rows, LANES), lambda b, j: (b, j, 0))
    return pl.pallas_call(
        kern,
        grid=(bsz, seq // rows),
        in_specs=[slab] * n_slab + [_layer_spec(t.shape, layer) for t in (kt, bt, ct, tab)],
        out_specs=[slab] * n_slab,
        out_shape=[jax.ShapeDtypeStruct((bsz, seq, LANES), F32)] * n_slab,
        scratch_shapes=[pltpu.VMEM((tc, width, c_lanes), BF16), pltpu.VMEM((n_groups, tc * p, c_lanes), F32),
                        pltpu.VMEM((n_groups // 2, SUBLANES, n2), F32)],
        compiler_params=_cparams("parallel", "arbitrary"),
        name="s5",
    )(*u_slabs, kt, bt, ct, tab)


def _merge_kernel(*refs, n_groups, n_experts, d_model, n_slab):
    x_ref, yr_ref, ya_ref = refs[:3]
    ys_refs = refs[3:3 + n_slab]
    (mg_ref, wg_ref, gw_ref, gb_ref, pr_ref, pa_ref, ps_ref, wo_ref, fg_ref, rw_ref, rb_ref,
     x1_ref, xn_ref, route_ref, routet_ref, tab_ref, cnt_ref, run_ref) = refs[3 + n_slab:]
    i = pl.program_id(0)

    @pl.when(i == 0)
    def _():
        run_ref[...] = jnp.zeros_like(run_ref)

    x = x_ref[...]
    tm = x.shape[0]
    xn = _rmsnorm(x, mg_ref[0], MIX_EPS)
    gates = _sigmoid(jnp.dot(xn.astype(BF16), wg_ref[0], preferred_element_type=F32))
    z = _gelu_tanh(jnp.concatenate([ref[...] for ref in ys_refs], axis=1))
    ys = z * _sigmoid(jnp.dot(z.astype(BF16), gw_ref[0], preferred_element_type=F32) + gb_ref[0])
    merged = (gates[:, :d_model] * jnp.dot(yr_ref[...], pr_ref[0], preferred_element_type=F32)
              + gates[:, d_model:2 * d_model] * jnp.dot(ya_ref[...], pa_ref[0], preferred_element_type=F32)
              + gates[:, 2 * d_model:] * jnp.dot(ys.astype(BF16), ps_ref[0], preferred_element_type=F32))
    x1 = x + jnp.dot(merged.astype(BF16), wo_ref[0], preferred_element_type=F32)
    x1_ref[...] = x1
    xn2 = _rmsnorm(x1, fg_ref[0], MIX_EPS)
    xn_ref[...] = xn2.astype(xn_ref.dtype)

    rw = rw_ref[0]
    x_hi = xn2.astype(BF16)
    x_lo = (xn2 - x_hi.astype(F32)).astype(BF16)
    w_hi = rw.astype(BF16)
    w_lo = (rw - w_hi.astype(F32)).astype(BF16)
    logits = (jnp.dot(x_hi, w_hi, preferred_element_type=F32) + jnp.dot(x_lo, w_hi, preferred_element_type=F32)
              + jnp.dot(x_hi, w_lo, preferred_element_type=F32) + rb_ref[0])
    lane = lax.broadcasted_iota(jnp.int32, logits.shape, 1).astype(F32)
    big = float(LANES)
    coarse = jnp.where(lane < n_groups, logits, NEG_INF)
    cmax = jnp.max(coarse, axis=1, keepdims=True)
    gsel = jnp.min(jnp.where(coarse == cmax, lane, big), axis=1, keepdims=True)
    p_sel = 1.0 / jnp.sum(jnp.where(lane < n_groups, jnp.exp(logits - cmax), 0.0), axis=1, keepdims=True)
    lo = n_groups + gsel * n_experts
    fine = jnp.where((lane >= lo) & (lane < lo + n_experts), logits, NEG_INF)
    m1 = jnp.max(fine, axis=1, keepdims=True)
    i1 = jnp.min(jnp.where(fine == m1, lane, big), axis=1, keepdims=True)
    fine2 = jnp.where(lane == i1, NEG_INF, fine)
    m2 = jnp.max(fine2, axis=1, keepdims=True)
    i2 = jnp.min(jnp.where(fine2 == m2, lane, big), axis=1, keepdims=True)
    e21 = jnp.exp(m2 - m1)
    w1 = p_sel / (1.0 + e21)
    w2 = p_sel * e21 / (1.0 + e21)
    oh1 = lane == i1
    oh2 = lane == i2
    onehot = jnp.where(oh1 | oh2, 1.0, 0.0)
    r_i = lax.broadcasted_iota(jnp.int32, (tm, tm), 0)
    c_i = lax.broadcasted_iota(jnp.int32, (tm, tm), 1)
    earlier = jnp.where(c_i < r_i, 1.0, 0.0).astype(BF16)
    rank = jnp.dot(earlier, onehot.astype(BF16), preferred_element_type=F32)
    cnt = jnp.sum(onehot, axis=0, keepdims=True)
    cnt = jnp.floor((cnt + (SUBLANES - 1)) * (1.0 / SUBLANES)) * SUBLANES
    k_i = lax.broadcasted_iota(jnp.int32, (LANES, LANES), 0)
    l_i = lax.broadcasted_iota(jnp.int32, (LANES, LANES), 1)
    lower = jnp.where(k_i < l_i, 1.0, 0.0)
    start = jnp.dot(jnp.broadcast_to(cnt, (SUBLANES, LANES)), lower, preferred_element_type=F32,
                    precision=lax.Precision.HIGHEST)[0:1]
    pos = rank + start
    lp1 = jnp.sum(jnp.where(oh1, pos, 0.0), axis=1, keepdims=True)
    lp2 = jnp.sum(jnp.where(oh2, pos, 0.0), axis=1, keepdims=True)
    route = jnp.where(lane == 0, w1, 0.0)
    route = jnp.where(lane == 1, w2, route)
    route = jnp.where(lane == 2, lp1, route)
    route = jnp.where(lane == 3, lp2, route)
    route_ref[...] = route
    routet_ref[...] = jnp.transpose(route)[:SUBLANES]
    sub = lax.broadcasted_iota(jnp.int32, (SUBLANES, LANES), 0)
    tab_ref[...] = jnp.where(sub == 0, cnt, jnp.where(sub == 1, start, jnp.where(sub == 2, run_ref[...], 0.0)))
    run_ref[...] = run_ref[...] + cnt
    cnt_ref[...] = run_ref[...]


def _merge(x2d, y_rnn, y_attn, y_s5_slabs, mix_g, w_gate, glu_w, glu_b, p_rnn, p_attn, p_ssm, w_out, ffn_g, rw, rb,
           layer, n_groups, n_experts, tm):
    n, d = x2d.shape
    row = lambda i: (i, 0)
    n_slab = len(y_s5_slabs)
    kern = functools.partial(_merge_kernel, n_groups=n_groups, n_experts=n_experts, d_model=d, n_slab=n_slab)
    consts = [mix_g, w_gate, glu_w, glu_b, p_rnn, p_attn, p_ssm, w_out, ffn_g, rw, rb]
    return pl.pallas_call(
        kern,
        grid=(n // tm,),
        in_specs=[pl.BlockSpec((tm, d), row), pl.BlockSpec((tm, y_rnn.shape[1]), row),
                  pl.BlockSpec((tm, y_attn.shape[1]), row)] + [pl.BlockSpec((tm, LANES), row)] * n_slab
                 + [_layer_spec(a.shape, layer) for a in consts],
        out_specs=[pl.BlockSpec((tm, d), row), pl.BlockSpec((tm, d), row), pl.BlockSpec((tm, LANES), row),
                   pl.BlockSpec((SUBLANES, tm), lambda i: (0, i)), pl.BlockSpec((SUBLANES, LANES), row),
                   pl.BlockSpec((1, LANES), lambda i: (0, 0))],
        out_shape=[jax.ShapeDtypeStruct((n, d), F32), jax.ShapeDtypeStruct((n, d), BF16),
                   jax.ShapeDtypeStruct((n, LANES), F32), jax.ShapeDtypeStruct((SUBLANES, n), F32),
                   jax.ShapeDtypeStruct((n // tm * SUBLANES, LANES), F32), jax.ShapeDtypeStruct((1, LANES), F32)],
        scratch_shapes=[pltpu.VMEM((1, LANES), F32)],
        compiler_params=_cparams("arbitrary"),
        name="merge_router",
    )(x2d, y_rnn, y_attn, *y_s5_slabs, *consts)


def _sorted_rows(tm, n_total):
    rows = TOP_K_FINE * tm + n_total * (SUBLANES - 1)
    return -(-rows // LANES) * LANES


RUN_BITS = 6
TILE_BITS = 3


def _run_copies(base, n_total, cnt_ref, loc_ref, dst_ref, local_ref, remote_ref, sem, to_remote, wait):
    def copy(e, off, size):
        lstart = 0 if loc_ref is None else pl.multiple_of(loc_ref[base + e] + off, SUBLANES)
        local = local_ref.at[pl.ds(lstart, size)]
        remote = remote_ref.at[pl.ds(pl.multiple_of(dst_ref[base + e] + off, SUBLANES), size)]
        desc = pltpu.make_async_copy(local, remote, sem) if to_remote else pltpu.make_async_copy(remote, local, sem)
        if wait:
            desc.wait()
        else:
            desc.start()

    def expert(e, _):
        cnt = cnt_ref[base + e]
        big = 1 << RUN_BITS

        def chunk(c, _):
            copy(e,c * big, big)
            return 0

        n_big = cnt >> RUN_BITS
        lax.fori_loop(0, n_big, chunk, 0)
        off = n_big * big
        for b in reversed(range(TILE_BITS, RUN_BITS)):
            size = 1 << b

            @pl.when((cnt & size) != 0)
            def _(off=off, size=size):
                copy(e,off, size)

            off = off + (cnt & size)
        return 0

    lax.fori_loop(0, n_total, expert, 0)


def _dispatch_kernel(cnt_ref, loc_ref, dst_ref, gap_cnt_ref, gap_dst_ref, x_ref, rt_ref, xs_ref, buf_ref, zero_ref,
                     sem, *, n_total):
    i = pl.program_id(0)
    last = pl.num_programs(0) - 1
    slot = i % 2
    tm = x_ref.shape[0]
    rows = buf_ref.shape[1]
    pos = rt_ref[...]
    j = lax.broadcasted_iota(jnp.int32, (rows, tm), 0).astype(F32)
    sel = jnp.where((j == pos[2:3]) | (j == pos[3:4]), 1.0, 0.0).astype(BF16)
    buf_ref[slot] = jnp.dot(sel, x_ref[...], preferred_element_type=F32)

    def copies(tile, tile_slot, wait):
        _run_copies(tile * n_total, n_total, cnt_ref, loc_ref, dst_ref, buf_ref.at[tile_slot], xs_ref,
                    sem.at[tile_slot], True, wait)

    copies(i, slot, False)

    @pl.when(i > 0)
    def _():
        copies(i - 1, 1 - slot, True)

    @pl.when(i == last)
    def _():
        copies(i, slot, True)
        zero_ref[...] = jnp.zeros_like(zero_ref)
        for wait in (False, True):
            _run_copies(0, gap_cnt_ref.shape[0], gap_cnt_ref, None, gap_dst_ref, zero_ref, xs_ref, sem.at[0], True,
                        wait)


def _dispatch(tile_cnt, tile_loc, tile_dst, gap_cnt, gap_dst, xn2, route_t, n_rows, n_total, tm):
    n, d = xn2.shape
    grid_spec = pltpu.PrefetchScalarGridSpec(
        num_scalar_prefetch=5,
        grid=(n // tm,),
        in_specs=[pl.BlockSpec((tm, d), lambda i, *_: (i, 0)),
                  pl.BlockSpec((SUBLANES, tm), lambda i, *_: (0, i))],
        out_specs=pl.BlockSpec(memory_space=pl.ANY),
        scratch_shapes=[pltpu.VMEM((2, _sorted_rows(tm, n_total), d), F32), pltpu.VMEM((1 << RUN_BITS, d), F32),
                        pltpu.SemaphoreType.DMA((2,))],
    )
    return pl.pallas_call(
        functools.partial(_dispatch_kernel, n_total=n_total),
        grid_spec=grid_spec,
        out_shape=jax.ShapeDtypeStruct((n_rows, d), F32),
        compiler_params=_cparams("arbitrary"),
        name="moe_dispatch",
    )(tile_cnt, tile_loc, tile_dst, gap_cnt, gap_dst, xn2, route_t)


def _experts_kernel(te_ref, nact_ref, x_ref, w1_ref, w3_ref, w2_ref, o_ref, w1b_ref, w3b_ref, w2b_ref):
    i = pl.program_id(0)
    fresh = jnp.logical_or(i == 0, te_ref[i] != te_ref[jnp.maximum(i - 1, 0)])

    @pl.when(jnp.logical_and(i < nact_ref[0], fresh))
    def _():
        w1b_ref[...] = w1_ref[0].astype(BF16)
        w3b_ref[...] = w3_ref[0].astype(BF16)
        w2b_ref[...] = w2_ref[0].astype(BF16)

    @pl.when(i < nact_ref[0])
    def _():
        xb = x_ref[...].astype(BF16)
        h1 = jnp.dot(xb, w1b_ref[...], preferred_element_type=F32)
        h3 = jnp.dot(xb, w3b_ref[...], preferred_element_type=F32)
        hid = h1 * _sigmoid(h1) * h3
        o_ref[...] = jnp.dot(hid.astype(BF16), w2b_ref[...], preferred_element_type=F32).astype(o_ref.dtype)

    @pl.when(i >= nact_ref[0])
    def _():
        o_ref[...] = jnp.zeros_like(o_ref)


def _experts(tile_expert, n_active, xs, w1, w3, w2, tm):
    n_rows, d = xs.shape
    f = w1.shape[2]
    row = lambda i, te, na: (jnp.minimum(i, na[0] - 1), 0)
    wsel = lambda i, te, na: (te[i], 0, 0)
    grid_spec = pltpu.PrefetchScalarGridSpec(
        num_scalar_prefetch=2,
        grid=(n_rows // tm,),
        in_specs=[pl.BlockSpec((tm, d), row), pl.BlockSpec((1, d, f), wsel), pl.BlockSpec((1, d, f), wsel),
                  pl.BlockSpec((1, f, d), wsel)],
        out_specs=pl.BlockSpec((tm, d), lambda i, te, na: (i, 0)),
        scratch_shapes=[pltpu.VMEM((d, f), BF16), pltpu.VMEM((d, f), BF16), pltpu.VMEM((f, d), BF16)],
    )
    return pl.pallas_call(
        _experts_kernel,
        grid_spec=grid_spec,
        out_shape=jax.ShapeDtypeStruct((n_rows, d), F32),
        compiler_params=_cparams("arbitrary"),
        name="moe_experts",
    )(tile_expert, n_active, xs, w1, w3, w2)


def _combine_kernel(cnt_ref, loc_ref, dst_ref, x_ref, route_ref, rt_ref, fg_ref, ys_ref, o_ref, buf_ref, sem,
                    *, n_total, final_norm):
    i = pl.program_id(0)
    slot = i % 2
    tm = x_ref.shape[0]
    rows = buf_ref.shape[1]

    def copies(tile, tile_slot, wait):
        _run_copies(tile * n_total, n_total, cnt_ref, loc_ref, dst_ref, buf_ref.at[tile_slot], ys_ref,
                    sem.at[tile_slot], False, wait)

    @pl.when(i == 0)
    def _():
        buf_ref[...] = jnp.zeros_like(buf_ref)
        copies(0, 0, False)

    @pl.when(i + 1 < pl.num_programs(0))
    def _():
        copies(i + 1, 1 - slot, False)

    copies(i, slot, True)
    rt = rt_ref[...]
    jr = lax.broadcasted_iota(jnp.int32, (rows, tm), 0).astype(F32)
    gate = jnp.sum(jnp.where(jr == rt[2:3], rt[0:1], 0.0) + jnp.where(jr == rt[3:4], rt[1:2], 0.0),
                   axis=1, keepdims=True)
    yb = (buf_ref[slot] * gate).astype(BF16)
    route = route_ref[...]
    jc = lax.broadcasted_iota(jnp.int32, (tm, rows), 1).astype(F32)
    pick = jnp.where((jc == route[:, 2:3]) | (jc == route[:, 3:4]), 1.0, 0.0).astype(BF16)
    out = x_ref[...] + jnp.dot(pick, yb, preferred_element_type=F32)
    if final_norm:
        out = _rmsnorm(out, fg_ref[...], MIX_EPS)
    o_ref[...] = out


def _combine(tile_cnt, tile_loc, tile_dst, x1, route, route_t, final_g, ys, n_total, tm, final_norm):
    n, d = x1.shape
    kern = functools.partial(_combine_kernel, n_total=n_total, final_norm=final_norm)
    grid_spec = pltpu.PrefetchScalarGridSpec(
        num_scalar_prefetch=3,
        grid=(n // tm,),
        in_specs=[pl.BlockSpec((tm, d), lambda i, *_: (i, 0)),
                  pl.BlockSpec((tm, LANES), lambda i, *_: (i, 0)),
                  pl.BlockSpec((SUBLANES, tm), lambda i, *_: (0, i)),
                  pl.BlockSpec((1, d), lambda i, *_: (0, 0)),
                  pl.BlockSpec(memory_space=pl.ANY)],
        out_specs=pl.BlockSpec((tm, d), lambda i, *_: (i, 0)),
        scratch_shapes=[pltpu.VMEM((2, _sorted_rows(tm, n_total), d), F32), pltpu.SemaphoreType.DMA((2,))],
    )
    return pl.pallas_call(
        kern,
        grid_spec=grid_spec,
        out_shape=jax.ShapeDtypeStruct((n, d), F32),
        compiler_params=_cparams("arbitrary"),
        name="moe_combine",
    )(tile_cnt, tile_loc, tile_dst, x1, route, route_t, final_g, ys)


def _tile_plan(n, seq):
    return dict(inproj=min(1024, n), rglru=min(256, seq), attn=min(512, seq), attn_heads=2,
                s5_lanes=min(LANES, seq // S5_CHUNK), merge=min(512, n), moe=min(384, n))


def kernel(x, positions, mix_norm_g, w_in, conv_w, conv_b, rg_wa, rg_ba, rg_wx, rg_bx, rg_lambda,
           lam_q1, lam_k1, lam_q2, lam_k2, subln_g,
           ssm_lambda_re, ssm_lambda_im, ssm_b_re, ssm_b_im, ssm_c_re, ssm_c_im, ssm_d, ssm_log_dt,
           ssm_glu_w, ssm_glu_b, proj_rnn, proj_attn, proj_ssm, w_out,
           ffn_norm_g, router_coarse_w, router_coarse_b, router_fine_w, router_fine_b,
           expert_w1, expert_w3, expert_w2, final_norm_g):
    return _forward(_tile_plan(x.shape[0] * x.shape[1], x.shape[1]),
                    x, positions, mix_norm_g, w_in, conv_w, conv_b, rg_wa, rg_ba, rg_wx, rg_bx, rg_lambda,
                    lam_q1, lam_k1, lam_q2, lam_k2, subln_g,
                    ssm_lambda_re, ssm_lambda_im, ssm_b_re, ssm_b_im, ssm_c_re, ssm_c_im, ssm_d, ssm_log_dt,
                    ssm_glu_w, ssm_glu_b, proj_rnn, proj_attn, proj_ssm, w_out,
                    ffn_norm_g, router_coarse_w, router_coarse_b, router_fine_w, router_fine_b,
                    expert_w1, expert_w3, expert_w2, final_norm_g)


def _forward(tiles, x, positions, mix_norm_g, w_in, conv_w, conv_b, rg_wa, rg_ba, rg_wx, rg_bx, rg_lambda,
             lam_q1, lam_k1, lam_q2, lam_k2, subln_g,
             ssm_lambda_re, ssm_lambda_im, ssm_b_re, ssm_b_im, ssm_c_re, ssm_c_im, ssm_d, ssm_log_dt,
             ssm_glu_w, ssm_glu_b, proj_rnn, proj_attn, proj_ssm, w_out,
             ffn_norm_g, router_coarse_w, router_coarse_b, router_fine_w, router_fine_b,
             expert_w1, expert_w3, expert_w2, final_norm_g):
    bsz, seq, d_model = x.shape
    depth = w_in.shape[0]
    n = bsz * seq
    r = conv_w.shape[2]
    sw = ssm_glu_w.shape[1]
    vdim = subln_g.shape[1]
    head_dim = vdim // 2
    in_cols = w_in.shape[2]
    qk = (in_cols - 2 * r - sw - 3 * d_model) // 3
    heads = qk // (2 * head_dim)
    splits = (r, 2 * r, 2 * r + qk, 2 * r + 2 * qk, 2 * r + 3 * qk, 2 * r + 3 * qk + sw)
    mix_cols = splits[-1]
    n_groups = router_coarse_w.shape[2]
    n_experts = expert_w1.shape[2]
    n_total = n_groups * n_experts
    rnn_blocks = rg_wa.shape[1]

    tm_in, tt_rnn, tq = tiles["inproj"], tiles["rglru"], tiles["attn"]
    s5_lanes, tm_merge, tm_moe = tiles["s5_lanes"], tiles["merge"], tiles["moe"]
    n_rows = TOP_K_FINE * n + (n // tm_merge) * n_total * (SUBLANES - 1) + n_total * tm_moe
    n_rows = -(-n_rows // tm_moe) * tm_moe

    posf = positions.astype(F32)
    pos_col = posf.reshape(n, 1)
    pos_blk = posf.reshape(bsz, seq, 1)
    inv_freq = ROPE_THETA ** (-jnp.arange(0, head_dim, 2, dtype=F32) / head_dim)
    invf = jnp.tile(inv_freq, LANES // (head_dim // 2)).reshape(1, LANES)

    row3 = lambda a: a.reshape(depth, 1, a.shape[-1])
    w_mix = w_in[:, :, :mix_cols].astype(BF16)
    w_gate = w_in[:, :, mix_cols:].astype(BF16)
    eye_blocks = jnp.eye(rnn_blocks, dtype=F32)
    block_diag = lambda w: jnp.einsum('lhij,hk->lhikj', w, eye_blocks).reshape(depth, r, r)
    w_gates = jnp.concatenate([block_diag(rg_wa), block_diag(rg_wx)], axis=2).astype(BF16)
    b_gates = row3(jnp.concatenate([rg_ba, rg_bx], axis=1))
    lamv = jnp.stack([lam_q1, lam_k1, lam_q2, lam_k2], axis=1)
    tables = jax.vmap(functools.partial(_s5_tables, n_steps=int(math.log2(s5_lanes))))(
        ssm_lambda_re, ssm_lambda_im, ssm_b_re, ssm_b_im, ssm_c_re, ssm_c_im, ssm_d, ssm_log_dt)
    rw = jnp.concatenate([router_coarse_w, router_fine_w], axis=2)
    rw = jnp.pad(rw, ((0, 0), (0, 0), (0, LANES - rw.shape[2])))
    rb = jnp.concatenate([router_coarse_b, router_fine_b], axis=1)
    rb = row3(jnp.pad(rb, ((0, 0), (0, LANES - rb.shape[1]))))
    merge_params = (row3(mix_norm_g), w_gate, ssm_glu_w.astype(BF16), row3(ssm_glu_b), proj_rnn.astype(BF16),
                    proj_attn.astype(BF16), proj_ssm.astype(BF16), w_out.astype(BF16), row3(ffn_norm_g), rw, rb)

    x2d = x.reshape(n, d_model)
    for l in range(depth):
        lambda_init = 0.8 - 0.6 * math.exp(-0.3 * l)
        x_rnn, g_rnn, q, k, vt, *u_slabs = _inproj(x2d, row3(mix_norm_g), pos_col, invf, w_mix, l,
                                                   splits, head_dim, tm_in, tq)
        y_rnn = _rglru(x_rnn.reshape(bsz, seq, r), g_rnn.reshape(bsz, seq, r), pos_blk, conv_w, row3(conv_b),
                       w_gates, b_gates, row3(rg_lambda), l, tt_rnn)
        y_attn = _diff_attention(q.reshape(bsz, seq, qk), k.reshape(bsz, seq, qk), vt, lamv, row3(subln_g), l,
                                 heads, head_dim, lambda_init, tq, min(tiles["attn_heads"], heads))
        y_s5 = _s5([u.reshape(bsz, seq, LANES) for u in u_slabs], tables, l, s5_lanes)
        x1, xn2, route, route_t, tile_tab, counts = _merge(
            x2d, y_rnn.reshape(n, r), y_attn.reshape(n, qk), [y.reshape(n, LANES) for y in y_s5],
            *merge_params, l, n_groups, n_experts, tm_merge)

        cnt = counts[0, n_groups:n_groups + n_total].astype(jnp.int32)
        n_tiles = (cnt + tm_moe - 1) // tm_moe
        tile_end = jnp.cumsum(n_tiles)
        offsets = (tile_end - n_tiles) * tm_moe
        tab = tile_tab.reshape(n // tm_merge, SUBLANES, LANES)[:, :, n_groups:n_groups + n_total].astype(jnp.int32)
        tile_cnt = tab[:, 0].reshape(-1)
        tile_loc = tab[:, 1].reshape(-1)
        tile_dst = (tab[:, 2] + offsets[None, :]).reshape(-1)
        n_active = tile_end[-1:]
        gap_dst = jnp.concatenate([offsets + cnt, n_active * tm_moe])
        gap_cnt = jnp.concatenate([tile_end * tm_moe, jnp.full((1,), n_rows, jnp.int32)]) - gap_dst
        tile_ids = jnp.minimum(jnp.arange(n_rows // tm_moe, dtype=jnp.int32), n_active[0] - 1)
        tile_expert = jnp.sum((tile_ids[:, None] >= tile_end[None, :]).astype(jnp.int32), axis=1)

        xs = _dispatch(tile_cnt, tile_loc, tile_dst, gap_cnt, gap_dst, xn2, route_t, n_rows, n_total, tm_merge)
        ys = _experts(tile_expert + l * n_total, n_active.astype(jnp.int32), xs,
                      expert_w1.reshape(depth * n_total, d_model, -1), expert_w3.reshape(depth * n_total, d_model, -1),
                      expert_w2.reshape(depth * n_total, -1, d_model), tm_moe)
        x2d = _combine(tile_cnt, tile_loc, tile_dst, x1, route, route_t, final_norm_g.reshape(1, d_model), ys,
                       n_total, tm_merge, l == depth - 1)
    return x2d.reshape(bsz, seq, d_model)
```

```python
import functools
import math

import jax
import jax.numpy as jnp
from jax import lax
from jax.experimental import pallas as pl
from jax.experimental.pallas import tpu as pltpu

F32 = jnp.float32
BF16 = jnp.bfloat16

RGLRU_C = 8.0
ROPE_THETA = 10000.0
TOP_K_FINE = 2
NEG_INF = -1e30
MIX_EPS = 1e-6
SUBLN_EPS = 1e-5

LANES = 128
SUBLANES = 8
VMEM_LIMIT_BYTES = 56 * 1024 * 1024

S5_CHUNK = 16


def _cparams(*sem):
    return pltpu.CompilerParams(dimension_semantics=sem, vmem_limit_bytes=VMEM_LIMIT_BYTES)


def _const_spec(shape):
    nd = len(shape)
    return pl.BlockSpec(shape, lambda *_: (0,) * nd, pipeline_mode=pl.Buffered(1))


def _layer_spec(shape, layer):
    nd = len(shape)
    return pl.BlockSpec((1,) + tuple(shape[1:]), lambda *_: (layer,) + (0,) * (nd - 1), pipeline_mode=pl.Buffered(1))


def _gelu_tanh(x):
    return 0.5 * x * (1.0 + jnp.tanh(math.sqrt(2.0 / math.pi) * (x + 0.044715 * (x * x * x))))


def _sigmoid(x):
    return 0.5 + 0.5 * jnp.tanh(0.5 * x)


def _rmsnorm(x, g, eps):
    return x * lax.rsqrt(jnp.mean(x * x, axis=-1, keepdims=True) + eps) * g


def _inproj_kernel(x_ref, g_ref, pos_ref, invf_ref, w_ref, q_ref, k_ref, vt_ref, *slab_refs,
                   splits, slab_cols, head_dim, q_scale):
    x = x_ref[...]
    xn = _rmsnorm(x, g_ref[0], MIX_EPS)
    h = jnp.dot(xn.astype(BF16), w_ref[0], preferred_element_type=F32)
    s0, s1, s2, s3, s4, s5 = splits
    for col, ref in zip(slab_cols, slab_refs):
        ref[...] = h[:, col:col + LANES]
    tkv = vt_ref.shape[2]
    for c in range(vt_ref.shape[0]):
        vt_ref[c] = jnp.transpose(h[c * tkv:(c + 1) * tkv, s3:s4]).astype(vt_ref.dtype)

    ang = pos_ref[...] * invf_ref[...]
    cos = jnp.cos(ang)
    sin = jnp.sin(ang)
    lane = lax.broadcasted_iota(jnp.int32, ang.shape, 1)
    first_half = (lane % head_dim) < (head_dim // 2)
    sin_signed = jnp.where(first_half, -sin, sin)

    def rope(t, scale):
        outs = []
        for a in range(t.shape[1] // LANES):
            xs = t[:, a * LANES:(a + 1) * LANES]
            fwd = pltpu.roll(xs, LANES - head_dim // 2, 1)
            bwd = pltpu.roll(xs, head_dim // 2, 1)
            rot = jnp.where(first_half, fwd, bwd)
            outs.append((xs * cos + rot * sin_signed) * scale)
        return jnp.concatenate(outs, axis=1)

    q_ref[...] = rope(h[:, s1:s2], q_scale).astype(q_ref.dtype)
    k_ref[...] = rope(h[:, s2:s3], 1.0).astype(k_ref.dtype)


def _inproj(x2d, g, posf, invf, w, layer, splits, head_dim, tm, tkv):
    n, d = x2d.shape
    widths = [splits[0]] + [splits[i] - splits[i - 1] for i in range(1, 6)]
    slab_cols = [start + a * LANES for start, width in ((0, widths[0]), (splits[0], widths[1]), (splits[4], widths[5]))
                 for a in range(width // LANES)]
    kern = functools.partial(_inproj_kernel, splits=splits, slab_cols=tuple(slab_cols), head_dim=head_dim,
                             q_scale=head_dim ** -0.5 * math.log2(math.e))
    row = lambda i: (i, 0)
    rows = lambda wd: pl.BlockSpec((tm, wd), row)
    out = lambda wd, dt: jax.ShapeDtypeStruct((n, wd), dt)
    return pl.pallas_call(
        kern,
        grid=(n // tm,),
        in_specs=[pl.BlockSpec((tm, d), row), _layer_spec(g.shape, layer), pl.BlockSpec((tm, 1), row),
                  _const_spec((1, LANES)), _layer_spec(w.shape, layer)],
        out_specs=[rows(widths[2]), rows(widths[3]),
                   pl.BlockSpec((tm // tkv, widths[4], tkv), lambda i: (i, 0, 0))] + [rows(LANES)] * len(slab_cols),
        out_shape=[out(widths[2], BF16), out(widths[3], BF16),
                   jax.ShapeDtypeStruct((n // tkv, widths[4], tkv), BF16)] + [out(LANES, F32)] * len(slab_cols),
        compiler_params=_cparams("parallel"),
        name="inproj",
    )(x2d, g, posf, invf, w)


def _rglru_kernel(*refs, n_slab):
    x_refs, g_refs = refs[:n_slab], refs[n_slab:2 * n_slab]
    pos_ref, cw_ref, cb_ref, w_ref, b_ref, lam_ref = refs[2 * n_slab:2 * n_slab + 6]
    o_refs = refs[2 * n_slab + 6:3 * n_slab + 6]
    halo_ref, h_ref = refs[3 * n_slab + 6:]
    j = pl.program_id(1)
    ph = SUBLANES
    m = x_refs[0].shape[1] // ph
    r = n_slab * LANES

    @pl.when(j == 0)
    def _():
        halo_ref[...] = jnp.zeros_like(halo_ref)
        h_ref[...] = jnp.zeros_like(h_ref)

    def phase(slabs, s):
        return jnp.concatenate([ref[0, pl.ds(s, m, stride=ph), :] for ref in slabs], axis=1)

    block = lax.broadcasted_iota(jnp.int32, (m, r), 0)

    def one_block_back(v, first):
        return jnp.where(block == 0, first, pltpu.roll(v, 1, 0))

    xs = [phase(x_refs, s) for s in range(ph)]
    halo = halo_ref[...]
    cw = cw_ref[0]
    taps = cw.shape[0]
    earlier = {s: one_block_back(xs[s], halo[s:s + 1]) for s in range(ph - taps + 1, ph)}
    xcs = []
    for s in range(ph):
        xc = cb_ref[0] + cw[0:1] * xs[s]
        for k in range(1, taps):
            xc = xc + cw[k:k + 1] * (xs[s - k] if s >= k else earlier[s - k + ph])
        xcs.append(xc)
    halo_ref[...] = jnp.concatenate([xs[s][m - 1:m] for s in range(ph)], axis=0)
    xc = jnp.concatenate(xcs, axis=0)

    gates = jnp.dot(xc.astype(BF16), w_ref[0], preferred_element_type=F32) + b_ref[0]
    rg = _sigmoid(gates[:, :r])
    ig = _sigmoid(gates[:, r:])
    z = -lam_ref[0]
    softplus = jnp.maximum(z, 0.0) + jnp.log(1.0 + jnp.exp(-jnp.abs(z)))
    a = jnp.exp((-RGLRU_C) * rg * softplus)
    mult = jnp.sqrt(1.0 - a * a)
    pos = pos_ref[0]
    reset = jnp.concatenate([pos[:, s:s + 1] for s in range(ph)], axis=0) == 0.0
    a = jnp.where(reset, 0.0, a)
    mult = jnp.where(reset, 1.0, mult)
    b = mult * ig * xc

    pa, pb = [a[0:m]], [b[0:m]]
    for s in range(1, ph):
        a_s = a[s * m:(s + 1) * m]
        pb.append(a_s * pb[-1] + b[s * m:(s + 1) * m])
        pa.append(a_s * pa[-1])
    ba, bb = pa[-1], pb[-1]
    d = 1
    while d < m:
        keep = block >= d
        a_sh = jnp.where(keep, pltpu.roll(ba, d, 0), 1.0)
        b_sh = jnp.where(keep, pltpu.roll(bb, d, 0), 0.0)
        bb = bb + ba * b_sh
        ba = ba * a_sh
        d *= 2
    h_prev = h_ref[...]
    h_end = bb + ba * h_prev
    h_in = one_block_back(h_end, h_prev)
    h_ref[...] = h_end[m - 1:m]
    for s in range(ph):
        out = (pb[s] + pa[s] * h_in) * _gelu_tanh(phase(g_refs, s))
        for c, o_ref in enumerate(o_refs):
            o_ref[0, pl.ds(s, m, stride=ph), :] = out[:, c * LANES:(c + 1) * LANES]


def _rglru(x_slabs, g_slabs, pos8, conv_w, conv_b, w_gates, b_gates, lam, layer, tt):
    n_slab = len(x_slabs)
    bsz, seq, _ = x_slabs[0].shape
    r = n_slab * LANES
    slab = pl.BlockSpec((1, tt, LANES), lambda b, j: (b, j, 0))
    return pl.pallas_call(
        functools.partial(_rglru_kernel, n_slab=n_slab),
        grid=(bsz, seq // tt),
        in_specs=[slab] * (2 * n_slab) + [pl.BlockSpec((1, tt // SUBLANES, SUBLANES), lambda b, j: (b, j, 0))]
                 + [_layer_spec(a.shape, layer) for a in (conv_w, conv_b, w_gates, b_gates, lam)],
        out_specs=[slab] * n_slab,
        out_shape=[jax.ShapeDtypeStruct((bsz, seq, LANES), F32)] * n_slab,
        scratch_shapes=[pltpu.VMEM((SUBLANES, r), F32), pltpu.VMEM((1, r), F32)],
        compiler_params=_cparams("parallel", "arbitrary"),
        name="rglru",
    )(*x_slabs, *g_slabs, pos8, conv_w, conv_b, w_gates, b_gates, lam)


def _attn_kernel(q_ref, k_ref, vt_ref, lamv_ref, sg_ref, o_ref, *, tq, head_dim, lambda_init):
    i = pl.program_id(2)
    hw = 2 * head_dim
    hp = q_ref.shape[2] // hw
    vdim = vt_ref.shape[1] // hp
    lane = lax.broadcasted_iota(jnp.int32, (tq, hw), 1)
    zero = jnp.zeros((tq, hw), q_ref.dtype)
    ones = jnp.ones((2 * SUBLANES, tq), BF16)
    qqs = []
    for a in range(hp):
        q = q_ref[0, :, a * hw:(a + 1) * hw]
        qqs.append(jnp.concatenate([jnp.where(lane < head_dim, q, zero), jnp.where(lane >= head_dim, q, zero)],
                                   axis=0))

    def step(j, carry, masked):
        row0 = pl.multiple_of(j * tq, tq)
        out = []
        for a in range(hp):
            m, acc = carry[2 * a], carry[2 * a + 1]
            kb = k_ref[0, pl.ds(row0, tq), a * hw:(a + 1) * hw]
            s = lax.dot_general(kb, qqs[a], (((1,), (1,)), ((), ())), preferred_element_type=F32)
            if masked:
                key = lax.broadcasted_iota(jnp.int32, s.shape, 0)
                qry = lax.broadcasted_iota(jnp.int32, s.shape, 1)
                qry = jnp.where(qry >= tq, qry - tq, qry)
                s = jnp.where(key <= qry, s, NEG_INF)
            m_new = jnp.maximum(m, jnp.max(s, axis=0, keepdims=True))
            p = jnp.exp2(s - m_new)
            alpha = jnp.exp2(m - m_new)
            vt = jnp.concatenate([vt_ref[j, a * vdim:(a + 1) * vdim, :], ones], axis=0)
            out += [m_new, alpha * acc + jnp.dot(vt, p.astype(BF16), preferred_element_type=F32)]
        return tuple(out)

    init = (jnp.full((1, 2 * tq), NEG_INF, F32), jnp.zeros((vdim + 2 * SUBLANES, 2 * tq), F32)) * hp
    carry = lax.fori_loop(0, i, lambda j, c: step(j, c, False), init)
    carry = step(i, carry, True)

    lamv = lamv_ref[0]
    lam = (jnp.exp(jnp.sum(lamv[0:1] * lamv[1:2], axis=1, keepdims=True))
           - jnp.exp(jnp.sum(lamv[2:3] * lamv[3:4], axis=1, keepdims=True)) + lambda_init)
    for a in range(hp):
        acc = carry[2 * a + 1]
        ot = acc[:vdim] / acc[vdim:vdim + 1]
        o = jnp.transpose(ot[:, :tq] - lam * ot[:, tq:])
        o = _rmsnorm(o, sg_ref[0], SUBLN_EPS) * (1.0 - lambda_init)
        o_ref[0, :, a * vdim:(a + 1) * vdim] = o.astype(o_ref.dtype)


def _diff_attention(q, k, vt, lamv, subln_g, layer, heads, head_dim, lambda_init, tq, hp):
    bsz, seq, _ = q.shape
    vdim = vt.shape[1] // heads
    nkv = seq // tq
    kern = functools.partial(_attn_kernel, tq=tq, head_dim=head_dim, lambda_init=lambda_init)
    return pl.pallas_call(
        kern,
        grid=(bsz, heads // hp, seq // tq),
        in_specs=[pl.BlockSpec((1, tq, hp * 2 * head_dim), lambda b, h, i: (b, i, h)),
                  pl.BlockSpec((1, seq, hp * 2 * head_dim), lambda b, h, i: (b, 0, h)),
                  pl.BlockSpec((nkv, hp * vdim, tq), lambda b, h, i: (b, h, 0)),
                  _layer_spec(lamv.shape, layer), _layer_spec(subln_g.shape, layer)],
        out_specs=pl.BlockSpec((1, tq, hp * vdim), lambda b, h, i: (b, i, h)),
        out_shape=jax.ShapeDtypeStruct((bsz, seq, heads * vdim), BF16),
        compiler_params=_cparams("parallel", "parallel", "arbitrary"),
        name="diff_attn",
    )(q, k, vt, lamv, subln_g)


def _s5_tables(lam_re, lam_im, b_re, b_im, c_re, c_im, d_skip, log_dt, n_steps):
    tc = S5_CHUNK
    g, n, p = b_re.shape
    lr = lam_re.astype(F32)
    li = lam_im.astype(F32)
    dt = jnp.exp(log_dt.astype(F32))[:, None]
    mag = jnp.exp(lr * dt)
    ar = mag * jnp.cos(li * dt)
    ai = mag * jnp.sin(li * dt)
    den = lr * lr + li * li
    cr = ((ar - 1.0) * lr + ai * li) / den
    ci = (ai * lr - (ar - 1.0) * li) / den
    bb_re = cr[..., None] * b_re - ci[..., None] * b_im
    bb_im = cr[..., None] * b_im + ci[..., None] * b_re

    def apow(e):
        e = jnp.asarray(e, F32)[None, None, :]
        m = jnp.exp(e * (lr * dt)[..., None])
        ph = e * (li * dt)[..., None]
        return m * jnp.cos(ph), m * jnp.sin(ph)

    lags = jnp.arange(tc)
    pw_re, pw_im = apow(lags)
    ab_re = pw_re[..., None] * bb_re[:, :, None, :] - pw_im[..., None] * bb_im[:, :, None, :]
    ab_im = pw_re[..., None] * bb_im[:, :, None, :] + pw_im[..., None] * bb_re[:, :, None, :]
    abt_re = ab_re.transpose(0, 1, 3, 2).reshape(g, n, 1, p * tc)
    abt_im = ab_im.transpose(0, 1, 3, 2).reshape(g, n, 1, p * tc)
    ct_re = c_re.transpose(0, 2, 1)[..., None]
    ct_im = c_im.transpose(0, 2, 1)[..., None]
    kl = jnp.sum(ct_re * abt_re - ct_im * abt_im, axis=1).reshape(g, p, p, tc)
    skip = jnp.eye(p, dtype=F32)[None] * d_skip[:, None, :]
    kl = jnp.concatenate([kl[..., :1] + skip[..., None], kl[..., 1:]], axis=-1)
    seq = kl.transpose(0, 1, 3, 2)[:, :, ::-1, :].reshape(g, p, tc * p)
    seq = jnp.concatenate([seq, jnp.zeros((g, p, (tc - 1) * p), F32)], axis=-1)
    rows = [seq[:, :, (tc - 1 - t) * p:(tc - 1 - t) * p + tc * p] for t in range(tc)]
    kt = jnp.stack(rows, axis=1).reshape(g, tc * p, tc * p)
    bt = jnp.concatenate([ab_re[:, :, ::-1, :].reshape(g, n, tc * p), ab_im[:, :, ::-1, :].reshape(g, n, tc * p)],
                         axis=1)
    p1_re, p1_im = apow(lags + 1)
    ca_re = jnp.einsum('gon,gnt->gton', c_re, p1_re) - jnp.einsum('gon,gnt->gton', c_im, p1_im)
    ca_im = jnp.einsum('gon,gnt->gton', c_re, p1_im) + jnp.einsum('gon,gnt->gton', c_im, p1_re)
    ct = jnp.concatenate([ca_re.reshape(g, tc * p, n), -ca_im.reshape(g, tc * p, n)], axis=2)
    st_re, st_im = apow(tc * (2 ** jnp.arange(n_steps)))
    tab = jnp.stack([st_re.transpose(0, 2, 1), st_im.transpose(0, 2, 1)], axis=2)
    tab = tab.reshape(g // 2, 2, 2 * n_steps, n).transpose(0, 2, 1, 3).reshape(g // 2, 2 * n_steps, 2 * n)
    tab = jnp.pad(tab, ((0, 0), (0, -(2 * n_steps) % SUBLANES), (0, 0)))
    return kt.astype(BF16), bt.astype(BF16), ct.astype(BF16), tab.astype(F32)


def _s5_kernel(*refs, n_groups, p, n_slab):
    u_refs = refs[:n_slab]
    kt_ref, bt_ref, ct_ref, tab_ref = refs[n_slab:n_slab + 4]
    o_refs = refs[n_slab + 4:2 * n_slab + 4]
    ut_ref, y_ref, carry_ref = refs[2 * n_slab + 4:]
    tc = S5_CHUNK
    c = u_refs[0].shape[1] // tc
    n2 = bt_ref.shape[2]
    half = n2 // 2
    n_steps = int(math.log2(c))

    @pl.when(pl.program_id(1) == 0)
    def _():
        carry_ref[...] = jnp.zeros_like(carry_ref)

    for k in range(tc):
        for a in range(n_slab):
            ut_ref[k, a * LANES:(a + 1) * LANES, :] = jnp.transpose(
                u_refs[a][0, pl.ds(k, c, stride=tc), :]).astype(BF16)

    row = lax.broadcasted_iota(jnp.int32, (c, n2), 0)

    def shift(x, d, fill):
        return jnp.where(row >= d, pltpu.roll(x, d, 0), fill)

    def pair(gp, _):
        gs = (2 * gp, 2 * gp + 1)
        local = []
        for g in gs:
            ug = jnp.concatenate([ut_ref[k, pl.ds(pl.multiple_of(g * p, p), p), :] for k in range(tc)], axis=0)
            y_ref[g] = jnp.dot(kt_ref[0, g], ug, preferred_element_type=F32)
            local.append(jnp.dot(bt_ref[0, g], ug, preferred_element_type=F32))
        sr = jnp.transpose(jnp.concatenate([local[0][:half], local[1][:half]], axis=0))
        si = jnp.transpose(jnp.concatenate([local[0][half:], local[1][half:]], axis=0))
        tab = tab_ref[0, gp]
        cin_r = carry_ref[gp, 0:1, :]
        cin_i = carry_ref[gp, 1:2, :]
        sr = sr + jnp.where(row == 0, tab[0:1] * cin_r - tab[1:2] * cin_i, 0.0)
        si = si + jnp.where(row == 0, tab[0:1] * cin_i + tab[1:2] * cin_r, 0.0)
        for s in range(n_steps):
            d = 1 << s
            ar, ai = tab[2 * s:2 * s + 1], tab[2 * s + 1:2 * s + 2]
            hr, hi = shift(sr, d, 0.0), shift(si, d, 0.0)
            sr, si = sr + ar * hr - ai * hi, si + ar * hi + ai * hr
        carry_ref[gp, 0:1, :] = sr[c - 1:c]
        carry_ref[gp, 1:2, :] = si[c - 1:c]
        pr = jnp.transpose(shift(sr, 1, cin_r))
        pi = jnp.transpose(shift(si, 1, cin_i))
        for idx, g in enumerate(gs):
            prev = jnp.concatenate([pr[idx * half:(idx + 1) * half], pi[idx * half:(idx + 1) * half]], axis=0)
            y_ref[g] = y_ref[g] + jnp.dot(ct_ref[0, g], prev.astype(BF16), preferred_element_type=F32)
        return 0

    lax.fori_loop(0, n_groups // 2, pair, 0, unroll=4)

    groups_per_slab = LANES // p
    for t in range(tc):
        for a in range(n_slab):
            yt = jnp.concatenate([y_ref[g, t * p:(t + 1) * p, :]
                                  for g in range(a * groups_per_slab, (a + 1) * groups_per_slab)], axis=0)
            o_refs[a][0, pl.ds(t, c, stride=tc), :] = jnp.transpose(yt)


def _s5(u_slabs, tables, layer, c_lanes):
    kt, bt, ct, tab = tables
    n_slab = len(u_slabs)
    bsz, seq, _ = u_slabs[0].shape
    width = n_slab * LANES
    tc = S5_CHUNK
    n_groups, n2 = bt.shape[1], bt.shape[2]
    p = width // n_groups
    rows = c_lanes * tc
    kern = functools.partial(_s5_kernel, n_groups=n_groups, p=p, n_slab=n_slab)
    slab = pl.BlockSpec((1, rows, LANES), lambda b, j: (b, j, 0))
    return pl.pallas_call(
        kern,
        grid=(bsz, seq // rows),
        in_specs=[slab] * n_slab + [_layer_spec(t.shape, layer) for t in (kt, bt, ct, tab)],
        out_specs=[slab] * n_slab,
        out_shape=[jax.ShapeDtypeStruct((bsz, seq, LANES), F32)] * n_slab,
        scratch_shapes=[pltpu.VMEM((tc, width, c_lanes), BF16), pltpu.VMEM((n_groups, tc * p, c_lanes), F32),
                        pltpu.VMEM((n_groups // 2, SUBLANES, n2), F32)],
        compiler_params=_cparams("parallel", "arbitrary"),
        name="s5",
    )(*u_slabs, kt, bt, ct, tab)


def _merge_kernel(*refs, n_groups, n_experts, d_model, n_rslab, n_slab):
    x_ref, ya_ref = refs[:2]
    yr_refs = refs[2:2 + n_rslab]
    ys_refs = refs[2 + n_rslab:2 + n_rslab + n_slab]
    (mg_ref, wg_ref, gw_ref, gb_ref, pr_ref, pa_ref, ps_ref, wo_ref, fg_ref, rw_ref, rb_ref,
     x1_ref, xn_ref, route_ref, routet_ref, tab_ref, cnt_ref, run_ref) = refs[2 + n_rslab + n_slab:]
    i = pl.program_id(0)

    @pl.when(i == 0)
    def _():
        run_ref[...] = jnp.zeros_like(run_ref)

    x = x_ref[...]
    tm = x.shape[0]
    xn = _rmsnorm(x, mg_ref[0], MIX_EPS)
    gates = _sigmoid(jnp.dot(xn.astype(BF16), wg_ref[0], preferred_element_type=F32))
    z = _gelu_tanh(jnp.concatenate([ref[...] for ref in ys_refs], axis=1))
    ys = z * _sigmoid(jnp.dot(z.astype(BF16), gw_ref[0], preferred_element_type=F32) + gb_ref[0])
    y_rnn = jnp.concatenate([ref[...] for ref in yr_refs], axis=1).astype(BF16)
    merged = (gates[:, :d_model] * jnp.dot(y_rnn, pr_ref[0], preferred_element_type=F32)
              + gates[:, d_model:2 * d_model] * jnp.dot(ya_ref[...], pa_ref[0], preferred_element_type=F32)
              + gates[:, 2 * d_model:] * jnp.dot(ys.astype(BF16), ps_ref[0], preferred_element_type=F32))
    x1 = x + jnp.dot(merged.astype(BF16), wo_ref[0], preferred_element_type=F32)
    x1_ref[...] = x1
    xn2 = _rmsnorm(x1, fg_ref[0], MIX_EPS)
    xn_ref[...] = xn2.astype(xn_ref.dtype)

    rw = rw_ref[0]
    x_hi = xn2.astype(BF16)
    x_lo = (xn2 - x_hi.astype(F32)).astype(BF16)
    w_hi = rw.astype(BF16)
    w_lo = (rw - w_hi.astype(F32)).astype(BF16)
    logits = (jnp.dot(x_hi, w_hi, preferred_element_type=F32) + jnp.dot(x_lo, w_hi, preferred_element_type=F32)
              + jnp.dot(x_hi, w_lo, preferred_element_type=F32) + rb_ref[0])
    lane = lax.broadcasted_iota(jnp.int32, logits.shape, 1).astype(F32)
    big = float(LANES)
    coarse = jnp.where(lane < n_groups, logits, NEG_INF)
    cmax = jnp.max(coarse, axis=1, keepdims=True)
    gsel = jnp.min(jnp.where(coarse == cmax, lane, big), axis=1, keepdims=True)
    p_sel = 1.0 / jnp.sum(jnp.where(lane < n_groups, jnp.exp(logits - cmax), 0.0), axis=1, keepdims=True)
    lo = n_groups + gsel * n_experts
    fine = jnp.where((lane >= lo) & (lane < lo + n_experts), logits, NEG_INF)
    m1 = jnp.max(fine, axis=1, keepdims=True)
    i1 = jnp.min(jnp.where(fine == m1, lane, big), axis=1, keepdims=True)
    fine2 = jnp.where(lane == i1, NEG_INF, fine)
    m2 = jnp.max(fine2, axis=1, keepdims=True)
    i2 = jnp.min(jnp.where(fine2 == m2, lane, big), axis=1, keepdims=True)
    e21 = jnp.exp(m2 - m1)
    w1 = p_sel / (1.0 + e21)
    w2 = p_sel * e21 / (1.0 + e21)
    oh1 = lane == i1
    oh2 = lane == i2
    onehot = jnp.where(oh1 | oh2, 1.0, 0.0)
    r_i = lax.broadcasted_iota(jnp.int32, (tm, tm), 0)
    c_i = lax.broadcasted_iota(jnp.int32, (tm, tm), 1)
    earlier = jnp.where(c_i < r_i, 1.0, 0.0).astype(BF16)
    rank = jnp.dot(earlier, onehot.astype(BF16), preferred_element_type=F32)
    cnt = jnp.sum(onehot, axis=0, keepdims=True)
    cnt = jnp.floor((cnt + (SUBLANES - 1)) * (1.0 / SUBLANES)) * SUBLANES
    k_i = lax.broadcasted_iota(jnp.int32, (LANES, LANES), 0)
    l_i = lax.broadcasted_iota(jnp.int32, (LANES, LANES), 1)
    lower = jnp.where(k_i < l_i, 1.0, 0.0)
    start = jnp.dot(jnp.broadcast_to(cnt, (SUBLANES, LANES)), lower, preferred_element_type=F32,
                    precision=lax.Precision.HIGHEST)[0:1]
    pos = rank + start
    lp1 = jnp.sum(jnp.where(oh1, pos, 0.0), axis=1, keepdims=True)
    lp2 = jnp.sum(jnp.where(oh2, pos, 0.0), axis=1, keepdims=True)
    route = jnp.where(lane == 0, w1, 0.0)
    route = jnp.where(lane == 1, w2, route)
    route = jnp.where(lane == 2, lp1, route)
    route = jnp.where(lane == 3, lp2, route)
    route_ref[...] = route
    routet_ref[...] = jnp.transpose(route)[:SUBLANES]
    sub = lax.broadcasted_iota(jnp.int32, (SUBLANES, LANES), 0)
    tab_ref[...] = jnp.where(sub == 0, cnt, jnp.where(sub == 1, start, jnp.where(sub == 2, run_ref[...], 0.0)))
    run_ref[...] = run_ref[...] + cnt
    cnt_ref[...] = run_ref[...]


def _merge(x2d, y_attn, y_rnn_slabs, y_s5_slabs, mix_g, w_gate, glu_w, glu_b, p_rnn, p_attn, p_ssm, w_out, ffn_g,
           rw, rb, layer, n_groups, n_experts, tm):
    n, d = x2d.shape
    row = lambda i: (i, 0)
    n_rslab, n_slab = len(y_rnn_slabs), len(y_s5_slabs)
    kern = functools.partial(_merge_kernel, n_groups=n_groups, n_experts=n_experts, d_model=d, n_rslab=n_rslab,
                             n_slab=n_slab)
    consts = [mix_g, w_gate, glu_w, glu_b, p_rnn, p_attn, p_ssm, w_out, ffn_g, rw, rb]
    return pl.pallas_call(
        kern,
        grid=(n // tm,),
        in_specs=[pl.BlockSpec((tm, d), row), pl.BlockSpec((tm, y_attn.shape[1]), row)]
                 + [pl.BlockSpec((tm, LANES), row)] * (n_rslab + n_slab)
                 + [_layer_spec(a.shape, layer) for a in consts],
        out_specs=[pl.BlockSpec((tm, d), row), pl.BlockSpec((tm, d), row), pl.BlockSpec((tm, LANES), row),
                   pl.BlockSpec((SUBLANES, tm), lambda i: (0, i)), pl.BlockSpec((SUBLANES, LANES), row),
                   pl.BlockSpec((1, LANES), lambda i: (0, 0))],
        out_shape=[jax.ShapeDtypeStruct((n, d), F32), jax.ShapeDtypeStruct((n, d), BF16),
                   jax.ShapeDtypeStruct((n, LANES), F32), jax.ShapeDtypeStruct((SUBLANES, n), F32),
                   jax.ShapeDtypeStruct((n // tm * SUBLANES, LANES), F32), jax.ShapeDtypeStruct((1, LANES), F32)],
        scratch_shapes=[pltpu.VMEM((1, LANES), F32)],
        compiler_params=_cparams("arbitrary"),
        name="merge_router",
    )(x2d, y_attn, *y_rnn_slabs, *y_s5_slabs, *consts)


def _sorted_rows(tm, n_total):
    rows = TOP_K_FINE * tm + n_total * (SUBLANES - 1)
    return -(-rows // LANES) * LANES


RUN_BITS = 6
TILE_BITS = 3


def _run_copies(base, n_total, cnt_ref, loc_ref, dst_ref, local_ref, remote_ref, sem, to_remote, wait):
    def copy(e, off, size):
        lstart = 0 if loc_ref is None else pl.multiple_of(loc_ref[base + e] + off, SUBLANES)
        local = local_ref.at[pl.ds(lstart, size)]
        remote = remote_ref.at[pl.ds(pl.multiple_of(dst_ref[base + e] + off, SUBLANES), size)]
        desc = pltpu.make_async_copy(local, remote, sem) if to_remote else pltpu.make_async_copy(remote, local, sem)
        if wait:
            desc.wait()
        else:
            desc.start()

    def expert(e, _):
        cnt = cnt_ref[base + e]
        big = 1 << RUN_BITS

        def chunk(c, _):
            copy(e,c * big, big)
            return 0

        n_big = cnt >> RUN_BITS
        lax.fori_loop(0, n_big, chunk, 0)
        off = n_big * big
        for b in reversed(range(TILE_BITS, RUN_BITS)):
            size = 1 << b

            @pl.when((cnt & size) != 0)
            def _(off=off, size=size):
                copy(e,off, size)

            off = off + (cnt & size)
        return 0

    lax.fori_loop(0, n_total, expert, 0)


def _dispatch_kernel(cnt_ref, loc_ref, dst_ref, gap_cnt_ref, gap_dst_ref, x_ref, rt_ref, xs_ref, buf_ref, zero_ref,
                     sem, *, n_total):
    i = pl.program_id(0)
    last = pl.num_programs(0) - 1
    slot = i % 2
    tm = x_ref.shape[0]
    rows = buf_ref.shape[1]
    pos = rt_ref[...]
    j = lax.broadcasted_iota(jnp.int32, (rows, tm), 0).astype(F32)
    sel = jnp.where((j == pos[2:3]) | (j == pos[3:4]), 1.0, 0.0).astype(BF16)
    buf_ref[slot] = jnp.dot(sel, x_ref[...], preferred_element_type=F32)

    def copies(tile, tile_slot, wait):
        _run_copies(tile * n_total, n_total, cnt_ref, loc_ref, dst_ref, buf_ref.at[tile_slot], xs_ref,
                    sem.at[tile_slot], True, wait)

    copies(i, slot, False)

    @pl.when(i > 0)
    def _():
        copies(i - 1, 1 - slot, True)

    @pl.when(i == last)
    def _():
        copies(i, slot, True)
        zero_ref[...] = jnp.zeros_like(zero_ref)
        for wait in (False, True):
            _run_copies(0, gap_cnt_ref.shape[0], gap_cnt_ref, None, gap_dst_ref, zero_ref, xs_ref, sem.at[0], True,
                        wait)


def _dispatch(tile_cnt, tile_loc, tile_dst, gap_cnt, gap_dst, xn2, route_t, n_rows, n_total, tm):
    n, d = xn2.shape
    grid_spec = pltpu.PrefetchScalarGridSpec(
        num_scalar_prefetch=5,
        grid=(n // tm,),
        in_specs=[pl.BlockSpec((tm, d), lambda i, *_: (i, 0)),
                  pl.BlockSpec((SUBLANES, tm), lambda i, *_: (0, i))],
        out_specs=pl.BlockSpec(memory_space=pl.ANY),
        scratch_shapes=[pltpu.VMEM((2, _sorted_rows(tm, n_total), d), F32), pltpu.VMEM((1 << RUN_BITS, d), F32),
                        pltpu.SemaphoreType.DMA((2,))],
    )
    return pl.pallas_call(
        functools.partial(_dispatch_kernel, n_total=n_total),
        grid_spec=grid_spec,
        out_shape=jax.ShapeDtypeStruct((n_rows, d), F32),
        compiler_params=_cparams("arbitrary"),
        name="moe_dispatch",
    )(tile_cnt, tile_loc, tile_dst, gap_cnt, gap_dst, xn2, route_t)


def _experts_kernel(te_ref, nact_ref, x_ref, w1_ref, w3_ref, w2_ref, o_ref, w1b_ref, w3b_ref, w2b_ref):
    i = pl.program_id(0)
    fresh = jnp.logical_or(i == 0, te_ref[i] != te_ref[jnp.maximum(i - 1, 0)])

    @pl.when(jnp.logical_and(i < nact_ref[0], fresh))
    def _():
        w1b_ref[...] = w1_ref[0].astype(BF16)
        w3b_ref[...] = w3_ref[0].astype(BF16)
        w2b_ref[...] = w2_ref[0].astype(BF16)

    @pl.when(i < nact_ref[0])
    def _():
        xb = x_ref[...].astype(BF16)
        h1 = jnp.dot(xb, w1b_ref[...], preferred_element_type=F32)
        h3 = jnp.dot(xb, w3b_ref[...], preferred_element_type=F32)
        hid = h1 * _sigmoid(h1) * h3
        o_ref[...] = jnp.dot(hid.astype(BF16), w2b_ref[...], preferred_element_type=F32).astype(o_ref.dtype)

    @pl.when(i >= nact_ref[0])
    def _():
        o_ref[...] = jnp.zeros_like(o_ref)


def _experts(tile_expert, n_active, xs, w1, w3, w2, tm):
    n_rows, d = xs.shape
    f = w1.shape[2]
    row = lambda i, te, na: (jnp.minimum(i, na[0] - 1), 0)
    wsel = lambda i, te, na: (te[i], 0, 0)
    grid_spec = pltpu.PrefetchScalarGridSpec(
        num_scalar_prefetch=2,
        grid=(n_rows // tm,),
        in_specs=[pl.BlockSpec((tm, d), row), pl.BlockSpec((1, d, f), wsel), pl.BlockSpec((1, d, f), wsel),
                  pl.BlockSpec((1, f, d), wsel)],
        out_specs=pl.BlockSpec((tm, d), lambda i, te, na: (i, 0)),
        scratch_shapes=[pltpu.VMEM((d, f), BF16), pltpu.VMEM((d, f), BF16), pltpu.VMEM((f, d), BF16)],
    )
    return pl.pallas_call(
        _experts_kernel,
        grid_spec=grid_spec,
        out_shape=jax.ShapeDtypeStruct((n_rows, d), F32),
        compiler_params=_cparams("arbitrary"),
        name="moe_experts",
    )(tile_expert, n_active, xs, w1, w3, w2)


def _combine_kernel(cnt_ref, loc_ref, dst_ref, x_ref, route_ref, rt_ref, fg_ref, ys_ref, o_ref, buf_ref, sem,
                    *, n_total, final_norm):
    i = pl.program_id(0)
    slot = i % 2
    tm = x_ref.shape[0]
    rows = buf_ref.shape[1]

    def copies(tile, tile_slot, wait):
        _run_copies(tile * n_total, n_total, cnt_ref, loc_ref, dst_ref, buf_ref.at[tile_slot], ys_ref,
                    sem.at[tile_slot], False, wait)

    @pl.when(i == 0)
    def _():
        buf_ref[...] = jnp.zeros_like(buf_ref)
        copies(0, 0, False)

    @pl.when(i + 1 < pl.num_programs(0))
    def _():
        copies(i + 1, 1 - slot, False)

    copies(i, slot, True)
    rt = rt_ref[...]
    jr = lax.broadcasted_iota(jnp.int32, (rows, tm), 0).astype(F32)
    gate = jnp.sum(jnp.where(jr == rt[2:3], rt[0:1], 0.0) + jnp.where(jr == rt[3:4], rt[1:2], 0.0),
                   axis=1, keepdims=True)
    yb = (buf_ref[slot] * gate).astype(BF16)
    route = route_ref[...]
    jc = lax.broadcasted_iota(jnp.int32, (tm, rows), 1).astype(F32)
    pick = jnp.where((jc == route[:, 2:3]) | (jc == route[:, 3:4]), 1.0, 0.0).astype(BF16)
    out = x_ref[...] + jnp.dot(pick, yb, preferred_element_type=F32)
    if final_norm:
        out = _rmsnorm(out, fg_ref[...], MIX_EPS)
    o_ref[...] = out


def _combine(tile_cnt, tile_loc, tile_dst, x1, route, route_t, final_g, ys, n_total, tm, final_norm):
    n, d = x1.shape
    kern = functools.partial(_combine_kernel, n_total=n_total, final_norm=final_norm)
    grid_spec = pltpu.PrefetchScalarGridSpec(
        num_scalar_prefetch=3,
        grid=(n // tm,),
        in_specs=[pl.BlockSpec((tm, d), lambda i, *_: (i, 0)),
                  pl.BlockSpec((tm, LANES), lambda i, *_: (i, 0)),
                  pl.BlockSpec((SUBLANES, tm), lambda i, *_: (0, i)),
                  pl.BlockSpec((1, d), lambda i, *_: (0, 0)),
                  pl.BlockSpec(memory_space=pl.ANY)],
        out_specs=pl.BlockSpec((tm, d), lambda i, *_: (i, 0)),
        scratch_shapes=[pltpu.VMEM((2, _sorted_rows(tm, n_total), d), F32), pltpu.SemaphoreType.DMA((2,))],
    )
    return pl.pallas_call(
        kern,
        grid_spec=grid_spec,
        out_shape=jax.ShapeDtypeStruct((n, d), F32),
        compiler_params=_cparams("arbitrary"),
        name="moe_combine",
    )(tile_cnt, tile_loc, tile_dst, x1, route, route_t, final_g, ys)


def _tile_plan(n, seq):
    return dict(inproj=min(1024, n), rglru=min(256, seq), attn=min(512, seq), attn_heads=2,
                s5_lanes=min(LANES, seq // S5_CHUNK), merge=min(512, n), moe=min(384, n))


def kernel(x, positions, mix_norm_g, w_in, conv_w, conv_b, rg_wa, rg_ba, rg_wx, rg_bx, rg_lambda,
           lam_q1, lam_k1, lam_q2, lam_k2, subln_g,
           ssm_lambda_re, ssm_lambda_im, ssm_b_re, ssm_b_im, ssm_c_re, ssm_c_im, ssm_d, ssm_log_dt,
           ssm_glu_w, ssm_glu_b, proj_rnn, proj_attn, proj_ssm, w_out,
           ffn_norm_g, router_coarse_w, router_coarse_b, router_fine_w, router_fine_b,
           expert_w1, expert_w3, expert_w2, final_norm_g):
    return _forward(_tile_plan(x.shape[0] * x.shape[1], x.shape[1]),
                    x, positions, mix_norm_g, w_in, conv_w, conv_b, rg_wa, rg_ba, rg_wx, rg_bx, rg_lambda,
                    lam_q1, lam_k1, lam_q2, lam_k2, subln_g,
                    ssm_lambda_re, ssm_lambda_im, ssm_b_re, ssm_b_im, ssm_c_re, ssm_c_im, ssm_d, ssm_log_dt,
                    ssm_glu_w, ssm_glu_b, proj_rnn, proj_attn, proj_ssm, w_out,
                    ffn_norm_g, router_coarse_w, router_coarse_b, router_fine_w, router_fine_b,
                    expert_w1, expert_w3, expert_w2, final_norm_g)


def _forward(tiles, x, positions, mix_norm_g, w_in, conv_w, conv_b, rg_wa, rg_ba, rg_wx, rg_bx, rg_lambda,
             lam_q1, lam_k1, lam_q2, lam_k2, subln_g,
             ssm_lambda_re, ssm_lambda_im, ssm_b_re, ssm_b_im, ssm_c_re, ssm_c_im, ssm_d, ssm_log_dt,
             ssm_glu_w, ssm_glu_b, proj_rnn, proj_attn, proj_ssm, w_out,
             ffn_norm_g, router_coarse_w, router_coarse_b, router_fine_w, router_fine_b,
             expert_w1, expert_w3, expert_w2, final_norm_g):
    bsz, seq, d_model = x.shape
    depth = w_in.shape[0]
    n = bsz * seq
    r = conv_w.shape[2]
    sw = ssm_glu_w.shape[1]
    vdim = subln_g.shape[1]
    head_dim = vdim // 2
    in_cols = w_in.shape[2]
    qk = (in_cols - 2 * r - sw - 3 * d_model) // 3
    heads = qk // (2 * head_dim)
    splits = (r, 2 * r, 2 * r + qk, 2 * r + 2 * qk, 2 * r + 3 * qk, 2 * r + 3 * qk + sw)
    mix_cols = splits[-1]
    n_groups = router_coarse_w.shape[2]
    n_experts = expert_w1.shape[2]
    n_total = n_groups * n_experts
    rnn_blocks = rg_wa.shape[1]

    tm_in, tt_rnn, tq = tiles["inproj"], tiles["rglru"], tiles["attn"]
    s5_lanes, tm_merge, tm_moe = tiles["s5_lanes"], tiles["merge"], tiles["moe"]
    n_rows = TOP_K_FINE * n + (n // tm_merge) * n_total * (SUBLANES - 1) + n_total * tm_moe
    n_rows = -(-n_rows // tm_moe) * tm_moe

    posf = positions.astype(F32)
    pos_col = posf.reshape(n, 1)
    pos8 = posf.reshape(bsz, seq // SUBLANES, SUBLANES)
    inv_freq = ROPE_THETA ** (-jnp.arange(0, head_dim, 2, dtype=F32) / head_dim)
    invf = jnp.tile(inv_freq, LANES // (head_dim // 2)).reshape(1, LANES)

    row3 = lambda a: a.reshape(depth, 1, a.shape[-1])
    w_mix = w_in[:, :, :mix_cols].astype(BF16)
    w_gate = w_in[:, :, mix_cols:].astype(BF16)
    eye_blocks = jnp.eye(rnn_blocks, dtype=F32)
    block_diag = lambda w: jnp.einsum('lhij,hk->lhikj', w, eye_blocks).reshape(depth, r, r)
    w_gates = jnp.concatenate([block_diag(rg_wa), block_diag(rg_wx)], axis=2).astype(BF16)
    b_gates = row3(jnp.concatenate([rg_ba, rg_bx], axis=1))
    lamv = jnp.stack([lam_q1, lam_k1, lam_q2, lam_k2], axis=1)
    tables = jax.vmap(functools.partial(_s5_tables, n_steps=int(math.log2(s5_lanes))))(
        ssm_lambda_re, ssm_lambda_im, ssm_b_re, ssm_b_im, ssm_c_re, ssm_c_im, ssm_d, ssm_log_dt)
    rw = jnp.concatenate([router_coarse_w, router_fine_w], axis=2)
    rw = jnp.pad(rw, ((0, 0), (0, 0), (0, LANES - rw.shape[2])))
    rb = jnp.concatenate([router_coarse_b, router_fine_b], axis=1)
    rb = row3(jnp.pad(rb, ((0, 0), (0, LANES - rb.shape[1]))))
    merge_params = (row3(mix_norm_g), w_gate, ssm_glu_w.astype(BF16), row3(ssm_glu_b), proj_rnn.astype(BF16),
                    proj_attn.astype(BF16), proj_ssm.astype(BF16), w_out.astype(BF16), row3(ffn_norm_g), rw, rb)

    x2d = x.reshape(n, d_model)
    for l in range(depth):
        lambda_init = 0.8 - 0.6 * math.exp(-0.3 * l)
        q, k, vt, *slabs = _inproj(x2d, row3(mix_norm_g), pos_col, invf, w_mix, l, splits, head_dim, tm_in, tq)
        slabs = [t.reshape(bsz, seq, LANES) for t in slabs]
        n_rs = r // LANES
        y_rnn = _rglru(slabs[:n_rs], slabs[n_rs:2 * n_rs], pos8, conv_w, row3(conv_b), w_gates, b_gates,
                       row3(rg_lambda), l, tt_rnn)
        y_attn = _diff_attention(q.reshape(bsz, seq, qk), k.reshape(bsz, seq, qk), vt, lamv, row3(subln_g), l,
                                 heads, head_dim, lambda_init, tq, min(tiles["attn_heads"], heads))
        y_s5 = _s5(slabs[2 * n_rs:], tables, l, s5_lanes)
        x1, xn2, route, route_t, tile_tab, counts = _merge(
            x2d, y_attn.reshape(n, qk), [y.reshape(n, LANES) for y in y_rnn], [y.reshape(n, LANES) for y in y_s5],
            *merge_params, l, n_groups, n_experts, tm_merge)

        cnt = counts[0, n_groups:n_groups + n_total].astype(jnp.int32)
        n_tiles = (cnt + tm_moe - 1) // tm_moe
        tile_end = jnp.cumsum(n_tiles)
        offsets = (tile_end - n_tiles) * tm_moe
        tab = tile_tab.reshape(n // tm_merge, SUBLANES, LANES)[:, :, n_groups:n_groups + n_total].astype(jnp.int32)
        tile_cnt = tab[:, 0].reshape(-1)
        tile_loc = tab[:, 1].reshape(-1)
        tile_dst = (tab[:, 2] + offsets[None, :]).reshape(-1)
        n_active = tile_end[-1:]
        gap_dst = jnp.concatenate([offsets + cnt, n_active * tm_moe])
        gap_cnt = jnp.concatenate([tile_end * tm_moe, jnp.full((1,), n_rows, jnp.int32)]) - gap_dst
        tile_ids = jnp.minimum(jnp.arange(n_rows // tm_moe, dtype=jnp.int32), n_active[0] - 1)
        tile_expert = jnp.sum((tile_ids[:, None] >= tile_end[None, :]).astype(jnp.int32), axis=1)

        xs = _dispatch(tile_cnt, tile_loc, tile_dst, gap_cnt, gap_dst, xn2, route_t, n_rows, n_total, tm_merge)
        ys = _experts(tile_expert + l * n_total, n_active.astype(jnp.int32), xs,
                      expert_w1.reshape(depth * n_total, d_model, -1), expert_w3.reshape(depth * n_total, d_model, -1),
                      expert_w2.reshape(depth * n_total, -1, d_model), tm_moe)
        x2d = _combine(tile_cnt, tile_loc, tile_dst, x1, route, route_t, final_norm_g.reshape(1, d_model), ys,
                       n_total, tm_merge, l == depth - 1)
    return x2d.reshape(bsz, seq, d_model)
```

```python
import functools
import math

import jax
import jax.numpy as jnp
from jax import lax
from jax.experimental import pallas as pl
from jax.experimental.pallas import tpu as pltpu

F32 = jnp.float32
BF16 = jnp.bfloat16

RGLRU_C = 8.0
ROPE_THETA = 10000.0
TOP_K_FINE = 2
NEG_INF = -1e30
MIX_EPS = 1e-6
SUBLN_EPS = 1e-5

LANES = 128
SUBLANES = 8
VMEM_LIMIT_BYTES = 56 * 1024 * 1024

S5_CHUNK = 16


def _cparams(*sem):
    return pltpu.CompilerParams(dimension_semantics=sem, vmem_limit_bytes=VMEM_LIMIT_BYTES)


def _const_spec(shape):
    nd = len(shape)
    return pl.BlockSpec(shape, lambda *_: (0,) * nd, pipeline_mode=pl.Buffered(1))


def _layer_spec(shape, layer):
    nd = len(shape)
    return pl.BlockSpec((1,) + tuple(shape[1:]), lambda *_: (layer,) + (0,) * (nd - 1), pipeline_mode=pl.Buffered(1))


def _gelu_tanh(x):
    return 0.5 * x * (1.0 + jnp.tanh(math.sqrt(2.0 / math.pi) * (x + 0.044715 * (x * x * x))))


def _sigmoid(x):
    return 0.5 + 0.5 * jnp.tanh(0.5 * x)


def _rmsnorm(x, g, eps):
    return x * lax.rsqrt(jnp.mean(x * x, axis=-1, keepdims=True) + eps) * g


def _inproj_kernel(x_ref, g_ref, pos_ref, invf_ref, w_ref, q_ref, k_ref, vt_ref, *slab_refs,
                   splits, slab_cols, head_dim, q_scale):
    x = x_ref[...]
    xn = _rmsnorm(x, g_ref[0], MIX_EPS)
    h = jnp.dot(xn.astype(BF16), w_ref[0], preferred_element_type=F32)
    s0, s1, s2, s3, s4, s5 = splits
    for col, ref in zip(slab_cols, slab_refs):
        ref[...] = h[:, col:col + LANES]
    tkv = vt_ref.shape[2]
    for c in range(vt_ref.shape[0]):
        vt_ref[c] = jnp.transpose(h[c * tkv:(c + 1) * tkv, s3:s4]).astype(vt_ref.dtype)

    ang = pos_ref[...] * invf_ref[...]
    cos = jnp.cos(ang)
    sin = jnp.sin(ang)
    lane = lax.broadcasted_iota(jnp.int32, ang.shape, 1)
    first_half = (lane % head_dim) < (head_dim // 2)
    sin_signed = jnp.where(first_half, -sin, sin)

    def rope(t, scale):
        outs = []
        for a in range(t.shape[1] // LANES):
            xs = t[:, a * LANES:(a + 1) * LANES]
            fwd = pltpu.roll(xs, LANES - head_dim // 2, 1)
            bwd = pltpu.roll(xs, head_dim // 2, 1)
            rot = jnp.where(first_half, fwd, bwd)
            outs.append((xs * cos + rot * sin_signed) * scale)
        return jnp.concatenate(outs, axis=1)

    q_ref[...] = rope(h[:, s1:s2], q_scale).astype(q_ref.dtype)
    k_ref[...] = rope(h[:, s2:s3], 1.0).astype(k_ref.dtype)


def _inproj(x2d, g, posf, invf, w, layer, splits, head_dim, tm, tkv):
    n, d = x2d.shape
    widths = [splits[0]] + [splits[i] - splits[i - 1] for i in range(1, 6)]
    slab_cols = [start + a * LANES for start, width in ((0, widths[0]), (splits[0], widths[1]), (splits[4], widths[5]))
                 for a in range(width // LANES)]
    kern = functools.partial(_inproj_kernel, splits=splits, slab_cols=tuple(slab_cols), head_dim=head_dim,
                             q_scale=head_dim ** -0.5 * math.log2(math.e))
    row = lambda i: (i, 0)
    rows = lambda wd: pl.BlockSpec((tm, wd), row)
    out = lambda wd, dt: jax.ShapeDtypeStruct((n, wd), dt)
    return pl.pallas_call(
        kern,
        grid=(n // tm,),
        in_specs=[pl.BlockSpec((tm, d), row), _layer_spec(g.shape, layer), pl.BlockSpec((tm, 1), row),
                  _const_spec((1, LANES)), _layer_spec(w.shape, layer)],
        out_specs=[rows(widths[2]), rows(widths[3]),
                   pl.BlockSpec((tm // tkv, widths[4], tkv), lambda i: (i, 0, 0))] + [rows(LANES)] * len(slab_cols),
        out_shape=[out(widths[2], BF16), out(widths[3], BF16),
                   jax.ShapeDtypeStruct((n // tkv, widths[4], tkv), BF16)] + [out(LANES, F32)] * len(slab_cols),
        compiler_params=_cparams("parallel"),
        name="inproj",
    )(x2d, g, posf, invf, w)


def _rglru_kernel(*refs, n_slab):
    x_refs, g_refs = refs[:n_slab], refs[n_slab:2 * n_slab]
    pos_ref, cw_ref, cb_ref, w_ref, b_ref, lam_ref = refs[2 * n_slab:2 * n_slab + 6]
    o_refs = refs[2 * n_slab + 6:3 * n_slab + 6]
    halo_ref, h_ref = refs[3 * n_slab + 6:]
    j = pl.program_id(1)
    ph = SUBLANES
    m = x_refs[0].shape[1] // ph
    r = n_slab * LANES

    @pl.when(j == 0)
    def _():
        halo_ref[...] = jnp.zeros_like(halo_ref)
        h_ref[...] = jnp.zeros_like(h_ref)

    def phase(slabs, s):
        return jnp.concatenate([ref[0, pl.ds(s, m, stride=ph), :] for ref in slabs], axis=1)

    block = lax.broadcasted_iota(jnp.int32, (m, r), 0)

    def one_block_back(v, first):
        return jnp.where(block == 0, first, pltpu.roll(v, 1, 0))

    xs = [phase(x_refs, s) for s in range(ph)]
    halo = halo_ref[...]
    cw = cw_ref[0]
    taps = cw.shape[0]
    earlier = {s: one_block_back(xs[s], halo[s:s + 1]) for s in range(ph - taps + 1, ph)}
    xcs = []
    for s in range(ph):
        xc = cb_ref[0] + cw[0:1] * xs[s]
        for k in range(1, taps):
            xc = xc + cw[k:k + 1] * (xs[s - k] if s >= k else earlier[s - k + ph])
        xcs.append(xc)
    halo_ref[...] = jnp.concatenate([xs[s][m - 1:m] for s in range(ph)], axis=0)
    xc = jnp.concatenate(xcs, axis=0)

    gates = jnp.dot(xc.astype(BF16), w_ref[0], preferred_element_type=F32) + b_ref[0]
    rg = _sigmoid(gates[:, :r])
    ig = _sigmoid(gates[:, r:])
    z = -lam_ref[0]
    softplus = jnp.maximum(z, 0.0) + jnp.log(1.0 + jnp.exp(-jnp.abs(z)))
    a = jnp.exp((-RGLRU_C) * rg * softplus)
    mult = jnp.sqrt(1.0 - a * a)
    pos = pos_ref[0]
    reset = jnp.concatenate([pos[:, s:s + 1] for s in range(ph)], axis=0) == 0.0
    a = jnp.where(reset, 0.0, a)
    mult = jnp.where(reset, 1.0, mult)
    b = mult * ig * xc

    pa, pb = [a[0:m]], [b[0:m]]
    for s in range(1, ph):
        a_s = a[s * m:(s + 1) * m]
        pb.append(a_s * pb[-1] + b[s * m:(s + 1) * m])
        pa.append(a_s * pa[-1])
    ba, bb = pa[-1], pb[-1]
    d = 1
    while d < m:
        keep = block >= d
        a_sh = jnp.where(keep, pltpu.roll(ba, d, 0), 1.0)
        b_sh = jnp.where(keep, pltpu.roll(bb, d, 0), 0.0)
        bb = bb + ba * b_sh
        ba = ba * a_sh
        d *= 2
    h_prev = h_ref[...]
    h_end = bb + ba * h_prev
    h_in = one_block_back(h_end, h_prev)
    h_ref[...] = h_end[m - 1:m]
    for s in range(ph):
        out = (pb[s] + pa[s] * h_in) * _gelu_tanh(phase(g_refs, s))
        for c, o_ref in enumerate(o_refs):
            o_ref[0, pl.ds(s, m, stride=ph), :] = out[:, c * LANES:(c + 1) * LANES]


def _rglru(x_slabs, g_slabs, pos8, conv_w, conv_b, w_gates, b_gates, lam, layer, tt):
    n_slab = len(x_slabs)
    bsz, seq, _ = x_slabs[0].shape
    r = n_slab * LANES
    slab = pl.BlockSpec((1, tt, LANES), lambda b, j: (b, j, 0))
    return pl.pallas_call(
        functools.partial(_rglru_kernel, n_slab=n_slab),
        grid=(bsz, seq // tt),
        in_specs=[slab] * (2 * n_slab) + [pl.BlockSpec((1, tt // SUBLANES, SUBLANES), lambda b, j: (b, j, 0))]
                 + [_layer_spec(a.shape, layer) for a in (conv_w, conv_b, w_gates, b_gates, lam)],
        out_specs=[slab] * n_slab,
        out_shape=[jax.ShapeDtypeStruct((bsz, seq, LANES), F32)] * n_slab,
        scratch_shapes=[pltpu.VMEM((SUBLANES, r), F32), pltpu.VMEM((1, r), F32)],
        compiler_params=_cparams("parallel", "arbitrary"),
        name="rglru",
    )(*x_slabs, *g_slabs, pos8, conv_w, conv_b, w_gates, b_gates, lam)


def _attn_kernel(q_ref, k_ref, vt_ref, lamv_ref, sg_ref, o_ref, *, tq, head_dim, lambda_init):
    i = pl.program_id(2)
    hw = 2 * head_dim
    hp = q_ref.shape[2] // hw
    vdim = vt_ref.shape[1] // hp
    lane = lax.broadcasted_iota(jnp.int32, (tq, hw), 1)
    zero = jnp.zeros((tq, hw), q_ref.dtype)
    ones = jnp.ones((2 * SUBLANES, tq), BF16)
    qqs = []
    for a in range(hp):
        q = q_ref[0, :, a * hw:(a + 1) * hw]
        qqs.append(jnp.concatenate([jnp.where(lane < head_dim, q, zero), jnp.where(lane >= head_dim, q, zero)],
                                   axis=0))

    def step(j, carry, masked):
        row0 = pl.multiple_of(j * tq, tq)
        out = []
        for a in range(hp):
            m, acc = carry[2 * a], carry[2 * a + 1]
            kb = k_ref[0, pl.ds(row0, tq), a * hw:(a + 1) * hw]
            s = lax.dot_general(kb, qqs[a], (((1,), (1,)), ((), ())), preferred_element_type=F32)
            if masked:
                key = lax.broadcasted_iota(jnp.int32, s.shape, 0)
                qry = lax.broadcasted_iota(jnp.int32, s.shape, 1)
                qry = jnp.where(qry >= tq, qry - tq, qry)
                s = jnp.where(key <= qry, s, NEG_INF)
            m_new = jnp.maximum(m, jnp.max(s, axis=0, keepdims=True))
            p = jnp.exp2(s - m_new)
            alpha = jnp.exp2(m - m_new)
            vt = jnp.concatenate([vt_ref[j, a * vdim:(a + 1) * vdim, :], ones], axis=0)
            out += [m_new, alpha * acc + jnp.dot(vt, p.astype(BF16), preferred_element_type=F32)]
        return tuple(out)

    init = (jnp.full((1, 2 * tq), NEG_INF, F32), jnp.zeros((vdim + 2 * SUBLANES, 2 * tq), F32)) * hp
    carry = lax.fori_loop(0, i, lambda j, c: step(j, c, False), init)
    carry = step(i, carry, True)

    lamv = lamv_ref[0]
    lam = (jnp.exp(jnp.sum(lamv[0:1] * lamv[1:2], axis=1, keepdims=True))
           - jnp.exp(jnp.sum(lamv[2:3] * lamv[3:4], axis=1, keepdims=True)) + lambda_init)
    for a in range(hp):
        acc = carry[2 * a + 1]
        ot = acc[:vdim] / acc[vdim:vdim + 1]
        o = jnp.transpose(ot[:, :tq] - lam * ot[:, tq:])
        o = _rmsnorm(o, sg_ref[0], SUBLN_EPS) * (1.0 - lambda_init)
        o_ref[0, :, a * vdim:(a + 1) * vdim] = o.astype(o_ref.dtype)


def _diff_attention(q, k, vt, lamv, subln_g, layer, heads, head_dim, lambda_init, tq, hp):
    bsz, seq, _ = q.shape
    vdim = vt.shape[1] // heads
    nkv = seq // tq
    kern = functools.partial(_attn_kernel, tq=tq, head_dim=head_dim, lambda_init=lambda_init)
    return pl.pallas_call(
        kern,
        grid=(bsz, heads // hp, seq // tq),
        in_specs=[pl.BlockSpec((1, tq, hp * 2 * head_dim), lambda b, h, i: (b, i, h)),
                  pl.BlockSpec((1, seq, hp * 2 * head_dim), lambda b, h, i: (b, 0, h)),
                  pl.BlockSpec((nkv, hp * vdim, tq), lambda b, h, i: (b, h, 0)),
                  _layer_spec(lamv.shape, layer), _layer_spec(subln_g.shape, layer)],
        out_specs=pl.BlockSpec((1, tq, hp * vdim), lambda b, h, i: (b, i, h)),
        out_shape=jax.ShapeDtypeStruct((bsz, seq, heads * vdim), BF16),
        compiler_params=_cparams("parallel", "parallel", "arbitrary"),
        name="diff_attn",
    )(q, k, vt, lamv, subln_g)


def _s5_tables(lam_re, lam_im, b_re, b_im, c_re, c_im, d_skip, log_dt, n_steps):
    tc = S5_CHUNK
    g, n, p = b_re.shape
    lr = lam_re.astype(F32)
    li = lam_im.astype(F32)
    dt = jnp.exp(log_dt.astype(F32))[:, None]
    mag = jnp.exp(lr * dt)
    ar = mag * jnp.cos(li * dt)
    ai = mag * jnp.sin(li * dt)
    den = lr * lr + li * li
    cr = ((ar - 1.0) * lr + ai * li) / den
    ci = (ai * lr - (ar - 1.0) * li) / den
    bb_re = cr[..., None] * b_re - ci[..., None] * b_im
    bb_im = cr[..., None] * b_im + ci[..., None] * b_re

    def apow(e):
        e = jnp.asarray(e, F32)[None, None, :]
        m = jnp.exp(e * (lr * dt)[..., None])
        ph = e * (li * dt)[..., None]
        return m * jnp.cos(ph), m * jnp.sin(ph)

    lags = jnp.arange(tc)
    pw_re, pw_im = apow(lags)
    ab_re = pw_re[..., None] * bb_re[:, :, None, :] - pw_im[..., None] * bb_im[:, :, None, :]
    ab_im = pw_re[..., None] * bb_im[:, :, None, :] + pw_im[..., None] * bb_re[:, :, None, :]
    abt_re = ab_re.transpose(0, 1, 3, 2).reshape(g, n, 1, p * tc)
    abt_im = ab_im.transpose(0, 1, 3, 2).reshape(g, n, 1, p * tc)
    ct_re = c_re.transpose(0, 2, 1)[..., None]
    ct_im = c_im.transpose(0, 2, 1)[..., None]
    kl = jnp.sum(ct_re * abt_re - ct_im * abt_im, axis=1).reshape(g, p, p, tc)
    skip = jnp.eye(p, dtype=F32)[None] * d_skip[:, None, :]
    kl = jnp.concatenate([kl[..., :1] + skip[..., None], kl[..., 1:]], axis=-1)
    seq = kl.transpose(0, 1, 3, 2)[:, :, ::-1, :].reshape(g, p, tc * p)
    seq = jnp.concatenate([seq, jnp.zeros((g, p, (tc - 1) * p), F32)], axis=-1)
    rows = [seq[:, :, (tc - 1 - t) * p:(tc - 1 - t) * p + tc * p] for t in range(tc)]
    kt = jnp.stack(rows, axis=1).reshape(g, tc * p, tc * p)
    bt = jnp.concatenate([ab_re[:, :, ::-1, :].reshape(g, n, tc * p), ab_im[:, :, ::-1, :].reshape(g, n, tc * p)],
                         axis=1)
    p1_re, p1_im = apow(lags + 1)
    ca_re = jnp.einsum('gon,gnt->gton', c_re, p1_re) - jnp.einsum('gon,gnt->gton', c_im, p1_im)
    ca_im = jnp.einsum('gon,gnt->gton', c_re, p1_im) + jnp.einsum('gon,gnt->gton', c_im, p1_re)
    ct = jnp.concatenate([ca_re.reshape(g, tc * p, n), -ca_im.reshape(g, tc * p, n)], axis=2)
    st_re, st_im = apow(tc * (2 ** jnp.arange(n_steps)))
    tab = jnp.stack([st_re.transpose(0, 2, 1), st_im.transpose(0, 2, 1)], axis=2)
    tab = tab.reshape(g // 2, 2, 2 * n_steps, n).transpose(0, 2, 1, 3).reshape(g // 2, 2 * n_steps, 2 * n)
    tab = jnp.pad(tab, ((0, 0), (0, -(2 * n_steps) % SUBLANES), (0, 0)))
    return kt.astype(BF16), bt.astype(BF16), ct.astype(BF16), tab.astype(F32)


def _s5_kernel(*refs, n_groups, p, n_slab):
    u_refs = refs[:n_slab]
    kt_ref, bt_ref, ct_ref, tab_ref = refs[n_slab:n_slab + 4]
    o_refs = refs[n_slab + 4:2 * n_slab + 4]
    ut_ref, y_ref, carry_ref = refs[2 * n_slab + 4:]
    tc = S5_CHUNK
    c = u_refs[0].shape[1] // tc
    n2 = bt_ref.shape[2]
    half = n2 // 2
    n_steps = int(math.log2(c))

    @pl.when(pl.program_id(1) == 0)
    def _():
        carry_ref[...] = jnp.zeros_like(carry_ref)

    for k in range(tc):
        for a in range(n_slab):
            ut_ref[k, a * LANES:(a + 1) * LANES, :] = jnp.transpose(
                u_refs[a][0, pl.ds(k, c, stride=tc), :]).astype(BF16)

    row = lax.broadcasted_iota(jnp.int32, (c, n2), 0)

    def shift(x, d, fill):
        return jnp.where(row >= d, pltpu.roll(x, d, 0), fill)

    def pair(gp, _):
        gs = (2 * gp, 2 * gp + 1)
        local = []
        for g in gs:
            ug = jnp.concatenate([ut_ref[k, pl.ds(pl.multiple_of(g * p, p), p), :] for k in range(tc)], axis=0)
            y_ref[g] = jnp.dot(kt_ref[0, g], ug, preferred_element_type=F32)
            local.append(jnp.dot(bt_ref[0, g], ug, preferred_element_type=F32))
        sr = jnp.transpose(jnp.concatenate([local[0][:half], local[1][:half]], axis=0))
        si = jnp.transpose(jnp.concatenate([local[0][half:], local[1][half:]], axis=0))
        tab = tab_ref[0, gp]
        cin_r = carry_ref[gp, 0:1, :]
        cin_i = carry_ref[gp, 1:2, :]
        sr = sr + jnp.where(row == 0, tab[0:1] * cin_r - tab[1:2] * cin_i, 0.0)
        si = si + jnp.where(row == 0, tab[0:1] * cin_i + tab[1:2] * cin_r, 0.0)
        for s in range(n_steps):
            d = 1 << s
            ar, ai = tab[2 * s:2 * s + 1], tab[2 * s + 1:2 * s + 2]
            hr, hi = shift(sr, d, 0.0), shift(si, d, 0.0)
            sr, si = sr + ar * hr - ai * hi, si + ar * hi + ai * hr
        carry_ref[gp, 0:1, :] = sr[c - 1:c]
        carry_ref[gp, 1:2, :] = si[c - 1:c]
        pr = jnp.transpose(shift(sr, 1, cin_r))
        pi = jnp.transpose(shift(si, 1, cin_i))
        for idx, g in enumerate(gs):
            prev = jnp.concatenate([pr[idx * half:(idx + 1) * half], pi[idx * half:(idx + 1) * half]], axis=0)
            y_ref[g] = y_ref[g] + jnp.dot(ct_ref[0, g], prev.astype(BF16), preferred_element_type=F32)
        return 0

    lax.fori_loop(0, n_groups // 2, pair, 0, unroll=4)

    groups_per_slab = LANES // p
    for t in range(tc):
        for a in range(n_slab):
            yt = jnp.concatenate([y_ref[g, t * p:(t + 1) * p, :]
                                  for g in range(a * groups_per_slab, (a + 1) * groups_per_slab)], axis=0)
            o_refs[a][0, pl.ds(t, c, stride=tc), :] = jnp.transpose(yt)


def _s5(u_slabs, tables, layer, c_lanes):
    kt, bt, ct, tab = tables
    n_slab = len(u_slabs)
    bsz, seq, _ = u_slabs[0].shape
    width = n_slab * LANES
    tc = S5_CHUNK
    n_groups, n2 = bt.shape[1], bt.shape[2]
    p = width // n_groups
    rows = c_lanes * tc
    kern = functools.partial(_s5_kernel, n_groups=n_groups, p=p, n_slab=n_slab)
    slab = pl.BlockSpec((1, rows, LANES), lambda b, j: (b, j, 0))
    return pl.pallas_call(
        kern,
        grid=(bsz, seq // rows),
        in_specs=[slab] * n_slab + [_layer_spec(t.shape, layer) for t in (kt, bt, ct, tab)],
        out_specs=[slab] * n_slab,
        out_shape=[jax.ShapeDtypeStruct((bsz, seq, LANES), F32)] * n_slab,
        scratch_shapes=[pltpu.VMEM((tc, width, c_lanes), BF16), pltpu.VMEM((n_groups, tc * p, c_lanes), F32),
                        pltpu.VMEM((n_groups // 2, SUBLANES, n2), F32)],
        compiler_params=_cparams("parallel", "arbitrary"),
        name="s5",
    )(*u_slabs, kt, bt, ct, tab)


def _merge_kernel(*refs, n_groups, n_experts, d_model, n_rslab, n_slab):
    x_ref, ya_ref = refs[:2]
    yr_refs = refs[2:2 + n_rslab]
    ys_refs = refs[2 + n_rslab:2 + n_rslab + n_slab]
    (mg_ref, wg_ref, gw_ref, gb_ref, pr_ref, pa_ref, ps_ref, wo_ref, fg_ref, rw_ref, rb_ref,
     x1_ref, xn_ref, route_ref, routet_ref, tab_ref, cnt_ref, run_ref) = refs[2 + n_rslab + n_slab:]
    i = pl.program_id(0)

    @pl.when(i == 0)
    def _():
        run_ref[...] = jnp.zeros_like(run_ref)

    x = x_ref[...]
    tm = x.shape[0]
    xn = _rmsnorm(x, mg_ref[0], MIX_EPS)
    gates = _sigmoid(jnp.dot(xn.astype(BF16), wg_ref[0], preferred_element_type=F32))
    z = _gelu_tanh(jnp.concatenate([ref[...] for ref in ys_refs], axis=1))
    ys = z * _sigmoid(jnp.dot(z.astype(BF16), gw_ref[0], preferred_element_type=F32) + gb_ref[0])
    y_rnn = jnp.concatenate([ref[...] for ref in yr_refs], axis=1).astype(BF16)
    merged = (gates[:, :d_model] * jnp.dot(y_rnn, pr_ref[0], preferred_element_type=F32)
              + gates[:, d_model:2 * d_model] * jnp.dot(ya_ref[...], pa_ref[0], preferred_element_type=F32)
              + gates[:, 2 * d_model:] * jnp.dot(ys.astype(BF16), ps_ref[0], preferred_element_type=F32))
    x1 = x + jnp.dot(merged.astype(BF16), wo_ref[0], preferred_element_type=F32)
    x1_ref[...] = x1
    xn2 = _rmsnorm(x1, fg_ref[0], MIX_EPS)
    xn_ref[...] = xn2.astype(xn_ref.dtype)

    rw = rw_ref[0]
    x_hi = xn2.astype(BF16)
    x_lo = (xn2 - x_hi.astype(F32)).astype(BF16)
    w_hi = rw.astype(BF16)
    w_lo = (rw - w_hi.astype(F32)).astype(BF16)
    logits = (jnp.dot(x_hi, w_hi, preferred_element_type=F32) + jnp.dot(x_lo, w_hi, preferred_element_type=F32)
              + jnp.dot(x_hi, w_lo, preferred_element_type=F32) + rb_ref[0])
    lane = lax.broadcasted_iota(jnp.int32, logits.shape, 1).astype(F32)
    big = float(LANES)
    coarse = jnp.where(lane < n_groups, logits, NEG_INF)
    cmax = jnp.max(coarse, axis=1, keepdims=True)
    gsel = jnp.min(jnp.where(coarse == cmax, lane, big), axis=1, keepdims=True)
    p_sel = 1.0 / jnp.sum(jnp.where(lane < n_groups, jnp.exp(logits - cmax), 0.0), axis=1, keepdims=True)
    lo = n_groups + gsel * n_experts
    fine = jnp.where((lane >= lo) & (lane < lo + n_experts), logits, NEG_INF)
    m1 = jnp.max(fine, axis=1, keepdims=True)
    i1 = jnp.min(jnp.where(fine == m1, lane, big), axis=1, keepdims=True)
    fine2 = jnp.where(lane == i1, NEG_INF, fine)
    m2 = jnp.max(fine2, axis=1, keepdims=True)
    i2 = jnp.min(jnp.where(fine2 == m2, lane, big), axis=1, keepdims=True)
    e21 = jnp.exp(m2 - m1)
    w1 = p_sel / (1.0 + e21)
    w2 = p_sel * e21 / (1.0 + e21)
    oh1 = lane == i1
    oh2 = lane == i2
    onehot = jnp.where(oh1 | oh2, 1.0, 0.0)
    r_i = lax.broadcasted_iota(jnp.int32, (tm, tm), 0)
    c_i = lax.broadcasted_iota(jnp.int32, (tm, tm), 1)
    earlier = jnp.where(c_i < r_i, 1.0, 0.0).astype(BF16)
    rank = jnp.dot(earlier, onehot.astype(BF16), preferred_element_type=F32)
    cnt = jnp.sum(onehot, axis=0, keepdims=True)
    cnt = jnp.floor((cnt + (SUBLANES - 1)) * (1.0 / SUBLANES)) * SUBLANES
    k_i = lax.broadcasted_iota(jnp.int32, (LANES, LANES), 0)
    l_i = lax.broadcasted_iota(jnp.int32, (LANES, LANES), 1)
    lower = jnp.where(k_i < l_i, 1.0, 0.0)
    start = jnp.dot(jnp.broadcast_to(cnt, (SUBLANES, LANES)), lower, preferred_element_type=F32,
                    precision=lax.Precision.HIGHEST)[0:1]
    pos = rank + start
    lp1 = jnp.sum(jnp.where(oh1, pos, 0.0), axis=1, keepdims=True)
    lp2 = jnp.sum(jnp.where(oh2, pos, 0.0), axis=1, keepdims=True)
    route = jnp.where(lane == 0, w1, 0.0)
    route = jnp.where(lane == 1, w2, route)
    route = jnp.where(lane == 2, lp1, route)
    route = jnp.where(lane == 3, lp2, route)
    route_ref[...] = route
    routet_ref[...] = jnp.transpose(route)[:SUBLANES]
    sub = lax.broadcasted_iota(jnp.int32, (SUBLANES, LANES), 0)
    tab_ref[...] = jnp.where(sub == 0, cnt, jnp.where(sub == 1, start, jnp.where(sub == 2, run_ref[...], 0.0)))
    run_ref[...] = run_ref[...] + cnt
    cnt_ref[...] = run_ref[...]


def _merge(x2d, y_attn, y_rnn_slabs, y_s5_slabs, mix_g, w_gate, glu_w, glu_b, p_rnn, p_attn, p_ssm, w_out, ffn_g,
           rw, rb, layer, n_groups, n_experts, tm):
    n, d = x2d.shape
    row = lambda i: (i, 0)
    n_rslab, n_slab = len(y_rnn_slabs), len(y_s5_slabs)
    kern = functools.partial(_merge_kernel, n_groups=n_groups, n_experts=n_experts, d_model=d, n_rslab=n_rslab,
                             n_slab=n_slab)
    consts = [mix_g, w_gate, glu_w, glu_b, p_rnn, p_attn, p_ssm, w_out, ffn_g, rw, rb]
    return pl.pallas_call(
        kern,
        grid=(n // tm,),
        in_specs=[pl.BlockSpec((tm, d), row), pl.BlockSpec((tm, y_attn.shape[1]), row)]
                 + [pl.BlockSpec((tm, LANES), row)] * (n_rslab + n_slab)
                 + [_layer_spec(a.shape, layer) for a in consts],
        out_specs=[pl.BlockSpec((tm, d), row), pl.BlockSpec((tm, d), row), pl.BlockSpec((tm, LANES), row),
                   pl.BlockSpec((SUBLANES, tm), lambda i: (0, i)), pl.BlockSpec((SUBLANES, LANES), row),
                   pl.BlockSpec((1, LANES), lambda i: (0, 0))],
        out_shape=[jax.ShapeDtypeStruct((n, d), F32), jax.ShapeDtypeStruct((n, d), BF16),
                   jax.ShapeDtypeStruct((n, LANES), F32), jax.ShapeDtypeStruct((SUBLANES, n), F32),
                   jax.ShapeDtypeStruct((n // tm * SUBLANES, LANES), F32), jax.ShapeDtypeStruct((1, LANES), F32)],
        scratch_shapes=[pltpu.VMEM((1, LANES), F32)],
        compiler_params=_cparams("arbitrary"),
        name="merge_router",
    )(x2d, y_attn, *y_rnn_slabs, *y_s5_slabs, *consts)


def _sorted_rows(tm, n_total):
    rows = TOP_K_FINE * tm + n_total * (SUBLANES - 1)
    return -(-rows // LANES) * LANES


RUN_BITS = 6
TILE_BITS = 3


def _run_copies(base, n_total, cnt_ref, loc_ref, dst_ref, local_ref, remote_ref, sem, to_remote, wait):
    def copy(e, off, size):
        lstart = 0 if loc_ref is None else pl.multiple_of(loc_ref[base + e] + off, SUBLANES)
        local = local_ref.at[pl.ds(lstart, size)]
        remote = remote_ref.at[pl.ds(pl.multiple_of(dst_ref[base + e] + off, SUBLANES), size)]
        desc = pltpu.make_async_copy(local, remote, sem) if to_remote else pltpu.make_async_copy(remote, local, sem)
        if wait:
            desc.wait()
        else:
            desc.start()

    def expert(e, _):
        cnt = cnt_ref[base + e]
        big = 1 << RUN_BITS

        def chunk(c, _):
            copy(e,c * big, big)
            return 0

        n_big = cnt >> RUN_BITS
        lax.fori_loop(0, n_big, chunk, 0)
        off = n_big * big
        for b in reversed(range(TILE_BITS, RUN_BITS)):
            size = 1 << b

            @pl.when((cnt & size) != 0)
            def _(off=off, size=size):
                copy(e,off, size)

            off = off + (cnt & size)
        return 0

    lax.fori_loop(0, n_total, expert, 0)


def _dispatch_kernel(cnt_ref, loc_ref, dst_ref, gap_cnt_ref, gap_dst_ref, x_ref, rt_ref, xs_ref, buf_ref, zero_ref,
                     sem, *, n_total):
    i = pl.program_id(0)
    last = pl.num_programs(0) - 1
    slot = i % 2
    tm = x_ref.shape[0]
    rows = buf_ref.shape[1]
    pos = rt_ref[...]
    j = lax.broadcasted_iota(jnp.int32, (rows, tm), 0).astype(F32)
    sel = jnp.where((j == pos[2:3]) | (j == pos[3:4]), 1.0, 0.0).astype(BF16)
    buf_ref[slot] = jnp.dot(sel, x_ref[...], preferred_element_type=F32)

    def copies(tile, tile_slot, wait):
        _run_copies(tile * n_total, n_total, cnt_ref, loc_ref, dst_ref, buf_ref.at[tile_slot], xs_ref,
                    sem.at[tile_slot], True, wait)

    copies(i, slot, False)

    @pl.when(i > 0)
    def _():
        copies(i - 1, 1 - slot, True)

    @pl.when(i == last)
    def _():
        copies(i, slot, True)
        zero_ref[...] = jnp.zeros_like(zero_ref)
        for wait in (False, True):
            _run_copies(0, gap_cnt_ref.shape[0], gap_cnt_ref, None, gap_dst_ref, zero_ref, xs_ref, sem.at[0], True,
                        wait)


def _dispatch(tile_cnt, tile_loc, tile_dst, gap_cnt, gap_dst, xn2, route_t, n_rows, n_total, tm):
    n, d = xn2.shape
    grid_spec = pltpu.PrefetchScalarGridSpec(
        num_scalar_prefetch=5,
        grid=(n // tm,),
        in_specs=[pl.BlockSpec((tm, d), lambda i, *_: (i, 0)),
                  pl.BlockSpec((SUBLANES, tm), lambda i, *_: (0, i))],
        out_specs=pl.BlockSpec(memory_space=pl.ANY),
        scratch_shapes=[pltpu.VMEM((2, _sorted_rows(tm, n_total), d), F32), pltpu.VMEM((1 << RUN_BITS, d), F32),
                        pltpu.SemaphoreType.DMA((2,))],
    )
    return pl.pallas_call(
        functools.partial(_dispatch_kernel, n_total=n_total),
        grid_spec=grid_spec,
        out_shape=jax.ShapeDtypeStruct((n_rows, d), F32),
        compiler_params=_cparams("arbitrary"),
        name="moe_dispatch",
    )(tile_cnt, tile_loc, tile_dst, gap_cnt, gap_dst, xn2, route_t)


def _experts_kernel(te_ref, first_ref, slot_ref, next_ref, nact_ref, x_ref, w1_ref, w3_ref, w2_ref, o_ref,
                    w1f_ref, w3f_ref, w2f_ref, w1b_ref, w3b_ref, w2b_ref, sem):
    i = pl.program_id(0)

    def fetch(expert, slot, wait):
        for src, dst in ((w1_ref, w1f_ref), (w3_ref, w3f_ref), (w2_ref, w2f_ref)):
            copy = pltpu.make_async_copy(src.at[expert], dst.at[slot], sem.at[slot])
            if wait:
                copy.wait()
            else:
                copy.start()

    @pl.when(i == 0)
    def _():
        fetch(te_ref[0], slot_ref[0], False)

    @pl.when(jnp.logical_and(i < nact_ref[0], first_ref[i] == 1))
    def _():
        slot = slot_ref[i]
        fetch(te_ref[i], slot, True)
        w1b_ref[...] = w1f_ref[slot].astype(BF16)
        w3b_ref[...] = w3f_ref[slot].astype(BF16)
        w2b_ref[...] = w2f_ref[slot].astype(BF16)

        @pl.when(next_ref[i] >= 0)
        def _():
            fetch(next_ref[i], 1 - slot, False)

    @pl.when(i < nact_ref[0])
    def _():
        xb = x_ref[...].astype(BF16)
        h1 = jnp.dot(xb, w1b_ref[...], preferred_element_type=F32)
        h3 = jnp.dot(xb, w3b_ref[...], preferred_element_type=F32)
        hid = h1 * _sigmoid(h1) * h3
        o_ref[...] = jnp.dot(hid.astype(BF16), w2b_ref[...], preferred_element_type=F32).astype(o_ref.dtype)

    @pl.when(i >= nact_ref[0])
    def _():
        o_ref[...] = jnp.zeros_like(o_ref)


def _experts(tile_expert, tile_first, tile_slot, tile_next, n_active, xs, w1, w3, w2, tm):
    n_rows, d = xs.shape
    f = w1.shape[2]
    grid_spec = pltpu.PrefetchScalarGridSpec(
        num_scalar_prefetch=5,
        grid=(n_rows // tm,),
        in_specs=[pl.BlockSpec((tm, d), lambda i, te, fi, sl, nx, na: (jnp.minimum(i, na[0] - 1), 0)),
                  pl.BlockSpec(memory_space=pl.ANY), pl.BlockSpec(memory_space=pl.ANY),
                  pl.BlockSpec(memory_space=pl.ANY)],
        out_specs=pl.BlockSpec((tm, d), lambda i, *_: (i, 0)),
        scratch_shapes=[pltpu.VMEM((2, d, f), F32), pltpu.VMEM((2, d, f), F32), pltpu.VMEM((2, f, d), F32),
                        pltpu.VMEM((d, f), BF16), pltpu.VMEM((d, f), BF16), pltpu.VMEM((f, d), BF16),
                        pltpu.SemaphoreType.DMA((2,))],
    )
    return pl.pallas_call(
        _experts_kernel,
        grid_spec=grid_spec,
        out_shape=jax.ShapeDtypeStruct((n_rows, d), F32),
        compiler_params=_cparams("arbitrary"),
        name="moe_experts",
    )(tile_expert, tile_first, tile_slot, tile_next, n_active, xs, w1, w3, w2)


def _combine_kernel(cnt_ref, loc_ref, dst_ref, x_ref, route_ref, rt_ref, fg_ref, ys_ref, o_ref, buf_ref, sem,
                    *, n_total, final_norm):
    i = pl.program_id(0)
    slot = i % 2
    tm = x_ref.shape[0]
    rows = buf_ref.shape[1]

    def copies(tile, tile_slot, wait):
        _run_copies(tile * n_total, n_total, cnt_ref, loc_ref, dst_ref, buf_ref.at[tile_slot], ys_ref,
                    sem.at[tile_slot], False, wait)

    @pl.when(i == 0)
    def _():
        buf_ref[...] = jnp.zeros_like(buf_ref)
        copies(0, 0, False)

    @pl.when(i + 1 < pl.num_programs(0))
    def _():
        copies(i + 1, 1 - slot, False)

    copies(i, slot, True)
    rt = rt_ref[...]
    jr = lax.broadcasted_iota(jnp.int32, (rows, tm), 0).astype(F32)
    gate = jnp.sum(jnp.where(jr == rt[2:3], rt[0:1], 0.0) + jnp.where(jr == rt[3:4], rt[1:2], 0.0),
                   axis=1, keepdims=True)
    yb = (buf_ref[slot] * gate).astype(BF16)
    route = route_ref[...]
    jc = lax.broadcasted_iota(jnp.int32, (tm, rows), 1).astype(F32)
    pick = jnp.where((jc == route[:, 2:3]) | (jc == route[:, 3:4]), 1.0, 0.0).astype(BF16)
    out = x_ref[...] + jnp.dot(pick, yb, preferred_element_type=F32)
    if final_norm:
        out = _rmsnorm(out, fg_ref[...], MIX_EPS)
    o_ref[...] = out


def _combine(tile_cnt, tile_loc, tile_dst, x1, route, route_t, final_g, ys, n_total, tm, final_norm):
    n, d = x1.shape
    kern = functools.partial(_combine_kernel, n_total=n_total, final_norm=final_norm)
    grid_spec = pltpu.PrefetchScalarGridSpec(
        num_scalar_prefetch=3,
        grid=(n // tm,),
        in_specs=[pl.BlockSpec((tm, d), lambda i, *_: (i, 0)),
                  pl.BlockSpec((tm, LANES), lambda i, *_: (i, 0)),
                  pl.BlockSpec((SUBLANES, tm), lambda i, *_: (0, i)),
                  pl.BlockSpec((1, d), lambda i, *_: (0, 0)),
                  pl.BlockSpec(memory_space=pl.ANY)],
        out_specs=pl.BlockSpec((tm, d), lambda i, *_: (i, 0)),
        scratch_shapes=[pltpu.VMEM((2, _sorted_rows(tm, n_total), d), F32), pltpu.SemaphoreType.DMA((2,))],
    )
    return pl.pallas_call(
        kern,
        grid_spec=grid_spec,
        out_shape=jax.ShapeDtypeStruct((n, d), F32),
        compiler_params=_cparams("arbitrary"),
        name="moe_combine",
    )(tile_cnt, tile_loc, tile_dst, x1, route, route_t, final_g, ys)


def _tile_plan(n, seq):
    return dict(inproj=min(1024, n), rglru=min(256, seq), attn=min(512, seq), attn_heads=4,
                s5_lanes=min(LANES, seq // S5_CHUNK), merge=min(512, n), moe=min(384, n))


def kernel(x, positions, mix_norm_g, w_in, conv_w, conv_b, rg_wa, rg_ba, rg_wx, rg_bx, rg_lambda,
           lam_q1, lam_k1, lam_q2, lam_k2, subln_g,
           ssm_lambda_re, ssm_lambda_im, ssm_b_re, ssm_b_im, ssm_c_re, ssm_c_im, ssm_d, ssm_log_dt,
           ssm_glu_w, ssm_glu_b, proj_rnn, proj_attn, proj_ssm, w_out,
           ffn_norm_g, router_coarse_w, router_coarse_b, router_fine_w, router_fine_b,
           expert_w1, expert_w3, expert_w2, final_norm_g):
    return _forward(_tile_plan(x.shape[0] * x.shape[1], x.shape[1]),
                    x, positions, mix_norm_g, w_in, conv_w, conv_b, rg_wa, rg_ba, rg_wx, rg_bx, rg_lambda,
                    lam_q1, lam_k1, lam_q2, lam_k2, subln_g,
                    ssm_lambda_re, ssm_lambda_im, ssm_b_re, ssm_b_im, ssm_c_re, ssm_c_im, ssm_d, ssm_log_dt,
                    ssm_glu_w, ssm_glu_b, proj_rnn, proj_attn, proj_ssm, w_out,
                    ffn_norm_g, router_coarse_w, router_coarse_b, router_fine_w, router_fine_b,
                    expert_w1, expert_w3, expert_w2, final_norm_g)


def _forward(tiles, x, positions, mix_norm_g, w_in, conv_w, conv_b, rg_wa, rg_ba, rg_wx, rg_bx, rg_lambda,
             lam_q1, lam_k1, lam_q2, lam_k2, subln_g,
             ssm_lambda_re, ssm_lambda_im, ssm_b_re, ssm_b_im, ssm_c_re, ssm_c_im, ssm_d, ssm_log_dt,
             ssm_glu_w, ssm_glu_b, proj_rnn, proj_attn, proj_ssm, w_out,
             ffn_norm_g, router_coarse_w, router_coarse_b, router_fine_w, router_fine_b,
             expert_w1, expert_w3, expert_w2, final_norm_g):
    bsz, seq, d_model = x.shape
    depth = w_in.shape[0]
    n = bsz * seq
    r = conv_w.shape[2]
    sw = ssm_glu_w.shape[1]
    vdim = subln_g.shape[1]
    head_dim = vdim // 2
    in_cols = w_in.shape[2]
    qk = (in_cols - 2 * r - sw - 3 * d_model) // 3
    heads = qk // (2 * head_dim)
    splits = (r, 2 * r, 2 * r + qk, 2 * r + 2 * qk, 2 * r + 3 * qk, 2 * r + 3 * qk + sw)
    mix_cols = splits[-1]
    n_groups = router_coarse_w.shape[2]
    n_experts = expert_w1.shape[2]
    n_total = n_groups * n_experts
    rnn_blocks = rg_wa.shape[1]

    tm_in, tt_rnn, tq = tiles["inproj"], tiles["rglru"], tiles["attn"]
    s5_lanes, tm_merge, tm_moe = tiles["s5_lanes"], tiles["merge"], tiles["moe"]
    n_rows = TOP_K_FINE * n + (n // tm_merge) * n_total * (SUBLANES - 1) + n_total * tm_moe
    n_rows = -(-n_rows // tm_moe) * tm_moe

    posf = positions.astype(F32)
    pos_col = posf.reshape(n, 1)
    pos8 = posf.reshape(bsz, seq // SUBLANES, SUBLANES)
    inv_freq = ROPE_THETA ** (-jnp.arange(0, head_dim, 2, dtype=F32) / head_dim)
    invf = jnp.tile(inv_freq, LANES // (head_dim // 2)).reshape(1, LANES)

    row3 = lambda a: a.reshape(depth, 1, a.shape[-1])
    w_mix = w_in[:, :, :mix_cols].astype(BF16)
    w_gate = w_in[:, :, mix_cols:].astype(BF16)
    eye_blocks = jnp.eye(rnn_blocks, dtype=F32)
    block_diag = lambda w: jnp.einsum('lhij,hk->lhikj', w, eye_blocks).reshape(depth, r, r)
    w_gates = jnp.concatenate([block_diag(rg_wa), block_diag(rg_wx)], axis=2).astype(BF16)
    b_gates = row3(jnp.concatenate([rg_ba, rg_bx], axis=1))
    lamv = jnp.stack([lam_q1, lam_k1, lam_q2, lam_k2], axis=1)
    tables = jax.vmap(functools.partial(_s5_tables, n_steps=int(math.log2(s5_lanes))))(
        ssm_lambda_re, ssm_lambda_im, ssm_b_re, ssm_b_im, ssm_c_re, ssm_c_im, ssm_d, ssm_log_dt)
    rw = jnp.concatenate([router_coarse_w, router_fine_w], axis=2)
    rw = jnp.pad(rw, ((0, 0), (0, 0), (0, LANES - rw.shape[2])))
    rb = jnp.concatenate([router_coarse_b, router_fine_b], axis=1)
    rb = row3(jnp.pad(rb, ((0, 0), (0, LANES - rb.shape[1]))))
    merge_params = (row3(mix_norm_g), w_gate, ssm_glu_w.astype(BF16), row3(ssm_glu_b), proj_rnn.astype(BF16),
                    proj_attn.astype(BF16), proj_ssm.astype(BF16), w_out.astype(BF16), row3(ffn_norm_g), rw, rb)

    x2d = x.reshape(n, d_model)
    for l in range(depth):
        lambda_init = 0.8 - 0.6 * math.exp(-0.3 * l)
        q, k, vt, *slabs = _inproj(x2d, row3(mix_norm_g), pos_col, invf, w_mix, l, splits, head_dim, tm_in, tq)
        slabs = [t.reshape(bsz, seq, LANES) for t in slabs]
        n_rs = r // LANES
        y_rnn = _rglru(slabs[:n_rs], slabs[n_rs:2 * n_rs], pos8, conv_w, row3(conv_b), w_gates, b_gates,
                       row3(rg_lambda), l, tt_rnn)
        y_attn = _diff_attention(q.reshape(bsz, seq, qk), k.reshape(bsz, seq, qk), vt, lamv, row3(subln_g), l,
                                 heads, head_dim, lambda_init, tq, min(tiles["attn_heads"], heads))
        y_s5 = _s5(slabs[2 * n_rs:], tables, l, s5_lanes)
        x1, xn2, route, route_t, tile_tab, counts = _merge(
            x2d, y_attn.reshape(n, qk), [y.reshape(n, LANES) for y in y_rnn], [y.reshape(n, LANES) for y in y_s5],
            *merge_params, l, n_groups, n_experts, tm_merge)

        cnt = counts[0, n_groups:n_groups + n_total].astype(jnp.int32)
        n_tiles = (cnt + tm_moe - 1) // tm_moe
        tile_end = jnp.cumsum(n_tiles)
        offsets = (tile_end - n_tiles) * tm_moe
        tab = tile_tab.reshape(n // tm_merge, SUBLANES, LANES)[:, :, n_groups:n_groups + n_total].astype(jnp.int32)
        tile_cnt = tab[:, 0].reshape(-1)
        tile_loc = tab[:, 1].reshape(-1)
        tile_dst = (tab[:, 2] + offsets[None, :]).reshape(-1)
        n_active = tile_end[-1:]
        gap_dst = jnp.concatenate([offsets + cnt, n_active * tm_moe])
        gap_cnt = jnp.concatenate([tile_end * tm_moe, jnp.full((1,), n_rows, jnp.int32)]) - gap_dst
        tile_ids = jnp.minimum(jnp.arange(n_rows // tm_moe, dtype=jnp.int32), n_active[0] - 1)
        tile_expert = jnp.sum((tile_ids[:, None] >= tile_end[None, :]).astype(jnp.int32), axis=1)
        e_ids = jnp.arange(n_total, dtype=jnp.int32)
        present = n_tiles > 0
        ordinal = jnp.cumsum(present.astype(jnp.int32)) - 1
        later = present[None, :] & (e_ids[None, :] > e_ids[:, None])
        succ = jnp.min(jnp.where(later, e_ids[None, :], n_total), axis=1)
        succ = jnp.where(succ < n_total, succ + l * n_total, -1)
        all_ids = jnp.arange(n_rows // tm_moe, dtype=jnp.int32)
        tile_first = ((all_ids < n_active[0]) & (all_ids == (tile_end - n_tiles)[tile_expert])).astype(jnp.int32)
        tile_slot = ordinal[tile_expert] % 2
        tile_next = succ[tile_expert]

        xs = _dispatch(tile_cnt, tile_loc, tile_dst, gap_cnt, gap_dst, xn2, route_t, n_rows, n_total, tm_merge)
        ys = _experts(tile_expert + l * n_total, tile_first, tile_slot, tile_next, n_active.astype(jnp.int32), xs,
                      expert_w1.reshape(depth * n_total, d_model, -1), expert_w3.reshape(depth * n_total, d_model, -1),
                      expert_w2.reshape(depth * n_total, -1, d_model), tm_moe)
        x2d = _combine(tile_cnt, tile_loc, tile_dst, x1, route, route_t, final_norm_g.reshape(1, d_model), ys,
                       n_total, tm_merge, l == depth - 1)
    return x2d.reshape(bsz, seq, d_model)
```

```python
import functools
import math

import jax
import jax.numpy as jnp
from jax import lax
from jax.experimental import pallas as pl
from jax.experimental.pallas import tpu as pltpu

F32 = jnp.float32
BF16 = jnp.bfloat16

RGLRU_C = 8.0
ROPE_THETA = 10000.0
TOP_K_FINE = 2
NEG_INF = -1e30
MIX_EPS = 1e-6
SUBLN_EPS = 1e-5

LANES = 128
SUBLANES = 8
VMEM_LIMIT_BYTES = 56 * 1024 * 1024

S5_CHUNK = 16


def _cparams(*sem):
    return pltpu.CompilerParams(dimension_semantics=sem, vmem_limit_bytes=VMEM_LIMIT_BYTES)


def _const_spec(shape):
    nd = len(shape)
    return pl.BlockSpec(shape, lambda *_: (0,) * nd, pipeline_mode=pl.Buffered(1))


def _layer_spec(shape, layer):
    nd = len(shape)
    return pl.BlockSpec((1,) + tuple(shape[1:]), lambda *_: (layer,) + (0,) * (nd - 1), pipeline_mode=pl.Buffered(1))


def _gelu_tanh(x):
    return 0.5 * x * (1.0 + jnp.tanh(math.sqrt(2.0 / math.pi) * (x + 0.044715 * (x * x * x))))


def _sigmoid(x):
    return 0.5 + 0.5 * jnp.tanh(0.5 * x)


def _rmsnorm(x, g, eps):
    return x * lax.rsqrt(jnp.mean(x * x, axis=-1, keepdims=True) + eps) * g


def _inproj_kernel(x_ref, g_ref, pos_ref, invf_ref, w_ref, q_ref, k_ref, vt_ref, *slab_refs,
                   splits, slab_cols, head_dim, q_scale):
    x = x_ref[...]
    xn = _rmsnorm(x, g_ref[0], MIX_EPS)
    h = jnp.dot(xn.astype(BF16), w_ref[0], preferred_element_type=F32)
    s0, s1, s2, s3, s4, s5 = splits
    for col, ref in zip(slab_cols, slab_refs):
        ref[...] = h[:, col:col + LANES]
    tkv = vt_ref.shape[2]
    for c in range(vt_ref.shape[0]):
        vt_ref[c] = jnp.transpose(h[c * tkv:(c + 1) * tkv, s3:s4]).astype(vt_ref.dtype)

    ang = pos_ref[...] * invf_ref[...]
    cos = jnp.cos(ang)
    sin = jnp.sin(ang)
    lane = lax.broadcasted_iota(jnp.int32, ang.shape, 1)
    first_half = (lane % head_dim) < (head_dim // 2)
    sin_signed = jnp.where(first_half, -sin, sin)

    def rope(t, scale):
        outs = []
        for a in range(t.shape[1] // LANES):
            xs = t[:, a * LANES:(a + 1) * LANES]
            fwd = pltpu.roll(xs, LANES - head_dim // 2, 1)
            bwd = pltpu.roll(xs, head_dim // 2, 1)
            rot = jnp.where(first_half, fwd, bwd)
            outs.append((xs * cos + rot * sin_signed) * scale)
        return jnp.concatenate(outs, axis=1)

    q_ref[...] = rope(h[:, s1:s2], q_scale).astype(q_ref.dtype)
    k_ref[...] = rope(h[:, s2:s3], 1.0).astype(k_ref.dtype)


def _inproj(x2d, g, posf, invf, w, layer, splits, head_dim, tm, tkv):
    n, d = x2d.shape
    widths = [splits[0]] + [splits[i] - splits[i - 1] for i in range(1, 6)]
    slab_cols = [start + a * LANES for start, width in ((0, widths[0]), (splits[0], widths[1]), (splits[4], widths[5]))
                 for a in range(width // LANES)]
    kern = functools.partial(_inproj_kernel, splits=splits, slab_cols=tuple(slab_cols), head_dim=head_dim,
                             q_scale=head_dim ** -0.5 * math.log2(math.e))
    row = lambda i: (i, 0)
    rows = lambda wd: pl.BlockSpec((tm, wd), row)
    out = lambda wd, dt: jax.ShapeDtypeStruct((n, wd), dt)
    return pl.pallas_call(
        kern,
        grid=(n // tm,),
        in_specs=[pl.BlockSpec((tm, d), row), _layer_spec(g.shape, layer), pl.BlockSpec((tm, 1), row),
                  _const_spec((1, LANES)), _layer_spec(w.shape, layer)],
        out_specs=[rows(widths[2]), rows(widths[3]),
                   pl.BlockSpec((tm // tkv, widths[4], tkv), lambda i: (i, 0, 0))] + [rows(LANES)] * len(slab_cols),
        out_shape=[out(widths[2], BF16), out(widths[3], BF16),
                   jax.ShapeDtypeStruct((n // tkv, widths[4], tkv), BF16)] + [out(LANES, F32)] * len(slab_cols),
        compiler_params=_cparams("parallel"),
        name="inproj",
    )(x2d, g, posf, invf, w)


def _rglru_kernel(*refs, n_slab):
    x_refs, g_refs = refs[:n_slab], refs[n_slab:2 * n_slab]
    pos_ref, cw_ref, cb_ref, w_ref, b_ref, lam_ref = refs[2 * n_slab:2 * n_slab + 6]
    o_refs = refs[2 * n_slab + 6:3 * n_slab + 6]
    halo_ref, h_ref = refs[3 * n_slab + 6:]
    j = pl.program_id(1)
    ph = SUBLANES
    m = x_refs[0].shape[1] // ph
    r = n_slab * LANES

    @pl.when(j == 0)
    def _():
        halo_ref[...] = jnp.zeros_like(halo_ref)
        h_ref[...] = jnp.zeros_like(h_ref)

    def phase(slabs, s):
        return jnp.concatenate([ref[0, pl.ds(s, m, stride=ph), :] for ref in slabs], axis=1)

    block = lax.broadcasted_iota(jnp.int32, (m, r), 0)

    def one_block_back(v, first):
        return jnp.where(block == 0, first, pltpu.roll(v, 1, 0))

    xs = [phase(x_refs, s) for s in range(ph)]
    halo = halo_ref[...]
    cw = cw_ref[0]
    taps = cw.shape[0]
    earlier = {s: one_block_back(xs[s], halo[s:s + 1]) for s in range(ph - taps + 1, ph)}
    xcs = []
    for s in range(ph):
        xc = cb_ref[0] + cw[0:1] * xs[s]
        for k in range(1, taps):
            xc = xc + cw[k:k + 1] * (xs[s - k] if s >= k else earlier[s - k + ph])
        xcs.append(xc)
    halo_ref[...] = jnp.concatenate([xs[s][m - 1:m] for s in range(ph)], axis=0)
    xc = jnp.concatenate(xcs, axis=0)

    gates = jnp.dot(xc.astype(BF16), w_ref[0], preferred_element_type=F32) + b_ref[0]
    rg = _sigmoid(gates[:, :r])
    ig = _sigmoid(gates[:, r:])
    z = -lam_ref[0]
    softplus = jnp.maximum(z, 0.0) + jnp.log(1.0 + jnp.exp(-jnp.abs(z)))
    a = jnp.exp((-RGLRU_C) * rg * softplus)
    mult = jnp.sqrt(1.0 - a * a)
    pos = pos_ref[0]
    reset = jnp.concatenate([pos[:, s:s + 1] for s in range(ph)], axis=0) == 0.0
    a = jnp.where(reset, 0.0, a)
    mult = jnp.where(reset, 1.0, mult)
    b = mult * ig * xc

    pa, pb = [a[0:m]], [b[0:m]]
    for s in range(1, ph):
        a_s = a[s * m:(s + 1) * m]
        pb.append(a_s * pb[-1] + b[s * m:(s + 1) * m])
        pa.append(a_s * pa[-1])
    ba, bb = pa[-1], pb[-1]
    d = 1
    while d < m:
        keep = block >= d
        a_sh = jnp.where(keep, pltpu.roll(ba, d, 0), 1.0)
        b_sh = jnp.where(keep, pltpu.roll(bb, d, 0), 0.0)
        bb = bb + ba * b_sh
        ba = ba * a_sh
        d *= 2
    h_prev = h_ref[...]
    h_end = bb + ba * h_prev
    h_in = one_block_back(h_end, h_prev)
    h_ref[...] = h_end[m - 1:m]
    for s in range(ph):
        out = (pb[s] + pa[s] * h_in) * _gelu_tanh(phase(g_refs, s))
        for c, o_ref in enumerate(o_refs):
            o_ref[0, pl.ds(s, m, stride=ph), :] = out[:, c * LANES:(c + 1) * LANES]


def _rglru(x_slabs, g_slabs, pos8, conv_w, conv_b, w_gates, b_gates, lam, layer, tt):
    n_slab = len(x_slabs)
    bsz, seq, _ = x_slabs[0].shape
    r = n_slab * LANES
    slab = pl.BlockSpec((1, tt, LANES), lambda b, j: (b, j, 0))
    return pl.pallas_call(
        functools.partial(_rglru_kernel, n_slab=n_slab),
        grid=(bsz, seq // tt),
        in_specs=[slab] * (2 * n_slab) + [pl.BlockSpec((1, tt // SUBLANES, SUBLANES), lambda b, j: (b, j, 0))]
                 + [_layer_spec(a.shape, layer) for a in (conv_w, conv_b, w_gates, b_gates, lam)],
        out_specs=[slab] * n_slab,
        out_shape=[jax.ShapeDtypeStruct((bsz, seq, LANES), F32)] * n_slab,
        scratch_shapes=[pltpu.VMEM((SUBLANES, r), F32), pltpu.VMEM((1, r), F32)],
        compiler_params=_cparams("parallel", "arbitrary"),
        name="rglru",
    )(*x_slabs, *g_slabs, pos8, conv_w, conv_b, w_gates, b_gates, lam)


def _attn_kernel(q_ref, k_ref, vt_ref, lamv_ref, sg_ref, o_ref, *, tq, head_dim, lambda_init):
    i = pl.program_id(2)
    hw = 2 * head_dim
    hp = q_ref.shape[2] // hw
    vdim = vt_ref.shape[1] // hp
    lane = lax.broadcasted_iota(jnp.int32, (tq, hw), 1)
    zero = jnp.zeros((tq, hw), q_ref.dtype)
    ones = jnp.ones((2 * SUBLANES, tq), BF16)
    qqs = []
    for a in range(hp):
        q = q_ref[0, :, a * hw:(a + 1) * hw]
        qqs.append(jnp.concatenate([jnp.where(lane < head_dim, q, zero), jnp.where(lane >= head_dim, q, zero)],
                                   axis=0))

    def step(j, carry, masked):
        row0 = pl.multiple_of(j * tq, tq)
        out = []
        for a in range(hp):
            m, acc = carry[2 * a], carry[2 * a + 1]
            kb = k_ref[0, pl.ds(row0, tq), a * hw:(a + 1) * hw]
            s = lax.dot_general(kb, qqs[a], (((1,), (1,)), ((), ())), preferred_element_type=F32)
            if masked:
                key = lax.broadcasted_iota(jnp.int32, s.shape, 0)
                qry = lax.broadcasted_iota(jnp.int32, s.shape, 1)
                qry = jnp.where(qry >= tq, qry - tq, qry)
                s = jnp.where(key <= qry, s, NEG_INF)
            m_new = jnp.maximum(m, jnp.max(s, axis=0, keepdims=True))
            p = jnp.exp2(s - m_new)
            alpha = jnp.exp2(m - m_new)
            vt = jnp.concatenate([vt_ref[j, a * vdim:(a + 1) * vdim, :], ones], axis=0)
            out += [m_new, alpha * acc + jnp.dot(vt, p.astype(BF16), preferred_element_type=F32)]
        return tuple(out)

    init = (jnp.full((1, 2 * tq), NEG_INF, F32), jnp.zeros((vdim + 2 * SUBLANES, 2 * tq), F32)) * hp
    carry = lax.fori_loop(0, i, lambda j, c: step(j, c, False), init)
    carry = step(i, carry, True)

    lamv = lamv_ref[0]
    lam = (jnp.exp(jnp.sum(lamv[0:1] * lamv[1:2], axis=1, keepdims=True))
           - jnp.exp(jnp.sum(lamv[2:3] * lamv[3:4], axis=1, keepdims=True)) + lambda_init)
    for a in range(hp):
        acc = carry[2 * a + 1]
        ot = acc[:vdim] / acc[vdim:vdim + 1]
        o = jnp.transpose(ot[:, :tq] - lam * ot[:, tq:])
        o = _rmsnorm(o, sg_ref[0], SUBLN_EPS) * (1.0 - lambda_init)
        o_ref[0, :, a * vdim:(a + 1) * vdim] = o.astype(o_ref.dtype)


def _diff_attention(q, k, vt, lamv, subln_g, layer, heads, head_dim, lambda_init, tq, hp):
    bsz, seq, _ = q.shape
    vdim = vt.shape[1] // heads
    nkv = seq // tq
    kern = functools.partial(_attn_kernel, tq=tq, head_dim=head_dim, lambda_init=lambda_init)
    return pl.pallas_call(
        kern,
        grid=(bsz, heads // hp, seq // tq),
        in_specs=[pl.BlockSpec((1, tq, hp * 2 * head_dim), lambda b, h, i: (b, i, h)),
                  pl.BlockSpec((1, seq, hp * 2 * head_dim), lambda b, h, i: (b, 0, h)),
                  pl.BlockSpec((nkv, hp * vdim, tq), lambda b, h, i: (b, h, 0)),
                  _layer_spec(lamv.shape, layer), _layer_spec(subln_g.shape, layer)],
        out_specs=pl.BlockSpec((1, tq, hp * vdim), lambda b, h, i: (b, i, h)),
        out_shape=jax.ShapeDtypeStruct((bsz, seq, heads * vdim), BF16),
        compiler_params=_cparams("parallel", "parallel", "arbitrary"),
        name="diff_attn",
    )(q, k, vt, lamv, subln_g)


def _s5_tables(lam_re, lam_im, b_re, b_im, c_re, c_im, d_skip, log_dt, n_steps):
    tc = S5_CHUNK
    g, n, p = b_re.shape
    lr = lam_re.astype(F32)
    li = lam_im.astype(F32)
    dt = jnp.exp(log_dt.astype(F32))[:, None]
    mag = jnp.exp(lr * dt)
    ar = mag * jnp.cos(li * dt)
    ai = mag * jnp.sin(li * dt)
    den = lr * lr + li * li
    cr = ((ar - 1.0) * lr + ai * li) / den
    ci = (ai * lr - (ar - 1.0) * li) / den
    bb_re = cr[..., None] * b_re - ci[..., None] * b_im
    bb_im = cr[..., None] * b_im + ci[..., None] * b_re

    def apow(e):
        e = jnp.asarray(e, F32)[None, None, :]
        m = jnp.exp(e * (lr * dt)[..., None])
        ph = e * (li * dt)[..., None]
        return m * jnp.cos(ph), m * jnp.sin(ph)

    lags = jnp.arange(tc)
    pw_re, pw_im = apow(lags)
    ab_re = pw_re[..., None] * bb_re[:, :, None, :] - pw_im[..., None] * bb_im[:, :, None, :]
    ab_im = pw_re[..., None] * bb_im[:, :, None, :] + pw_im[..., None] * bb_re[:, :, None, :]
    abt_re = ab_re.transpose(0, 1, 3, 2).reshape(g, n, 1, p * tc)
    abt_im = ab_im.transpose(0, 1, 3, 2).reshape(g, n, 1, p * tc)
    ct_re = c_re.transpose(0, 2, 1)[..., None]
    ct_im = c_im.transpose(0, 2, 1)[..., None]
    kl = jnp.sum(ct_re * abt_re - ct_im * abt_im, axis=1).reshape(g, p, p, tc)
    skip = jnp.eye(p, dtype=F32)[None] * d_skip[:, None, :]
    kl = jnp.concatenate([kl[..., :1] + skip[..., None], kl[..., 1:]], axis=-1)
    seq = kl.transpose(0, 1, 3, 2)[:, :, ::-1, :].reshape(g, p, tc * p)
    seq = jnp.concatenate([seq, jnp.zeros((g, p, (tc - 1) * p), F32)], axis=-1)
    rows = [seq[:, :, (tc - 1 - t) * p:(tc - 1 - t) * p + tc * p] for t in range(tc)]
    kt = jnp.stack(rows, axis=1).reshape(g, tc * p, tc * p)
    bt = jnp.concatenate([ab_re[:, :, ::-1, :].reshape(g, n, tc * p), ab_im[:, :, ::-1, :].reshape(g, n, tc * p)],
                         axis=1)
    p1_re, p1_im = apow(lags + 1)
    ca_re = jnp.einsum('gon,gnt->gton', c_re, p1_re) - jnp.einsum('gon,gnt->gton', c_im, p1_im)
    ca_im = jnp.einsum('gon,gnt->gton', c_re, p1_im) + jnp.einsum('gon,gnt->gton', c_im, p1_re)
    ct = jnp.concatenate([ca_re.reshape(g, tc * p, n), -ca_im.reshape(g, tc * p, n)], axis=2)
    st_re, st_im = apow(tc * (2 ** jnp.arange(n_steps)))
    tab = jnp.stack([st_re.transpose(0, 2, 1), st_im.transpose(0, 2, 1)], axis=2)
    tab = tab.reshape(g // 2, 2, 2 * n_steps, n).transpose(0, 2, 1, 3).reshape(g // 2, 2 * n_steps, 2 * n)
    tab = jnp.pad(tab, ((0, 0), (0, -(2 * n_steps) % SUBLANES), (0, 0)))
    return kt.astype(BF16), bt.astype(BF16), ct.astype(BF16), tab.astype(F32)


def _s5_kernel(*refs, n_groups, p, n_slab):
    u_refs = refs[:n_slab]
    kt_ref, bt_ref, ct_ref, tab_ref = refs[n_slab:n_slab + 4]
    o_refs = refs[n_slab + 4:2 * n_slab + 4]
    ut_ref, y_ref, carry_ref, sr_ref, si_ref = refs[2 * n_slab + 4:]
    tc = S5_CHUNK
    c = u_refs[0].shape[1] // tc
    n2 = bt_ref.shape[2]
    half = n2 // 2
    n_steps = int(math.log2(c))

    @pl.when(pl.program_id(1) == 0)
    def _():
        carry_ref[...] = jnp.zeros_like(carry_ref)

    for k in range(tc):
        for a in range(n_slab):
            ut_ref[k, a * LANES:(a + 1) * LANES, :] = jnp.transpose(
                u_refs[a][0, pl.ds(k, c, stride=tc), :]).astype(BF16)

    n_pairs = n_groups // 2

    def local_states(gp, _):
        local = []
        for g in (2 * gp, 2 * gp + 1):
            ug = jnp.concatenate([ut_ref[k, pl.ds(pl.multiple_of(g * p, p), p), :] for k in range(tc)], axis=0)
            y_ref[g] = jnp.dot(kt_ref[0, g], ug, preferred_element_type=F32)
            local.append(jnp.dot(bt_ref[0, g], ug, preferred_element_type=F32))
        sr_ref[gp] = jnp.transpose(jnp.concatenate([local[0][:half], local[1][:half]], axis=0))
        si_ref[gp] = jnp.transpose(jnp.concatenate([local[0][half:], local[1][half:]], axis=0))
        return 0

    lax.fori_loop(0, n_pairs, local_states, 0, unroll=2)

    row = lax.broadcasted_iota(jnp.int32, (n_pairs, c, n2), 1)

    def shift(x, d, fill):
        return jnp.where(row >= d, pltpu.roll(x, d, 1), fill)

    tab = tab_ref[0]
    cin_r = carry_ref[:, 0:1, :]
    cin_i = carry_ref[:, 1:2, :]
    sr = sr_ref[...] + jnp.where(row == 0, tab[:, 0:1] * cin_r - tab[:, 1:2] * cin_i, 0.0)
    si = si_ref[...] + jnp.where(row == 0, tab[:, 0:1] * cin_i + tab[:, 1:2] * cin_r, 0.0)
    for s in range(n_steps):
        d = 1 << s
        ar, ai = tab[:, 2 * s:2 * s + 1], tab[:, 2 * s + 1:2 * s + 2]
        hr, hi = shift(sr, d, 0.0), shift(si, d, 0.0)
        sr, si = sr + ar * hr - ai * hi, si + ar * hi + ai * hr
    carry_ref[:, 0:1, :] = sr[:, c - 1:c]
    carry_ref[:, 1:2, :] = si[:, c - 1:c]
    sr_ref[...] = shift(sr, 1, cin_r)
    si_ref[...] = shift(si, 1, cin_i)

    def carried_response(gp, _):
        pr = jnp.transpose(sr_ref[gp])
        pi = jnp.transpose(si_ref[gp])
        for idx, g in enumerate((2 * gp, 2 * gp + 1)):
            prev = jnp.concatenate([pr[idx * half:(idx + 1) * half], pi[idx * half:(idx + 1) * half]], axis=0)
            y_ref[g] = y_ref[g] + jnp.dot(ct_ref[0, g], prev.astype(BF16), preferred_element_type=F32)
        return 0

    lax.fori_loop(0, n_pairs, carried_response, 0, unroll=2)

    groups_per_slab = LANES // p
    for t in range(tc):
        for a in range(n_slab):
            yt = jnp.concatenate([y_ref[g, t * p:(t + 1) * p, :]
                                  for g in range(a * groups_per_slab, (a + 1) * groups_per_slab)], axis=0)
            o_refs[a][0, pl.ds(t, c, stride=tc), :] = jnp.transpose(yt)


def _s5(u_slabs, tables, layer, c_lanes):
    kt, bt, ct, tab = tables
    n_slab = len(u_slabs)
    bsz, seq, _ = u_slabs[0].shape
    width = n_slab * LANES
    tc = S5_CHUNK
    n_groups, n2 = bt.shape[1], bt.shape[2]
    p = width // n_groups
    rows = c_lanes * tc
    kern = functools.partial(_s5_kernel, n_groups=n_groups, p=p, n_slab=n_slab)
    slab = pl.BlockSpec((1, rows, LANES), lambda b, j: (b, j, 0))
    return pl.pallas_call(
        kern,
        grid=(bsz, seq // rows),
        in_specs=[slab] * n_slab + [_layer_spec(t.shape, layer) for t in (kt, bt, ct, tab)],
        out_specs=[slab] * n_slab,
        out_shape=[jax.ShapeDtypeStruct((bsz, seq, LANES), F32)] * n_slab,
        scratch_shapes=[pltpu.VMEM((tc, width, c_lanes), BF16), pltpu.VMEM((n_groups, tc * p, c_lanes), F32),
                        pltpu.VMEM((n_groups // 2, SUBLANES, n2), F32),
                        pltpu.VMEM((n_groups // 2, c_lanes, n2), F32), pltpu.VMEM((n_groups // 2, c_lanes, n2), F32)],
        compiler_params=_cparams("parallel", "arbitrary"),
        name="s5",
    )(*u_slabs, kt, bt, ct, tab)


def _merge_kernel(*refs, n_groups, n_experts, d_model, n_rslab, n_slab):
    x_ref, ya_ref = refs[:2]
    yr_refs = refs[2:2 + n_rslab]
    ys_refs = refs[2 + n_rslab:2 + n_rslab + n_slab]
    (mg_ref, wg_ref, gw_ref, gb_ref, pr_ref, pa_ref, ps_ref, wo_ref, fg_ref, rw_ref, rb_ref,
     x1_ref, xn_ref, route_ref, routet_ref, tab_ref, cnt_ref, run_ref) = refs[2 + n_rslab + n_slab:]
    i = pl.program_id(0)

    @pl.when(i == 0)
    def _():
        run_ref[...] = jnp.zeros_like(run_ref)

    x = x_ref[...]
    tm = x.shape[0]
    xn = _rmsnorm(x, mg_ref[0], MIX_EPS)
    gates = _sigmoid(jnp.dot(xn.astype(BF16), wg_ref[0], preferred_element_type=F32))
    z = _gelu_tanh(jnp.concatenate([ref[...] for ref in ys_refs], axis=1))
    ys = z * _sigmoid(jnp.dot(z.astype(BF16), gw_ref[0], preferred_element_type=F32) + gb_ref[0])
    y_rnn = jnp.concatenate([ref[...] for ref in yr_refs], axis=1).astype(BF16)
    merged = (gates[:, :d_model] * jnp.dot(y_rnn, pr_ref[0], preferred_element_type=F32)
              + gates[:, d_model:2 * d_model] * jnp.dot(ya_ref[...], pa_ref[0], preferred_element_type=F32)
              + gates[:, 2 * d_model:] * jnp.dot(ys.astype(BF16), ps_ref[0], preferred_element_type=F32))
    x1 = x + jnp.dot(merged.astype(BF16), wo_ref[0], preferred_element_type=F32)
    x1_ref[...] = x1
    xn2 = _rmsnorm(x1, fg_ref[0], MIX_EPS)
    xn_ref[...] = xn2.astype(xn_ref.dtype)

    rw = rw_ref[0]
    x_hi = xn2.astype(BF16)
    x_lo = (xn2 - x_hi.astype(F32)).astype(BF16)
    w_hi = rw.astype(BF16)
    w_lo = (rw - w_hi.astype(F32)).astype(BF16)
    logits = (jnp.dot(x_hi, w_hi, preferred_element_type=F32) + jnp.dot(x_lo, w_hi, preferred_element_type=F32)
              + jnp.dot(x_hi, w_lo, preferred_element_type=F32) + rb_ref[0])
    lane = lax.broadcasted_iota(jnp.int32, logits.shape, 1).astype(F32)
    big = float(LANES)
    coarse = jnp.where(lane < n_groups, logits, NEG_INF)
    cmax = jnp.max(coarse, axis=1, keepdims=True)
    gsel = jnp.min(jnp.where(coarse == cmax, lane, big), axis=1, keepdims=True)
    p_sel = 1.0 / jnp.sum(jnp.where(lane < n_groups, jnp.exp(logits - cmax), 0.0), axis=1, keepdims=True)
    lo = n_groups + gsel * n_experts
    fine = jnp.where((lane >= lo) & (lane < lo + n_experts), logits, NEG_INF)
    m1 = jnp.max(fine, axis=1, keepdims=True)
    i1 = jnp.min(jnp.where(fine == m1, lane, big), axis=1, keepdims=True)
    fine2 = jnp.where(lane == i1, NEG_INF, fine)
    m2 = jnp.max(fine2, axis=1, keepdims=True)
    i2 = jnp.min(jnp.where(fine2 == m2, lane, big), axis=1, keepdims=True)
    e21 = jnp.exp(m2 - m1)
    w1 = p_sel / (1.0 + e21)
    w2 = p_sel * e21 / (1.0 + e21)
    oh1 = lane == i1
    oh2 = lane == i2
    onehot = jnp.where(oh1 | oh2, 1.0, 0.0)
    r_i = lax.broadcasted_iota(jnp.int32, (tm, tm), 0)
    c_i = lax.broadcasted_iota(jnp.int32, (tm, tm), 1)
    earlier = jnp.where(c_i < r_i, 1.0, 0.0).astype(BF16)
    rank = jnp.dot(earlier, onehot.astype(BF16), preferred_element_type=F32)
    cnt = jnp.sum(onehot, axis=0, keepdims=True)
    cnt = jnp.floor((cnt + (SUBLANES - 1)) * (1.0 / SUBLANES)) * SUBLANES
    k_i = lax.broadcasted_iota(jnp.int32, (LANES, LANES), 0)
    l_i = lax.broadcasted_iota(jnp.int32, (LANES, LANES), 1)
    lower = jnp.where(k_i < l_i, 1.0, 0.0)
    start = jnp.dot(jnp.broadcast_to(cnt, (SUBLANES, LANES)), lower, preferred_element_type=F32,
                    precision=lax.Precision.HIGHEST)[0:1]
    pos = rank + start
    lp1 = jnp.sum(jnp.where(oh1, pos, 0.0), axis=1, keepdims=True)
    lp2 = jnp.sum(jnp.where(oh2, pos, 0.0), axis=1, keepdims=True)
    route = jnp.where(lane == 0, w1, 0.0)
    route = jnp.where(lane == 1, w2, route)
    route = jnp.where(lane == 2, lp1, route)
    route = jnp.where(lane == 3, lp2, route)
    route_ref[...] = route
    routet_ref[...] = jnp.transpose(route)[:SUBLANES]
    sub = lax.broadcasted_iota(jnp.int32, (SUBLANES, LANES), 0)
    tab_ref[...] = jnp.where(sub == 0, cnt, jnp.where(sub == 1, start, jnp.where(sub == 2, run_ref[...], 0.0)))
    run_ref[...] = run_ref[...] + cnt
    cnt_ref[...] = run_ref[...]


def _merge(x2d, y_attn, y_rnn_slabs, y_s5_slabs, mix_g, w_gate, glu_w, glu_b, p_rnn, p_attn, p_ssm, w_out, ffn_g,
           rw, rb, layer, n_groups, n_experts, tm):
    n, d = x2d.shape
    row = lambda i: (i, 0)
    n_rslab, n_slab = len(y_rnn_slabs), len(y_s5_slabs)
    kern = functools.partial(_merge_kernel, n_groups=n_groups, n_experts=n_experts, d_model=d, n_rslab=n_rslab,
                             n_slab=n_slab)
    consts = [mix_g, w_gate, glu_w, glu_b, p_rnn, p_attn, p_ssm, w_out, ffn_g, rw, rb]
    return pl.pallas_call(
        kern,
        grid=(n // tm,),
        in_specs=[pl.BlockSpec((tm, d), row), pl.BlockSpec((tm, y_attn.shape[1]), row)]
                 + [pl.BlockSpec((tm, LANES), row)] * (n_rslab + n_slab)
                 + [_layer_spec(a.shape, layer) for a in consts],
        out_specs=[pl.BlockSpec((tm, d), row), pl.BlockSpec((tm, d), row), pl.BlockSpec((tm, LANES), row),
                   pl.BlockSpec((SUBLANES, tm), lambda i: (0, i)), pl.BlockSpec((SUBLANES, LANES), row),
                   pl.BlockSpec((1, LANES), lambda i: (0, 0))],
        out_shape=[jax.ShapeDtypeStruct((n, d), F32), jax.ShapeDtypeStruct((n, d), BF16),
                   jax.ShapeDtypeStruct((n, LANES), F32), jax.ShapeDtypeStruct((SUBLANES, n), F32),
                   jax.ShapeDtypeStruct((n // tm * SUBLANES, LANES), F32), jax.ShapeDtypeStruct((1, LANES), F32)],
        scratch_shapes=[pltpu.VMEM((1, LANES), F32)],
        compiler_params=_cparams("arbitrary"),
        name="merge_router",
    )(x2d, y_attn, *y_rnn_slabs, *y_s5_slabs, *consts)


def _sorted_rows(tm, n_total):
    rows = TOP_K_FINE * tm + n_total * (SUBLANES - 1)
    return -(-rows // LANES) * LANES


RUN_BITS = 6
TILE_BITS = 3


def _run_copies(base, n_total, cnt_ref, loc_ref, dst_ref, local_ref, remote_ref, sem, to_remote, wait):
    def copy(e, off, size):
        lstart = 0 if loc_ref is None else pl.multiple_of(loc_ref[base + e] + off, SUBLANES)
        local = local_ref.at[pl.ds(lstart, size)]
        remote = remote_ref.at[pl.ds(pl.multiple_of(dst_ref[base + e] + off, SUBLANES), size)]
        desc = pltpu.make_async_copy(local, remote, sem) if to_remote else pltpu.make_async_copy(remote, local, sem)
        if wait:
            desc.wait()
        else:
            desc.start()

    def expert(e, _):
        cnt = cnt_ref[base + e]
        big = 1 << RUN_BITS

        def chunk(c, _):
            copy(e,c * big, big)
            return 0

        n_big = cnt >> RUN_BITS
        lax.fori_loop(0, n_big, chunk, 0)
        off = n_big * big
        for b in reversed(range(TILE_BITS, RUN_BITS)):
            size = 1 << b

            @pl.when((cnt & size) != 0)
            def _(off=off, size=size):
                copy(e,off, size)

            off = off + (cnt & size)
        return 0

    lax.fori_loop(0, n_total, expert, 0, unroll=4 if n_total % 4 == 0 else 1)


def _wait_tile(base, n_total, min_rows, cnt_ref, loc_ref, local_ref, remote_ref, sem, to_remote):
    total = loc_ref[base + n_total - 1] + cnt_ref[base + n_total - 1]
    rest = total - min_rows

    def wait(size):
        local, remote = local_ref.at[pl.ds(0, size)], remote_ref.at[pl.ds(0, size)]
        (pltpu.make_async_copy(local, remote, sem) if to_remote else pltpu.make_async_copy(remote, local, sem)).wait()

    wait(min_rows)
    for b in range(TILE_BITS, (n_total * (SUBLANES - 1)).bit_length()):
        @pl.when((rest & (1 << b)) != 0)
        def _(b=b):
            wait(1 << b)


def _dispatch_kernel(cnt_ref, loc_ref, dst_ref, gap_cnt_ref, gap_dst_ref, x_ref, rt_ref, xs_ref, buf_ref, zero_ref,
                     sem, *, n_total):
    i = pl.program_id(0)
    last = pl.num_programs(0) - 1
    slot = i % 2
    tm = x_ref.shape[0]
    rows = buf_ref.shape[1]
    pos = rt_ref[...]
    j = lax.broadcasted_iota(jnp.int32, (rows, tm), 0).astype(F32)
    sel = jnp.where((j == pos[2:3]) | (j == pos[3:4]), 1.0, 0.0).astype(BF16)
    buf_ref[slot] = jnp.dot(sel, x_ref[...], preferred_element_type=F32)

    def copies(tile, tile_slot, wait):
        if wait:
            _wait_tile(tile * n_total, n_total, TOP_K_FINE * tm, cnt_ref, loc_ref, buf_ref.at[tile_slot], xs_ref,
                       sem.at[tile_slot], True)
        else:
            _run_copies(tile * n_total, n_total, cnt_ref, loc_ref, dst_ref, buf_ref.at[tile_slot], xs_ref,
                        sem.at[tile_slot], True, False)

    copies(i, slot, False)

    @pl.when(i > 0)
    def _():
        copies(i - 1, 1 - slot, True)

    @pl.when(i == last)
    def _():
        copies(i, slot, True)
        zero_ref[...] = jnp.zeros_like(zero_ref)
        for wait in (False, True):
            _run_copies(0, gap_cnt_ref.shape[0], gap_cnt_ref, None, gap_dst_ref, zero_ref, xs_ref, sem.at[0], True,
                        wait)


def _dispatch(tile_cnt, tile_loc, tile_dst, gap_cnt, gap_dst, xn2, route_t, n_rows, n_total, tm):
    n, d = xn2.shape
    grid_spec = pltpu.PrefetchScalarGridSpec(
        num_scalar_prefetch=5,
        grid=(n // tm,),
        in_specs=[pl.BlockSpec((tm, d), lambda i, *_: (i, 0)),
                  pl.BlockSpec((SUBLANES, tm), lambda i, *_: (0, i))],
        out_specs=pl.BlockSpec(memory_space=pl.ANY),
        scratch_shapes=[pltpu.VMEM((2, _sorted_rows(tm, n_total), d), F32), pltpu.VMEM((1 << RUN_BITS, d), F32),
                        pltpu.SemaphoreType.DMA((2,))],
    )
    return pl.pallas_call(
        functools.partial(_dispatch_kernel, n_total=n_total),
        grid_spec=grid_spec,
        out_shape=jax.ShapeDtypeStruct((n_rows, d), F32),
        compiler_params=_cparams("arbitrary"),
        name="moe_dispatch",
    )(tile_cnt, tile_loc, tile_dst, gap_cnt, gap_dst, xn2, route_t)


def _experts_kernel(te_ref, first_ref, slot_ref, next_ref, nact_ref, x_ref, w1_ref, w3_ref, w2_ref, o_ref,
                    w1f_ref, w3f_ref, w2f_ref, w1b_ref, w3b_ref, w2b_ref, sem):
    i = pl.program_id(0)

    def fetch(expert, slot, wait):
        for src, dst in ((w1_ref, w1f_ref), (w3_ref, w3f_ref), (w2_ref, w2f_ref)):
            copy = pltpu.make_async_copy(src.at[expert], dst.at[slot], sem.at[slot])
            if wait:
                copy.wait()
            else:
                copy.start()

    @pl.when(i == 0)
    def _():
        fetch(te_ref[0], slot_ref[0], False)

    @pl.when(jnp.logical_and(i < nact_ref[0], first_ref[i] == 1))
    def _():
        slot = slot_ref[i]
        fetch(te_ref[i], slot, True)
        w1b_ref[...] = w1f_ref[slot].astype(BF16)
        w3b_ref[...] = w3f_ref[slot].astype(BF16)
        w2b_ref[...] = w2f_ref[slot].astype(BF16)

        @pl.when(next_ref[i] >= 0)
        def _():
            fetch(next_ref[i], 1 - slot, False)

    @pl.when(i < nact_ref[0])
    def _():
        xb = x_ref[...].astype(BF16)
        h1 = jnp.dot(xb, w1b_ref[...], preferred_element_type=F32)
        h3 = jnp.dot(xb, w3b_ref[...], preferred_element_type=F32)
        hid = h1 * _sigmoid(h1) * h3
        o_ref[...] = jnp.dot(hid.astype(BF16), w2b_ref[...], preferred_element_type=F32).astype(o_ref.dtype)

    @pl.when(i >= nact_ref[0])
    def _():
        o_ref[...] = jnp.zeros_like(o_ref)


def _experts(tile_expert, tile_first, tile_slot, tile_next, n_active, xs, w1, w3, w2, tm):
    n_rows, d = xs.shape
    f = w1.shape[2]
    grid_spec = pltpu.PrefetchScalarGridSpec(
        num_scalar_prefetch=5,
        grid=(n_rows // tm,),
        in_specs=[pl.BlockSpec((tm, d), lambda i, te, fi, sl, nx, na: (jnp.minimum(i, na[0] - 1), 0)),
                  pl.BlockSpec(memory_space=pl.ANY), pl.BlockSpec(memory_space=pl.ANY),
                  pl.BlockSpec(memory_space=pl.ANY)],
        out_specs=pl.BlockSpec((tm, d), lambda i, *_: (i, 0)),
        scratch_shapes=[pltpu.VMEM((2, d, f), F32), pltpu.VMEM((2, d, f), F32), pltpu.VMEM((2, f, d), F32),
                        pltpu.VMEM((d, f), BF16), pltpu.VMEM((d, f), BF16), pltpu.VMEM((f, d), BF16),
                        pltpu.SemaphoreType.DMA((2,))],
    )
    return pl.pallas_call(
        _experts_kernel,
        grid_spec=grid_spec,
        out_shape=jax.ShapeDtypeStruct((n_rows, d), F32),
        compiler_params=_cparams("arbitrary"),
        name="moe_experts",
    )(tile_expert, tile_first, tile_slot, tile_next, n_active, xs, w1, w3, w2)


def _combine_kernel(cnt_ref, loc_ref, dst_ref, x_ref, route_ref, rt_ref, fg_ref, ys_ref, o_ref, buf_ref, sem,
                    *, n_total, final_norm):
    i = pl.program_id(0)
    slot = i % 2
    tm = x_ref.shape[0]
    rows = buf_ref.shape[1]

    def copies(tile, tile_slot, wait):
        if wait:
            _wait_tile(tile * n_total, n_total, TOP_K_FINE * tm, cnt_ref, loc_ref, buf_ref.at[tile_slot], ys_ref,
                       sem.at[tile_slot], False)
        else:
            _run_copies(tile * n_total, n_total, cnt_ref, loc_ref, dst_ref, buf_ref.at[tile_slot], ys_ref,
                        sem.at[tile_slot], False, False)

    @pl.when(i == 0)
    def _():
        buf_ref[...] = jnp.zeros_like(buf_ref)
        copies(0, 0, False)

    @pl.when(i + 1 < pl.num_programs(0))
    def _():
        copies(i + 1, 1 - slot, False)

    copies(i, slot, True)
    rt = rt_ref[...]
    jr = lax.broadcasted_iota(jnp.int32, (rows, tm), 0).astype(F32)
    gate = jnp.sum(jnp.where(jr == rt[2:3], rt[0:1], 0.0) + jnp.where(jr == rt[3:4], rt[1:2], 0.0),
                   axis=1, keepdims=True)
    yb = (buf_ref[slot] * gate).astype(BF16)
    route = route_ref[...]
    jc = lax.broadcasted_iota(jnp.int32, (tm, rows), 1).astype(F32)
    pick = jnp.where((jc == route[:, 2:3]) | (jc == route[:, 3:4]), 1.0, 0.0).astype(BF16)
    out = x_ref[...] + jnp.dot(pick, yb, preferred_element_type=F32)
    if final_norm:
        out = _rmsnorm(out, fg_ref[...], MIX_EPS)
    o_ref[...] = out


def _combine(tile_cnt, tile_loc, tile_dst, x1, route, route_t, final_g, ys, n_total, tm, final_norm):
    n, d = x1.shape
    kern = functools.partial(_combine_kernel, n_total=n_total, final_norm=final_norm)
    grid_spec = pltpu.PrefetchScalarGridSpec(
        num_scalar_prefetch=3,
        grid=(n // tm,),
        in_specs=[pl.BlockSpec((tm, d), lambda i, *_: (i, 0)),
                  pl.BlockSpec((tm, LANES), lambda i, *_: (i, 0)),
                  pl.BlockSpec((SUBLANES, tm), lambda i, *_: (0, i)),
                  pl.BlockSpec((1, d), lambda i, *_: (0, 0)),
                  pl.BlockSpec(memory_space=pl.ANY)],
        out_specs=pl.BlockSpec((tm, d), lambda i, *_: (i, 0)),
        scratch_shapes=[pltpu.VMEM((2, _sorted_rows(tm, n_total), d), F32), pltpu.SemaphoreType.DMA((2,))],
    )
    return pl.pallas_call(
        kern,
        grid_spec=grid_spec,
        out_shape=jax.ShapeDtypeStruct((n, d), F32),
        compiler_params=_cparams("arbitrary"),
        name="moe_combine",
    )(tile_cnt, tile_loc, tile_dst, x1, route, route_t, final_g, ys)


def _tile_plan(n, seq):
    return dict(inproj=min(1024, n), rglru=min(256, seq), attn=min(512, seq), attn_heads=4,
                s5_lanes=min(LANES, seq // S5_CHUNK), merge=min(512, n), moe=min(512, n))


def kernel(x, positions, mix_norm_g, w_in, conv_w, conv_b, rg_wa, rg_ba, rg_wx, rg_bx, rg_lambda,
           lam_q1, lam_k1, lam_q2, lam_k2, subln_g,
           ssm_lambda_re, ssm_lambda_im, ssm_b_re, ssm_b_im, ssm_c_re, ssm_c_im, ssm_d, ssm_log_dt,
           ssm_glu_w, ssm_glu_b, proj_rnn, proj_attn, proj_ssm, w_out,
           ffn_norm_g, router_coarse_w, router_coarse_b, router_fine_w, router_fine_b,
           expert_w1, expert_w3, expert_w2, final_norm_g):
    return _forward(_tile_plan(x.shape[0] * x.shape[1], x.shape[1]),
                    x, positions, mix_norm_g, w_in, conv_w, conv_b, rg_wa, rg_ba, rg_wx, rg_bx, rg_lambda,
                    lam_q1, lam_k1, lam_q2, lam_k2, subln_g,
                    ssm_lambda_re, ssm_lambda_im, ssm_b_re, ssm_b_im, ssm_c_re, ssm_c_im, ssm_d, ssm_log_dt,
                    ssm_glu_w, ssm_glu_b, proj_rnn, proj_attn, proj_ssm, w_out,
                    ffn_norm_g, router_coarse_w, router_coarse_b, router_fine_w, router_fine_b,
                    expert_w1, expert_w3, expert_w2, final_norm_g)


def _forward(tiles, x, positions, mix_norm_g, w_in, conv_w, conv_b, rg_wa, rg_ba, rg_wx, rg_bx, rg_lambda,
             lam_q1, lam_k1, lam_q2, lam_k2, subln_g,
             ssm_lambda_re, ssm_lambda_im, ssm_b_re, ssm_b_im, ssm_c_re, ssm_c_im, ssm_d, ssm_log_dt,
             ssm_glu_w, ssm_glu_b, proj_rnn, proj_attn, proj_ssm, w_out,
             ffn_norm_g, router_coarse_w, router_coarse_b, router_fine_w, router_fine_b,
             expert_w1, expert_w3, expert_w2, final_norm_g):
    bsz, seq, d_model = x.shape
    depth = w_in.shape[0]
    n = bsz * seq
    r = conv_w.shape[2]
    sw = ssm_glu_w.shape[1]
    vdim = subln_g.shape[1]
    head_dim = vdim // 2
    in_cols = w_in.shape[2]
    qk = (in_cols - 2 * r - sw - 3 * d_model) // 3
    heads = qk // (2 * head_dim)
    splits = (r, 2 * r, 2 * r + qk, 2 * r + 2 * qk, 2 * r + 3 * qk, 2 * r + 3 * qk + sw)
    mix_cols = splits[-1]
    n_groups = router_coarse_w.shape[2]
    n_experts = expert_w1.shape[2]
    n_total = n_groups * n_experts
    rnn_blocks = rg_wa.shape[1]

    tm_in, tt_rnn, tq = tiles["inproj"], tiles["rglru"], tiles["attn"]
    s5_lanes, tm_merge, tm_moe = tiles["s5_lanes"], tiles["merge"], tiles["moe"]
    n_rows = TOP_K_FINE * n + (n // tm_merge) * n_total * (SUBLANES - 1) + n_total * tm_moe
    n_rows = -(-n_rows // tm_moe) * tm_moe

    posf = positions.astype(F32)
    pos_col = posf.reshape(n, 1)
    pos8 = posf.reshape(bsz, seq // SUBLANES, SUBLANES)
    inv_freq = ROPE_THETA ** (-jnp.arange(0, head_dim, 2, dtype=F32) / head_dim)
    invf = jnp.tile(inv_freq, LANES // (head_dim // 2)).reshape(1, LANES)

    row3 = lambda a: a.reshape(depth, 1, a.shape[-1])
    w_mix = w_in[:, :, :mix_cols].astype(BF16)
    w_gate = w_in[:, :, mix_cols:].astype(BF16)
    eye_blocks = jnp.eye(rnn_blocks, dtype=F32)
    block_diag = lambda w: jnp.einsum('lhij,hk->lhikj', w, eye_blocks).reshape(depth, r, r)
    w_gates = jnp.concatenate([block_diag(rg_wa), block_diag(rg_wx)], axis=2).astype(BF16)
    b_gates = row3(jnp.concatenate([rg_ba, rg_bx], axis=1))
    lamv = jnp.stack([lam_q1, lam_k1, lam_q2, lam_k2], axis=1)
    tables = jax.vmap(functools.partial(_s5_tables, n_steps=int(math.log2(s5_lanes))))(
        ssm_lambda_re, ssm_lambda_im, ssm_b_re, ssm_b_im, ssm_c_re, ssm_c_im, ssm_d, ssm_log_dt)
    rw = jnp.concatenate([router_coarse_w, router_fine_w], axis=2)
    rw = jnp.pad(rw, ((0, 0), (0, 0), (0, LANES - rw.shape[2])))
    rb = jnp.concatenate([router_coarse_b, router_fine_b], axis=1)
    rb = row3(jnp.pad(rb, ((0, 0), (0, LANES - rb.shape[1]))))
    merge_params = (row3(mix_norm_g), w_gate, ssm_glu_w.astype(BF16), row3(ssm_glu_b), proj_rnn.astype(BF16),
                    proj_attn.astype(BF16), proj_ssm.astype(BF16), w_out.astype(BF16), row3(ffn_norm_g), rw, rb)

    x2d = x.reshape(n, d_model)
    for l in range(depth):
        lambda_init = 0.8 - 0.6 * math.exp(-0.3 * l)
        q, k, vt, *slabs = _inproj(x2d, row3(mix_norm_g), pos_col, invf, w_mix, l, splits, head_dim, tm_in, tq)
        slabs = [t.reshape(bsz, seq, LANES) for t in slabs]
        n_rs = r // LANES
        y_rnn = _rglru(slabs[:n_rs], slabs[n_rs:2 * n_rs], pos8, conv_w, row3(conv_b), w_gates, b_gates,
                       row3(rg_lambda), l, tt_rnn)
        y_attn = _diff_attention(q.reshape(bsz, seq, qk), k.reshape(bsz, seq, qk), vt, lamv, row3(subln_g), l,
                                 heads, head_dim, lambda_init, tq, min(tiles["attn_heads"], heads))
        y_s5 = _s5(slabs[2 * n_rs:], tables, l, s5_lanes)
        x1, xn2, route, route_t, tile_tab, counts = _merge(
            x2d, y_attn.reshape(n, qk), [y.reshape(n, LANES) for y in y_rnn], [y.reshape(n, LANES) for y in y_s5],
            *merge_params, l, n_groups, n_experts, tm_merge)

        cnt = counts[0, n_groups:n_groups + n_total].astype(jnp.int32)
        n_tiles = (cnt + tm_moe - 1) // tm_moe
        tile_end = jnp.cumsum(n_tiles)
        offsets = (tile_end - n_tiles) * tm_moe
        tab = tile_tab.reshape(n // tm_merge, SUBLANES, LANES)[:, :, n_groups:n_groups + n_total].astype(jnp.int32)
        tile_cnt = tab[:, 0].reshape(-1)
        tile_loc = tab[:, 1].reshape(-1)
        tile_dst = (tab[:, 2] + offsets[None, :]).reshape(-1)
        n_active = tile_end[-1:]
        gap_dst = jnp.concatenate([offsets + cnt, n_active * tm_moe])
        gap_cnt = jnp.concatenate([tile_end * tm_moe, jnp.full((1,), n_rows, jnp.int32)]) - gap_dst
        tile_ids = jnp.minimum(jnp.arange(n_rows // tm_moe, dtype=jnp.int32), n_active[0] - 1)
        tile_expert = jnp.sum((tile_ids[:, None] >= tile_end[None, :]).astype(jnp.int32), axis=1)
        e_ids = jnp.arange(n_total, dtype=jnp.int32)
        present = n_tiles > 0
        ordinal = jnp.cumsum(present.astype(jnp.int32)) - 1
        later = present[None, :] & (e_ids[None, :] > e_ids[:, None])
        succ = jnp.min(jnp.where(later, e_ids[None, :], n_total), axis=1)
        succ = jnp.where(succ < n_total, succ + l * n_total, -1)
        all_ids = jnp.arange(n_rows // tm_moe, dtype=jnp.int32)
        tile_first = ((all_ids < n_active[0]) & (all_ids == (tile_end - n_tiles)[tile_expert])).astype(jnp.int32)
        tile_slot = ordinal[tile_expert] % 2
        tile_next = succ[tile_expert]

        xs = _dispatch(tile_cnt, tile_loc, tile_dst, gap_cnt, gap_dst, xn2, route_t, n_rows, n_total, tm_merge)
        ys = _experts(tile_expert + l * n_total, tile_first, tile_slot, tile_next, n_active.astype(jnp.int32), xs,
                      expert_w1.reshape(depth * n_total, d_model, -1), expert_w3.reshape(depth * n_total, d_model, -1),
                      expert_w2.reshape(depth * n_total, -1, d_model), tm_moe)
        x2d = _combine(tile_cnt, tile_loc, tile_dst, x1, route, route_t, final_norm_g.reshape(1, d_model), ys,
                       n_total, tm_merge, l == depth - 1)
    return x2d.reshape(bsz, seq, d_model)
```

```python
import functools
import math

import jax
import jax.numpy as jnp
from jax import lax
from jax.experimental import pallas as pl
from jax.experimental.pallas import tpu as pltpu

F32 = jnp.float32
BF16 = jnp.bfloat16

RGLRU_C = 8.0
ROPE_THETA = 10000.0
TOP_K_FINE = 2
NEG_INF = -1e30
MIX_EPS = 1e-6
SUBLN_EPS = 1e-5

LANES = 128
SUBLANES = 8
VMEM_LIMIT_BYTES = 56 * 1024 * 1024

S5_CHUNK = 16


def _cparams(*sem):
    return pltpu.CompilerParams(dimension_semantics=sem, vmem_limit_bytes=VMEM_LIMIT_BYTES)


def _const_spec(shape):
    nd = len(shape)
    return pl.BlockSpec(shape, lambda *_: (0,) * nd, pipeline_mode=pl.Buffered(1))


def _layer_spec(shape, layer):
    nd = len(shape)
    return pl.BlockSpec((1,) + tuple(shape[1:]), lambda *_: (layer,) + (0,) * (nd - 1), pipeline_mode=pl.Buffered(1))


def _gelu_tanh(x):
    return 0.5 * x * (1.0 + jnp.tanh(math.sqrt(2.0 / math.pi) * (x + 0.044715 * (x * x * x))))


def _sigmoid(x):
    return 0.5 + 0.5 * jnp.tanh(0.5 * x)


def _rmsnorm(x, g, eps):
    return x * lax.rsqrt(jnp.mean(x * x, axis=-1, keepdims=True) + eps) * g


def _inproj_kernel(x_ref, g_ref, pos_ref, invf_ref, w_ref, q_ref, k_ref, vt_ref, *slab_refs,
                   splits, slab_cols, head_dim, q_scale):
    x = x_ref[...]
    xn = _rmsnorm(x, g_ref[0], MIX_EPS)
    h = jnp.dot(xn.astype(BF16), w_ref[0], preferred_element_type=F32)
    s0, s1, s2, s3, s4, s5 = splits
    for col, ref in zip(slab_cols, slab_refs):
        ref[...] = h[:, col:col + LANES]
    tkv = vt_ref.shape[2]
    for c in range(vt_ref.shape[0]):
        vt_ref[c] = jnp.transpose(h[c * tkv:(c + 1) * tkv, s3:s4]).astype(vt_ref.dtype)

    ang = pos_ref[...] * invf_ref[...]
    cos = jnp.cos(ang)
    sin = jnp.sin(ang)
    lane = lax.broadcasted_iota(jnp.int32, ang.shape, 1)
    first_half = (lane % head_dim) < (head_dim // 2)
    sin_signed = jnp.where(first_half, -sin, sin)

    def rope(t, scale):
        outs = []
        for a in range(t.shape[1] // LANES):
            xs = t[:, a * LANES:(a + 1) * LANES]
            fwd = pltpu.roll(xs, LANES - head_dim // 2, 1)
            bwd = pltpu.roll(xs, head_dim // 2, 1)
            rot = jnp.where(first_half, fwd, bwd)
            outs.append((xs * cos + rot * sin_signed) * scale)
        return jnp.concatenate(outs, axis=1)

    q_ref[...] = rope(h[:, s1:s2], q_scale).astype(q_ref.dtype)
    k_ref[...] = rope(h[:, s2:s3], 1.0).astype(k_ref.dtype)


def _inproj(x2d, g, posf, invf, w, layer, splits, head_dim, tm, tkv):
    n, d = x2d.shape
    widths = [splits[0]] + [splits[i] - splits[i - 1] for i in range(1, 6)]
    slab_cols = [start + a * LANES for start, width in ((0, widths[0]), (splits[0], widths[1]), (splits[4], widths[5]))
                 for a in range(width // LANES)]
    kern = functools.partial(_inproj_kernel, splits=splits, slab_cols=tuple(slab_cols), head_dim=head_dim,
                             q_scale=head_dim ** -0.5 * math.log2(math.e))
    row = lambda i: (i, 0)
    rows = lambda wd: pl.BlockSpec((tm, wd), row)
    out = lambda wd, dt: jax.ShapeDtypeStruct((n, wd), dt)
    return pl.pallas_call(
        kern,
        grid=(n // tm,),
        in_specs=[pl.BlockSpec((tm, d), row), _layer_spec(g.shape, layer), pl.BlockSpec((tm, 1), row),
                  _const_spec((1, LANES)), _layer_spec(w.shape, layer)],
        out_specs=[rows(widths[2]), rows(widths[3]),
                   pl.BlockSpec((tm // tkv, widths[4], tkv), lambda i: (i, 0, 0))] + [rows(LANES)] * len(slab_cols),
        out_shape=[out(widths[2], BF16), out(widths[3], BF16),
                   jax.ShapeDtypeStruct((n // tkv, widths[4], tkv), BF16)] + [out(LANES, F32)] * len(slab_cols),
        compiler_params=_cparams("parallel"),
        name="inproj",
    )(x2d, g, posf, invf, w)


def _rglru_kernel(*refs, n_slab):
    x_refs, g_refs = refs[:n_slab], refs[n_slab:2 * n_slab]
    pos_ref, cw_ref, cb_ref, w_ref, b_ref, lam_ref = refs[2 * n_slab:2 * n_slab + 6]
    o_refs = refs[2 * n_slab + 6:3 * n_slab + 6]
    halo_ref, h_ref = refs[3 * n_slab + 6:]
    j = pl.program_id(1)
    ph = SUBLANES
    m = x_refs[0].shape[1] // ph
    r = n_slab * LANES

    @pl.when(j == 0)
    def _():
        halo_ref[...] = jnp.zeros_like(halo_ref)
        h_ref[...] = jnp.zeros_like(h_ref)

    def phase(slabs, s):
        return jnp.concatenate([ref[0, pl.ds(s, m, stride=ph), :] for ref in slabs], axis=1)

    block = lax.broadcasted_iota(jnp.int32, (m, r), 0)

    def one_block_back(v, first):
        return jnp.where(block == 0, first, pltpu.roll(v, 1, 0))

    xs = [phase(x_refs, s) for s in range(ph)]
    halo = halo_ref[...]
    cw = cw_ref[0]
    taps = cw.shape[0]
    earlier = {s: one_block_back(xs[s], halo[s:s + 1]) for s in range(ph - taps + 1, ph)}
    xcs = []
    for s in range(ph):
        xc = cb_ref[0] + cw[0:1] * xs[s]
        for k in range(1, taps):
            xc = xc + cw[k:k + 1] * (xs[s - k] if s >= k else earlier[s - k + ph])
        xcs.append(xc)
    halo_ref[...] = jnp.concatenate([xs[s][m - 1:m] for s in range(ph)], axis=0)
    xc = jnp.concatenate(xcs, axis=0)

    gates = jnp.dot(xc.astype(BF16), w_ref[0], preferred_element_type=F32) + b_ref[0]
    rg = _sigmoid(gates[:, :r])
    ig = _sigmoid(gates[:, r:])
    z = -lam_ref[0]
    softplus = jnp.maximum(z, 0.0) + jnp.log(1.0 + jnp.exp(-jnp.abs(z)))
    a = jnp.exp((-RGLRU_C) * rg * softplus)
    mult = jnp.sqrt(1.0 - a * a)
    pos = pos_ref[0]
    reset = jnp.concatenate([pos[:, s:s + 1] for s in range(ph)], axis=0) == 0.0
    a = jnp.where(reset, 0.0, a)
    mult = jnp.where(reset, 1.0, mult)
    b = mult * ig * xc

    pa, pb = [a[0:m]], [b[0:m]]
    for s in range(1, ph):
        a_s = a[s * m:(s + 1) * m]
        pb.append(a_s * pb[-1] + b[s * m:(s + 1) * m])
        pa.append(a_s * pa[-1])
    ba, bb = pa[-1], pb[-1]
    d = 1
    while d < m:
        keep = block >= d
        a_sh = jnp.where(keep, pltpu.roll(ba, d, 0), 1.0)
        b_sh = jnp.where(keep, pltpu.roll(bb, d, 0), 0.0)
        bb = bb + ba * b_sh
        ba = ba * a_sh
        d *= 2
    h_prev = h_ref[...]
    h_end = bb + ba * h_prev
    h_in = one_block_back(h_end, h_prev)
    h_ref[...] = h_end[m - 1:m]
    for s in range(ph):
        out = (pb[s] + pa[s] * h_in) * _gelu_tanh(phase(g_refs, s))
        for c, o_ref in enumerate(o_refs):
            o_ref[0, pl.ds(s, m, stride=ph), :] = out[:, c * LANES:(c + 1) * LANES]


def _rglru(x_slabs, g_slabs, pos8, conv_w, conv_b, w_gates, b_gates, lam, layer, tt):
    n_slab = len(x_slabs)
    bsz, seq, _ = x_slabs[0].shape
    r = n_slab * LANES
    slab = pl.BlockSpec((1, tt, LANES), lambda b, j: (b, j, 0))
    return pl.pallas_call(
        functools.partial(_rglru_kernel, n_slab=n_slab),
        grid=(bsz, seq // tt),
        in_specs=[slab] * (2 * n_slab) + [pl.BlockSpec((1, tt // SUBLANES, SUBLANES), lambda b, j: (b, j, 0))]
                 + [_layer_spec(a.shape, layer) for a in (conv_w, conv_b, w_gates, b_gates, lam)],
        out_specs=[slab] * n_slab,
        out_shape=[jax.ShapeDtypeStruct((bsz, seq, LANES), F32)] * n_slab,
        scratch_shapes=[pltpu.VMEM((SUBLANES, r), F32), pltpu.VMEM((1, r), F32)],
        compiler_params=_cparams("parallel", "arbitrary"),
        name="rglru",
    )(*x_slabs, *g_slabs, pos8, conv_w, conv_b, w_gates, b_gates, lam)


def _attn_kernel(q_ref, k_ref, vt_ref, lamv_ref, sg_ref, o_ref, *, tq, head_dim, lambda_init):
    i = pl.program_id(2)
    hw = 2 * head_dim
    hp = q_ref.shape[2] // hw
    vdim = vt_ref.shape[1] // hp
    lane = lax.broadcasted_iota(jnp.int32, (tq, hw), 1)
    zero = jnp.zeros((tq, hw), q_ref.dtype)
    ones = jnp.ones((2 * SUBLANES, tq), BF16)
    qqs = []
    for a in range(hp):
        q = q_ref[0, :, a * hw:(a + 1) * hw]
        qqs.append(jnp.concatenate([jnp.where(lane < head_dim, q, zero), jnp.where(lane >= head_dim, q, zero)],
                                   axis=0))

    def step(j, carry, masked):
        row0 = pl.multiple_of(j * tq, tq)
        out = []
        for a in range(hp):
            m, acc = carry[2 * a], carry[2 * a + 1]
            kb = k_ref[0, pl.ds(row0, tq), a * hw:(a + 1) * hw]
            s = lax.dot_general(kb, qqs[a], (((1,), (1,)), ((), ())), preferred_element_type=F32)
            if masked:
                key = lax.broadcasted_iota(jnp.int32, s.shape, 0)
                qry = lax.broadcasted_iota(jnp.int32, s.shape, 1)
                qry = jnp.where(qry >= tq, qry - tq, qry)
                s = jnp.where(key <= qry, s, NEG_INF)
            m_new = jnp.maximum(m, jnp.max(s, axis=0, keepdims=True))
            p = jnp.exp2(s - m_new)
            alpha = jnp.exp2(m - m_new)
            vt = jnp.concatenate([vt_ref[j, a * vdim:(a + 1) * vdim, :], ones], axis=0)
            out += [m_new, alpha * acc + jnp.dot(vt, p.astype(BF16), preferred_element_type=F32)]
        return tuple(out)

    init = (jnp.full((1, 2 * tq), NEG_INF, F32), jnp.zeros((vdim + 2 * SUBLANES, 2 * tq), F32)) * hp
    carry = lax.fori_loop(0, i, lambda j, c: step(j, c, False), init)
    carry = step(i, carry, True)

    lamv = lamv_ref[0]
    lam = (jnp.exp(jnp.sum(lamv[0:1] * lamv[1:2], axis=1, keepdims=True))
           - jnp.exp(jnp.sum(lamv[2:3] * lamv[3:4], axis=1, keepdims=True)) + lambda_init)
    for a in range(hp):
        acc = carry[2 * a + 1]
        ot = acc[:vdim] / acc[vdim:vdim + 1]
        o = jnp.transpose(ot[:, :tq] - lam * ot[:, tq:])
        o = _rmsnorm(o, sg_ref[0], SUBLN_EPS) * (1.0 - lambda_init)
        o_ref[0, :, a * vdim:(a + 1) * vdim] = o.astype(o_ref.dtype)


def _diff_attention(q, k, vt, lamv, subln_g, layer, heads, head_dim, lambda_init, tq, hp):
    bsz, seq, _ = q.shape
    vdim = vt.shape[1] // heads
    nkv = seq // tq
    kern = functools.partial(_attn_kernel, tq=tq, head_dim=head_dim, lambda_init=lambda_init)
    return pl.pallas_call(
        kern,
        grid=(bsz, heads // hp, seq // tq),
        in_specs=[pl.BlockSpec((1, tq, hp * 2 * head_dim), lambda b, h, i: (b, i, h)),
                  pl.BlockSpec((1, seq, hp * 2 * head_dim), lambda b, h, i: (b, 0, h)),
                  pl.BlockSpec((nkv, hp * vdim, tq), lambda b, h, i: (b, h, 0)),
                  _layer_spec(lamv.shape, layer), _layer_spec(subln_g.shape, layer)],
        out_specs=pl.BlockSpec((1, tq, hp * vdim), lambda b, h, i: (b, i, h)),
        out_shape=jax.ShapeDtypeStruct((bsz, seq, heads * vdim), BF16),
        compiler_params=_cparams("parallel", "parallel", "arbitrary"),
        name="diff_attn",
    )(q, k, vt, lamv, subln_g)


def _s5_tables(lam_re, lam_im, b_re, b_im, c_re, c_im, d_skip, log_dt, n_steps):
    tc = S5_CHUNK
    g, n, p = b_re.shape
    lr = lam_re.astype(F32)
    li = lam_im.astype(F32)
    dt = jnp.exp(log_dt.astype(F32))[:, None]
    mag = jnp.exp(lr * dt)
    ar = mag * jnp.cos(li * dt)
    ai = mag * jnp.sin(li * dt)
    den = lr * lr + li * li
    cr = ((ar - 1.0) * lr + ai * li) / den
    ci = (ai * lr - (ar - 1.0) * li) / den
    bb_re = cr[..., None] * b_re - ci[..., None] * b_im
    bb_im = cr[..., None] * b_im + ci[..., None] * b_re

    def apow(e):
        e = jnp.asarray(e, F32)[None, None, :]
        m = jnp.exp(e * (lr * dt)[..., None])
        ph = e * (li * dt)[..., None]
        return m * jnp.cos(ph), m * jnp.sin(ph)

    lags = jnp.arange(tc)
    pw_re, pw_im = apow(lags[::-1])
    pw_re, pw_im = jnp.repeat(pw_re, p, axis=2), jnp.repeat(pw_im, p, axis=2)
    bt_re, bt_im = jnp.tile(bb_re, tc), jnp.tile(bb_im, tc)
    bt = jnp.concatenate([pw_re * bt_re - pw_im * bt_im, pw_re * bt_im + pw_im * bt_re], axis=1)
    cc = jnp.concatenate([c_re, -c_im], axis=2)
    skip = jnp.eye(p, dtype=F32)[None] * d_skip[:, None, :]
    p1_re, p1_im = apow(lags + 1)
    ca_re = jnp.einsum('gon,gnt->gton', c_re, p1_re) - jnp.einsum('gon,gnt->gton', c_im, p1_im)
    ca_im = jnp.einsum('gon,gnt->gton', c_re, p1_im) + jnp.einsum('gon,gnt->gton', c_im, p1_re)
    ct = jnp.concatenate([ca_re.reshape(g, tc * p, n), -ca_im.reshape(g, tc * p, n)], axis=2)
    st_re, st_im = apow(tc * (2 ** jnp.arange(n_steps)))
    tab = jnp.stack([st_re.transpose(0, 2, 1), st_im.transpose(0, 2, 1)], axis=2)
    tab = tab.reshape(g // 2, 2, 2 * n_steps, n).transpose(0, 2, 1, 3).reshape(g // 2, 2 * n_steps, 2 * n)
    tab = jnp.pad(tab, ((0, 0), (0, -(2 * n_steps) % SUBLANES), (0, 0)))
    return cc, bt, skip, ct.astype(BF16), tab.astype(F32)


def _toeplitz_kernel(cc_ref, bt_ref, skip_ref, o_ref):
    tc = S5_CHUNK
    p = cc_ref.shape[1]
    for gi in range(cc_ref.shape[0]):
        k = jnp.dot(cc_ref[gi], bt_ref[gi], preferred_element_type=F32, precision=lax.Precision.HIGHEST)
        k = jnp.concatenate([k[:, :(tc - 1) * p], k[:, (tc - 1) * p:] + skip_ref[gi]], axis=1)
        s = jnp.concatenate([k, jnp.zeros((p, (tc - 1) * p), F32)], axis=1)
        o_ref[gi] = jnp.concatenate([s[:, (tc - 1 - t) * p:(tc - 1 - t) * p + tc * p] for t in range(tc)],
                                    axis=0).astype(o_ref.dtype)


def _toeplitz(cc, bt, skip, groups_per_step):
    m, p, n2 = cc.shape
    side = bt.shape[2]
    blk = lambda a: pl.BlockSpec((groups_per_step,) + a.shape[1:], lambda i: (i, 0, 0))
    return pl.pallas_call(
        _toeplitz_kernel,
        grid=(m // groups_per_step,),
        in_specs=[blk(cc), blk(bt), blk(skip)],
        out_specs=pl.BlockSpec((groups_per_step, side, side), lambda i: (i, 0, 0)),
        out_shape=jax.ShapeDtypeStruct((m, side, side), BF16),
        compiler_params=_cparams("parallel"),
        name="s5_toeplitz",
    )(cc, bt, skip)


def _s5_kernel(*refs, n_groups, p, n_slab):
    u_refs = refs[:n_slab]
    kt_ref, bt_ref, ct_ref, tab_ref = refs[n_slab:n_slab + 4]
    o_refs = refs[n_slab + 4:2 * n_slab + 4]
    ut_ref, y_ref, carry_ref, sr_ref, si_ref = refs[2 * n_slab + 4:]
    tc = S5_CHUNK
    c = u_refs[0].shape[1] // tc
    n2 = bt_ref.shape[2]
    half = n2 // 2
    n_steps = int(math.log2(c))

    @pl.when(pl.program_id(1) == 0)
    def _():
        carry_ref[...] = jnp.zeros_like(carry_ref)

    for k in range(tc):
        for a in range(n_slab):
            ut_ref[k, a * LANES:(a + 1) * LANES, :] = jnp.transpose(
                u_refs[a][0, pl.ds(k, c, stride=tc), :]).astype(BF16)

    n_pairs = n_groups // 2

    def local_states(gp, _):
        local = []
        for g in (2 * gp, 2 * gp + 1):
            ug = jnp.concatenate([ut_ref[k, pl.ds(pl.multiple_of(g * p, p), p), :] for k in range(tc)], axis=0)
            y_ref[g] = jnp.dot(kt_ref[0, g], ug, preferred_element_type=F32)
            local.append(jnp.dot(bt_ref[0, g], ug, preferred_element_type=F32))
        sr_ref[gp] = jnp.transpose(jnp.concatenate([local[0][:half], local[1][:half]], axis=0))
        si_ref[gp] = jnp.transpose(jnp.concatenate([local[0][half:], local[1][half:]], axis=0))
        return 0

    lax.fori_loop(0, n_pairs, local_states, 0, unroll=2)

    row = lax.broadcasted_iota(jnp.int32, (n_pairs, c, n2), 1)

    def shift(x, d, fill):
        return jnp.where(row >= d, pltpu.roll(x, d, 1), fill)

    tab = tab_ref[0]
    cin_r = carry_ref[:, 0:1, :]
    cin_i = carry_ref[:, 1:2, :]
    sr = sr_ref[...] + jnp.where(row == 0, tab[:, 0:1] * cin_r - tab[:, 1:2] * cin_i, 0.0)
    si = si_ref[...] + jnp.where(row == 0, tab[:, 0:1] * cin_i + tab[:, 1:2] * cin_r, 0.0)
    for s in range(n_steps):
        d = 1 << s
        ar, ai = tab[:, 2 * s:2 * s + 1], tab[:, 2 * s + 1:2 * s + 2]
        hr, hi = shift(sr, d, 0.0), shift(si, d, 0.0)
        sr, si = sr + ar * hr - ai * hi, si + ar * hi + ai * hr
    carry_ref[:, 0:1, :] = sr[:, c - 1:c]
    carry_ref[:, 1:2, :] = si[:, c - 1:c]
    sr_ref[...] = shift(sr, 1, cin_r)
    si_ref[...] = shift(si, 1, cin_i)

    def carried_response(gp, _):
        pr = jnp.transpose(sr_ref[gp])
        pi = jnp.transpose(si_ref[gp])
        for idx, g in enumerate((2 * gp, 2 * gp + 1)):
            prev = jnp.concatenate([pr[idx * half:(idx + 1) * half], pi[idx * half:(idx + 1) * half]], axis=0)
            y_ref[g] = y_ref[g] + jnp.dot(ct_ref[0, g], prev.astype(BF16), preferred_element_type=F32)
        return 0

    lax.fori_loop(0, n_pairs, carried_response, 0, unroll=2)

    groups_per_slab = LANES // p
    for t in range(tc):
        for a in range(n_slab):
            yt = jnp.concatenate([y_ref[g, t * p:(t + 1) * p, :]
                                  for g in range(a * groups_per_slab, (a + 1) * groups_per_slab)], axis=0)
            o_refs[a][0, pl.ds(t, c, stride=tc), :] = jnp.transpose(yt)


def _s5(u_slabs, tables, layer, c_lanes):
    kt, bt, ct, tab = tables
    n_slab = len(u_slabs)
    bsz, seq, _ = u_slabs[0].shape
    width = n_slab * LANES
    tc = S5_CHUNK
    n_groups, n2 = bt.shape[1], bt.shape[2]
    p = width // n_groups
    rows = c_lanes * tc
    kern = functools.partial(_s5_kernel, n_groups=n_groups, p=p, n_slab=n_slab)
    slab = pl.BlockSpec((1, rows, LANES), lambda b, j: (b, j, 0))
    return pl.pallas_call(
        kern,
        grid=(bsz, seq // rows),
        in_specs=[slab] * n_slab + [_layer_spec(t.shape, layer) for t in (kt, bt, ct, tab)],
        out_specs=[slab] * n_slab,
        out_shape=[jax.ShapeDtypeStruct((bsz, seq, LANES), F32)] * n_slab,
        scratch_shapes=[pltpu.VMEM((tc, width, c_lanes), BF16), pltpu.VMEM((n_groups, tc * p, c_lanes), F32),
                        pltpu.VMEM((n_groups // 2, SUBLANES, n2), F32),
                        pltpu.VMEM((n_groups // 2, c_lanes, n2), F32), pltpu.VMEM((n_groups // 2, c_lanes, n2), F32)],
        compiler_params=_cparams("parallel", "arbitrary"),
        name="s5",
    )(*u_slabs, kt, bt, ct, tab)


def _merge_kernel(*refs, n_groups, n_experts, d_model, n_rslab, n_slab):
    x_ref, ya_ref = refs[:2]
    yr_refs = refs[2:2 + n_rslab]
    ys_refs = refs[2 + n_rslab:2 + n_rslab + n_slab]
    (mg_ref, wg_ref, gw_ref, gb_ref, pr_ref, pa_ref, ps_ref, wo_ref, fg_ref, rw_ref, rb_ref,
     x1_ref, xn_ref, route_ref, routet_ref, tab_ref, cnt_ref, run_ref) = refs[2 + n_rslab + n_slab:]
    i = pl.program_id(0)

    @pl.when(i == 0)
    def _():
        run_ref[...] = jnp.zeros_like(run_ref)

    x = x_ref[...]
    tm = x.shape[0]
    xn = _rmsnorm(x, mg_ref[0], MIX_EPS)
    gates = _sigmoid(jnp.dot(xn.astype(BF16), wg_ref[0], preferred_element_type=F32))
    z = _gelu_tanh(jnp.concatenate([ref[...] for ref in ys_refs], axis=1))
    ys = z * _sigmoid(jnp.dot(z.astype(BF16), gw_ref[0], preferred_element_type=F32) + gb_ref[0])
    y_rnn = jnp.concatenate([ref[...] for ref in yr_refs], axis=1).astype(BF16)
    merged = (gates[:, :d_model] * jnp.dot(y_rnn, pr_ref[0], preferred_element_type=F32)
              + gates[:, d_model:2 * d_model] * jnp.dot(ya_ref[...], pa_ref[0], preferred_element_type=F32)
              + gates[:, 2 * d_model:] * jnp.dot(ys.astype(BF16), ps_ref[0], preferred_element_type=F32))
    x1 = x + jnp.dot(merged.astype(BF16), wo_ref[0], preferred_element_type=F32)
    x1_ref[...] = x1
    xn2 = _rmsnorm(x1, fg_ref[0], MIX_EPS)
    xn_ref[...] = xn2.astype(xn_ref.dtype)

    rw = rw_ref[0]
    x_hi = xn2.astype(BF16)
    x_lo = (xn2 - x_hi.astype(F32)).astype(BF16)
    w_hi = rw.astype(BF16)
    w_lo = (rw - w_hi.astype(F32)).astype(BF16)
    logits = (jnp.dot(x_hi, w_hi, preferred_element_type=F32) + jnp.dot(x_lo, w_hi, preferred_element_type=F32)
              + jnp.dot(x_hi, w_lo, preferred_element_type=F32) + rb_ref[0])
    lane = lax.broadcasted_iota(jnp.int32, logits.shape, 1).astype(F32)
    big = float(LANES)
    coarse = jnp.where(lane < n_groups, logits, NEG_INF)
    cmax = jnp.max(coarse, axis=1, keepdims=True)
    gsel = jnp.min(jnp.where(coarse == cmax, lane, big), axis=1, keepdims=True)
    p_sel = 1.0 / jnp.sum(jnp.where(lane < n_groups, jnp.exp(logits - cmax), 0.0), axis=1, keepdims=True)
    lo = n_groups + gsel * n_experts
    fine = jnp.where((lane >= lo) & (lane < lo + n_experts), logits, NEG_INF)
    m1 = jnp.max(fine, axis=1, keepdims=True)
    i1 = jnp.min(jnp.where(fine == m1, lane, big), axis=1, keepdims=True)
    fine2 = jnp.where(lane == i1, NEG_INF, fine)
    m2 = jnp.max(fine2, axis=1, keepdims=True)
    i2 = jnp.min(jnp.where(fine2 == m2, lane, big), axis=1, keepdims=True)
    e21 = jnp.exp(m2 - m1)
    w1 = p_sel / (1.0 + e21)
    w2 = p_sel * e21 / (1.0 + e21)
    oh1 = lane == i1
    oh2 = lane == i2
    onehot = jnp.where(oh1 | oh2, 1.0, 0.0)
    r_i = lax.broadcasted_iota(jnp.int32, (tm, tm), 0)
    c_i = lax.broadcasted_iota(jnp.int32, (tm, tm), 1)
    earlier = jnp.where(c_i < r_i, 1.0, 0.0).astype(BF16)
    rank = jnp.dot(earlier, onehot.astype(BF16), preferred_element_type=F32)
    cnt = jnp.sum(onehot, axis=0, keepdims=True)
    cnt = jnp.floor((cnt + (SUBLANES - 1)) * (1.0 / SUBLANES)) * SUBLANES
    k_i = lax.broadcasted_iota(jnp.int32, (LANES, LANES), 0)
    l_i = lax.broadcasted_iota(jnp.int32, (LANES, LANES), 1)
    lower = jnp.where(k_i < l_i, 1.0, 0.0)
    start = jnp.dot(jnp.broadcast_to(cnt, (SUBLANES, LANES)), lower, preferred_element_type=F32,
                    precision=lax.Precision.HIGHEST)[0:1]
    pos = rank + start
    lp1 = jnp.sum(jnp.where(oh1, pos, 0.0), axis=1, keepdims=True)
    lp2 = jnp.sum(jnp.where(oh2, pos, 0.0), axis=1, keepdims=True)
    route = jnp.where(lane == 0, w1, 0.0)
    route = jnp.where(lane == 1, w2, route)
    route = jnp.where(lane == 2, lp1, route)
    route = jnp.where(lane == 3, lp2, route)
    route_ref[...] = route
    routet_ref[...] = jnp.transpose(route)[:SUBLANES]
    sub = lax.broadcasted_iota(jnp.int32, (SUBLANES, LANES), 0)
    tab_ref[...] = jnp.where(sub == 0, cnt, jnp.where(sub == 1, start, jnp.where(sub == 2, run_ref[...], 0.0)))
    run_ref[...] = run_ref[...] + cnt
    cnt_ref[...] = run_ref[...]


def _merge(x2d, y_attn, y_rnn_slabs, y_s5_slabs, mix_g, w_gate, glu_w, glu_b, p_rnn, p_attn, p_ssm, w_out, ffn_g,
           rw, rb, layer, n_groups, n_experts, tm):
    n, d = x2d.shape
    row = lambda i: (i, 0)
    n_rslab, n_slab = len(y_rnn_slabs), len(y_s5_slabs)
    kern = functools.partial(_merge_kernel, n_groups=n_groups, n_experts=n_experts, d_model=d, n_rslab=n_rslab,
                             n_slab=n_slab)
    consts = [mix_g, w_gate, glu_w, glu_b, p_rnn, p_attn, p_ssm, w_out, ffn_g, rw, rb]
    return pl.pallas_call(
        kern,
        grid=(n // tm,),
        in_specs=[pl.BlockSpec((tm, d), row), pl.BlockSpec((tm, y_attn.shape[1]), row)]
                 + [pl.BlockSpec((tm, LANES), row)] * (n_rslab + n_slab)
                 + [_layer_spec(a.shape, layer) for a in consts],
        out_specs=[pl.BlockSpec((tm, d), row), pl.BlockSpec((tm, d), row), pl.BlockSpec((tm, LANES), row),
                   pl.BlockSpec((SUBLANES, tm), lambda i: (0, i)), pl.BlockSpec((SUBLANES, LANES), row),
                   pl.BlockSpec((1, LANES), lambda i: (0, 0))],
        out_shape=[jax.ShapeDtypeStruct((n, d), F32), jax.ShapeDtypeStruct((n, d), BF16),
                   jax.ShapeDtypeStruct((n, LANES), F32), jax.ShapeDtypeStruct((SUBLANES, n), F32),
                   jax.ShapeDtypeStruct((n // tm * SUBLANES, LANES), F32), jax.ShapeDtypeStruct((1, LANES), F32)],
        scratch_shapes=[pltpu.VMEM((1, LANES), F32)],
        compiler_params=_cparams("arbitrary"),
        name="merge_router",
    )(x2d, y_attn, *y_rnn_slabs, *y_s5_slabs, *consts)


def _sorted_rows(tm, n_total):
    rows = TOP_K_FINE * tm + n_total * (SUBLANES - 1)
    return -(-rows // LANES) * LANES


RUN_BITS = 6
TILE_BITS = 3


def _run_copies(base, n_total, cnt_ref, loc_ref, dst_ref, local_ref, remote_ref, sem, to_remote, wait):
    def copy(e, off, size):
        lstart = 0 if loc_ref is None else pl.multiple_of(loc_ref[base + e] + off, SUBLANES)
        local = local_ref.at[pl.ds(lstart, size)]
        remote = remote_ref.at[pl.ds(pl.multiple_of(dst_ref[base + e] + off, SUBLANES), size)]
        desc = pltpu.make_async_copy(local, remote, sem) if to_remote else pltpu.make_async_copy(remote, local, sem)
        if wait:
            desc.wait()
        else:
            desc.start()

    def expert(e, _):
        cnt = cnt_ref[base + e]
        big = 1 << RUN_BITS

        def chunk(c, _):
            copy(e,c * big, big)
            return 0

        n_big = cnt >> RUN_BITS
        lax.fori_loop(0, n_big, chunk, 0)
        off = n_big * big
        for b in reversed(range(TILE_BITS, RUN_BITS)):
            size = 1 << b

            @pl.when((cnt & size) != 0)
            def _(off=off, size=size):
                copy(e,off, size)

            off = off + (cnt & size)
        return 0

    lax.fori_loop(0, n_total, expert, 0, unroll=4 if n_total % 4 == 0 else 1)


def _wait_tile(base, n_total, min_rows, cnt_ref, loc_ref, local_ref, remote_ref, sem, to_remote):
    total = loc_ref[base + n_total - 1] + cnt_ref[base + n_total - 1]
    rest = total - min_rows

    def wait(size):
        local, remote = local_ref.at[pl.ds(0, size)], remote_ref.at[pl.ds(0, size)]
        (pltpu.make_async_copy(local, remote, sem) if to_remote else pltpu.make_async_copy(remote, local, sem)).wait()

    wait(min_rows)
    for b in range(TILE_BITS, (n_total * (SUBLANES - 1)).bit_length()):
        @pl.when((rest & (1 << b)) != 0)
        def _(b=b):
            wait(1 << b)


def _dispatch_kernel(cnt_ref, loc_ref, dst_ref, gap_cnt_ref, gap_dst_ref, x_ref, rt_ref, xs_ref, buf_ref, zero_ref,
                     sem, *, n_total):
    i = pl.program_id(0)
    last = pl.num_programs(0) - 1
    slot = i % 2
    tm = x_ref.shape[0]
    rows = buf_ref.shape[1]
    pos = rt_ref[...]
    j = lax.broadcasted_iota(jnp.int32, (rows, tm), 0).astype(F32)
    sel = jnp.where((j == pos[2:3]) | (j == pos[3:4]), 1.0, 0.0).astype(BF16)
    buf_ref[slot] = jnp.dot(sel, x_ref[...], preferred_element_type=F32)

    def copies(tile, tile_slot, wait):
        if wait:
            _wait_tile(tile * n_total, n_total, TOP_K_FINE * tm, cnt_ref, loc_ref, buf_ref.at[tile_slot], xs_ref,
                       sem.at[tile_slot], True)
        else:
            _run_copies(tile * n_total, n_total, cnt_ref, loc_ref, dst_ref, buf_ref.at[tile_slot], xs_ref,
                        sem.at[tile_slot], True, False)

    copies(i, slot, False)

    @pl.when(i > 0)
    def _():
        copies(i - 1, 1 - slot, True)

    @pl.when(i == last)
    def _():
        copies(i, slot, True)
        zero_ref[...] = jnp.zeros_like(zero_ref)
        for wait in (False, True):
            _run_copies(0, gap_cnt_ref.shape[0], gap_cnt_ref, None, gap_dst_ref, zero_ref, xs_ref, sem.at[0], True,
                        wait)


def _dispatch(tile_cnt, tile_loc, tile_dst, gap_cnt, gap_dst, xn2, route_t, n_rows, n_total, tm):
    n, d = xn2.shape
    grid_spec = pltpu.PrefetchScalarGridSpec(
        num_scalar_prefetch=5,
        grid=(n // tm,),
        in_specs=[pl.BlockSpec((tm, d), lambda i, *_: (i, 0)),
                  pl.BlockSpec((SUBLANES, tm), lambda i, *_: (0, i))],
        out_specs=pl.BlockSpec(memory_space=pl.ANY),
        scratch_shapes=[pltpu.VMEM((2, _sorted_rows(tm, n_total), d), F32), pltpu.VMEM((1 << RUN_BITS, d), F32),
                        pltpu.SemaphoreType.DMA((2,))],
    )
    return pl.pallas_call(
        functools.partial(_dispatch_kernel, n_total=n_total),
        grid_spec=grid_spec,
        out_shape=jax.ShapeDtypeStruct((n_rows, d), F32),
        compiler_params=_cparams("arbitrary"),
        name="moe_dispatch",
    )(tile_cnt, tile_loc, tile_dst, gap_cnt, gap_dst, xn2, route_t)


def _experts_kernel(te_ref, first_ref, slot_ref, next_ref, nact_ref, x_ref, w1_ref, w3_ref, w2_ref, o_ref,
                    w1f_ref, w3f_ref, w2f_ref, w1b_ref, w3b_ref, w2b_ref, sem):
    i = pl.program_id(0)

    def fetch(expert, slot, wait):
        for src, dst in ((w1_ref, w1f_ref), (w3_ref, w3f_ref), (w2_ref, w2f_ref)):
            copy = pltpu.make_async_copy(src.at[expert], dst.at[slot], sem.at[slot])
            if wait:
                copy.wait()
            else:
                copy.start()

    @pl.when(i == 0)
    def _():
        fetch(te_ref[0], slot_ref[0], False)

    @pl.when(jnp.logical_and(i < nact_ref[0], first_ref[i] == 1))
    def _():
        slot = slot_ref[i]
        fetch(te_ref[i], slot, True)
        w1b_ref[...] = w1f_ref[slot].astype(BF16)
        w3b_ref[...] = w3f_ref[slot].astype(BF16)
        w2b_ref[...] = w2f_ref[slot].astype(BF16)

        @pl.when(next_ref[i] >= 0)
        def _():
            fetch(next_ref[i], 1 - slot, False)

    @pl.when(i < nact_ref[0])
    def _():
        xb = x_ref[...].astype(BF16)
        h1 = jnp.dot(xb, w1b_ref[...], preferred_element_type=F32)
        h3 = jnp.dot(xb, w3b_ref[...], preferred_element_type=F32)
        hid = h1 * _sigmoid(h1) * h3
        o_ref[...] = jnp.dot(hid.astype(BF16), w2b_ref[...], preferred_element_type=F32).astype(o_ref.dtype)

    @pl.when(i >= nact_ref[0])
    def _():
        o_ref[...] = jnp.zeros_like(o_ref)


def _experts(tile_expert, tile_first, tile_slot, tile_next, n_active, xs, w1, w3, w2, tm):
    n_rows, d = xs.shape
    f = w1.shape[2]
    grid_spec = pltpu.PrefetchScalarGridSpec(
        num_scalar_prefetch=5,
        grid=(n_rows // tm,),
        in_specs=[pl.BlockSpec((tm, d), lambda i, te, fi, sl, nx, na: (jnp.minimum(i, na[0] - 1), 0)),
                  pl.BlockSpec(memory_space=pl.ANY), pl.BlockSpec(memory_space=pl.ANY),
                  pl.BlockSpec(memory_space=pl.ANY)],
        out_specs=pl.BlockSpec((tm, d), lambda i, *_: (i, 0)),
        scratch_shapes=[pltpu.VMEM((2, d, f), F32), pltpu.VMEM((2, d, f), F32), pltpu.VMEM((2, f, d), F32),
                        pltpu.VMEM((d, f), BF16), pltpu.VMEM((d, f), BF16), pltpu.VMEM((f, d), BF16),
                        pltpu.SemaphoreType.DMA((2,))],
    )
    return pl.pallas_call(
        _experts_kernel,
        grid_spec=grid_spec,
        out_shape=jax.ShapeDtypeStruct((n_rows, d), F32),
        compiler_params=_cparams("arbitrary"),
        name="moe_experts",
    )(tile_expert, tile_first, tile_slot, tile_next, n_active, xs, w1, w3, w2)


def _combine_kernel(cnt_ref, loc_ref, dst_ref, x_ref, route_ref, rt_ref, fg_ref, ys_ref, o_ref, buf_ref, sem,
                    *, n_total, final_norm):
    i = pl.program_id(0)
    slot = i % 2
    tm = x_ref.shape[0]
    rows = buf_ref.shape[1]

    def copies(tile, tile_slot, wait):
        if wait:
            _wait_tile(tile * n_total, n_total, TOP_K_FINE * tm, cnt_ref, loc_ref, buf_ref.at[tile_slot], ys_ref,
                       sem.at[tile_slot], False)
        else:
            _run_copies(tile * n_total, n_total, cnt_ref, loc_ref, dst_ref, buf_ref.at[tile_slot], ys_ref,
                        sem.at[tile_slot], False, False)

    @pl.when(i == 0)
    def _():
        buf_ref[...] = jnp.zeros_like(buf_ref)
        copies(0, 0, False)

    @pl.when(i + 1 < pl.num_programs(0))
    def _():
        copies(i + 1, 1 - slot, False)

    copies(i, slot, True)
    rt = rt_ref[...]
    jr = lax.broadcasted_iota(jnp.int32, (rows, tm), 0).astype(F32)
    gate = jnp.sum(jnp.where(jr == rt[2:3], rt[0:1], 0.0) + jnp.where(jr == rt[3:4], rt[1:2], 0.0),
                   axis=1, keepdims=True)
    yb = (buf_ref[slot] * gate).astype(BF16)
    route = route_ref[...]
    jc = lax.broadcasted_iota(jnp.int32, (tm, rows), 1).astype(F32)
    pick = jnp.where((jc == route[:, 2:3]) | (jc == route[:, 3:4]), 1.0, 0.0).astype(BF16)
    out = x_ref[...] + jnp.dot(pick, yb, preferred_element_type=F32)
    if final_norm:
        out = _rmsnorm(out, fg_ref[...], MIX_EPS)
    o_ref[...] = out


def _combine(tile_cnt, tile_loc, tile_dst, x1, route, route_t, final_g, ys, n_total, tm, final_norm):
    n, d = x1.shape
    kern = functools.partial(_combine_kernel, n_total=n_total, final_norm=final_norm)
    grid_spec = pltpu.PrefetchScalarGridSpec(
        num_scalar_prefetch=3,
        grid=(n // tm,),
        in_specs=[pl.BlockSpec((tm, d), lambda i, *_: (i, 0)),
                  pl.BlockSpec((tm, LANES), lambda i, *_: (i, 0)),
                  pl.BlockSpec((SUBLANES, tm), lambda i, *_: (0, i)),
                  pl.BlockSpec((1, d), lambda i, *_: (0, 0)),
                  pl.BlockSpec(memory_space=pl.ANY)],
        out_specs=pl.BlockSpec((tm, d), lambda i, *_: (i, 0)),
        scratch_shapes=[pltpu.VMEM((2, _sorted_rows(tm, n_total), d), F32), pltpu.SemaphoreType.DMA((2,))],
    )
    return pl.pallas_call(
        kern,
        grid_spec=grid_spec,
        out_shape=jax.ShapeDtypeStruct((n, d), F32),
        compiler_params=_cparams("arbitrary"),
        name="moe_combine",
    )(tile_cnt, tile_loc, tile_dst, x1, route, route_t, final_g, ys)


def _tile_plan(n, seq):
    return dict(inproj=min(1024, n), rglru=min(256, seq), attn=min(512, seq), attn_heads=4,
                s5_lanes=min(LANES, seq // S5_CHUNK), merge=min(512, n), moe=min(512, n))


def kernel(x, positions, mix_norm_g, w_in, conv_w, conv_b, rg_wa, rg_ba, rg_wx, rg_bx, rg_lambda,
           lam_q1, lam_k1, lam_q2, lam_k2, subln_g,
           ssm_lambda_re, ssm_lambda_im, ssm_b_re, ssm_b_im, ssm_c_re, ssm_c_im, ssm_d, ssm_log_dt,
           ssm_glu_w, ssm_glu_b, proj_rnn, proj_attn, proj_ssm, w_out,
           ffn_norm_g, router_coarse_w, router_coarse_b, router_fine_w, router_fine_b,
           expert_w1, expert_w3, expert_w2, final_norm_g):
    return _forward(_tile_plan(x.shape[0] * x.shape[1], x.shape[1]),
                    x, positions, mix_norm_g, w_in, conv_w, conv_b, rg_wa, rg_ba, rg_wx, rg_bx, rg_lambda,
                    lam_q1, lam_k1, lam_q2, lam_k2, subln_g,
                    ssm_lambda_re, ssm_lambda_im, ssm_b_re, ssm_b_im, ssm_c_re, ssm_c_im, ssm_d, ssm_log_dt,
                    ssm_glu_w, ssm_glu_b, proj_rnn, proj_attn, proj_ssm, w_out,
                    ffn_norm_g, router_coarse_w, router_coarse_b, router_fine_w, router_fine_b,
                    expert_w1, expert_w3, expert_w2, final_norm_g)


def _forward(tiles, x, positions, mix_norm_g, w_in, conv_w, conv_b, rg_wa, rg_ba, rg_wx, rg_bx, rg_lambda,
             lam_q1, lam_k1, lam_q2, lam_k2, subln_g,
             ssm_lambda_re, ssm_lambda_im, ssm_b_re, ssm_b_im, ssm_c_re, ssm_c_im, ssm_d, ssm_log_dt,
             ssm_glu_w, ssm_glu_b, proj_rnn, proj_attn, proj_ssm, w_out,
             ffn_norm_g, router_coarse_w, router_coarse_b, router_fine_w, router_fine_b,
             expert_w1, expert_w3, expert_w2, final_norm_g):
    bsz, seq, d_model = x.shape
    depth = w_in.shape[0]
    n = bsz * seq
    r = conv_w.shape[2]
    sw = ssm_glu_w.shape[1]
    vdim = subln_g.shape[1]
    head_dim = vdim // 2
    in_cols = w_in.shape[2]
    qk = (in_cols - 2 * r - sw - 3 * d_model) // 3
    heads = qk // (2 * head_dim)
    splits = (r, 2 * r, 2 * r + qk, 2 * r + 2 * qk, 2 * r + 3 * qk, 2 * r + 3 * qk + sw)
    mix_cols = splits[-1]
    n_groups = router_coarse_w.shape[2]
    n_experts = expert_w1.shape[2]
    n_total = n_groups * n_experts
    rnn_blocks = rg_wa.shape[1]

    tm_in, tt_rnn, tq = tiles["inproj"], tiles["rglru"], tiles["attn"]
    s5_lanes, tm_merge, tm_moe = tiles["s5_lanes"], tiles["merge"], tiles["moe"]
    n_rows = TOP_K_FINE * n + (n // tm_merge) * n_total * (SUBLANES - 1) + n_total * tm_moe
    n_rows = -(-n_rows // tm_moe) * tm_moe

    posf = positions.astype(F32)
    pos_col = posf.reshape(n, 1)
    pos8 = posf.reshape(bsz, seq // SUBLANES, SUBLANES)
    inv_freq = ROPE_THETA ** (-jnp.arange(0, head_dim, 2, dtype=F32) / head_dim)
    invf = jnp.tile(inv_freq, LANES // (head_dim // 2)).reshape(1, LANES)

    row3 = lambda a: a.reshape(depth, 1, a.shape[-1])
    w_mix = w_in[:, :, :mix_cols].astype(BF16)
    w_gate = w_in[:, :, mix_cols:].astype(BF16)
    eye_blocks = jnp.eye(rnn_blocks, dtype=F32)
    block_diag = lambda w: jnp.einsum('lhij,hk->lhikj', w, eye_blocks).reshape(depth, r, r)
    w_gates = jnp.concatenate([block_diag(rg_wa), block_diag(rg_wx)], axis=2).astype(BF16)
    b_gates = row3(jnp.concatenate([rg_ba, rg_bx], axis=1))
    lamv = jnp.stack([lam_q1, lam_k1, lam_q2, lam_k2], axis=1)
    cc, bt_f32, skip, ct_t, tab_t = jax.vmap(functools.partial(_s5_tables, n_steps=int(math.log2(s5_lanes))))(
        ssm_lambda_re, ssm_lambda_im, ssm_b_re, ssm_b_im, ssm_c_re, ssm_c_im, ssm_d, ssm_log_dt)
    n_ssm_groups = cc.shape[1]
    flat = lambda a: a.reshape((depth * n_ssm_groups,) + a.shape[2:])
    kt = _toeplitz(flat(cc), flat(bt_f32), flat(skip), math.gcd(n_ssm_groups, SUBLANES))
    tables = (kt.reshape((depth, n_ssm_groups) + kt.shape[1:]), bt_f32.astype(BF16), ct_t, tab_t)
    rw = jnp.concatenate([router_coarse_w, router_fine_w], axis=2)
    rw = jnp.pad(rw, ((0, 0), (0, 0), (0, LANES - rw.shape[2])))
    rb = jnp.concatenate([router_coarse_b, router_fine_b], axis=1)
    rb = row3(jnp.pad(rb, ((0, 0), (0, LANES - rb.shape[1]))))
    merge_params = (row3(mix_norm_g), w_gate, ssm_glu_w.astype(BF16), row3(ssm_glu_b), proj_rnn.astype(BF16),
                    proj_attn.astype(BF16), proj_ssm.astype(BF16), w_out.astype(BF16), row3(ffn_norm_g), rw, rb)

    x2d = x.reshape(n, d_model)
    for l in range(depth):
        lambda_init = 0.8 - 0.6 * math.exp(-0.3 * l)
        q, k, vt, *slabs = _inproj(x2d, row3(mix_norm_g), pos_col, invf, w_mix, l, splits, head_dim, tm_in, tq)
        slabs = [t.reshape(bsz, seq, LANES) for t in slabs]
        n_rs = r // LANES
        y_rnn = _rglru(slabs[:n_rs], slabs[n_rs:2 * n_rs], pos8, conv_w, row3(conv_b), w_gates, b_gates,
                       row3(rg_lambda), l, tt_rnn)
        y_attn = _diff_attention(q.reshape(bsz, seq, qk), k.reshape(bsz, seq, qk), vt, lamv, row3(subln_g), l,
                                 heads, head_dim, lambda_init, tq, min(tiles["attn_heads"], heads))
        y_s5 = _s5(slabs[2 * n_rs:], tables, l, s5_lanes)
        x1, xn2, route, route_t, tile_tab, counts = _merge(
            x2d, y_attn.reshape(n, qk), [y.reshape(n, LANES) for y in y_rnn], [y.reshape(n, LANES) for y in y_s5],
            *merge_params, l, n_groups, n_experts, tm_merge)

        cnt = counts[0, n_groups:n_groups + n_total].astype(jnp.int32)
        n_tiles = (cnt + tm_moe - 1) // tm_moe
        tile_end = jnp.cumsum(n_tiles)
        offsets = (tile_end - n_tiles) * tm_moe
        tab = tile_tab.reshape(n // tm_merge, SUBLANES, LANES)[:, :, n_groups:n_groups + n_total].astype(jnp.int32)
        tile_cnt = tab[:, 0].reshape(-1)
        tile_loc = tab[:, 1].reshape(-1)
        tile_dst = (tab[:, 2] + offsets[None, :]).reshape(-1)
        n_active = tile_end[-1:]
        gap_dst = jnp.concatenate([offsets + cnt, n_active * tm_moe])
        gap_cnt = jnp.concatenate([tile_end * tm_moe, jnp.full((1,), n_rows, jnp.int32)]) - gap_dst
        tile_ids = jnp.minimum(jnp.arange(n_rows // tm_moe, dtype=jnp.int32), n_active[0] - 1)
        tile_expert = jnp.sum((tile_ids[:, None] >= tile_end[None, :]).astype(jnp.int32), axis=1)
        e_ids = jnp.arange(n_total, dtype=jnp.int32)
        present = n_tiles > 0
        ordinal = jnp.cumsum(present.astype(jnp.int32)) - 1
        later = present[None, :] & (e_ids[None, :] > e_ids[:, None])
        succ = jnp.min(jnp.where(later, e_ids[None, :], n_total), axis=1)
        succ = jnp.where(succ < n_total, succ + l * n_total, -1)
        all_ids = jnp.arange(n_rows // tm_moe, dtype=jnp.int32)
        tile_first = ((all_ids < n_active[0]) & (all_ids == (tile_end - n_tiles)[tile_expert])).astype(jnp.int32)
        tile_slot = ordinal[tile_expert] % 2
        tile_next = succ[tile_expert]

        xs = _dispatch(tile_cnt, tile_loc, tile_dst, gap_cnt, gap_dst, xn2, route_t, n_rows, n_total, tm_merge)
        ys = _experts(tile_expert + l * n_total, tile_first, tile_slot, tile_next, n_active.astype(jnp.int32), xs,
                      expert_w1.reshape(depth * n_total, d_model, -1), expert_w3.reshape(depth * n_total, d_model, -1),
                      expert_w2.reshape(depth * n_total, -1, d_model), tm_moe)
        x2d = _combine(tile_cnt, tile_loc, tile_dst, x1, route, route_t, final_norm_g.reshape(1, d_model), ys,
                       n_total, tm_merge, l == depth - 1)
    return x2d.reshape(bsz, seq, d_model)
```

```python
import functools
import math

import jax
import jax.numpy as jnp
from jax import lax
from jax.experimental import pallas as pl
from jax.experimental.pallas import tpu as pltpu

F32 = jnp.float32
BF16 = jnp.bfloat16

RGLRU_C = 8.0
ROPE_THETA = 10000.0
TOP_K_FINE = 2
NEG_INF = -1e30
MIX_EPS = 1e-6
SUBLN_EPS = 1e-5

LANES = 128
SUBLANES = 8
VMEM_LIMIT_BYTES = 56 * 1024 * 1024

S5_CHUNK = 16


def _cparams(*sem):
    return pltpu.CompilerParams(dimension_semantics=sem, vmem_limit_bytes=VMEM_LIMIT_BYTES)


def _const_spec(shape):
    nd = len(shape)
    return pl.BlockSpec(shape, lambda *_: (0,) * nd, pipeline_mode=pl.Buffered(1))


def _layer_spec(shape, layer):
    nd = len(shape)
    return pl.BlockSpec((1,) + tuple(shape[1:]), lambda *_: (layer,) + (0,) * (nd - 1), pipeline_mode=pl.Buffered(1))


def _gelu_tanh(x):
    return 0.5 * x * (1.0 + jnp.tanh(math.sqrt(2.0 / math.pi) * (x + 0.044715 * (x * x * x))))


def _sigmoid(x):
    return 0.5 + 0.5 * jnp.tanh(0.5 * x)


def _rmsnorm(x, g, eps):
    return x * lax.rsqrt(jnp.mean(x * x, axis=-1, keepdims=True) + eps) * g


def _inproj_kernel(x_ref, g_ref, pos_ref, invf_ref, w_ref, *refs, splits, slab_cols, head_dim, q_scale):
    q_ref, k_ref, vt_ref = refs[:3]
    slab_refs, wb_ref = refs[3:-1], refs[-1]

    @pl.when(pl.program_id(0) == 0)
    def _():
        wb_ref[...] = w_ref[0].astype(BF16)

    x = x_ref[...]
    xn = _rmsnorm(x, g_ref[0], MIX_EPS)
    h = jnp.dot(xn.astype(BF16), wb_ref[...], preferred_element_type=F32)
    s0, s1, s2, s3, s4, s5 = splits
    for col, ref in zip(slab_cols, slab_refs):
        ref[...] = h[:, col:col + LANES]
    tkv = vt_ref.shape[2]
    for c in range(vt_ref.shape[0]):
        vt_ref[c] = jnp.transpose(h[c * tkv:(c + 1) * tkv, s3:s4]).astype(vt_ref.dtype)

    ang = pos_ref[...] * invf_ref[...]
    cos = jnp.cos(ang)
    sin = jnp.sin(ang)
    lane = lax.broadcasted_iota(jnp.int32, ang.shape, 1)
    first_half = (lane % head_dim) < (head_dim // 2)
    sin_signed = jnp.where(first_half, -sin, sin)

    def rope(t, scale):
        outs = []
        for a in range(t.shape[1] // LANES):
            xs = t[:, a * LANES:(a + 1) * LANES]
            fwd = pltpu.roll(xs, LANES - head_dim // 2, 1)
            bwd = pltpu.roll(xs, head_dim // 2, 1)
            rot = jnp.where(first_half, fwd, bwd)
            outs.append((xs * cos + rot * sin_signed) * scale)
        return jnp.concatenate(outs, axis=1)

    q_ref[...] = rope(h[:, s1:s2], q_scale).astype(q_ref.dtype)
    k_ref[...] = rope(h[:, s2:s3], 1.0).astype(k_ref.dtype)


def _inproj(x2d, g, posf, invf, w, layer, splits, head_dim, tm, tkv):
    n, d = x2d.shape
    mix_cols = splits[-1]
    widths = [splits[0]] + [splits[i] - splits[i - 1] for i in range(1, 6)]
    slab_cols = [start + a * LANES for start, width in ((0, widths[0]), (splits[0], widths[1]), (splits[4], widths[5]))
                 for a in range(width // LANES)]
    kern = functools.partial(_inproj_kernel, splits=splits, slab_cols=tuple(slab_cols), head_dim=head_dim,
                             q_scale=head_dim ** -0.5 * math.log2(math.e))
    row = lambda i: (i, 0)
    rows = lambda wd: pl.BlockSpec((tm, wd), row)
    out = lambda wd, dt: jax.ShapeDtypeStruct((n, wd), dt)
    return pl.pallas_call(
        kern,
        grid=(n // tm,),
        in_specs=[pl.BlockSpec((tm, d), row), _layer_spec(g.shape, layer), pl.BlockSpec((tm, 1), row),
                  _const_spec((1, LANES)), _layer_spec((w.shape[0], d, mix_cols), layer)],
        out_specs=[rows(widths[2]), rows(widths[3]),
                   pl.BlockSpec((tm // tkv, widths[4], tkv), lambda i: (i, 0, 0))] + [rows(LANES)] * len(slab_cols),
        out_shape=[out(widths[2], BF16), out(widths[3], BF16),
                   jax.ShapeDtypeStruct((n // tkv, widths[4], tkv), BF16)] + [out(LANES, F32)] * len(slab_cols),
        scratch_shapes=[pltpu.VMEM((d, mix_cols), BF16)],
        compiler_params=_cparams("arbitrary"),
        name="inproj",
    )(x2d, g, posf, invf, w)


def _rglru_kernel(*refs, n_slab):
    x_refs, g_refs = refs[:n_slab], refs[n_slab:2 * n_slab]
    pos_ref, cw_ref, cb_ref, w_ref, b_ref, lam_ref = refs[2 * n_slab:2 * n_slab + 6]
    o_refs = refs[2 * n_slab + 6:3 * n_slab + 6]
    halo_ref, h_ref = refs[3 * n_slab + 6:]
    j = pl.program_id(1)
    ph = SUBLANES
    m = x_refs[0].shape[1] // ph
    r = n_slab * LANES

    @pl.when(j == 0)
    def _():
        halo_ref[...] = jnp.zeros_like(halo_ref)
        h_ref[...] = jnp.zeros_like(h_ref)

    def phase(slabs, s):
        return jnp.concatenate([ref[0, pl.ds(s, m, stride=ph), :] for ref in slabs], axis=1)

    block = lax.broadcasted_iota(jnp.int32, (m, r), 0)

    def one_block_back(v, first):
        return jnp.where(block == 0, first, pltpu.roll(v, 1, 0))

    xs = [phase(x_refs, s) for s in range(ph)]
    halo = halo_ref[...]
    cw = cw_ref[0]
    taps = cw.shape[0]
    earlier = {s: one_block_back(xs[s], halo[s:s + 1]) for s in range(ph - taps + 1, ph)}
    xcs = []
    for s in range(ph):
        xc = cb_ref[0] + cw[0:1] * xs[s]
        for k in range(1, taps):
            xc = xc + cw[k:k + 1] * (xs[s - k] if s >= k else earlier[s - k + ph])
        xcs.append(xc)
    halo_ref[...] = jnp.concatenate([xs[s][m - 1:m] for s in range(ph)], axis=0)
    xc = jnp.concatenate(xcs, axis=0)

    gates = jnp.dot(xc.astype(BF16), w_ref[0], preferred_element_type=F32) + b_ref[0]
    rg = _sigmoid(gates[:, :r])
    ig = _sigmoid(gates[:, r:])
    z = -lam_ref[0]
    softplus = jnp.maximum(z, 0.0) + jnp.log(1.0 + jnp.exp(-jnp.abs(z)))
    a = jnp.exp((-RGLRU_C) * rg * softplus)
    mult = jnp.sqrt(1.0 - a * a)
    pos = pos_ref[0]
    reset = jnp.concatenate([pos[:, s:s + 1] for s in range(ph)], axis=0) == 0.0
    a = jnp.where(reset, 0.0, a)
    mult = jnp.where(reset, 1.0, mult)
    b = mult * ig * xc

    pa, pb = [a[0:m]], [b[0:m]]
    for s in range(1, ph):
        a_s = a[s * m:(s + 1) * m]
        pb.append(a_s * pb[-1] + b[s * m:(s + 1) * m])
        pa.append(a_s * pa[-1])
    ba, bb = pa[-1], pb[-1]
    d = 1
    while d < m:
        keep = block >= d
        a_sh = jnp.where(keep, pltpu.roll(ba, d, 0), 1.0)
        b_sh = jnp.where(keep, pltpu.roll(bb, d, 0), 0.0)
        bb = bb + ba * b_sh
        ba = ba * a_sh
        d *= 2
    h_prev = h_ref[...]
    h_end = bb + ba * h_prev
    h_in = one_block_back(h_end, h_prev)
    h_ref[...] = h_end[m - 1:m]
    for s in range(ph):
        out = (pb[s] + pa[s] * h_in) * _gelu_tanh(phase(g_refs, s))
        for c, o_ref in enumerate(o_refs):
            o_ref[0, pl.ds(s, m, stride=ph), :] = out[:, c * LANES:(c + 1) * LANES]


def _rglru(x_slabs, g_slabs, pos8, conv_w, conv_b, w_gates, b_gates, lam, layer, tt):
    n_slab = len(x_slabs)
    bsz, seq, _ = x_slabs[0].shape
    r = n_slab * LANES
    slab = pl.BlockSpec((1, tt, LANES), lambda b, j: (b, j, 0))
    return pl.pallas_call(
        functools.partial(_rglru_kernel, n_slab=n_slab),
        grid=(bsz, seq // tt),
        in_specs=[slab] * (2 * n_slab) + [pl.BlockSpec((1, tt // SUBLANES, SUBLANES), lambda b, j: (b, j, 0))]
                 + [_layer_spec(a.shape, layer) for a in (conv_w, conv_b, w_gates, b_gates, lam)],
        out_specs=[slab] * n_slab,
        out_shape=[jax.ShapeDtypeStruct((bsz, seq, LANES), F32)] * n_slab,
        scratch_shapes=[pltpu.VMEM((SUBLANES, r), F32), pltpu.VMEM((1, r), F32)],
        compiler_params=_cparams("parallel", "arbitrary"),
        name="rglru",
    )(*x_slabs, *g_slabs, pos8, conv_w, conv_b, w_gates, b_gates, lam)


def _attn_kernel(q_ref, k_ref, vt_ref, lamv_ref, sg_ref, o_ref, *, tq, head_dim, lambda_init):
    i = pl.program_id(2)
    hw = 2 * head_dim
    hp = q_ref.shape[2] // hw
    vdim = vt_ref.shape[1] // hp
    lane = lax.broadcasted_iota(jnp.int32, (tq, hw), 1)
    zero = jnp.zeros((tq, hw), q_ref.dtype)
    ones = jnp.ones((2 * SUBLANES, tq), BF16)
    qqs = []
    for a in range(hp):
        q = q_ref[0, :, a * hw:(a + 1) * hw]
        qqs.append(jnp.concatenate([jnp.where(lane < head_dim, q, zero), jnp.where(lane >= head_dim, q, zero)],
                                   axis=0))

    def step(j, carry, masked):
        row0 = pl.multiple_of(j * tq, tq)
        out = []
        for a in range(hp):
            m, acc = carry[2 * a], carry[2 * a + 1]
            kb = k_ref[0, pl.ds(row0, tq), a * hw:(a + 1) * hw]
            s = lax.dot_general(kb, qqs[a], (((1,), (1,)), ((), ())), preferred_element_type=F32)
            if masked:
                key = lax.broadcasted_iota(jnp.int32, s.shape, 0)
                qry = lax.broadcasted_iota(jnp.int32, s.shape, 1)
                qry = jnp.where(qry >= tq, qry - tq, qry)
                s = jnp.where(key <= qry, s, NEG_INF)
            m_new = jnp.maximum(m, jnp.max(s, axis=0, keepdims=True))
            p = jnp.exp2(s - m_new)
            alpha = jnp.exp2(m - m_new)
            vt = jnp.concatenate([vt_ref[j, a * vdim:(a + 1) * vdim, :], ones], axis=0)
            out += [m_new, alpha * acc + jnp.dot(vt, p.astype(BF16), preferred_element_type=F32)]
        return tuple(out)

    init = (jnp.full((1, 2 * tq), NEG_INF, F32), jnp.zeros((vdim + 2 * SUBLANES, 2 * tq), F32)) * hp
    carry = lax.fori_loop(0, i, lambda j, c: step(j, c, False), init)
    carry = step(i, carry, True)

    lamv = lamv_ref[0]
    lam = (jnp.exp(jnp.sum(lamv[0:1] * lamv[1:2], axis=1, keepdims=True))
           - jnp.exp(jnp.sum(lamv[2:3] * lamv[3:4], axis=1, keepdims=True)) + lambda_init)
    for a in range(hp):
        acc = carry[2 * a + 1]
        ot = acc[:vdim] / acc[vdim:vdim + 1]
        o = jnp.transpose(ot[:, :tq] - lam * ot[:, tq:])
        o = _rmsnorm(o, sg_ref[0], SUBLN_EPS) * (1.0 - lambda_init)
        o_ref[0, :, a * vdim:(a + 1) * vdim] = o.astype(o_ref.dtype)


def _diff_attention(q, k, vt, lamv, subln_g, layer, heads, head_dim, lambda_init, tq, hp):
    bsz, seq, _ = q.shape
    vdim = vt.shape[1] // heads
    nkv = seq // tq
    kern = functools.partial(_attn_kernel, tq=tq, head_dim=head_dim, lambda_init=lambda_init)
    return pl.pallas_call(
        kern,
        grid=(bsz, heads // hp, seq // tq),
        in_specs=[pl.BlockSpec((1, tq, hp * 2 * head_dim), lambda b, h, i: (b, i, h)),
                  pl.BlockSpec((1, seq, hp * 2 * head_dim), lambda b, h, i: (b, 0, h)),
                  pl.BlockSpec((nkv, hp * vdim, tq), lambda b, h, i: (b, h, 0)),
                  _layer_spec(lamv.shape, layer), _layer_spec(subln_g.shape, layer)],
        out_specs=pl.BlockSpec((1, tq, hp * vdim), lambda b, h, i: (b, i, h)),
        out_shape=jax.ShapeDtypeStruct((bsz, seq, heads * vdim), BF16),
        compiler_params=_cparams("parallel", "parallel", "arbitrary"),
        name="diff_attn",
    )(q, k, vt, lamv, subln_g)


def _s5_tables(lam_re, lam_im, b_re, b_im, c_re, c_im, d_skip, log_dt, n_steps):
    tc = S5_CHUNK
    g, n, p = b_re.shape
    lr = lam_re.astype(F32)
    li = lam_im.astype(F32)
    dt = jnp.exp(log_dt.astype(F32))[:, None]
    mag = jnp.exp(lr * dt)
    ar = mag * jnp.cos(li * dt)
    ai = mag * jnp.sin(li * dt)
    den = lr * lr + li * li
    cr = ((ar - 1.0) * lr + ai * li) / den
    ci = (ai * lr - (ar - 1.0) * li) / den
    bb_re = cr[..., None] * b_re - ci[..., None] * b_im
    bb_im = cr[..., None] * b_im + ci[..., None] * b_re

    def apow(e):
        e = jnp.asarray(e, F32)[None, None, :]
        m = jnp.exp(e * (lr * dt)[..., None])
        ph = e * (li * dt)[..., None]
        return m * jnp.cos(ph), m * jnp.sin(ph)

    lags = jnp.arange(tc)
    pw_re, pw_im = apow(lags[::-1])
    pw_re, pw_im = jnp.repeat(pw_re, p, axis=2), jnp.repeat(pw_im, p, axis=2)
    bt_re, bt_im = jnp.tile(bb_re, tc), jnp.tile(bb_im, tc)
    bt = jnp.concatenate([pw_re * bt_re - pw_im * bt_im, pw_re * bt_im + pw_im * bt_re], axis=1)
    cc = jnp.concatenate([c_re, -c_im], axis=2)
    skip = jnp.eye(p, dtype=F32)[None] * d_skip[:, None, :]
    p1_re, p1_im = apow(lags + 1)
    ca_re = jnp.einsum('gon,gnt->gton', c_re, p1_re) - jnp.einsum('gon,gnt->gton', c_im, p1_im)
    ca_im = jnp.einsum('gon,gnt->gton', c_re, p1_im) + jnp.einsum('gon,gnt->gton', c_im, p1_re)
    ct = jnp.concatenate([ca_re.reshape(g, tc * p, n), -ca_im.reshape(g, tc * p, n)], axis=2)
    st_re, st_im = apow(tc * (2 ** jnp.arange(n_steps)))
    tab = jnp.stack([st_re.transpose(0, 2, 1), st_im.transpose(0, 2, 1)], axis=2)
    tab = tab.reshape(g // 2, 2, 2 * n_steps, n).transpose(0, 2, 1, 3).reshape(g // 2, 2 * n_steps, 2 * n)
    tab = jnp.pad(tab, ((0, 0), (0, -(2 * n_steps) % SUBLANES), (0, 0)))
    return cc, bt, skip, ct.astype(BF16), tab.astype(F32)


def _toeplitz_kernel(cc_ref, bt_ref, skip_ref, o_ref):
    tc = S5_CHUNK
    p = cc_ref.shape[1]
    for gi in range(cc_ref.shape[0]):
        k = jnp.dot(cc_ref[gi], bt_ref[gi], preferred_element_type=F32, precision=lax.Precision.HIGHEST)
        k = jnp.concatenate([k[:, :(tc - 1) * p], k[:, (tc - 1) * p:] + skip_ref[gi]], axis=1)
        s = jnp.concatenate([k, jnp.zeros((p, (tc - 1) * p), F32)], axis=1)
        o_ref[gi] = jnp.concatenate([s[:, (tc - 1 - t) * p:(tc - 1 - t) * p + tc * p] for t in range(tc)],
                                    axis=0).astype(o_ref.dtype)


def _toeplitz(cc, bt, skip, groups_per_step):
    m, p, n2 = cc.shape
    side = bt.shape[2]
    blk = lambda a: pl.BlockSpec((groups_per_step,) + a.shape[1:], lambda i: (i, 0, 0))
    return pl.pallas_call(
        _toeplitz_kernel,
        grid=(m // groups_per_step,),
        in_specs=[blk(cc), blk(bt), blk(skip)],
        out_specs=pl.BlockSpec((groups_per_step, side, side), lambda i: (i, 0, 0)),
        out_shape=jax.ShapeDtypeStruct((m, side, side), BF16),
        compiler_params=_cparams("parallel"),
        name="s5_toeplitz",
    )(cc, bt, skip)


def _s5_kernel(*refs, n_groups, p, n_slab):
    u_refs = refs[:n_slab]
    kt_ref, bt_ref, ct_ref, tab_ref = refs[n_slab:n_slab + 4]
    o_refs = refs[n_slab + 4:2 * n_slab + 4]
    ut_ref, y_ref, carry_ref, sr_ref, si_ref = refs[2 * n_slab + 4:]
    tc = S5_CHUNK
    c = u_refs[0].shape[1] // tc
    n2 = bt_ref.shape[2]
    half = n2 // 2
    n_steps = int(math.log2(c))

    @pl.when(pl.program_id(1) == 0)
    def _():
        carry_ref[...] = jnp.zeros_like(carry_ref)

    for k in range(tc):
        for a in range(n_slab):
            ut_ref[k, a * LANES:(a + 1) * LANES, :] = jnp.transpose(
                u_refs[a][0, pl.ds(k, c, stride=tc), :]).astype(BF16)

    n_pairs = n_groups // 2

    def local_states(gp, _):
        local = []
        for g in (2 * gp, 2 * gp + 1):
            ug = jnp.concatenate([ut_ref[k, pl.ds(pl.multiple_of(g * p, p), p), :] for k in range(tc)], axis=0)
            y_ref[g] = jnp.dot(kt_ref[0, g], ug, preferred_element_type=F32)
            local.append(jnp.dot(bt_ref[0, g], ug, preferred_element_type=F32))
        sr_ref[gp] = jnp.transpose(jnp.concatenate([local[0][:half], local[1][:half]], axis=0))
        si_ref[gp] = jnp.transpose(jnp.concatenate([local[0][half:], local[1][half:]], axis=0))
        return 0

    lax.fori_loop(0, n_pairs, local_states, 0, unroll=2)

    row = lax.broadcasted_iota(jnp.int32, (n_pairs, c, n2), 1)

    def shift(x, d, fill):
        return jnp.where(row >= d, pltpu.roll(x, d, 1), fill)

    tab = tab_ref[0]
    cin_r = carry_ref[:, 0:1, :]
    cin_i = carry_ref[:, 1:2, :]
    sr = sr_ref[...] + jnp.where(row == 0, tab[:, 0:1] * cin_r - tab[:, 1:2] * cin_i, 0.0)
    si = si_ref[...] + jnp.where(row == 0, tab[:, 0:1] * cin_i + tab[:, 1:2] * cin_r, 0.0)
    for s in range(n_steps):
        d = 1 << s
        ar, ai = tab[:, 2 * s:2 * s + 1], tab[:, 2 * s + 1:2 * s + 2]
        hr, hi = shift(sr, d, 0.0), shift(si, d, 0.0)
        sr, si = sr + ar * hr - ai * hi, si + ar * hi + ai * hr
    carry_ref[:, 0:1, :] = sr[:, c - 1:c]
    carry_ref[:, 1:2, :] = si[:, c - 1:c]
    sr_ref[...] = shift(sr, 1, cin_r)
    si_ref[...] = shift(si, 1, cin_i)

    def carried_response(gp, _):
        pr = jnp.transpose(sr_ref[gp])
        pi = jnp.transpose(si_ref[gp])
        for idx, g in enumerate((2 * gp, 2 * gp + 1)):
            prev = jnp.concatenate([pr[idx * half:(idx + 1) * half], pi[idx * half:(idx + 1) * half]], axis=0)
            y_ref[g] = y_ref[g] + jnp.dot(ct_ref[0, g], prev.astype(BF16), preferred_element_type=F32)
        return 0

    lax.fori_loop(0, n_pairs, carried_response, 0, unroll=2)

    groups_per_slab = LANES // p
    for t in range(tc):
        for a in range(n_slab):
            yt = jnp.concatenate([y_ref[g, t * p:(t + 1) * p, :]
                                  for g in range(a * groups_per_slab, (a + 1) * groups_per_slab)], axis=0)
            o_refs[a][0, pl.ds(t, c, stride=tc), :] = jnp.transpose(yt)


def _s5(u_slabs, tables, layer, c_lanes):
    kt, bt, ct, tab = tables
    n_slab = len(u_slabs)
    bsz, seq, _ = u_slabs[0].shape
    width = n_slab * LANES
    tc = S5_CHUNK
    n_groups, n2 = bt.shape[1], bt.shape[2]
    p = width // n_groups
    rows = c_lanes * tc
    kern = functools.partial(_s5_kernel, n_groups=n_groups, p=p, n_slab=n_slab)
    slab = pl.BlockSpec((1, rows, LANES), lambda b, j: (b, j, 0))
    return pl.pallas_call(
        kern,
        grid=(bsz, seq // rows),
        in_specs=[slab] * n_slab + [_layer_spec(t.shape, layer) for t in (kt, bt, ct, tab)],
        out_specs=[slab] * n_slab,
        out_shape=[jax.ShapeDtypeStruct((bsz, seq, LANES), F32)] * n_slab,
        scratch_shapes=[pltpu.VMEM((tc, width, c_lanes), BF16), pltpu.VMEM((n_groups, tc * p, c_lanes), F32),
                        pltpu.VMEM((n_groups // 2, SUBLANES, n2), F32),
                        pltpu.VMEM((n_groups // 2, c_lanes, n2), F32), pltpu.VMEM((n_groups // 2, c_lanes, n2), F32)],
        compiler_params=_cparams("parallel", "arbitrary"),
        name="s5",
    )(*u_slabs, kt, bt, ct, tab)


def _merge_kernel(*refs, n_groups, n_experts, d_model, n_rslab, n_slab):
    x_ref, ya_ref = refs[:2]
    yr_refs = refs[2:2 + n_rslab]
    ys_refs = refs[2 + n_rslab:2 + n_rslab + n_slab]
    (mg_ref, wg_ref, gw_ref, gb_ref, pr_ref, pa_ref, ps_ref, wo_ref, fg_ref, rw_ref, rb_ref,
     x1_ref, xn_ref, route_ref, routet_ref, tab_ref, cnt_ref, run_ref, wgb_ref) = refs[2 + n_rslab + n_slab:]
    i = pl.program_id(0)

    @pl.when(i == 0)
    def _():
        run_ref[...] = jnp.zeros_like(run_ref)
        wgb_ref[...] = wg_ref[0].astype(BF16)

    x = x_ref[...]
    tm = x.shape[0]
    xn = _rmsnorm(x, mg_ref[0], MIX_EPS)
    gates = _sigmoid(jnp.dot(xn.astype(BF16), wgb_ref[...], preferred_element_type=F32))
    z = _gelu_tanh(jnp.concatenate([ref[...] for ref in ys_refs], axis=1))
    ys = z * _sigmoid(jnp.dot(z.astype(BF16), gw_ref[0], preferred_element_type=F32) + gb_ref[0])
    y_rnn = jnp.concatenate([ref[...] for ref in yr_refs], axis=1).astype(BF16)
    merged = (gates[:, :d_model] * jnp.dot(y_rnn, pr_ref[0], preferred_element_type=F32)
              + gates[:, d_model:2 * d_model] * jnp.dot(ya_ref[...], pa_ref[0], preferred_element_type=F32)
              + gates[:, 2 * d_model:] * jnp.dot(ys.astype(BF16), ps_ref[0], preferred_element_type=F32))
    x1 = x + jnp.dot(merged.astype(BF16), wo_ref[0], preferred_element_type=F32)
    x1_ref[...] = x1
    xn2 = _rmsnorm(x1, fg_ref[0], MIX_EPS)
    xn_ref[...] = xn2.astype(xn_ref.dtype)

    rw = rw_ref[0]
    x_hi = xn2.astype(BF16)
    x_lo = (xn2 - x_hi.astype(F32)).astype(BF16)
    w_hi = rw.astype(BF16)
    w_lo = (rw - w_hi.astype(F32)).astype(BF16)
    logits = (jnp.dot(x_hi, w_hi, preferred_element_type=F32) + jnp.dot(x_lo, w_hi, preferred_element_type=F32)
              + jnp.dot(x_hi, w_lo, preferred_element_type=F32) + rb_ref[0])
    lane = lax.broadcasted_iota(jnp.int32, logits.shape, 1).astype(F32)
    big = float(LANES)
    coarse = jnp.where(lane < n_groups, logits, NEG_INF)
    cmax = jnp.max(coarse, axis=1, keepdims=True)
    gsel = jnp.min(jnp.where(coarse == cmax, lane, big), axis=1, keepdims=True)
    p_sel = 1.0 / jnp.sum(jnp.where(lane < n_groups, jnp.exp(logits - cmax), 0.0), axis=1, keepdims=True)
    lo = n_groups + gsel * n_experts
    fine = jnp.where((lane >= lo) & (lane < lo + n_experts), logits, NEG_INF)
    m1 = jnp.max(fine, axis=1, keepdims=True)
    i1 = jnp.min(jnp.where(fine == m1, lane, big), axis=1, keepdims=True)
    fine2 = jnp.where(lane == i1, NEG_INF, fine)
    m2 = jnp.max(fine2, axis=1, keepdims=True)
    i2 = jnp.min(jnp.where(fine2 == m2, lane, big), axis=1, keepdims=True)
    e21 = jnp.exp(m2 - m1)
    w1 = p_sel / (1.0 + e21)
    w2 = p_sel * e21 / (1.0 + e21)
    oh1 = lane == i1
    oh2 = lane == i2
    onehot = jnp.where(oh1 | oh2, 1.0, 0.0)
    r_i = lax.broadcasted_iota(jnp.int32, (tm, tm), 0)
    c_i = lax.broadcasted_iota(jnp.int32, (tm, tm), 1)
    earlier = jnp.where(c_i < r_i, 1.0, 0.0).astype(BF16)
    rank = jnp.dot(earlier, onehot.astype(BF16), preferred_element_type=F32)
    cnt = jnp.sum(onehot, axis=0, keepdims=True)
    cnt = jnp.floor((cnt + (SUBLANES - 1)) * (1.0 / SUBLANES)) * SUBLANES
    k_i = lax.broadcasted_iota(jnp.int32, (LANES, LANES), 0)
    l_i = lax.broadcasted_iota(jnp.int32, (LANES, LANES), 1)
    lower = jnp.where(k_i < l_i, 1.0, 0.0)
    start = jnp.dot(jnp.broadcast_to(cnt, (SUBLANES, LANES)), lower, preferred_element_type=F32,
                    precision=lax.Precision.HIGHEST)[0:1]
    pos = rank + start
    lp1 = jnp.sum(jnp.where(oh1, pos, 0.0), axis=1, keepdims=True)
    lp2 = jnp.sum(jnp.where(oh2, pos, 0.0), axis=1, keepdims=True)
    route = jnp.where(lane == 0, w1, 0.0)
    route = jnp.where(lane == 1, w2, route)
    route = jnp.where(lane == 2, lp1, route)
    route = jnp.where(lane == 3, lp2, route)
    route_ref[...] = route
    routet_ref[...] = jnp.transpose(route)[:SUBLANES]
    sub = lax.broadcasted_iota(jnp.int32, (SUBLANES, LANES), 0)
    tab_ref[...] = jnp.where(sub == 0, cnt, jnp.where(sub == 1, start, jnp.where(sub == 2, run_ref[...], 0.0)))
    run_ref[...] = run_ref[...] + cnt
    cnt_ref[...] = run_ref[...]


def _merge(x2d, y_attn, y_rnn_slabs, y_s5_slabs, mix_g, w_gate, glu_w, glu_b, p_rnn, p_attn, p_ssm, w_out, ffn_g,
           rw, rb, layer, gate_block, n_groups, n_experts, tm):
    n, d = x2d.shape
    row = lambda i: (i, 0)
    n_rslab, n_slab = len(y_rnn_slabs), len(y_s5_slabs)
    n_branch_cols = 3 * d
    kern = functools.partial(_merge_kernel, n_groups=n_groups, n_experts=n_experts, d_model=d, n_rslab=n_rslab,
                             n_slab=n_slab)
    consts = [mix_g, w_gate, glu_w, glu_b, p_rnn, p_attn, p_ssm, w_out, ffn_g, rw, rb]
    const_specs = [_layer_spec(a.shape, layer) for a in consts]
    const_specs[1] = pl.BlockSpec((1, d, n_branch_cols), lambda *_: (layer, 0, gate_block),
                                  pipeline_mode=pl.Buffered(1))
    return pl.pallas_call(
        kern,
        grid=(n // tm,),
        in_specs=[pl.BlockSpec((tm, d), row), pl.BlockSpec((tm, y_attn.shape[1]), row)]
                 + [pl.BlockSpec((tm, LANES), row)] * (n_rslab + n_slab) + const_specs,
        out_specs=[pl.BlockSpec((tm, d), row), pl.BlockSpec((tm, d), row), pl.BlockSpec((tm, LANES), row),
                   pl.BlockSpec((SUBLANES, tm), lambda i: (0, i)), pl.BlockSpec((SUBLANES, LANES), row),
                   pl.BlockSpec((1, LANES), lambda i: (0, 0))],
        out_shape=[jax.ShapeDtypeStruct((n, d), F32), jax.ShapeDtypeStruct((n, d), BF16),
                   jax.ShapeDtypeStruct((n, LANES), F32), jax.ShapeDtypeStruct((SUBLANES, n), F32),
                   jax.ShapeDtypeStruct((n // tm * SUBLANES, LANES), F32), jax.ShapeDtypeStruct((1, LANES), F32)],
        scratch_shapes=[pltpu.VMEM((1, LANES), F32), pltpu.VMEM((d, n_branch_cols), BF16)],
        compiler_params=_cparams("arbitrary"),
        name="merge_router",
    )(x2d, y_attn, *y_rnn_slabs, *y_s5_slabs, *consts)


def _sorted_rows(tm, n_total):
    rows = TOP_K_FINE * tm + n_total * (SUBLANES - 1)
    return -(-rows // LANES) * LANES


RUN_BITS = 6
TILE_BITS = 3


def _run_copies(base, n_total, cnt_ref, loc_ref, dst_ref, local_ref, remote_ref, sem, to_remote, wait):
    def copy(e, off, size):
        lstart = 0 if loc_ref is None else pl.multiple_of(loc_ref[base + e] + off, SUBLANES)
        local = local_ref.at[pl.ds(lstart, size)]
        remote = remote_ref.at[pl.ds(pl.multiple_of(dst_ref[base + e] + off, SUBLANES), size)]
        desc = pltpu.make_async_copy(local, remote, sem) if to_remote else pltpu.make_async_copy(remote, local, sem)
        if wait:
            desc.wait()
        else:
            desc.start()

    def expert(e, _):
        cnt = cnt_ref[base + e]
        big = 1 << RUN_BITS

        def chunk(c, _):
            copy(e,c * big, big)
            return 0

        n_big = cnt >> RUN_BITS
        lax.fori_loop(0, n_big, chunk, 0)
        off = n_big * big
        for b in reversed(range(TILE_BITS, RUN_BITS)):
            size = 1 << b

            @pl.when((cnt & size) != 0)
            def _(off=off, size=size):
                copy(e,off, size)

            off = off + (cnt & size)
        return 0

    lax.fori_loop(0, n_total, expert, 0, unroll=4 if n_total % 4 == 0 else 1)


def _wait_tile(base, n_total, min_rows, cnt_ref, loc_ref, local_ref, remote_ref, sem, to_remote):
    total = loc_ref[base + n_total - 1] + cnt_ref[base + n_total - 1]
    rest = total - min_rows

    def wait(size):
        local, remote = local_ref.at[pl.ds(0, size)], remote_ref.at[pl.ds(0, size)]
        (pltpu.make_async_copy(local, remote, sem) if to_remote else pltpu.make_async_copy(remote, local, sem)).wait()

    wait(min_rows)
    for b in range(TILE_BITS, (n_total * (SUBLANES - 1)).bit_length()):
        @pl.when((rest & (1 << b)) != 0)
        def _(b=b):
            wait(1 << b)


def _dispatch_kernel(cnt_ref, loc_ref, dst_ref, gap_cnt_ref, gap_dst_ref, x_ref, rt_ref, xs_ref, buf_ref, zero_ref,
                     sem, *, n_total):
    i = pl.program_id(0)
    last = pl.num_programs(0) - 1
    slot = i % 2
    tm = x_ref.shape[0]
    rows = buf_ref.shape[1]
    pos = rt_ref[...]
    j = lax.broadcasted_iota(jnp.int32, (rows, tm), 0).astype(F32)
    sel = jnp.where((j == pos[2:3]) | (j == pos[3:4]), 1.0, 0.0).astype(BF16)
    buf_ref[slot] = jnp.dot(sel, x_ref[...], preferred_element_type=F32)

    def copies(tile, tile_slot, wait):
        if wait:
            _wait_tile(tile * n_total, n_total, TOP_K_FINE * tm, cnt_ref, loc_ref, buf_ref.at[tile_slot], xs_ref,
                       sem.at[tile_slot], True)
        else:
            _run_copies(tile * n_total, n_total, cnt_ref, loc_ref, dst_ref, buf_ref.at[tile_slot], xs_ref,
                        sem.at[tile_slot], True, False)

    copies(i, slot, False)

    @pl.when(i > 0)
    def _():
        copies(i - 1, 1 - slot, True)

    @pl.when(i == last)
    def _():
        copies(i, slot, True)
        zero_ref[...] = jnp.zeros_like(zero_ref)
        for wait in (False, True):
            _run_copies(0, gap_cnt_ref.shape[0], gap_cnt_ref, None, gap_dst_ref, zero_ref, xs_ref, sem.at[0], True,
                        wait)


def _dispatch(tile_cnt, tile_loc, tile_dst, gap_cnt, gap_dst, xn2, route_t, n_rows, n_total, tm):
    n, d = xn2.shape
    grid_spec = pltpu.PrefetchScalarGridSpec(
        num_scalar_prefetch=5,
        grid=(n // tm,),
        in_specs=[pl.BlockSpec((tm, d), lambda i, *_: (i, 0)),
                  pl.BlockSpec((SUBLANES, tm), lambda i, *_: (0, i))],
        out_specs=pl.BlockSpec(memory_space=pl.ANY),
        scratch_shapes=[pltpu.VMEM((2, _sorted_rows(tm, n_total), d), F32), pltpu.VMEM((1 << RUN_BITS, d), F32),
                        pltpu.SemaphoreType.DMA((2,))],
    )
    return pl.pallas_call(
        functools.partial(_dispatch_kernel, n_total=n_total),
        grid_spec=grid_spec,
        out_shape=jax.ShapeDtypeStruct((n_rows, d), F32),
        compiler_params=_cparams("arbitrary"),
        name="moe_dispatch",
    )(tile_cnt, tile_loc, tile_dst, gap_cnt, gap_dst, xn2, route_t)


def _experts_kernel(te_ref, first_ref, slot_ref, next_ref, nact_ref, x_ref, w1_ref, w3_ref, w2_ref, o_ref,
                    w1f_ref, w3f_ref, w2f_ref, w1b_ref, w3b_ref, w2b_ref, sem):
    i = pl.program_id(0)

    def fetch(expert, slot, wait):
        for src, dst in ((w1_ref, w1f_ref), (w3_ref, w3f_ref), (w2_ref, w2f_ref)):
            copy = pltpu.make_async_copy(src.at[expert], dst.at[slot], sem.at[slot])
            if wait:
                copy.wait()
            else:
                copy.start()

    @pl.when(i == 0)
    def _():
        fetch(te_ref[0], slot_ref[0], False)

    @pl.when(jnp.logical_and(i < nact_ref[0], first_ref[i] == 1))
    def _():
        slot = slot_ref[i]
        fetch(te_ref[i], slot, True)
        w1b_ref[...] = w1f_ref[slot].astype(BF16)
        w3b_ref[...] = w3f_ref[slot].astype(BF16)
        w2b_ref[...] = w2f_ref[slot].astype(BF16)

        @pl.when(next_ref[i] >= 0)
        def _():
            fetch(next_ref[i], 1 - slot, False)

    @pl.when(i < nact_ref[0])
    def _():
        xb = x_ref[...].astype(BF16)
        h1 = jnp.dot(xb, w1b_ref[...], preferred_element_type=F32)
        h3 = jnp.dot(xb, w3b_ref[...], preferred_element_type=F32)
        hid = h1 * _sigmoid(h1) * h3
        o_ref[...] = jnp.dot(hid.astype(BF16), w2b_ref[...], preferred_element_type=F32).astype(o_ref.dtype)

    @pl.when(i >= nact_ref[0])
    def _():
        o_ref[...] = jnp.zeros_like(o_ref)


def _experts(tile_expert, tile_first, tile_slot, tile_next, n_active, xs, w1, w3, w2, tm):
    n_rows, d = xs.shape
    f = w1.shape[2]
    grid_spec = pltpu.PrefetchScalarGridSpec(
        num_scalar_prefetch=5,
        grid=(n_rows // tm,),
        in_specs=[pl.BlockSpec((tm, d), lambda i, te, fi, sl, nx, na: (jnp.minimum(i, na[0] - 1), 0)),
                  pl.BlockSpec(memory_space=pl.ANY), pl.BlockSpec(memory_space=pl.ANY),
                  pl.BlockSpec(memory_space=pl.ANY)],
        out_specs=pl.BlockSpec((tm, d), lambda i, *_: (i, 0)),
        scratch_shapes=[pltpu.VMEM((2, d, f), F32), pltpu.VMEM((2, d, f), F32), pltpu.VMEM((2, f, d), F32),
                        pltpu.VMEM((d, f), BF16), pltpu.VMEM((d, f), BF16), pltpu.VMEM((f, d), BF16),
                        pltpu.SemaphoreType.DMA((2,))],
    )
    return pl.pallas_call(
        _experts_kernel,
        grid_spec=grid_spec,
        out_shape=jax.ShapeDtypeStruct((n_rows, d), F32),
        compiler_params=_cparams("arbitrary"),
        name="moe_experts",
    )(tile_expert, tile_first, tile_slot, tile_next, n_active, xs, w1, w3, w2)


def _combine_kernel(cnt_ref, loc_ref, dst_ref, x_ref, route_ref, rt_ref, fg_ref, ys_ref, o_ref, buf_ref, sem,
                    *, n_total, final_norm):
    i = pl.program_id(0)
    slot = i % 2
    tm = x_ref.shape[0]
    rows = buf_ref.shape[1]

    def copies(tile, tile_slot, wait):
        if wait:
            _wait_tile(tile * n_total, n_total, TOP_K_FINE * tm, cnt_ref, loc_ref, buf_ref.at[tile_slot], ys_ref,
                       sem.at[tile_slot], False)
        else:
            _run_copies(tile * n_total, n_total, cnt_ref, loc_ref, dst_ref, buf_ref.at[tile_slot], ys_ref,
                        sem.at[tile_slot], False, False)

    @pl.when(i == 0)
    def _():
        buf_ref[...] = jnp.zeros_like(buf_ref)
        copies(0, 0, False)

    @pl.when(i + 1 < pl.num_programs(0))
    def _():
        copies(i + 1, 1 - slot, False)

    copies(i, slot, True)
    rt = rt_ref[...]
    jr = lax.broadcasted_iota(jnp.int32, (rows, tm), 0).astype(F32)
    gate = jnp.sum(jnp.where(jr == rt[2:3], rt[0:1], 0.0) + jnp.where(jr == rt[3:4], rt[1:2], 0.0),
                   axis=1, keepdims=True)
    yb = (buf_ref[slot] * gate).astype(BF16)
    route = route_ref[...]
    jc = lax.broadcasted_iota(jnp.int32, (tm, rows), 1).astype(F32)
    pick = jnp.where((jc == route[:, 2:3]) | (jc == route[:, 3:4]), 1.0, 0.0).astype(BF16)
    out = x_ref[...] + jnp.dot(pick, yb, preferred_element_type=F32)
    if final_norm:
        out = _rmsnorm(out, fg_ref[...], MIX_EPS)
    o_ref[...] = out


def _combine(tile_cnt, tile_loc, tile_dst, x1, route, route_t, final_g, ys, n_total, tm, final_norm):
    n, d = x1.shape
    kern = functools.partial(_combine_kernel, n_total=n_total, final_norm=final_norm)
    grid_spec = pltpu.PrefetchScalarGridSpec(
        num_scalar_prefetch=3,
        grid=(n // tm,),
        in_specs=[pl.BlockSpec((tm, d), lambda i, *_: (i, 0)),
                  pl.BlockSpec((tm, LANES), lambda i, *_: (i, 0)),
                  pl.BlockSpec((SUBLANES, tm), lambda i, *_: (0, i)),
                  pl.BlockSpec((1, d), lambda i, *_: (0, 0)),
                  pl.BlockSpec(memory_space=pl.ANY)],
        out_specs=pl.BlockSpec((tm, d), lambda i, *_: (i, 0)),
        scratch_shapes=[pltpu.VMEM((2, _sorted_rows(tm, n_total), d), F32), pltpu.SemaphoreType.DMA((2,))],
    )
    return pl.pallas_call(
        kern,
        grid_spec=grid_spec,
        out_shape=jax.ShapeDtypeStruct((n, d), F32),
        compiler_params=_cparams("arbitrary"),
        name="moe_combine",
    )(tile_cnt, tile_loc, tile_dst, x1, route, route_t, final_g, ys)


def _tile_plan(n, seq):
    return dict(inproj=min(512, n), rglru=min(256, seq), attn=min(512, seq), attn_heads=4,
                s5_lanes=min(LANES, seq // S5_CHUNK), merge=min(512, n), moe=min(512, n))


def kernel(x, positions, mix_norm_g, w_in, conv_w, conv_b, rg_wa, rg_ba, rg_wx, rg_bx, rg_lambda,
           lam_q1, lam_k1, lam_q2, lam_k2, subln_g,
           ssm_lambda_re, ssm_lambda_im, ssm_b_re, ssm_b_im, ssm_c_re, ssm_c_im, ssm_d, ssm_log_dt,
           ssm_glu_w, ssm_glu_b, proj_rnn, proj_attn, proj_ssm, w_out,
           ffn_norm_g, router_coarse_w, router_coarse_b, router_fine_w, router_fine_b,
           expert_w1, expert_w3, expert_w2, final_norm_g):
    return _forward(_tile_plan(x.shape[0] * x.shape[1], x.shape[1]),
                    x, positions, mix_norm_g, w_in, conv_w, conv_b, rg_wa, rg_ba, rg_wx, rg_bx, rg_lambda,
                    lam_q1, lam_k1, lam_q2, lam_k2, subln_g,
                    ssm_lambda_re, ssm_lambda_im, ssm_b_re, ssm_b_im, ssm_c_re, ssm_c_im, ssm_d, ssm_log_dt,
                    ssm_glu_w, ssm_glu_b, proj_rnn, proj_attn, proj_ssm, w_out,
                    ffn_norm_g, router_coarse_w, router_coarse_b, router_fine_w, router_fine_b,
                    expert_w1, expert_w3, expert_w2, final_norm_g)


def _forward(tiles, x, positions, mix_norm_g, w_in, conv_w, conv_b, rg_wa, rg_ba, rg_wx, rg_bx, rg_lambda,
             lam_q1, lam_k1, lam_q2, lam_k2, subln_g,
             ssm_lambda_re, ssm_lambda_im, ssm_b_re, ssm_b_im, ssm_c_re, ssm_c_im, ssm_d, ssm_log_dt,
             ssm_glu_w, ssm_glu_b, proj_rnn, proj_attn, proj_ssm, w_out,
             ffn_norm_g, router_coarse_w, router_coarse_b, router_fine_w, router_fine_b,
             expert_w1, expert_w3, expert_w2, final_norm_g):
    bsz, seq, d_model = x.shape
    depth = w_in.shape[0]
    n = bsz * seq
    r = conv_w.shape[2]
    sw = ssm_glu_w.shape[1]
    vdim = subln_g.shape[1]
    head_dim = vdim // 2
    in_cols = w_in.shape[2]
    qk = (in_cols - 2 * r - sw - 3 * d_model) // 3
    heads = qk // (2 * head_dim)
    splits = (r, 2 * r, 2 * r + qk, 2 * r + 2 * qk, 2 * r + 3 * qk, 2 * r + 3 * qk + sw)
    mix_cols = splits[-1]
    n_groups = router_coarse_w.shape[2]
    n_experts = expert_w1.shape[2]
    n_total = n_groups * n_experts
    rnn_blocks = rg_wa.shape[1]

    tm_in, tt_rnn, tq = tiles["inproj"], tiles["rglru"], tiles["attn"]
    s5_lanes, tm_merge, tm_moe = tiles["s5_lanes"], tiles["merge"], tiles["moe"]
    n_rows = TOP_K_FINE * n + (n // tm_merge) * n_total * (SUBLANES - 1) + n_total * tm_moe
    n_rows = -(-n_rows // tm_moe) * tm_moe

    posf = positions.astype(F32)
    pos_col = posf.reshape(n, 1)
    pos8 = posf.reshape(bsz, seq // SUBLANES, SUBLANES)
    inv_freq = ROPE_THETA ** (-jnp.arange(0, head_dim, 2, dtype=F32) / head_dim)
    invf = jnp.tile(inv_freq, LANES // (head_dim // 2)).reshape(1, LANES)

    row3 = lambda a: a.reshape(depth, 1, a.shape[-1])
    gate_cols = in_cols - mix_cols
    in_place = mix_cols % gate_cols == 0
    w_gate = w_in if in_place else w_in[:, :, mix_cols:]
    gate_block = mix_cols // gate_cols if in_place else 0
    eye_blocks = jnp.eye(rnn_blocks, dtype=F32)
    block_diag = lambda w: jnp.einsum('lhij,hk->lhikj', w, eye_blocks).reshape(depth, r, r)
    w_gates = jnp.concatenate([block_diag(rg_wa), block_diag(rg_wx)], axis=2).astype(BF16)
    b_gates = row3(jnp.concatenate([rg_ba, rg_bx], axis=1))
    lamv = jnp.stack([lam_q1, lam_k1, lam_q2, lam_k2], axis=1)
    cc, bt_f32, skip, ct_t, tab_t = jax.vmap(functools.partial(_s5_tables, n_steps=int(math.log2(s5_lanes))))(
        ssm_lambda_re, ssm_lambda_im, ssm_b_re, ssm_b_im, ssm_c_re, ssm_c_im, ssm_d, ssm_log_dt)
    n_ssm_groups = cc.shape[1]
    flat = lambda a: a.reshape((depth * n_ssm_groups,) + a.shape[2:])
    kt = _toeplitz(flat(cc), flat(bt_f32), flat(skip), math.gcd(n_ssm_groups, SUBLANES))
    tables = (kt.reshape((depth, n_ssm_groups) + kt.shape[1:]), bt_f32.astype(BF16), ct_t, tab_t)
    rw = jnp.concatenate([router_coarse_w, router_fine_w], axis=2)
    rw = jnp.pad(rw, ((0, 0), (0, 0), (0, LANES - rw.shape[2])))
    rb = jnp.concatenate([router_coarse_b, router_fine_b], axis=1)
    rb = row3(jnp.pad(rb, ((0, 0), (0, LANES - rb.shape[1]))))
    merge_params = (row3(mix_norm_g), w_gate, ssm_glu_w.astype(BF16), row3(ssm_glu_b), proj_rnn.astype(BF16),
                    proj_attn.astype(BF16), proj_ssm.astype(BF16), w_out.astype(BF16), row3(ffn_norm_g), rw, rb)

    x2d = x.reshape(n, d_model)
    for l in range(depth):
        lambda_init = 0.8 - 0.6 * math.exp(-0.3 * l)
        q, k, vt, *slabs = _inproj(x2d, row3(mix_norm_g), pos_col, invf, w_in, l, splits, head_dim, tm_in, tq)
        slabs = [t.reshape(bsz, seq, LANES) for t in slabs]
        n_rs = r // LANES
        y_rnn = _rglru(slabs[:n_rs], slabs[n_rs:2 * n_rs], pos8, conv_w, row3(conv_b), w_gates, b_gates,
                       row3(rg_lambda), l, tt_rnn)
        y_attn = _diff_attention(q.reshape(bsz, seq, qk), k.reshape(bsz, seq, qk), vt, lamv, row3(subln_g), l,
                                 heads, head_dim, lambda_init, tq, min(tiles["attn_heads"], heads))
        y_s5 = _s5(slabs[2 * n_rs:], tables, l, s5_lanes)
        x1, xn2, route, route_t, tile_tab, counts = _merge(
            x2d, y_attn.reshape(n, qk), [y.reshape(n, LANES) for y in y_rnn], [y.reshape(n, LANES) for y in y_s5],
            *merge_params, l, gate_block, n_groups, n_experts, tm_merge)

        cnt = counts[0, n_groups:n_groups + n_total].astype(jnp.int32)
        n_tiles = (cnt + tm_moe - 1) // tm_moe
        tile_end = jnp.cumsum(n_tiles)
        offsets = (tile_end - n_tiles) * tm_moe
        tab = tile_tab.reshape(n // tm_merge, SUBLANES, LANES)[:, :, n_groups:n_groups + n_total].astype(jnp.int32)
        tile_cnt = tab[:, 0].reshape(-1)
        tile_loc = tab[:, 1].reshape(-1)
        tile_dst = (tab[:, 2] + offsets[None, :]).reshape(-1)
        n_active = tile_end[-1:]
        gap_dst = jnp.concatenate([offsets + cnt, n_active * tm_moe])
        gap_cnt = jnp.concatenate([tile_end * tm_moe, jnp.full((1,), n_rows, jnp.int32)]) - gap_dst
        tile_ids = jnp.minimum(jnp.arange(n_rows // tm_moe, dtype=jnp.int32), n_active[0] - 1)
        tile_expert = jnp.sum((tile_ids[:, None] >= tile_end[None, :]).astype(jnp.int32), axis=1)
        e_ids = jnp.arange(n_total, dtype=jnp.int32)
        present = n_tiles > 0
        ordinal = jnp.cumsum(present.astype(jnp.int32)) - 1
        later = present[None, :] & (e_ids[None, :] > e_ids[:, None])
        succ = jnp.min(jnp.where(later, e_ids[None, :], n_total), axis=1)
        succ = jnp.where(succ < n_total, succ + l * n_total, -1)
        all_ids = jnp.arange(n_rows // tm_moe, dtype=jnp.int32)
        tile_first = ((all_ids < n_active[0]) & (all_ids == (tile_end - n_tiles)[tile_expert])).astype(jnp.int32)
        tile_slot = ordinal[tile_expert] % 2
        tile_next = succ[tile_expert]

        xs = _dispatch(tile_cnt, tile_loc, tile_dst, gap_cnt, gap_dst, xn2, route_t, n_rows, n_total, tm_merge)
        ys = _experts(tile_expert + l * n_total, tile_first, tile_slot, tile_next, n_active.astype(jnp.int32), xs,
                      expert_w1.reshape(depth * n_total, d_model, -1), expert_w3.reshape(depth * n_total, d_model, -1),
                      expert_w2.reshape(depth * n_total, -1, d_model), tm_moe)
        x2d = _combine(tile_cnt, tile_loc, tile_dst, x1, route, route_t, final_norm_g.reshape(1, d_model), ys,
                       n_total, tm_merge, l == depth - 1)
    return x2d.reshape(bsz, seq, d_model)
```

```python
import functools
import math

import jax
import jax.numpy as jnp
from jax import lax
from jax.experimental import pallas as pl
from jax.experimental.pallas import tpu as pltpu

F32 = jnp.float32
BF16 = jnp.bfloat16

RGLRU_C = 8.0
ROPE_THETA = 10000.0
TOP_K_FINE = 2
NEG_INF = -1e30
MIX_EPS = 1e-6
SUBLN_EPS = 1e-5

LANES = 128
SUBLANES = 8
VMEM_LIMIT_BYTES = 56 * 1024 * 1024

S5_CHUNK = 16


def _cparams(*sem):
    return pltpu.CompilerParams(dimension_semantics=sem, vmem_limit_bytes=VMEM_LIMIT_BYTES)


def _const_spec(shape):
    nd = len(shape)
    return pl.BlockSpec(shape, lambda *_: (0,) * nd, pipeline_mode=pl.Buffered(1))


def _layer_spec(shape, layer):
    nd = len(shape)
    return pl.BlockSpec((1,) + tuple(shape[1:]), lambda *_: (layer,) + (0,) * (nd - 1), pipeline_mode=pl.Buffered(1))


def _gelu_tanh(x):
    return 0.5 * x * (1.0 + jnp.tanh(math.sqrt(2.0 / math.pi) * (x + 0.044715 * (x * x * x))))


def _sigmoid(x):
    return 0.5 + 0.5 * jnp.tanh(0.5 * x)


def _rmsnorm(x, g, eps):
    return x * lax.rsqrt(jnp.mean(x * x, axis=-1, keepdims=True) + eps) * g


def _inproj_kernel(x_ref, g_ref, pos_ref, invf_ref, w_ref, *refs, splits, slab_cols, head_dim, q_scale):
    q_ref, k_ref, vt_ref = refs[:3]
    slab_refs, wb_ref = refs[3:-1], refs[-1]

    @pl.when(pl.program_id(0) == 0)
    def _():
        wb_ref[...] = w_ref[0].astype(BF16)

    x = x_ref[...]
    xn = _rmsnorm(x, g_ref[0], MIX_EPS)
    h = jnp.dot(xn.astype(BF16), wb_ref[...], preferred_element_type=F32)
    s0, s1, s2, s3, s4, s5 = splits
    for col, ref in zip(slab_cols, slab_refs):
        ref[...] = h[:, col:col + LANES]
    tkv = vt_ref.shape[2]
    for c in range(vt_ref.shape[0]):
        vt_ref[c] = jnp.transpose(h[c * tkv:(c + 1) * tkv, s3:s4]).astype(vt_ref.dtype)

    ang = pos_ref[...] * invf_ref[...]
    cos = jnp.cos(ang)
    sin = jnp.sin(ang)
    lane = lax.broadcasted_iota(jnp.int32, ang.shape, 1)
    first_half = (lane % head_dim) < (head_dim // 2)
    sin_signed = jnp.where(first_half, -sin, sin)

    def rope(t, scale):
        outs = []
        for a in range(t.shape[1] // LANES):
            xs = t[:, a * LANES:(a + 1) * LANES]
            fwd = pltpu.roll(xs, LANES - head_dim // 2, 1)
            bwd = pltpu.roll(xs, head_dim // 2, 1)
            rot = jnp.where(first_half, fwd, bwd)
            outs.append((xs * cos + rot * sin_signed) * scale)
        return jnp.concatenate(outs, axis=1)

    q_ref[...] = rope(h[:, s1:s2], q_scale).astype(q_ref.dtype)
    k_ref[...] = rope(h[:, s2:s3], 1.0).astype(k_ref.dtype)


def _inproj(x2d, g, posf, invf, w, layer, splits, head_dim, tm, tkv):
    n, d = x2d.shape
    mix_cols = splits[-1]
    widths = [splits[0]] + [splits[i] - splits[i - 1] for i in range(1, 6)]
    slab_cols = [start + a * LANES for start, width in ((0, widths[0]), (splits[0], widths[1]), (splits[4], widths[5]))
                 for a in range(width // LANES)]
    kern = functools.partial(_inproj_kernel, splits=splits, slab_cols=tuple(slab_cols), head_dim=head_dim,
                             q_scale=head_dim ** -0.5 * math.log2(math.e))
    row = lambda i: (i, 0)
    rows = lambda wd: pl.BlockSpec((tm, wd), row)
    out = lambda wd, dt: jax.ShapeDtypeStruct((n, wd), dt)
    return pl.pallas_call(
        kern,
        grid=(n // tm,),
        in_specs=[pl.BlockSpec((tm, d), row), _layer_spec(g.shape, layer), pl.BlockSpec((tm, 1), row),
                  _const_spec((1, LANES)), _layer_spec((w.shape[0], d, mix_cols), layer)],
        out_specs=[rows(widths[2]), rows(widths[3]),
                   pl.BlockSpec((tm // tkv, widths[4], tkv), lambda i: (i, 0, 0))] + [rows(LANES)] * len(slab_cols),
        out_shape=[out(widths[2], BF16), out(widths[3], BF16),
                   jax.ShapeDtypeStruct((n // tkv, widths[4], tkv), BF16)] + [out(LANES, F32)] * len(slab_cols),
        scratch_shapes=[pltpu.VMEM((d, mix_cols), BF16)],
        compiler_params=_cparams("arbitrary"),
        name="inproj",
    )(x2d, g, posf, invf, w)


def _rglru_kernel(*refs, n_slab):
    x_refs, g_refs = refs[:n_slab], refs[n_slab:2 * n_slab]
    pos_ref, cw_ref, cb_ref, wa_ref, wx_ref, b_ref, lam_ref = refs[2 * n_slab:2 * n_slab + 7]
    o_refs = refs[2 * n_slab + 7:3 * n_slab + 7]
    halo_ref, h_ref, w_ref = refs[3 * n_slab + 7:]
    j = pl.program_id(1)
    ph = SUBLANES
    m = x_refs[0].shape[1] // ph
    r = n_slab * LANES

    @pl.when(j == 0)
    def _():
        halo_ref[...] = jnp.zeros_like(halo_ref)
        h_ref[...] = jnp.zeros_like(h_ref)
        w_ref[...] = jnp.zeros_like(w_ref)
        blk = wa_ref.shape[2]
        for h in range(wa_ref.shape[1]):
            w_ref[h * blk:(h + 1) * blk, h * blk:(h + 1) * blk] = wa_ref[0, h].astype(BF16)
            w_ref[h * blk:(h + 1) * blk, r + h * blk:r + (h + 1) * blk] = wx_ref[0, h].astype(BF16)

    def phase(slabs, s):
        return jnp.concatenate([ref[0, pl.ds(s, m, stride=ph), :] for ref in slabs], axis=1)

    block = lax.broadcasted_iota(jnp.int32, (m, r), 0)

    def one_block_back(v, first):
        return jnp.where(block == 0, first, pltpu.roll(v, 1, 0))

    xs = [phase(x_refs, s) for s in range(ph)]
    halo = halo_ref[...]
    cw = cw_ref[0]
    taps = cw.shape[0]
    earlier = {s: one_block_back(xs[s], halo[s:s + 1]) for s in range(ph - taps + 1, ph)}
    xcs = []
    for s in range(ph):
        xc = cb_ref[0] + cw[0:1] * xs[s]
        for k in range(1, taps):
            xc = xc + cw[k:k + 1] * (xs[s - k] if s >= k else earlier[s - k + ph])
        xcs.append(xc)
    halo_ref[...] = jnp.concatenate([xs[s][m - 1:m] for s in range(ph)], axis=0)
    xc = jnp.concatenate(xcs, axis=0)

    gates = jnp.dot(xc.astype(BF16), w_ref[...], preferred_element_type=F32) + b_ref[0]
    rg = _sigmoid(gates[:, :r])
    ig = _sigmoid(gates[:, r:])
    z = -lam_ref[0]
    softplus = jnp.maximum(z, 0.0) + jnp.log(1.0 + jnp.exp(-jnp.abs(z)))
    a = jnp.exp((-RGLRU_C) * rg * softplus)
    mult = jnp.sqrt(1.0 - a * a)
    pos = pos_ref[0]
    reset = jnp.concatenate([pos[:, s:s + 1] for s in range(ph)], axis=0) == 0.0
    a = jnp.where(reset, 0.0, a)
    mult = jnp.where(reset, 1.0, mult)
    b = mult * ig * xc

    pa, pb = [a[0:m]], [b[0:m]]
    for s in range(1, ph):
        a_s = a[s * m:(s + 1) * m]
        pb.append(a_s * pb[-1] + b[s * m:(s + 1) * m])
        pa.append(a_s * pa[-1])
    ba, bb = pa[-1], pb[-1]
    d = 1
    while d < m:
        keep = block >= d
        a_sh = jnp.where(keep, pltpu.roll(ba, d, 0), 1.0)
        b_sh = jnp.where(keep, pltpu.roll(bb, d, 0), 0.0)
        bb = bb + ba * b_sh
        ba = ba * a_sh
        d *= 2
    h_prev = h_ref[...]
    h_end = bb + ba * h_prev
    h_in = one_block_back(h_end, h_prev)
    h_ref[...] = h_end[m - 1:m]
    for s in range(ph):
        out = (pb[s] + pa[s] * h_in) * _gelu_tanh(phase(g_refs, s))
        for c, o_ref in enumerate(o_refs):
            o_ref[0, pl.ds(s, m, stride=ph), :] = out[:, c * LANES:(c + 1) * LANES]


def _rglru(x_slabs, g_slabs, pos8, conv_w, conv_b, wa, wx, b_gates, lam, layer, tt):
    n_slab = len(x_slabs)
    bsz, seq, _ = x_slabs[0].shape
    r = n_slab * LANES
    slab = pl.BlockSpec((1, tt, LANES), lambda b, j: (b, j, 0))
    return pl.pallas_call(
        functools.partial(_rglru_kernel, n_slab=n_slab),
        grid=(bsz, seq // tt),
        in_specs=[slab] * (2 * n_slab) + [pl.BlockSpec((1, tt // SUBLANES, SUBLANES), lambda b, j: (b, j, 0))]
                 + [_layer_spec(a.shape, layer) for a in (conv_w, conv_b, wa, wx, b_gates, lam)],
        out_specs=[slab] * n_slab,
        out_shape=[jax.ShapeDtypeStruct((bsz, seq, LANES), F32)] * n_slab,
        scratch_shapes=[pltpu.VMEM((SUBLANES, r), F32), pltpu.VMEM((1, r), F32), pltpu.VMEM((r, 2 * r), BF16)],
        compiler_params=_cparams("parallel", "arbitrary"),
        name="rglru",
    )(*x_slabs, *g_slabs, pos8, conv_w, conv_b, wa, wx, b_gates, lam)


def _attn_kernel(q_ref, k_ref, vt_ref, lamv_ref, sg_ref, o_ref, *, tq, head_dim, lambda_init):
    i = pl.program_id(2)
    hw = 2 * head_dim
    hp = q_ref.shape[2] // hw
    vdim = vt_ref.shape[1] // hp
    lane = lax.broadcasted_iota(jnp.int32, (tq, hw), 1)
    zero = jnp.zeros((tq, hw), q_ref.dtype)
    ones = jnp.ones((2 * SUBLANES, tq), BF16)
    qqs = []
    for a in range(hp):
        q = q_ref[0, :, a * hw:(a + 1) * hw]
        qqs.append(jnp.concatenate([jnp.where(lane < head_dim, q, zero), jnp.where(lane >= head_dim, q, zero)],
                                   axis=0))

    def step(j, carry, masked):
        row0 = pl.multiple_of(j * tq, tq)
        out = []
        for a in range(hp):
            m, acc = carry[2 * a], carry[2 * a + 1]
            kb = k_ref[0, pl.ds(row0, tq), a * hw:(a + 1) * hw]
            s = lax.dot_general(kb, qqs[a], (((1,), (1,)), ((), ())), preferred_element_type=F32)
            if masked:
                key = lax.broadcasted_iota(jnp.int32, s.shape, 0)
                qry = lax.broadcasted_iota(jnp.int32, s.shape, 1)
                qry = jnp.where(qry >= tq, qry - tq, qry)
                s = jnp.where(key <= qry, s, NEG_INF)
            m_new = jnp.maximum(m, jnp.max(s, axis=0, keepdims=True))
            p = jnp.exp2(s - m_new)
            alpha = jnp.exp2(m - m_new)
            vt = jnp.concatenate([vt_ref[j, a * vdim:(a + 1) * vdim, :], ones], axis=0)
            out += [m_new, alpha * acc + jnp.dot(vt, p.astype(BF16), preferred_element_type=F32)]
        return tuple(out)

    init = (jnp.full((1, 2 * tq), NEG_INF, F32), jnp.zeros((vdim + 2 * SUBLANES, 2 * tq), F32)) * hp
    carry = lax.fori_loop(0, i, lambda j, c: step(j, c, False), init)
    carry = step(i, carry, True)

    lamv = lamv_ref[0]
    lam = (jnp.exp(jnp.sum(lamv[0:1] * lamv[1:2], axis=1, keepdims=True))
           - jnp.exp(jnp.sum(lamv[2:3] * lamv[3:4], axis=1, keepdims=True)) + lambda_init)
    for a in range(hp):
        acc = carry[2 * a + 1]
        ot = acc[:vdim] / acc[vdim:vdim + 1]
        o = jnp.transpose(ot[:, :tq] - lam * ot[:, tq:])
        o = _rmsnorm(o, sg_ref[0], SUBLN_EPS) * (1.0 - lambda_init)
        o_ref[0, :, a * vdim:(a + 1) * vdim] = o.astype(o_ref.dtype)


def _diff_attention(q, k, vt, lamv, subln_g, layer, heads, head_dim, lambda_init, tq, hp):
    bsz, seq, _ = q.shape
    vdim = vt.shape[1] // heads
    nkv = seq // tq
    kern = functools.partial(_attn_kernel, tq=tq, head_dim=head_dim, lambda_init=lambda_init)
    return pl.pallas_call(
        kern,
        grid=(bsz, heads // hp, seq // tq),
        in_specs=[pl.BlockSpec((1, tq, hp * 2 * head_dim), lambda b, h, i: (b, i, h)),
                  pl.BlockSpec((1, seq, hp * 2 * head_dim), lambda b, h, i: (b, 0, h)),
                  pl.BlockSpec((nkv, hp * vdim, tq), lambda b, h, i: (b, h, 0)),
                  _layer_spec(lamv.shape, layer), _layer_spec(subln_g.shape, layer)],
        out_specs=pl.BlockSpec((1, tq, hp * vdim), lambda b, h, i: (b, i, h)),
        out_shape=jax.ShapeDtypeStruct((bsz, seq, heads * vdim), BF16),
        compiler_params=_cparams("parallel", "parallel", "arbitrary"),
        name="diff_attn",
    )(q, k, vt, lamv, subln_g)


def _s5_tables(lam_re, lam_im, b_re, b_im, c_re, c_im, d_skip, log_dt, n_steps):
    tc = S5_CHUNK
    g, n, p = b_re.shape
    lr = lam_re.astype(F32)
    li = lam_im.astype(F32)
    dt = jnp.exp(log_dt.astype(F32))[:, None]
    mag = jnp.exp(lr * dt)
    ar = mag * jnp.cos(li * dt)
    ai = mag * jnp.sin(li * dt)
    den = lr * lr + li * li
    cr = ((ar - 1.0) * lr + ai * li) / den
    ci = (ai * lr - (ar - 1.0) * li) / den
    bb_re = cr[..., None] * b_re - ci[..., None] * b_im
    bb_im = cr[..., None] * b_im + ci[..., None] * b_re

    def apow(e):
        e = jnp.asarray(e, F32)[None, None, :]
        m = jnp.exp(e * (lr * dt)[..., None])
        ph = e * (li * dt)[..., None]
        return m * jnp.cos(ph), m * jnp.sin(ph)

    lags = jnp.arange(tc)
    pw_re, pw_im = apow(lags[::-1])
    cc = jnp.concatenate([c_re, -c_im], axis=2)
    skip = jnp.eye(p, dtype=F32)[None] * d_skip[:, None, :]
    p1_re, p1_im = apow(lags + 1)
    ca_re = jnp.einsum('gon,gnt->gton', c_re, p1_re) - jnp.einsum('gon,gnt->gton', c_im, p1_im)
    ca_im = jnp.einsum('gon,gnt->gton', c_re, p1_im) + jnp.einsum('gon,gnt->gton', c_im, p1_re)
    ct = jnp.concatenate([ca_re.reshape(g, tc * p, n), -ca_im.reshape(g, tc * p, n)], axis=2)
    st_re, st_im = apow(tc * (2 ** jnp.arange(n_steps)))
    tab = jnp.stack([st_re.transpose(0, 2, 1), st_im.transpose(0, 2, 1)], axis=2)
    tab = tab.reshape(g // 2, 2, 2 * n_steps, n).transpose(0, 2, 1, 3).reshape(g // 2, 2 * n_steps, 2 * n)
    tab = jnp.pad(tab, ((0, 0), (0, -(2 * n_steps) % SUBLANES), (0, 0)))
    return (pw_re, pw_im, bb_re, bb_im, cc, skip), ct.astype(BF16), tab.astype(F32)


def _toeplitz_kernel(pwr_ref, pwi_ref, bbr_ref, bbi_ref, cc_ref, skip_ref, o_ref, bt_ref):
    tc = S5_CHUNK
    p = cc_ref.shape[1]
    side = tc * p
    exact = functools.partial(jnp.dot, preferred_element_type=F32, precision=lax.Precision.HIGHEST)
    lane = lax.broadcasted_iota(jnp.int32, (tc, side), 1)
    row = lax.broadcasted_iota(jnp.int32, (tc, side), 0)
    spread = jnp.where((lane >= row * p) & (lane < (row + 1) * p), 1.0, 0.0).astype(BF16)
    lane = lax.broadcasted_iota(jnp.int32, (p, side), 1)
    row = lax.broadcasted_iota(jnp.int32, (p, side), 0)
    repeat = jnp.where(lane % p == row, 1.0, 0.0).astype(BF16)

    def place(re, im, sel):
        rest = jnp.concatenate([re, im], axis=0)
        out = None
        for _ in range(3):
            piece = rest.astype(BF16)
            rest = rest - piece.astype(F32)
            term = jnp.dot(piece, sel, preferred_element_type=F32)
            out = term if out is None else out + term
        return out[:re.shape[0]], out[re.shape[0]:]

    for gi in range(cc_ref.shape[0]):
        pr, pi = place(pwr_ref[gi], pwi_ref[gi], spread)
        br, bi = place(bbr_ref[gi], bbi_ref[gi], repeat)
        bt = jnp.concatenate([pr * br - pi * bi, pr * bi + pi * br], axis=0)
        bt_ref[gi] = bt.astype(bt_ref.dtype)
        k = exact(cc_ref[gi], bt)
        k = jnp.concatenate([k[:, :(tc - 1) * p], k[:, (tc - 1) * p:] + skip_ref[gi]], axis=1)
        s = jnp.concatenate([k, jnp.zeros((p, (tc - 1) * p), F32)], axis=1)
        o_ref[gi] = jnp.concatenate([s[:, (tc - 1 - t) * p:(tc - 1 - t) * p + tc * p] for t in range(tc)],
                                    axis=0).astype(o_ref.dtype)


def _toeplitz(parts, groups_per_step):
    cc = parts[4]
    m, p, n2 = cc.shape
    side = S5_CHUNK * p
    blk = lambda a: pl.BlockSpec((groups_per_step,) + a.shape[1:], lambda i: (i, 0, 0))
    return pl.pallas_call(
        _toeplitz_kernel,
        grid=(m // groups_per_step,),
        in_specs=[blk(a) for a in parts],
        out_specs=[pl.BlockSpec((groups_per_step, side, side), lambda i: (i, 0, 0)),
                   pl.BlockSpec((groups_per_step, n2, side), lambda i: (i, 0, 0))],
        out_shape=[jax.ShapeDtypeStruct((m, side, side), BF16), jax.ShapeDtypeStruct((m, n2, side), BF16)],
        compiler_params=_cparams("parallel"),
        name="s5_toeplitz",
    )(*parts)


def _s5_kernel(*refs, n_groups, p, n_slab):
    u_refs = refs[:n_slab]
    kt_ref, bt_ref, ct_ref, tab_ref = refs[n_slab:n_slab + 4]
    o_refs = refs[n_slab + 4:2 * n_slab + 4]
    ut_ref, y_ref, carry_ref, sr_ref, si_ref = refs[2 * n_slab + 4:]
    tc = S5_CHUNK
    c = u_refs[0].shape[1] // tc
    n2 = bt_ref.shape[2]
    half = n2 // 2
    n_steps = int(math.log2(c))

    @pl.when(pl.program_id(1) == 0)
    def _():
        carry_ref[...] = jnp.zeros_like(carry_ref)

    for k in range(tc):
        for a in range(n_slab):
            ut_ref[k, a * LANES:(a + 1) * LANES, :] = jnp.transpose(
                u_refs[a][0, pl.ds(k, c, stride=tc), :]).astype(BF16)

    n_pairs = n_groups // 2

    def local_states(gp, _):
        local = []
        for g in (2 * gp, 2 * gp + 1):
            ug = jnp.concatenate([ut_ref[k, pl.ds(pl.multiple_of(g * p, p), p), :] for k in range(tc)], axis=0)
            y_ref[g] = jnp.dot(kt_ref[0, g], ug, preferred_element_type=F32)
            local.append(jnp.dot(bt_ref[0, g], ug, preferred_element_type=F32))
        sr_ref[gp] = jnp.transpose(jnp.concatenate([local[0][:half], local[1][:half]], axis=0))
        si_ref[gp] = jnp.transpose(jnp.concatenate([local[0][half:], local[1][half:]], axis=0))
        return 0

    lax.fori_loop(0, n_pairs, local_states, 0, unroll=2)

    row = lax.broadcasted_iota(jnp.int32, (n_pairs, c, n2), 1)

    def shift(x, d, fill):
        return jnp.where(row >= d, pltpu.roll(x, d, 1), fill)

    tab = tab_ref[0]
    cin_r = carry_ref[:, 0:1, :]
    cin_i = carry_ref[:, 1:2, :]
    sr = sr_ref[...] + jnp.where(row == 0, tab[:, 0:1] * cin_r - tab[:, 1:2] * cin_i, 0.0)
    si = si_ref[...] + jnp.where(row == 0, tab[:, 0:1] * cin_i + tab[:, 1:2] * cin_r, 0.0)
    for s in range(n_steps):
        d = 1 << s
        ar, ai = tab[:, 2 * s:2 * s + 1], tab[:, 2 * s + 1:2 * s + 2]
        hr, hi = shift(sr, d, 0.0), shift(si, d, 0.0)
        sr, si = sr + ar * hr - ai * hi, si + ar * hi + ai * hr
    carry_ref[:, 0:1, :] = sr[:, c - 1:c]
    carry_ref[:, 1:2, :] = si[:, c - 1:c]
    sr_ref[...] = shift(sr, 1, cin_r)
    si_ref[...] = shift(si, 1, cin_i)

    def carried_response(gp, _):
        pr = jnp.transpose(sr_ref[gp])
        pi = jnp.transpose(si_ref[gp])
        for idx, g in enumerate((2 * gp, 2 * gp + 1)):
            prev = jnp.concatenate([pr[idx * half:(idx + 1) * half], pi[idx * half:(idx + 1) * half]], axis=0)
            y_ref[g] = y_ref[g] + jnp.dot(ct_ref[0, g], prev.astype(BF16), preferred_element_type=F32)
        return 0

    lax.fori_loop(0, n_pairs, carried_response, 0, unroll=2)

    groups_per_slab = LANES // p
    for t in range(tc):
        for a in range(n_slab):
            yt = jnp.concatenate([y_ref[g, t * p:(t + 1) * p, :]
                                  for g in range(a * groups_per_slab, (a + 1) * groups_per_slab)], axis=0)
            o_refs[a][0, pl.ds(t, c, stride=tc), :] = jnp.transpose(yt)


def _s5(u_slabs, tables, layer, c_lanes):
    kt, bt, ct, tab = tables
    n_slab = len(u_slabs)
    bsz, seq, _ = u_slabs[0].shape
    width = n_slab * LANES
    tc = S5_CHUNK
    n_groups, n2 = bt.shape[1], bt.shape[2]
    p = width // n_groups
    rows = c_lanes * tc
    kern = functools.partial(_s5_kernel, n_groups=n_groups, p=p, n_slab=n_slab)
    slab = pl.BlockSpec((1, rows, LANES), lambda b, j: (b, j, 0))
    return pl.pallas_call(
        kern,
        grid=(bsz, seq // rows),
        in_specs=[slab] * n_slab + [_layer_spec(t.shape, layer) for t in (kt, bt, ct, tab)],
        out_specs=[slab] * n_slab,
        out_shape=[jax.ShapeDtypeStruct((bsz, seq, LANES), F32)] * n_slab,
        scratch_shapes=[pltpu.VMEM((tc, width, c_lanes), BF16), pltpu.VMEM((n_groups, tc * p, c_lanes), F32),
                        pltpu.VMEM((n_groups // 2, SUBLANES, n2), F32),
                        pltpu.VMEM((n_groups // 2, c_lanes, n2), F32), pltpu.VMEM((n_groups // 2, c_lanes, n2), F32)],
        compiler_params=_cparams("parallel", "arbitrary"),
        name="s5",
    )(*u_slabs, kt, bt, ct, tab)


def _merge_kernel(*refs, n_groups, n_experts, d_model, n_rslab, n_slab):
    x_ref, ya_ref = refs[:2]
    yr_refs = refs[2:2 + n_rslab]
    ys_refs = refs[2 + n_rslab:2 + n_rslab + n_slab]
    (mg_ref, wg_ref, gw_ref, gb_ref, pr_ref, pa_ref, ps_ref, wo_ref, fg_ref, rw_ref, rb_ref,
     x1_ref, xn_ref, route_ref, routet_ref, tab_ref, cnt_ref, run_ref, wgb_ref) = refs[2 + n_rslab + n_slab:]
    i = pl.program_id(0)

    @pl.when(i == 0)
    def _():
        run_ref[...] = jnp.zeros_like(run_ref)
        wgb_ref[...] = wg_ref[0].astype(BF16)

    x = x_ref[...]
    tm = x.shape[0]
    xn = _rmsnorm(x, mg_ref[0], MIX_EPS)
    gates = _sigmoid(jnp.dot(xn.astype(BF16), wgb_ref[...], preferred_element_type=F32))
    z = _gelu_tanh(jnp.concatenate([ref[...] for ref in ys_refs], axis=1))
    ys = z * _sigmoid(jnp.dot(z.astype(BF16), gw_ref[0], preferred_element_type=F32) + gb_ref[0])
    y_rnn = jnp.concatenate([ref[...] for ref in yr_refs], axis=1).astype(BF16)
    merged = (gates[:, :d_model] * jnp.dot(y_rnn, pr_ref[0], preferred_element_type=F32)
              + gates[:, d_model:2 * d_model] * jnp.dot(ya_ref[...], pa_ref[0], preferred_element_type=F32)
              + gates[:, 2 * d_model:] * jnp.dot(ys.astype(BF16), ps_ref[0], preferred_element_type=F32))
    x1 = x + jnp.dot(merged.astype(BF16), wo_ref[0], preferred_element_type=F32)
    x1_ref[...] = x1
    xn2 = _rmsnorm(x1, fg_ref[0], MIX_EPS)
    xn_ref[...] = xn2.astype(xn_ref.dtype)

    rw = rw_ref[0]
    x_hi = xn2.astype(BF16)
    x_lo = (xn2 - x_hi.astype(F32)).astype(BF16)
    w_hi = rw.astype(BF16)
    w_lo = (rw - w_hi.astype(F32)).astype(BF16)
    logits = (jnp.dot(x_hi, w_hi, preferred_element_type=F32) + jnp.dot(x_lo, w_hi, preferred_element_type=F32)
              + jnp.dot(x_hi, w_lo, preferred_element_type=F32) + rb_ref[0])
    lane = lax.broadcasted_iota(jnp.int32, logits.shape, 1).astype(F32)
    big = float(LANES)
    coarse = jnp.where(lane < n_groups, logits, NEG_INF)
    cmax = jnp.max(coarse, axis=1, keepdims=True)
    gsel = jnp.min(jnp.where(coarse == cmax, lane, big), axis=1, keepdims=True)
    p_sel = 1.0 / jnp.sum(jnp.where(lane < n_groups, jnp.exp(logits - cmax), 0.0), axis=1, keepdims=True)
    lo = n_groups + gsel * n_experts
    fine = jnp.where((lane >= lo) & (lane < lo + n_experts), logits, NEG_INF)
    m1 = jnp.max(fine, axis=1, keepdims=True)
    i1 = jnp.min(jnp.where(fine == m1, lane, big), axis=1, keepdims=True)
    fine2 = jnp.where(lane == i1, NEG_INF, fine)
    m2 = jnp.max(fine2, axis=1, keepdims=True)
    i2 = jnp.min(jnp.where(fine2 == m2, lane, big), axis=1, keepdims=True)
    e21 = jnp.exp(m2 - m1)
    w1 = p_sel / (1.0 + e21)
    w2 = p_sel * e21 / (1.0 + e21)
    oh1 = lane == i1
    oh2 = lane == i2
    onehot = jnp.where(oh1 | oh2, 1.0, 0.0)
    r_i = lax.broadcasted_iota(jnp.int32, (tm, tm), 0)
    c_i = lax.broadcasted_iota(jnp.int32, (tm, tm), 1)
    earlier = jnp.where(c_i < r_i, 1.0, 0.0).astype(BF16)
    rank = jnp.dot(earlier, onehot.astype(BF16), preferred_element_type=F32)
    cnt = jnp.sum(onehot, axis=0, keepdims=True)
    cnt = jnp.floor((cnt + (SUBLANES - 1)) * (1.0 / SUBLANES)) * SUBLANES
    k_i = lax.broadcasted_iota(jnp.int32, (LANES, LANES), 0)
    l_i = lax.broadcasted_iota(jnp.int32, (LANES, LANES), 1)
    lower = jnp.where(k_i < l_i, 1.0, 0.0)
    start = jnp.dot(jnp.broadcast_to(cnt, (SUBLANES, LANES)), lower, preferred_element_type=F32,
                    precision=lax.Precision.HIGHEST)[0:1]
    pos = rank + start
    lp1 = jnp.sum(jnp.where(oh1, pos, 0.0), axis=1, keepdims=True)
    lp2 = jnp.sum(jnp.where(oh2, pos, 0.0), axis=1, keepdims=True)
    route = jnp.where(lane == 0, w1, 0.0)
    route = jnp.where(lane == 1, w2, route)
    route = jnp.where(lane == 2, lp1, route)
    route = jnp.where(lane == 3, lp2, route)
    route_ref[...] = route
    routet_ref[...] = jnp.transpose(route)[:SUBLANES]
    sub = lax.broadcasted_iota(jnp.int32, (SUBLANES, LANES), 0)
    tab_ref[...] = jnp.where(sub == 0, cnt, jnp.where(sub == 1, start, jnp.where(sub == 2, run_ref[...], 0.0)))
    run_ref[...] = run_ref[...] + cnt
    cnt_ref[...] = run_ref[...]


def _merge(x2d, y_attn, y_rnn_slabs, y_s5_slabs, mix_g, w_gate, glu_w, glu_b, p_rnn, p_attn, p_ssm, w_out, ffn_g,
           rw, rb, layer, gate_block, n_groups, n_experts, tm):
    n, d = x2d.shape
    row = lambda i: (i, 0)
    n_rslab, n_slab = len(y_rnn_slabs), len(y_s5_slabs)
    n_branch_cols = 3 * d
    kern = functools.partial(_merge_kernel, n_groups=n_groups, n_experts=n_experts, d_model=d, n_rslab=n_rslab,
                             n_slab=n_slab)
    consts = [mix_g, w_gate, glu_w, glu_b, p_rnn, p_attn, p_ssm, w_out, ffn_g, rw, rb]
    const_specs = [_layer_spec(a.shape, layer) for a in consts]
    const_specs[1] = pl.BlockSpec((1, d, n_branch_cols), lambda *_: (layer, 0, gate_block),
                                  pipeline_mode=pl.Buffered(1))
    return pl.pallas_call(
        kern,
        grid=(n // tm,),
        in_specs=[pl.BlockSpec((tm, d), row), pl.BlockSpec((tm, y_attn.shape[1]), row)]
                 + [pl.BlockSpec((tm, LANES), row)] * (n_rslab + n_slab) + const_specs,
        out_specs=[pl.BlockSpec((tm, d), row), pl.BlockSpec((tm, d), row), pl.BlockSpec((tm, LANES), row),
                   pl.BlockSpec((SUBLANES, tm), lambda i: (0, i)), pl.BlockSpec((SUBLANES, LANES), row),
                   pl.BlockSpec((1, LANES), lambda i: (0, 0))],
        out_shape=[jax.ShapeDtypeStruct((n, d), F32), jax.ShapeDtypeStruct((n, d), BF16),
                   jax.ShapeDtypeStruct((n, LANES), F32), jax.ShapeDtypeStruct((SUBLANES, n), F32),
                   jax.ShapeDtypeStruct((n // tm * SUBLANES, LANES), F32), jax.ShapeDtypeStruct((1, LANES), F32)],
        scratch_shapes=[pltpu.VMEM((1, LANES), F32), pltpu.VMEM((d, n_branch_cols), BF16)],
        compiler_params=_cparams("arbitrary"),
        name="merge_router",
    )(x2d, y_attn, *y_rnn_slabs, *y_s5_slabs, *consts)


def _sorted_rows(tm, n_total):
    rows = TOP_K_FINE * tm + n_total * (SUBLANES - 1)
    return -(-rows // LANES) * LANES


RUN_BITS = 6
TILE_BITS = 3


def _run_copies(base, n_total, cnt_ref, loc_ref, dst_ref, local_ref, remote_ref, sem, to_remote, wait):
    def copy(e, off, size):
        lstart = 0 if loc_ref is None else pl.multiple_of(loc_ref[base + e] + off, SUBLANES)
        local = local_ref.at[pl.ds(lstart, size)]
        remote = remote_ref.at[pl.ds(pl.multiple_of(dst_ref[base + e] + off, SUBLANES), size)]
        desc = pltpu.make_async_copy(local, remote, sem) if to_remote else pltpu.make_async_copy(remote, local, sem)
        if wait:
            desc.wait()
        else:
            desc.start()

    def expert(e, _):
        cnt = cnt_ref[base + e]
        big = 1 << RUN_BITS

        def chunk(c, _):
            copy(e,c * big, big)
            return 0

        n_big = cnt >> RUN_BITS
        lax.fori_loop(0, n_big, chunk, 0)
        off = n_big * big
        for b in reversed(range(TILE_BITS, RUN_BITS)):
            size = 1 << b

            @pl.when((cnt & size) != 0)
            def _(off=off, size=size):
                copy(e,off, size)

            off = off + (cnt & size)
        return 0

    lax.fori_loop(0, n_total, expert, 0, unroll=4 if n_total % 4 == 0 else 1)


def _wait_tile(base, n_total, min_rows, cnt_ref, loc_ref, local_ref, remote_ref, sem, to_remote):
    total = loc_ref[base + n_total - 1] + cnt_ref[base + n_total - 1]
    rest = total - min_rows

    def wait(size):
        local, remote = local_ref.at[pl.ds(0, size)], remote_ref.at[pl.ds(0, size)]
        (pltpu.make_async_copy(local, remote, sem) if to_remote else pltpu.make_async_copy(remote, local, sem)).wait()

    wait(min_rows)
    for b in range(TILE_BITS, (n_total * (SUBLANES - 1)).bit_length()):
        @pl.when((rest & (1 << b)) != 0)
        def _(b=b):
            wait(1 << b)


def _dispatch_kernel(cnt_ref, loc_ref, dst_ref, gap_cnt_ref, gap_dst_ref, x_ref, rt_ref, xs_ref, buf_ref, zero_ref,
                     sem, *, n_total):
    i = pl.program_id(0)
    last = pl.num_programs(0) - 1
    slot = i % 2
    tm = x_ref.shape[0]
    rows = buf_ref.shape[1]
    pos = rt_ref[...]
    j = lax.broadcasted_iota(jnp.int32, (rows, tm), 0).astype(F32)
    sel = jnp.where((j == pos[2:3]) | (j == pos[3:4]), 1.0, 0.0).astype(BF16)
    buf_ref[slot] = jnp.dot(sel, x_ref[...], preferred_element_type=F32)

    def copies(tile, tile_slot, wait):
        if wait:
            _wait_tile(tile * n_total, n_total, TOP_K_FINE * tm, cnt_ref, loc_ref, buf_ref.at[tile_slot], xs_ref,
                       sem.at[tile_slot], True)
        else:
            _run_copies(tile * n_total, n_total, cnt_ref, loc_ref, dst_ref, buf_ref.at[tile_slot], xs_ref,
                        sem.at[tile_slot], True, False)

    copies(i, slot, False)

    @pl.when(i > 0)
    def _():
        copies(i - 1, 1 - slot, True)

    @pl.when(i == last)
    def _():
        copies(i, slot, True)
        zero_ref[...] = jnp.zeros_like(zero_ref)
        for wait in (False, True):
            _run_copies(0, gap_cnt_ref.shape[0], gap_cnt_ref, None, gap_dst_ref, zero_ref, xs_ref, sem.at[0], True,
                        wait)


def _dispatch(tile_cnt, tile_loc, tile_dst, gap_cnt, gap_dst, xn2, route_t, n_rows, n_total, tm):
    n, d = xn2.shape
    grid_spec = pltpu.PrefetchScalarGridSpec(
        num_scalar_prefetch=5,
        grid=(n // tm,),
        in_specs=[pl.BlockSpec((tm, d), lambda i, *_: (i, 0)),
                  pl.BlockSpec((SUBLANES, tm), lambda i, *_: (0, i))],
        out_specs=pl.BlockSpec(memory_space=pl.ANY),
        scratch_shapes=[pltpu.VMEM((2, _sorted_rows(tm, n_total), d), F32), pltpu.VMEM((1 << RUN_BITS, d), F32),
                        pltpu.SemaphoreType.DMA((2,))],
    )
    return pl.pallas_call(
        functools.partial(_dispatch_kernel, n_total=n_total),
        grid_spec=grid_spec,
        out_shape=jax.ShapeDtypeStruct((n_rows, d), F32),
        compiler_params=_cparams("arbitrary"),
        name="moe_dispatch",
    )(tile_cnt, tile_loc, tile_dst, gap_cnt, gap_dst, xn2, route_t)


def _experts_kernel(te_ref, first_ref, slot_ref, next_ref, nact_ref, x_ref, w1_ref, w3_ref, w2_ref, o_ref,
                    w1f_ref, w3f_ref, w2f_ref, w1b_ref, w3b_ref, w2b_ref, sem):
    i = pl.program_id(0)

    def fetch(expert, slot, wait):
        for src, dst in ((w1_ref, w1f_ref), (w3_ref, w3f_ref), (w2_ref, w2f_ref)):
            copy = pltpu.make_async_copy(src.at[expert], dst.at[slot], sem.at[slot])
            if wait:
                copy.wait()
            else:
                copy.start()

    @pl.when(i == 0)
    def _():
        fetch(te_ref[0], slot_ref[0], False)

    @pl.when(jnp.logical_and(i < nact_ref[0], first_ref[i] == 1))
    def _():
        slot = slot_ref[i]
        fetch(te_ref[i], slot, True)
        w1b_ref[...] = w1f_ref[slot].astype(BF16)
        w3b_ref[...] = w3f_ref[slot].astype(BF16)
        w2b_ref[...] = w2f_ref[slot].astype(BF16)

        @pl.when(next_ref[i] >= 0)
        def _():
            fetch(next_ref[i], 1 - slot, False)

    @pl.when(i < nact_ref[0])
    def _():
        xb = x_ref[...].astype(BF16)
        h1 = jnp.dot(xb, w1b_ref[...], preferred_element_type=F32)
        h3 = jnp.dot(xb, w3b_ref[...], preferred_element_type=F32)
        hid = h1 * _sigmoid(h1) * h3
        o_ref[...] = jnp.dot(hid.astype(BF16), w2b_ref[...], preferred_element_type=F32).astype(o_ref.dtype)

    @pl.when(i >= nact_ref[0])
    def _():
        o_ref[...] = jnp.zeros_like(o_ref)


def _experts(tile_expert, tile_first, tile_slot, tile_next, n_active, xs, w1, w3, w2, tm):
    n_rows, d = xs.shape
    f = w1.shape[2]
    grid_spec = pltpu.PrefetchScalarGridSpec(
        num_scalar_prefetch=5,
        grid=(n_rows // tm,),
        in_specs=[pl.BlockSpec((tm, d), lambda i, te, fi, sl, nx, na: (jnp.minimum(i, na[0] - 1), 0)),
                  pl.BlockSpec(memory_space=pl.ANY), pl.BlockSpec(memory_space=pl.ANY),
                  pl.BlockSpec(memory_space=pl.ANY)],
        out_specs=pl.BlockSpec((tm, d), lambda i, *_: (i, 0)),
        scratch_shapes=[pltpu.VMEM((2, d, f), F32), pltpu.VMEM((2, d, f), F32), pltpu.VMEM((2, f, d), F32),
                        pltpu.VMEM((d, f), BF16), pltpu.VMEM((d, f), BF16), pltpu.VMEM((f, d), BF16),
                        pltpu.SemaphoreType.DMA((2,))],
    )
    return pl.pallas_call(
        _experts_kernel,
        grid_spec=grid_spec,
        out_shape=jax.ShapeDtypeStruct((n_rows, d), F32),
        compiler_params=_cparams("arbitrary"),
        name="moe_experts",
    )(tile_expert, tile_first, tile_slot, tile_next, n_active, xs, w1, w3, w2)


def _combine_kernel(cnt_ref, loc_ref, dst_ref, x_ref, route_ref, rt_ref, fg_ref, ys_ref, o_ref, buf_ref, sem,
                    *, n_total, final_norm):
    i = pl.program_id(0)
    slot = i % 2
    tm = x_ref.shape[0]
    rows = buf_ref.shape[1]

    def copies(tile, tile_slot, wait):
        if wait:
            _wait_tile(tile * n_total, n_total, TOP_K_FINE * tm, cnt_ref, loc_ref, buf_ref.at[tile_slot], ys_ref,
                       sem.at[tile_slot], False)
        else:
            _run_copies(tile * n_total, n_total, cnt_ref, loc_ref, dst_ref, buf_ref.at[tile_slot], ys_ref,
                        sem.at[tile_slot], False, False)

    @pl.when(i == 0)
    def _():
        buf_ref[...] = jnp.zeros_like(buf_ref)
        copies(0, 0, False)

    @pl.when(i + 1 < pl.num_programs(0))
    def _():
        copies(i + 1, 1 - slot, False)

    copies(i, slot, True)
    rt = rt_ref[...]
    jr = lax.broadcasted_iota(jnp.int32, (rows, tm), 0).astype(F32)
    gate = jnp.sum(jnp.where(jr == rt[2:3], rt[0:1], 0.0) + jnp.where(jr == rt[3:4], rt[1:2], 0.0),
                   axis=1, keepdims=True)
    yb = (buf_ref[slot] * gate).astype(BF16)
    route = route_ref[...]
    jc = lax.broadcasted_iota(jnp.int32, (tm, rows), 1).astype(F32)
    pick = jnp.where((jc == route[:, 2:3]) | (jc == route[:, 3:4]), 1.0, 0.0).astype(BF16)
    out = x_ref[...] + jnp.dot(pick, yb, preferred_element_type=F32)
    if final_norm:
        out = _rmsnorm(out, fg_ref[...], MIX_EPS)
    o_ref[...] = out


def _combine(tile_cnt, tile_loc, tile_dst, x1, route, route_t, final_g, ys, n_total, tm, final_norm):
    n, d = x1.shape
    kern = functools.partial(_combine_kernel, n_total=n_total, final_norm=final_norm)
    grid_spec = pltpu.PrefetchScalarGridSpec(
        num_scalar_prefetch=3,
        grid=(n // tm,),
        in_specs=[pl.BlockSpec((tm, d), lambda i, *_: (i, 0)),
                  pl.BlockSpec((tm, LANES), lambda i, *_: (i, 0)),
                  pl.BlockSpec((SUBLANES, tm), lambda i, *_: (0, i)),
                  pl.BlockSpec((1, d), lambda i, *_: (0, 0)),
                  pl.BlockSpec(memory_space=pl.ANY)],
        out_specs=pl.BlockSpec((tm, d), lambda i, *_: (i, 0)),
        scratch_shapes=[pltpu.VMEM((2, _sorted_rows(tm, n_total), d), F32), pltpu.SemaphoreType.DMA((2,))],
    )
    return pl.pallas_call(
        kern,
        grid_spec=grid_spec,
        out_shape=jax.ShapeDtypeStruct((n, d), F32),
        compiler_params=_cparams("arbitrary"),
        name="moe_combine",
    )(tile_cnt, tile_loc, tile_dst, x1, route, route_t, final_g, ys)


def _tile_plan(n, seq):
    return dict(inproj=min(512, n), rglru=min(256, seq), attn=min(512, seq), attn_heads=4,
                s5_lanes=min(LANES, seq // S5_CHUNK), merge=min(512, n), moe=min(512, n))


def kernel(x, positions, mix_norm_g, w_in, conv_w, conv_b, rg_wa, rg_ba, rg_wx, rg_bx, rg_lambda,
           lam_q1, lam_k1, lam_q2, lam_k2, subln_g,
           ssm_lambda_re, ssm_lambda_im, ssm_b_re, ssm_b_im, ssm_c_re, ssm_c_im, ssm_d, ssm_log_dt,
           ssm_glu_w, ssm_glu_b, proj_rnn, proj_attn, proj_ssm, w_out,
           ffn_norm_g, router_coarse_w, router_coarse_b, router_fine_w, router_fine_b,
           expert_w1, expert_w3, expert_w2, final_norm_g):
    return _forward(_tile_plan(x.shape[0] * x.shape[1], x.shape[1]),
                    x, positions, mix_norm_g, w_in, conv_w, conv_b, rg_wa, rg_ba, rg_wx, rg_bx, rg_lambda,
                    lam_q1, lam_k1, lam_q2, lam_k2, subln_g,
                    ssm_lambda_re, ssm_lambda_im, ssm_b_re, ssm_b_im, ssm_c_re, ssm_c_im, ssm_d, ssm_log_dt,
                    ssm_glu_w, ssm_glu_b, proj_rnn, proj_attn, proj_ssm, w_out,
                    ffn_norm_g, router_coarse_w, router_coarse_b, router_fine_w, router_fine_b,
                    expert_w1, expert_w3, expert_w2, final_norm_g)


def _forward(tiles, x, positions, mix_norm_g, w_in, conv_w, conv_b, rg_wa, rg_ba, rg_wx, rg_bx, rg_lambda,
             lam_q1, lam_k1, lam_q2, lam_k2, subln_g,
             ssm_lambda_re, ssm_lambda_im, ssm_b_re, ssm_b_im, ssm_c_re, ssm_c_im, ssm_d, ssm_log_dt,
             ssm_glu_w, ssm_glu_b, proj_rnn, proj_attn, proj_ssm, w_out,
             ffn_norm_g, router_coarse_w, router_coarse_b, router_fine_w, router_fine_b,
             expert_w1, expert_w3, expert_w2, final_norm_g):
    bsz, seq, d_model = x.shape
    depth = w_in.shape[0]
    n = bsz * seq
    r = conv_w.shape[2]
    sw = ssm_glu_w.shape[1]
    vdim = subln_g.shape[1]
    head_dim = vdim // 2
    in_cols = w_in.shape[2]
    qk = (in_cols - 2 * r - sw - 3 * d_model) // 3
    heads = qk // (2 * head_dim)
    splits = (r, 2 * r, 2 * r + qk, 2 * r + 2 * qk, 2 * r + 3 * qk, 2 * r + 3 * qk + sw)
    mix_cols = splits[-1]
    n_groups = router_coarse_w.shape[2]
    n_experts = expert_w1.shape[2]
    n_total = n_groups * n_experts

    tm_in, tt_rnn, tq = tiles["inproj"], tiles["rglru"], tiles["attn"]
    s5_lanes, tm_merge, tm_moe = tiles["s5_lanes"], tiles["merge"], tiles["moe"]
    n_rows = TOP_K_FINE * n + (n // tm_merge) * n_total * (SUBLANES - 1) + n_total * tm_moe
    n_rows = -(-n_rows // tm_moe) * tm_moe

    posf = positions.astype(F32)
    pos_col = posf.reshape(n, 1)
    pos8 = posf.reshape(bsz, seq // SUBLANES, SUBLANES)
    inv_freq = ROPE_THETA ** (-jnp.arange(0, head_dim, 2, dtype=F32) / head_dim)
    invf = jnp.tile(inv_freq, LANES // (head_dim // 2)).reshape(1, LANES)

    row3 = lambda a: a.reshape(depth, 1, a.shape[-1])
    gate_cols = in_cols - mix_cols
    in_place = mix_cols % gate_cols == 0
    w_gate = w_in if in_place else w_in[:, :, mix_cols:]
    gate_block = mix_cols // gate_cols if in_place else 0
    b_gates = row3(jnp.concatenate([rg_ba, rg_bx], axis=1))
    lamv = jnp.stack([lam_q1, lam_k1, lam_q2, lam_k2], axis=1)
    parts, ct_t, tab_t = jax.vmap(functools.partial(_s5_tables, n_steps=int(math.log2(s5_lanes))))(
        ssm_lambda_re, ssm_lambda_im, ssm_b_re, ssm_b_im, ssm_c_re, ssm_c_im, ssm_d, ssm_log_dt)
    n_ssm_groups = parts[0].shape[1]
    flat = lambda a: a.reshape((depth * n_ssm_groups,) + a.shape[2:])
    layered = lambda a: a.reshape((depth, n_ssm_groups) + a.shape[1:])
    kt, bt = _toeplitz(tuple(flat(a) for a in parts), math.gcd(n_ssm_groups, SUBLANES))
    tables = (layered(kt), layered(bt), ct_t, tab_t)
    rw = jnp.concatenate([router_coarse_w, router_fine_w], axis=2)
    rw = jnp.pad(rw, ((0, 0), (0, 0), (0, LANES - rw.shape[2])))
    rb = jnp.concatenate([router_coarse_b, router_fine_b], axis=1)
    rb = row3(jnp.pad(rb, ((0, 0), (0, LANES - rb.shape[1]))))
    merge_params = (row3(mix_norm_g), w_gate, ssm_glu_w.astype(BF16), row3(ssm_glu_b), proj_rnn.astype(BF16),
                    proj_attn.astype(BF16), proj_ssm.astype(BF16), w_out.astype(BF16), row3(ffn_norm_g), rw, rb)

    x2d = x.reshape(n, d_model)
    for l in range(depth):
        lambda_init = 0.8 - 0.6 * math.exp(-0.3 * l)
        q, k, vt, *slabs = _inproj(x2d, row3(mix_norm_g), pos_col, invf, w_in, l, splits, head_dim, tm_in, tq)
        slabs = [t.reshape(bsz, seq, LANES) for t in slabs]
        n_rs = r // LANES
        y_rnn = _rglru(slabs[:n_rs], slabs[n_rs:2 * n_rs], pos8, conv_w, row3(conv_b), rg_wa, rg_wx, b_gates,
                       row3(rg_lambda), l, tt_rnn)
        y_attn = _diff_attention(q.reshape(bsz, seq, qk), k.reshape(bsz, seq, qk), vt, lamv, row3(subln_g), l,
                                 heads, head_dim, lambda_init, tq, min(tiles["attn_heads"], heads))
        y_s5 = _s5(slabs[2 * n_rs:], tables, l, s5_lanes)
        x1, xn2, route, route_t, tile_tab, counts = _merge(
            x2d, y_attn.reshape(n, qk), [y.reshape(n, LANES) for y in y_rnn], [y.reshape(n, LANES) for y in y_s5],
            *merge_params, l, gate_block, n_groups, n_experts, tm_merge)

        cnt = counts[0, n_groups:n_groups + n_total].astype(jnp.int32)
        n_tiles = (cnt + tm_moe - 1) // tm_moe
        tile_end = jnp.cumsum(n_tiles)
        offsets = (tile_end - n_tiles) * tm_moe
        tab = tile_tab.reshape(n // tm_merge, SUBLANES, LANES)[:, :, n_groups:n_groups + n_total].astype(jnp.int32)
        tile_cnt = tab[:, 0].reshape(-1)
        tile_loc = tab[:, 1].reshape(-1)
        tile_dst = (tab[:, 2] + offsets[None, :]).reshape(-1)
        n_active = tile_end[-1:]
        gap_dst = jnp.concatenate([offsets + cnt, n_active * tm_moe])
        gap_cnt = jnp.concatenate([tile_end * tm_moe, jnp.full((1,), n_rows, jnp.int32)]) - gap_dst
        tile_ids = jnp.minimum(jnp.arange(n_rows // tm_moe, dtype=jnp.int32), n_active[0] - 1)
        tile_expert = jnp.sum((tile_ids[:, None] >= tile_end[None, :]).astype(jnp.int32), axis=1)
        e_ids = jnp.arange(n_total, dtype=jnp.int32)
        present = n_tiles > 0
        ordinal = jnp.cumsum(present.astype(jnp.int32)) - 1
        later = present[None, :] & (e_ids[None, :] > e_ids[:, None])
        succ = jnp.min(jnp.where(later, e_ids[None, :], n_total), axis=1)
        succ = jnp.where(succ < n_total, succ + l * n_total, -1)
        all_ids = jnp.arange(n_rows // tm_moe, dtype=jnp.int32)
        tile_first = ((all_ids < n_active[0]) & (all_ids == (tile_end - n_tiles)[tile_expert])).astype(jnp.int32)
        tile_slot = ordinal[tile_expert] % 2
        tile_next = succ[tile_expert]

        xs = _dispatch(tile_cnt, tile_loc, tile_dst, gap_cnt, gap_dst, xn2, route_t, n_rows, n_total, tm_merge)
        ys = _experts(tile_expert + l * n_total, tile_first, tile_slot, tile_next, n_active.astype(jnp.int32), xs,
                      expert_w1.reshape(depth * n_total, d_model, -1), expert_w3.reshape(depth * n_total, d_model, -1),
                      expert_w2.reshape(depth * n_total, -1, d_model), tm_moe)
        x2d = _combine(tile_cnt, tile_loc, tile_dst, x1, route, route_t, final_norm_g.reshape(1, d_model), ys,
                       n_total, tm_merge, l == depth - 1)
    return x2d.reshape(bsz, seq, d_model)
```

```python
import functools
import math

import jax
import jax.numpy as jnp
from jax import lax
from jax.experimental import pallas as pl
from jax.experimental.pallas import tpu as pltpu

F32 = jnp.float32
BF16 = jnp.bfloat16

RGLRU_C = 8.0
ROPE_THETA = 10000.0
TOP_K_FINE = 2
NEG_INF = -1e30
MIX_EPS = 1e-6
SUBLN_EPS = 1e-5

LANES = 128
SUBLANES = 8
VMEM_LIMIT_BYTES = 56 * 1024 * 1024

S5_CHUNK = 16


def _cparams(*sem, fusible=None):
    return pltpu.CompilerParams(dimension_semantics=sem, vmem_limit_bytes=VMEM_LIMIT_BYTES,
                                allow_input_fusion=fusible)


def _const_spec(shape):
    nd = len(shape)
    return pl.BlockSpec(shape, lambda *_: (0,) * nd, pipeline_mode=pl.Buffered(1))


def _layer_spec(shape, layer):
    nd = len(shape)
    return pl.BlockSpec((1,) + tuple(shape[1:]), lambda *_: (layer,) + (0,) * (nd - 1), pipeline_mode=pl.Buffered(1))


def _gelu_tanh(x):
    return 0.5 * x * (1.0 + jnp.tanh(math.sqrt(2.0 / math.pi) * (x + 0.044715 * (x * x * x))))


def _sigmoid(x):
    return 0.5 + 0.5 * jnp.tanh(0.5 * x)


def _rmsnorm(x, g, eps):
    return x * lax.rsqrt(jnp.mean(x * x, axis=-1, keepdims=True) + eps) * g


def _inproj_kernel(x_ref, g_ref, pos_ref, invf_ref, w_ref, *refs, splits, slab_cols, head_dim, q_scale):
    q_ref, k_ref, vt_ref = refs[:3]
    slab_refs, wb_ref = refs[3:-1], refs[-1]

    @pl.when(pl.program_id(0) == 0)
    def _():
        wb_ref[...] = w_ref[0].astype(BF16)

    x = x_ref[...]
    xn = _rmsnorm(x, g_ref[0], MIX_EPS)
    h = jnp.dot(xn.astype(BF16), wb_ref[...], preferred_element_type=F32)
    s0, s1, s2, s3, s4, s5 = splits
    for col, ref in zip(slab_cols, slab_refs):
        ref[...] = h[:, col:col + LANES]
    tkv = vt_ref.shape[2]
    for c in range(vt_ref.shape[0]):
        vt_ref[c] = jnp.transpose(h[c * tkv:(c + 1) * tkv, s3:s4]).astype(vt_ref.dtype)

    ang = pos_ref[...] * invf_ref[...]
    cos = jnp.cos(ang)
    sin = jnp.sin(ang)
    lane = lax.broadcasted_iota(jnp.int32, ang.shape, 1)
    first_half = (lane % head_dim) < (head_dim // 2)
    sin_signed = jnp.where(first_half, -sin, sin)

    def rope(t, scale):
        outs = []
        for a in range(t.shape[1] // LANES):
            xs = t[:, a * LANES:(a + 1) * LANES]
            fwd = pltpu.roll(xs, LANES - head_dim // 2, 1)
            bwd = pltpu.roll(xs, head_dim // 2, 1)
            rot = jnp.where(first_half, fwd, bwd)
            outs.append((xs * cos + rot * sin_signed) * scale)
        return jnp.concatenate(outs, axis=1)

    q_ref[...] = rope(h[:, s1:s2], q_scale).astype(q_ref.dtype)
    k_ref[...] = rope(h[:, s2:s3], 1.0).astype(k_ref.dtype)


def _inproj(x2d, g, posf, invf, w, layer, splits, head_dim, tm, tkv):
    n, d = x2d.shape
    mix_cols = splits[-1]
    widths = [splits[0]] + [splits[i] - splits[i - 1] for i in range(1, 6)]
    slab_cols = [start + a * LANES for start, width in ((0, widths[0]), (splits[0], widths[1]), (splits[4], widths[5]))
                 for a in range(width // LANES)]
    kern = functools.partial(_inproj_kernel, splits=splits, slab_cols=tuple(slab_cols), head_dim=head_dim,
                             q_scale=head_dim ** -0.5 * math.log2(math.e))
    row = lambda i: (i, 0)
    rows = lambda wd: pl.BlockSpec((tm, wd), row)
    out = lambda wd, dt: jax.ShapeDtypeStruct((n, wd), dt)
    return pl.pallas_call(
        kern,
        grid=(n // tm,),
        in_specs=[pl.BlockSpec((tm, d), row), _layer_spec(g.shape, layer), pl.BlockSpec((tm, 1), row),
                  _const_spec((1, LANES)), _layer_spec((w.shape[0], d, mix_cols), layer)],
        out_specs=[rows(widths[2]), rows(widths[3]),
                   pl.BlockSpec((tm // tkv, widths[4], tkv), lambda i: (i, 0, 0))] + [rows(LANES)] * len(slab_cols),
        out_shape=[out(widths[2], BF16), out(widths[3], BF16),
                   jax.ShapeDtypeStruct((n // tkv, widths[4], tkv), BF16)] + [out(LANES, F32)] * len(slab_cols),
        scratch_shapes=[pltpu.VMEM((d, mix_cols), BF16)],
        compiler_params=_cparams("arbitrary"),
        name="inproj",
    )(x2d, g, posf, invf, w)


def _rglru_kernel(*refs, n_slab):
    x_refs, g_refs = refs[:n_slab], refs[n_slab:2 * n_slab]
    pos_ref, cw_ref, cb_ref, wa_ref, wx_ref, b_ref, lam_ref = refs[2 * n_slab:2 * n_slab + 7]
    o_refs = refs[2 * n_slab + 7:3 * n_slab + 7]
    halo_ref, h_ref, w_ref = refs[3 * n_slab + 7:]
    j = pl.program_id(1)
    ph = SUBLANES
    m = x_refs[0].shape[1] // ph
    r = n_slab * LANES

    @pl.when(j == 0)
    def _():
        halo_ref[...] = jnp.zeros_like(halo_ref)
        h_ref[...] = jnp.zeros_like(h_ref)
        w_ref[...] = jnp.zeros_like(w_ref)
        blk = wa_ref.shape[2]
        for h in range(wa_ref.shape[1]):
            w_ref[h * blk:(h + 1) * blk, h * blk:(h + 1) * blk] = wa_ref[0, h].astype(BF16)
            w_ref[h * blk:(h + 1) * blk, r + h * blk:r + (h + 1) * blk] = wx_ref[0, h].astype(BF16)

    def phase(slabs, s):
        return jnp.concatenate([ref[0, pl.ds(s, m, stride=ph), :] for ref in slabs], axis=1)

    block = lax.broadcasted_iota(jnp.int32, (m, r), 0)

    def one_block_back(v, first):
        return jnp.where(block == 0, first, pltpu.roll(v, 1, 0))

    xs = [phase(x_refs, s) for s in range(ph)]
    halo = halo_ref[...]
    cw = cw_ref[0]
    taps = cw.shape[0]
    earlier = {s: one_block_back(xs[s], halo[s:s + 1]) for s in range(ph - taps + 1, ph)}
    xcs = []
    for s in range(ph):
        xc = cb_ref[0] + cw[0:1] * xs[s]
        for k in range(1, taps):
            xc = xc + cw[k:k + 1] * (xs[s - k] if s >= k else earlier[s - k + ph])
        xcs.append(xc)
    halo_ref[...] = jnp.concatenate([xs[s][m - 1:m] for s in range(ph)], axis=0)
    xc = jnp.concatenate(xcs, axis=0)

    gates = jnp.dot(xc.astype(BF16), w_ref[...], preferred_element_type=F32) + b_ref[0]
    rg = _sigmoid(gates[:, :r])
    ig = _sigmoid(gates[:, r:])
    z = -lam_ref[0]
    softplus = jnp.maximum(z, 0.0) + jnp.log(1.0 + jnp.exp(-jnp.abs(z)))
    a = jnp.exp((-RGLRU_C) * rg * softplus)
    mult = jnp.sqrt(1.0 - a * a)
    pos = pos_ref[0]
    reset = jnp.concatenate([pos[:, s:s + 1] for s in range(ph)], axis=0) == 0.0
    a = jnp.where(reset, 0.0, a)
    mult = jnp.where(reset, 1.0, mult)
    b = mult * ig * xc

    pa, pb = [a[0:m]], [b[0:m]]
    for s in range(1, ph):
        a_s = a[s * m:(s + 1) * m]
        pb.append(a_s * pb[-1] + b[s * m:(s + 1) * m])
        pa.append(a_s * pa[-1])
    ba, bb = pa[-1], pb[-1]
    d = 1
    while d < m:
        keep = block >= d
        a_sh = jnp.where(keep, pltpu.roll(ba, d, 0), 1.0)
        b_sh = jnp.where(keep, pltpu.roll(bb, d, 0), 0.0)
        bb = bb + ba * b_sh
        ba = ba * a_sh
        d *= 2
    h_prev = h_ref[...]
    h_end = bb + ba * h_prev
    h_in = one_block_back(h_end, h_prev)
    h_ref[...] = h_end[m - 1:m]
    for s in range(ph):
        out = (pb[s] + pa[s] * h_in) * _gelu_tanh(phase(g_refs, s))
        for c, o_ref in enumerate(o_refs):
            o_ref[0, pl.ds(s, m, stride=ph), :] = out[:, c * LANES:(c + 1) * LANES]


def _rglru(x_slabs, g_slabs, pos8, conv_w, conv_b, wa, wx, b_gates, lam, layer, tt):
    n_slab = len(x_slabs)
    bsz, seq, _ = x_slabs[0].shape
    r = n_slab * LANES
    slab = pl.BlockSpec((1, tt, LANES), lambda b, j: (b, j, 0))
    return pl.pallas_call(
        functools.partial(_rglru_kernel, n_slab=n_slab),
        grid=(bsz, seq // tt),
        in_specs=[slab] * (2 * n_slab) + [pl.BlockSpec((1, tt // SUBLANES, SUBLANES), lambda b, j: (b, j, 0))]
                 + [_layer_spec(a.shape, layer) for a in (conv_w, conv_b, wa, wx, b_gates, lam)],
        out_specs=[slab] * n_slab,
        out_shape=[jax.ShapeDtypeStruct((bsz, seq, LANES), F32)] * n_slab,
        scratch_shapes=[pltpu.VMEM((SUBLANES, r), F32), pltpu.VMEM((1, r), F32), pltpu.VMEM((r, 2 * r), BF16)],
        compiler_params=_cparams("parallel", "arbitrary"),
        name="rglru",
    )(*x_slabs, *g_slabs, pos8, conv_w, conv_b, wa, wx, b_gates, lam)


def _attn_kernel(q_ref, k_ref, vt_ref, lamv_ref, sg_ref, o_ref, *, tq, head_dim, lambda_init):
    i = pl.program_id(2)
    hw = 2 * head_dim
    hp = q_ref.shape[2] // hw
    vdim = vt_ref.shape[1] // hp
    lane = lax.broadcasted_iota(jnp.int32, (tq, hw), 1)
    zero = jnp.zeros((tq, hw), q_ref.dtype)
    ones = jnp.ones((2 * SUBLANES, tq), BF16)
    qqs = []
    for a in range(hp):
        q = q_ref[0, :, a * hw:(a + 1) * hw]
        qqs.append(jnp.concatenate([jnp.where(lane < head_dim, q, zero), jnp.where(lane >= head_dim, q, zero)],
                                   axis=0))

    def step(j, carry, masked):
        row0 = pl.multiple_of(j * tq, tq)
        out = []
        for a in range(hp):
            m, acc = carry[2 * a], carry[2 * a + 1]
            kb = k_ref[0, pl.ds(row0, tq), a * hw:(a + 1) * hw]
            s = lax.dot_general(kb, qqs[a], (((1,), (1,)), ((), ())), preferred_element_type=F32)
            if masked:
                key = lax.broadcasted_iota(jnp.int32, s.shape, 0)
                qry = lax.broadcasted_iota(jnp.int32, s.shape, 1)
                qry = jnp.where(qry >= tq, qry - tq, qry)
                s = jnp.where(key <= qry, s, NEG_INF)
            m_new = jnp.maximum(m, jnp.max(s, axis=0, keepdims=True))
            p = jnp.exp2(s - m_new)
            alpha = jnp.exp2(m - m_new)
            vt = jnp.concatenate([vt_ref[j, a * vdim:(a + 1) * vdim, :], ones], axis=0)
            out += [m_new, alpha * acc + jnp.dot(vt, p.astype(BF16), preferred_element_type=F32)]
        return tuple(out)

    init = (jnp.full((1, 2 * tq), NEG_INF, F32), jnp.zeros((vdim + 2 * SUBLANES, 2 * tq), F32)) * hp
    carry = lax.fori_loop(0, i, lambda j, c: step(j, c, False), init)
    carry = step(i, carry, True)

    lamv = lamv_ref[0]
    lam = (jnp.exp(jnp.sum(lamv[0:1] * lamv[1:2], axis=1, keepdims=True))
           - jnp.exp(jnp.sum(lamv[2:3] * lamv[3:4], axis=1, keepdims=True)) + lambda_init)
    for a in range(hp):
        acc = carry[2 * a + 1]
        ot = acc[:vdim] / acc[vdim:vdim + 1]
        o = jnp.transpose(ot[:, :tq] - lam * ot[:, tq:])
        o = _rmsnorm(o, sg_ref[0], SUBLN_EPS) * (1.0 - lambda_init)
        o_ref[0, :, a * vdim:(a + 1) * vdim] = o.astype(o_ref.dtype)


def _diff_attention(q, k, vt, lamv, subln_g, layer, heads, head_dim, lambda_init, tq, hp):
    bsz, seq, _ = q.shape
    vdim = vt.shape[1] // heads
    nkv = seq // tq
    kern = functools.partial(_attn_kernel, tq=tq, head_dim=head_dim, lambda_init=lambda_init)
    return pl.pallas_call(
        kern,
        grid=(bsz, heads // hp, seq // tq),
        in_specs=[pl.BlockSpec((1, tq, hp * 2 * head_dim), lambda b, h, i: (b, i, h)),
                  pl.BlockSpec((1, seq, hp * 2 * head_dim), lambda b, h, i: (b, 0, h)),
                  pl.BlockSpec((nkv, hp * vdim, tq), lambda b, h, i: (b, h, 0)),
                  _layer_spec(lamv.shape, layer), _layer_spec(subln_g.shape, layer)],
        out_specs=pl.BlockSpec((1, tq, hp * vdim), lambda b, h, i: (b, i, h)),
        out_shape=jax.ShapeDtypeStruct((bsz, seq, heads * vdim), BF16),
        compiler_params=_cparams("parallel", "parallel", "arbitrary"),
        name="diff_attn",
    )(q, k, vt, lamv, subln_g)


def _s5_tables(lam_re, lam_im, b_re, b_im, c_re, c_im, d_skip, log_dt, n_steps):
    tc = S5_CHUNK
    g, n, p = b_re.shape
    lr = lam_re.astype(F32)
    li = lam_im.astype(F32)
    dt = jnp.exp(log_dt.astype(F32))[:, None]
    mag = jnp.exp(lr * dt)
    ar = mag * jnp.cos(li * dt)
    ai = mag * jnp.sin(li * dt)
    den = lr * lr + li * li
    cr = ((ar - 1.0) * lr + ai * li) / den
    ci = (ai * lr - (ar - 1.0) * li) / den
    bb_re = cr[..., None] * b_re - ci[..., None] * b_im
    bb_im = cr[..., None] * b_im + ci[..., None] * b_re

    def apow(e):
        e = jnp.asarray(e, F32)[None, None, :]
        m = jnp.exp(e * (lr * dt)[..., None])
        ph = e * (li * dt)[..., None]
        return m * jnp.cos(ph), m * jnp.sin(ph)

    lags = jnp.arange(tc)
    pw_re, pw_im = apow(lags[::-1])
    cc = jnp.concatenate([c_re, -c_im], axis=2)
    skip = jnp.eye(p, dtype=F32)[None] * d_skip[:, None, :]
    p1_re, p1_im = apow(lags + 1)
    ca_re = jnp.einsum('gon,gnt->gton', c_re, p1_re) - jnp.einsum('gon,gnt->gton', c_im, p1_im)
    ca_im = jnp.einsum('gon,gnt->gton', c_re, p1_im) + jnp.einsum('gon,gnt->gton', c_im, p1_re)
    ct = jnp.concatenate([ca_re.reshape(g, tc * p, n), -ca_im.reshape(g, tc * p, n)], axis=2)
    st_re, st_im = apow(tc * (2 ** jnp.arange(n_steps)))
    tab = jnp.stack([st_re.transpose(0, 2, 1), st_im.transpose(0, 2, 1)], axis=2)
    tab = tab.reshape(g // 2, 2, 2 * n_steps, n).transpose(0, 2, 1, 3).reshape(g // 2, 2 * n_steps, 2 * n)
    tab = jnp.pad(tab, ((0, 0), (0, -(2 * n_steps) % SUBLANES), (0, 0)))
    return (pw_re, pw_im, bb_re, bb_im, cc, skip), ct.astype(BF16), tab.astype(F32)


def _toeplitz_kernel(pwr_ref, pwi_ref, bbr_ref, bbi_ref, cc_ref, skip_ref, o_ref, bt_ref):
    tc = S5_CHUNK
    p = cc_ref.shape[1]
    side = tc * p
    exact = functools.partial(jnp.dot, preferred_element_type=F32, precision=lax.Precision.HIGHEST)
    lane = lax.broadcasted_iota(jnp.int32, (tc, side), 1)
    row = lax.broadcasted_iota(jnp.int32, (tc, side), 0)
    spread = jnp.where((lane >= row * p) & (lane < (row + 1) * p), 1.0, 0.0).astype(BF16)
    lane = lax.broadcasted_iota(jnp.int32, (p, side), 1)
    row = lax.broadcasted_iota(jnp.int32, (p, side), 0)
    repeat = jnp.where(lane % p == row, 1.0, 0.0).astype(BF16)

    def place(re, im, sel):
        rest = jnp.concatenate([re, im], axis=0)
        out = None
        for _ in range(3):
            piece = rest.astype(BF16)
            rest = rest - piece.astype(F32)
            term = jnp.dot(piece, sel, preferred_element_type=F32)
            out = term if out is None else out + term
        return out[:re.shape[0]], out[re.shape[0]:]

    for gi in range(cc_ref.shape[0]):
        pr, pi = place(pwr_ref[gi], pwi_ref[gi], spread)
        br, bi = place(bbr_ref[gi], bbi_ref[gi], repeat)
        bt = jnp.concatenate([pr * br - pi * bi, pr * bi + pi * br], axis=0)
        bt_ref[gi] = bt.astype(bt_ref.dtype)
        k = exact(cc_ref[gi], bt)
        k = jnp.concatenate([k[:, :(tc - 1) * p], k[:, (tc - 1) * p:] + skip_ref[gi]], axis=1)
        s = jnp.concatenate([k, jnp.zeros((p, (tc - 1) * p), F32)], axis=1)
        o_ref[gi] = jnp.concatenate([s[:, (tc - 1 - t) * p:(tc - 1 - t) * p + tc * p] for t in range(tc)],
                                    axis=0).astype(o_ref.dtype)


def _toeplitz(parts, groups_per_step):
    cc = parts[4]
    m, p, n2 = cc.shape
    side = S5_CHUNK * p
    blk = lambda a: pl.BlockSpec((groups_per_step,) + a.shape[1:], lambda i: (i, 0, 0))
    return pl.pallas_call(
        _toeplitz_kernel,
        grid=(m // groups_per_step,),
        in_specs=[blk(a) for a in parts],
        out_specs=[pl.BlockSpec((groups_per_step, side, side), lambda i: (i, 0, 0)),
                   pl.BlockSpec((groups_per_step, n2, side), lambda i: (i, 0, 0))],
        out_shape=[jax.ShapeDtypeStruct((m, side, side), BF16), jax.ShapeDtypeStruct((m, n2, side), BF16)],
        compiler_params=_cparams("parallel"),
        name="s5_toeplitz",
    )(*parts)


def _s5_kernel(*refs, n_groups, p, n_slab):
    u_refs = refs[:n_slab]
    kt_ref, bt_ref, ct_ref, tab_ref = refs[n_slab:n_slab + 4]
    o_refs = refs[n_slab + 4:2 * n_slab + 4]
    ut_ref, y_ref, carry_ref, sr_ref, si_ref = refs[2 * n_slab + 4:]
    tc = S5_CHUNK
    c = u_refs[0].shape[1] // tc
    n2 = bt_ref.shape[2]
    half = n2 // 2
    n_steps = int(math.log2(c))

    @pl.when(pl.program_id(1) == 0)
    def _():
        carry_ref[...] = jnp.zeros_like(carry_ref)

    for k in range(tc):
        for a in range(n_slab):
            ut_ref[k, a * LANES:(a + 1) * LANES, :] = jnp.transpose(
                u_refs[a][0, pl.ds(k, c, stride=tc), :]).astype(BF16)

    n_pairs = n_groups // 2

    def local_states(gp, _):
        local = []
        for g in (2 * gp, 2 * gp + 1):
            ug = jnp.concatenate([ut_ref[k, pl.ds(pl.multiple_of(g * p, p), p), :] for k in range(tc)], axis=0)
            y_ref[g] = jnp.dot(kt_ref[0, g], ug, preferred_element_type=F32)
            local.append(jnp.dot(bt_ref[0, g], ug, preferred_element_type=F32))
        sr_ref[gp] = jnp.transpose(jnp.concatenate([local[0][:half], local[1][:half]], axis=0))
        si_ref[gp] = jnp.transpose(jnp.concatenate([local[0][half:], local[1][half:]], axis=0))
        return 0

    lax.fori_loop(0, n_pairs, local_states, 0, unroll=2)

    row = lax.broadcasted_iota(jnp.int32, (n_pairs, c, n2), 1)

    def shift(x, d, fill):
        return jnp.where(row >= d, pltpu.roll(x, d, 1), fill)

    tab = tab_ref[0]
    cin_r = carry_ref[:, 0:1, :]
    cin_i = carry_ref[:, 1:2, :]
    sr = sr_ref[...] + jnp.where(row == 0, tab[:, 0:1] * cin_r - tab[:, 1:2] * cin_i, 0.0)
    si = si_ref[...] + jnp.where(row == 0, tab[:, 0:1] * cin_i + tab[:, 1:2] * cin_r, 0.0)
    for s in range(n_steps):
        d = 1 << s
        ar, ai = tab[:, 2 * s:2 * s + 1], tab[:, 2 * s + 1:2 * s + 2]
        hr, hi = shift(sr, d, 0.0), shift(si, d, 0.0)
        sr, si = sr + ar * hr - ai * hi, si + ar * hi + ai * hr
    carry_ref[:, 0:1, :] = sr[:, c - 1:c]
    carry_ref[:, 1:2, :] = si[:, c - 1:c]
    sr_ref[...] = shift(sr, 1, cin_r)
    si_ref[...] = shift(si, 1, cin_i)

    def carried_response(gp, _):
        pr = jnp.transpose(sr_ref[gp])
        pi = jnp.transpose(si_ref[gp])
        for idx, g in enumerate((2 * gp, 2 * gp + 1)):
            prev = jnp.concatenate([pr[idx * half:(idx + 1) * half], pi[idx * half:(idx + 1) * half]], axis=0)
            y_ref[g] = y_ref[g] + jnp.dot(ct_ref[0, g], prev.astype(BF16), preferred_element_type=F32)
        return 0

    lax.fori_loop(0, n_pairs, carried_response, 0, unroll=2)

    groups_per_slab = LANES // p
    for t in range(tc):
        for a in range(n_slab):
            yt = jnp.concatenate([y_ref[g, t * p:(t + 1) * p, :]
                                  for g in range(a * groups_per_slab, (a + 1) * groups_per_slab)], axis=0)
            o_refs[a][0, pl.ds(t, c, stride=tc), :] = jnp.transpose(yt)


def _s5(u_slabs, tables, layer, c_lanes):
    kt, bt, ct, tab = tables
    n_slab = len(u_slabs)
    bsz, seq, _ = u_slabs[0].shape
    width = n_slab * LANES
    tc = S5_CHUNK
    n_groups, n2 = bt.shape[1], bt.shape[2]
    p = width // n_groups
    rows = c_lanes * tc
    kern = functools.partial(_s5_kernel, n_groups=n_groups, p=p, n_slab=n_slab)
    slab = pl.BlockSpec((1, rows, LANES), lambda b, j: (b, j, 0))
    return pl.pallas_call(
        kern,
        grid=(bsz, seq // rows),
        in_specs=[slab] * n_slab + [_layer_spec(t.shape, layer) for t in (kt, bt, ct, tab)],
        out_specs=[slab] * n_slab,
        out_shape=[jax.ShapeDtypeStruct((bsz, seq, LANES), F32)] * n_slab,
        scratch_shapes=[pltpu.VMEM((tc, width, c_lanes), BF16), pltpu.VMEM((n_groups, tc * p, c_lanes), F32),
                        pltpu.VMEM((n_groups // 2, SUBLANES, n2), F32),
                        pltpu.VMEM((n_groups // 2, c_lanes, n2), F32), pltpu.VMEM((n_groups // 2, c_lanes, n2), F32)],
        compiler_params=_cparams("parallel", "arbitrary"),
        name="s5",
    )(*u_slabs, kt, bt, ct, tab)


def _merge_kernel(*refs, n_groups, n_experts, d_model, n_rslab, n_slab):
    x_ref, ya_ref = refs[:2]
    yr_refs = refs[2:2 + n_rslab]
    ys_refs = refs[2 + n_rslab:2 + n_rslab + n_slab]
    (mg_ref, wg_ref, gw_ref, gb_ref, pr_ref, pa_ref, ps_ref, wo_ref, fg_ref, rw_ref, rb_ref,
     x1_ref, xn_ref, route_ref, routet_ref, tab_ref, cnt_ref, run_ref, wgb_ref) = refs[2 + n_rslab + n_slab:]
    i = pl.program_id(0)

    @pl.when(i == 0)
    def _():
        run_ref[...] = jnp.zeros_like(run_ref)
        wgb_ref[...] = wg_ref[0].astype(BF16)

    x = x_ref[...]
    tm = x.shape[0]
    xn = _rmsnorm(x, mg_ref[0], MIX_EPS)
    gates = _sigmoid(jnp.dot(xn.astype(BF16), wgb_ref[...], preferred_element_type=F32))
    z = _gelu_tanh(jnp.concatenate([ref[...] for ref in ys_refs], axis=1))
    ys = z * _sigmoid(jnp.dot(z.astype(BF16), gw_ref[0], preferred_element_type=F32) + gb_ref[0])
    y_rnn = jnp.concatenate([ref[...] for ref in yr_refs], axis=1).astype(BF16)
    merged = (gates[:, :d_model] * jnp.dot(y_rnn, pr_ref[0], preferred_element_type=F32)
              + gates[:, d_model:2 * d_model] * jnp.dot(ya_ref[...], pa_ref[0], preferred_element_type=F32)
              + gates[:, 2 * d_model:] * jnp.dot(ys.astype(BF16), ps_ref[0], preferred_element_type=F32))
    x1 = x + jnp.dot(merged.astype(BF16), wo_ref[0], preferred_element_type=F32)
    x1_ref[...] = x1
    xn2 = _rmsnorm(x1, fg_ref[0], MIX_EPS)
    xn_ref[...] = xn2.astype(xn_ref.dtype)

    rw = rw_ref[0]
    x_hi = xn2.astype(BF16)
    x_lo = (xn2 - x_hi.astype(F32)).astype(BF16)
    w_hi = rw.astype(BF16)
    w_lo = (rw - w_hi.astype(F32)).astype(BF16)
    logits = (jnp.dot(x_hi, w_hi, preferred_element_type=F32) + jnp.dot(x_lo, w_hi, preferred_element_type=F32)
              + jnp.dot(x_hi, w_lo, preferred_element_type=F32) + rb_ref[0])
    lane = lax.broadcasted_iota(jnp.int32, logits.shape, 1).astype(F32)
    big = float(LANES)
    coarse = jnp.where(lane < n_groups, logits, NEG_INF)
    cmax = jnp.max(coarse, axis=1, keepdims=True)
    gsel = jnp.min(jnp.where(coarse == cmax, lane, big), axis=1, keepdims=True)
    p_sel = 1.0 / jnp.sum(jnp.where(lane < n_groups, jnp.exp(logits - cmax), 0.0), axis=1, keepdims=True)
    lo = n_groups + gsel * n_experts
    fine = jnp.where((lane >= lo) & (lane < lo + n_experts), logits, NEG_INF)
    m1 = jnp.max(fine, axis=1, keepdims=True)
    i1 = jnp.min(jnp.where(fine == m1, lane, big), axis=1, keepdims=True)
    fine2 = jnp.where(lane == i1, NEG_INF, fine)
    m2 = jnp.max(fine2, axis=1, keepdims=True)
    i2 = jnp.min(jnp.where(fine2 == m2, lane, big), axis=1, keepdims=True)
    e21 = jnp.exp(m2 - m1)
    w1 = p_sel / (1.0 + e21)
    w2 = p_sel * e21 / (1.0 + e21)
    oh1 = lane == i1
    oh2 = lane == i2
    onehot = jnp.where(oh1 | oh2, 1.0, 0.0)
    r_i = lax.broadcasted_iota(jnp.int32, (tm, tm), 0)
    c_i = lax.broadcasted_iota(jnp.int32, (tm, tm), 1)
    earlier = jnp.where(c_i < r_i, 1.0, 0.0).astype(BF16)
    rank = jnp.dot(earlier, onehot.astype(BF16), preferred_element_type=F32)
    cnt = jnp.sum(onehot, axis=0, keepdims=True)
    cnt = jnp.floor((cnt + (SUBLANES - 1)) * (1.0 / SUBLANES)) * SUBLANES
    k_i = lax.broadcasted_iota(jnp.int32, (LANES, LANES), 0)
    l_i = lax.broadcasted_iota(jnp.int32, (LANES, LANES), 1)
    lower = jnp.where(k_i < l_i, 1.0, 0.0)
    start = jnp.dot(jnp.broadcast_to(cnt, (SUBLANES, LANES)), lower, preferred_element_type=F32,
                    precision=lax.Precision.HIGHEST)[0:1]
    pos = rank + start
    lp1 = jnp.sum(jnp.where(oh1, pos, 0.0), axis=1, keepdims=True)
    lp2 = jnp.sum(jnp.where(oh2, pos, 0.0), axis=1, keepdims=True)
    route = jnp.where(lane == 0, w1, 0.0)
    route = jnp.where(lane == 1, w2, route)
    route = jnp.where(lane == 2, lp1, route)
    route = jnp.where(lane == 3, lp2, route)
    route_ref[...] = route
    routet_ref[...] = jnp.transpose(route)[:SUBLANES]
    sub = lax.broadcasted_iota(jnp.int32, (SUBLANES, LANES), 0)
    tab_ref[...] = jnp.where(sub == 0, cnt, jnp.where(sub == 1, start, jnp.where(sub == 2, run_ref[...], 0.0)))
    run_ref[...] = run_ref[...] + cnt
    cnt_ref[...] = run_ref[...]


def _merge(x2d, y_attn, y_rnn_slabs, y_s5_slabs, mix_g, w_gate, glu_w, glu_b, p_rnn, p_attn, p_ssm, w_out, ffn_g,
           rw, rb, layer, gate_block, n_groups, n_experts, tm):
    n, d = x2d.shape
    row = lambda i: (i, 0)
    n_rslab, n_slab = len(y_rnn_slabs), len(y_s5_slabs)
    n_branch_cols = 3 * d
    kern = functools.partial(_merge_kernel, n_groups=n_groups, n_experts=n_experts, d_model=d, n_rslab=n_rslab,
                             n_slab=n_slab)
    consts = [mix_g, w_gate, glu_w, glu_b, p_rnn, p_attn, p_ssm, w_out, ffn_g, rw, rb]
    const_specs = [_layer_spec(a.shape, layer) for a in consts]
    const_specs[1] = pl.BlockSpec((1, d, n_branch_cols), lambda *_: (layer, 0, gate_block),
                                  pipeline_mode=pl.Buffered(1))
    return pl.pallas_call(
        kern,
        grid=(n // tm,),
        in_specs=[pl.BlockSpec((tm, d), row), pl.BlockSpec((tm, y_attn.shape[1]), row)]
                 + [pl.BlockSpec((tm, LANES), row)] * (n_rslab + n_slab) + const_specs,
        out_specs=[pl.BlockSpec((tm, d), row), pl.BlockSpec((tm, d), row), pl.BlockSpec((tm, LANES), row),
                   pl.BlockSpec((SUBLANES, tm), lambda i: (0, i)), pl.BlockSpec((SUBLANES, LANES), row),
                   pl.BlockSpec((1, LANES), lambda i: (0, 0))],
        out_shape=[jax.ShapeDtypeStruct((n, d), F32), jax.ShapeDtypeStruct((n, d), BF16),
                   jax.ShapeDtypeStruct((n, LANES), F32), jax.ShapeDtypeStruct((SUBLANES, n), F32),
                   jax.ShapeDtypeStruct((n // tm * SUBLANES, LANES), F32), jax.ShapeDtypeStruct((1, LANES), F32)],
        scratch_shapes=[pltpu.VMEM((1, LANES), F32), pltpu.VMEM((d, n_branch_cols), BF16)],
        compiler_params=_cparams("arbitrary", fusible=[False] * (2 + n_rslab + n_slab)
                                 + [k != 1 for k in range(len(consts))]),
        name="merge_router",
    )(x2d, y_attn, *y_rnn_slabs, *y_s5_slabs, *consts)


def _sorted_rows(tm, n_total):
    rows = TOP_K_FINE * tm + n_total * (SUBLANES - 1)
    return -(-rows // LANES) * LANES


RUN_BITS = 6
TILE_BITS = 3


def _run_copies(base, n_total, cnt_ref, loc_ref, dst_ref, local_ref, remote_ref, sem, to_remote, wait):
    def copy(e, off, size):
        lstart = 0 if loc_ref is None else pl.multiple_of(loc_ref[base + e] + off, SUBLANES)
        local = local_ref.at[pl.ds(lstart, size)]
        remote = remote_ref.at[pl.ds(pl.multiple_of(dst_ref[base + e] + off, SUBLANES), size)]
        desc = pltpu.make_async_copy(local, remote, sem) if to_remote else pltpu.make_async_copy(remote, local, sem)
        if wait:
            desc.wait()
        else:
            desc.start()

    def expert(e, _):
        cnt = cnt_ref[base + e]
        big = 1 << RUN_BITS

        def chunk(c, _):
            copy(e,c * big, big)
            return 0

        n_big = cnt >> RUN_BITS
        lax.fori_loop(0, n_big, chunk, 0)
        off = n_big * big
        for b in reversed(range(TILE_BITS, RUN_BITS)):
            size = 1 << b

            @pl.when((cnt & size) != 0)
            def _(off=off, size=size):
                copy(e,off, size)

            off = off + (cnt & size)
        return 0

    lax.fori_loop(0, n_total, expert, 0, unroll=4 if n_total % 4 == 0 else 1)


def _wait_tile(base, n_total, min_rows, cnt_ref, loc_ref, local_ref, remote_ref, sem, to_remote):
    total = loc_ref[base + n_total - 1] + cnt_ref[base + n_total - 1]
    rest = total - min_rows

    def wait(size):
        local, remote = local_ref.at[pl.ds(0, size)], remote_ref.at[pl.ds(0, size)]
        (pltpu.make_async_copy(local, remote, sem) if to_remote else pltpu.make_async_copy(remote, local, sem)).wait()

    wait(min_rows)
    for b in range(TILE_BITS, (n_total * (SUBLANES - 1)).bit_length()):
        @pl.when((rest & (1 << b)) != 0)
        def _(b=b):
            wait(1 << b)


def _dispatch_kernel(cnt_ref, loc_ref, dst_ref, gap_cnt_ref, gap_dst_ref, x_ref, rt_ref, xs_ref, buf_ref, zero_ref,
                     sem, *, n_total):
    i = pl.program_id(0)
    last = pl.num_programs(0) - 1
    slot = i % 2
    tm = x_ref.shape[0]
    rows = buf_ref.shape[1]
    pos = rt_ref[...]
    j = lax.broadcasted_iota(jnp.int32, (rows, tm), 0).astype(F32)
    sel = jnp.where((j == pos[2:3]) | (j == pos[3:4]), 1.0, 0.0).astype(BF16)
    buf_ref[slot] = jnp.dot(sel, x_ref[...], preferred_element_type=F32)

    def copies(tile, tile_slot, wait):
        if wait:
            _wait_tile(tile * n_total, n_total, TOP_K_FINE * tm, cnt_ref, loc_ref, buf_ref.at[tile_slot], xs_ref,
                       sem.at[tile_slot], True)
        else:
            _run_copies(tile * n_total, n_total, cnt_ref, loc_ref, dst_ref, buf_ref.at[tile_slot], xs_ref,
                        sem.at[tile_slot], True, False)

    copies(i, slot, False)

    @pl.when(i > 0)
    def _():
        copies(i - 1, 1 - slot, True)

    @pl.when(i == last)
    def _():
        copies(i, slot, True)
        zero_ref[...] = jnp.zeros_like(zero_ref)
        for wait in (False, True):
            _run_copies(0, gap_cnt_ref.shape[0], gap_cnt_ref, None, gap_dst_ref, zero_ref, xs_ref, sem.at[0], True,
                        wait)


def _dispatch(tile_cnt, tile_loc, tile_dst, gap_cnt, gap_dst, xn2, route_t, n_rows, n_total, tm):
    n, d = xn2.shape
    grid_spec = pltpu.PrefetchScalarGridSpec(
        num_scalar_prefetch=5,
        grid=(n // tm,),
        in_specs=[pl.BlockSpec((tm, d), lambda i, *_: (i, 0)),
                  pl.BlockSpec((SUBLANES, tm), lambda i, *_: (0, i))],
        out_specs=pl.BlockSpec(memory_space=pl.ANY),
        scratch_shapes=[pltpu.VMEM((2, _sorted_rows(tm, n_total), d), F32), pltpu.VMEM((1 << RUN_BITS, d), F32),
                        pltpu.SemaphoreType.DMA((2,))],
    )
    return pl.pallas_call(
        functools.partial(_dispatch_kernel, n_total=n_total),
        grid_spec=grid_spec,
        out_shape=jax.ShapeDtypeStruct((n_rows, d), F32),
        compiler_params=_cparams("arbitrary"),
        name="moe_dispatch",
    )(tile_cnt, tile_loc, tile_dst, gap_cnt, gap_dst, xn2, route_t)


def _experts_kernel(te_ref, first_ref, slot_ref, next_ref, nact_ref, x_ref, w1_ref, w3_ref, w2_ref, o_ref,
                    w1f_ref, w3f_ref, w2f_ref, w1b_ref, w3b_ref, w2b_ref, sem):
    i = pl.program_id(0)

    def fetch(expert, slot, wait):
        for src, dst in ((w1_ref, w1f_ref), (w3_ref, w3f_ref), (w2_ref, w2f_ref)):
            copy = pltpu.make_async_copy(src.at[expert], dst.at[slot], sem.at[slot])
            if wait:
                copy.wait()
            else:
                copy.start()

    @pl.when(i == 0)
    def _():
        fetch(te_ref[0], slot_ref[0], False)

    @pl.when(jnp.logical_and(i < nact_ref[0], first_ref[i] == 1))
    def _():
        slot = slot_ref[i]
        fetch(te_ref[i], slot, True)
        w1b_ref[...] = w1f_ref[slot].astype(BF16)
        w3b_ref[...] = w3f_ref[slot].astype(BF16)
        w2b_ref[...] = w2f_ref[slot].astype(BF16)

        @pl.when(next_ref[i] >= 0)
        def _():
            fetch(next_ref[i], 1 - slot, False)

    @pl.when(i < nact_ref[0])
    def _():
        xb = x_ref[...].astype(BF16)
        h1 = jnp.dot(xb, w1b_ref[...], preferred_element_type=F32)
        h3 = jnp.dot(xb, w3b_ref[...], preferred_element_type=F32)
        hid = h1 * _sigmoid(h1) * h3
        o_ref[...] = jnp.dot(hid.astype(BF16), w2b_ref[...], preferred_element_type=F32).astype(o_ref.dtype)

    @pl.when(i >= nact_ref[0])
    def _():
        o_ref[...] = jnp.zeros_like(o_ref)


def _experts(tile_expert, tile_first, tile_slot, tile_next, n_active, xs, w1, w3, w2, tm):
    n_rows, d = xs.shape
    f = w1.shape[2]
    grid_spec = pltpu.PrefetchScalarGridSpec(
        num_scalar_prefetch=5,
        grid=(n_rows // tm,),
        in_specs=[pl.BlockSpec((tm, d), lambda i, te, fi, sl, nx, na: (jnp.minimum(i, na[0] - 1), 0)),
                  pl.BlockSpec(memory_space=pl.ANY), pl.BlockSpec(memory_space=pl.ANY),
                  pl.BlockSpec(memory_space=pl.ANY)],
        out_specs=pl.BlockSpec((tm, d), lambda i, *_: (i, 0)),
        scratch_shapes=[pltpu.VMEM((2, d, f), F32), pltpu.VMEM((2, d, f), F32), pltpu.VMEM((2, f, d), F32),
                        pltpu.VMEM((d, f), BF16), pltpu.VMEM((d, f), BF16), pltpu.VMEM((f, d), BF16),
                        pltpu.SemaphoreType.DMA((2,))],
    )
    return pl.pallas_call(
        _experts_kernel,
        grid_spec=grid_spec,
        out_shape=jax.ShapeDtypeStruct((n_rows, d), F32),
        compiler_params=_cparams("arbitrary"),
        name="moe_experts",
    )(tile_expert, tile_first, tile_slot, tile_next, n_active, xs, w1, w3, w2)


def _combine_kernel(cnt_ref, loc_ref, dst_ref, x_ref, route_ref, rt_ref, fg_ref, ys_ref, o_ref, buf_ref, sem,
                    *, n_total, final_norm):
    i = pl.program_id(0)
    slot = i % 2
    tm = x_ref.shape[0]
    rows = buf_ref.shape[1]

    def copies(tile, tile_slot, wait):
        if wait:
            _wait_tile(tile * n_total, n_total, TOP_K_FINE * tm, cnt_ref, loc_ref, buf_ref.at[tile_slot], ys_ref,
                       sem.at[tile_slot], False)
        else:
            _run_copies(tile * n_total, n_total, cnt_ref, loc_ref, dst_ref, buf_ref.at[tile_slot], ys_ref,
                        sem.at[tile_slot], False, False)

    @pl.when(i == 0)
    def _():
        buf_ref[...] = jnp.zeros_like(buf_ref)
        copies(0, 0, False)

    @pl.when(i + 1 < pl.num_programs(0))
    def _():
        copies(i + 1, 1 - slot, False)

    copies(i, slot, True)
    rt = rt_ref[...]
    jr = lax.broadcasted_iota(jnp.int32, (rows, tm), 0).astype(F32)
    gate = jnp.sum(jnp.where(jr == rt[2:3], rt[0:1], 0.0) + jnp.where(jr == rt[3:4], rt[1:2], 0.0),
                   axis=1, keepdims=True)
    yb = (buf_ref[slot] * gate).astype(BF16)
    route = route_ref[...]
    jc = lax.broadcasted_iota(jnp.int32, (tm, rows), 1).astype(F32)
    pick = jnp.where((jc == route[:, 2:3]) | (jc == route[:, 3:4]), 1.0, 0.0).astype(BF16)
    out = x_ref[...] + jnp.dot(pick, yb, preferred_element_type=F32)
    if final_norm:
        out = _rmsnorm(out, fg_ref[...], MIX_EPS)
    o_ref[...] = out


def _combine(tile_cnt, tile_loc, tile_dst, x1, route, route_t, final_g, ys, n_total, tm, final_norm):
    n, d = x1.shape
    kern = functools.partial(_combine_kernel, n_total=n_total, final_norm=final_norm)
    grid_spec = pltpu.PrefetchScalarGridSpec(
        num_scalar_prefetch=3,
        grid=(n // tm,),
        in_specs=[pl.BlockSpec((tm, d), lambda i, *_: (i, 0)),
                  pl.BlockSpec((tm, LANES), lambda i, *_: (i, 0)),
                  pl.BlockSpec((SUBLANES, tm), lambda i, *_: (0, i)),
                  pl.BlockSpec((1, d), lambda i, *_: (0, 0)),
                  pl.BlockSpec(memory_space=pl.ANY)],
        out_specs=pl.BlockSpec((tm, d), lambda i, *_: (i, 0)),
        scratch_shapes=[pltpu.VMEM((2, _sorted_rows(tm, n_total), d), F32), pltpu.SemaphoreType.DMA((2,))],
    )
    return pl.pallas_call(
        kern,
        grid_spec=grid_spec,
        out_shape=jax.ShapeDtypeStruct((n, d), F32),
        compiler_params=_cparams("arbitrary"),
        name="moe_combine",
    )(tile_cnt, tile_loc, tile_dst, x1, route, route_t, final_g, ys)


def _tile_plan(n, seq):
    return dict(inproj=min(512, n), rglru=min(256, seq), attn=min(512, seq), attn_heads=4,
                s5_lanes=min(LANES, seq // S5_CHUNK), merge=min(512, n), moe=min(512, n))


def kernel(x, positions, mix_norm_g, w_in, conv_w, conv_b, rg_wa, rg_ba, rg_wx, rg_bx, rg_lambda,
           lam_q1, lam_k1, lam_q2, lam_k2, subln_g,
           ssm_lambda_re, ssm_lambda_im, ssm_b_re, ssm_b_im, ssm_c_re, ssm_c_im, ssm_d, ssm_log_dt,
           ssm_glu_w, ssm_glu_b, proj_rnn, proj_attn, proj_ssm, w_out,
           ffn_norm_g, router_coarse_w, router_coarse_b, router_fine_w, router_fine_b,
           expert_w1, expert_w3, expert_w2, final_norm_g):
    return _forward(_tile_plan(x.shape[0] * x.shape[1], x.shape[1]),
                    x, positions, mix_norm_g, w_in, conv_w, conv_b, rg_wa, rg_ba, rg_wx, rg_bx, rg_lambda,
                    lam_q1, lam_k1, lam_q2, lam_k2, subln_g,
                    ssm_lambda_re, ssm_lambda_im, ssm_b_re, ssm_b_im, ssm_c_re, ssm_c_im, ssm_d, ssm_log_dt,
                    ssm_glu_w, ssm_glu_b, proj_rnn, proj_attn, proj_ssm, w_out,
                    ffn_norm_g, router_coarse_w, router_coarse_b, router_fine_w, router_fine_b,
                    expert_w1, expert_w3, expert_w2, final_norm_g)


def _forward(tiles, x, positions, mix_norm_g, w_in, conv_w, conv_b, rg_wa, rg_ba, rg_wx, rg_bx, rg_lambda,
             lam_q1, lam_k1, lam_q2, lam_k2, subln_g,
             ssm_lambda_re, ssm_lambda_im, ssm_b_re, ssm_b_im, ssm_c_re, ssm_c_im, ssm_d, ssm_log_dt,
             ssm_glu_w, ssm_glu_b, proj_rnn, proj_attn, proj_ssm, w_out,
             ffn_norm_g, router_coarse_w, router_coarse_b, router_fine_w, router_fine_b,
             expert_w1, expert_w3, expert_w2, final_norm_g):
    bsz, seq, d_model = x.shape
    depth = w_in.shape[0]
    n = bsz * seq
    r = conv_w.shape[2]
    sw = ssm_glu_w.shape[1]
    vdim = subln_g.shape[1]
    head_dim = vdim // 2
    in_cols = w_in.shape[2]
    qk = (in_cols - 2 * r - sw - 3 * d_model) // 3
    heads = qk // (2 * head_dim)
    splits = (r, 2 * r, 2 * r + qk, 2 * r + 2 * qk, 2 * r + 3 * qk, 2 * r + 3 * qk + sw)
    mix_cols = splits[-1]
    n_groups = router_coarse_w.shape[2]
    n_experts = expert_w1.shape[2]
    n_total = n_groups * n_experts

    tm_in, tt_rnn, tq = tiles["inproj"], tiles["rglru"], tiles["attn"]
    s5_lanes, tm_merge, tm_moe = tiles["s5_lanes"], tiles["merge"], tiles["moe"]
    n_rows = TOP_K_FINE * n + (n // tm_merge) * n_total * (SUBLANES - 1) + n_total * tm_moe
    n_rows = -(-n_rows // tm_moe) * tm_moe

    posf = positions.astype(F32)
    pos_col = posf.reshape(n, 1)
    pos8 = posf.reshape(bsz, seq // SUBLANES, SUBLANES)
    inv_freq = ROPE_THETA ** (-jnp.arange(0, head_dim, 2, dtype=F32) / head_dim)
    invf = jnp.tile(inv_freq, LANES // (head_dim // 2)).reshape(1, LANES)

    row3 = lambda a: a.reshape(depth, 1, a.shape[-1])
    gate_cols = in_cols - mix_cols
    in_place = mix_cols % gate_cols == 0
    w_gate = w_in if in_place else w_in[:, :, mix_cols:]
    gate_block = mix_cols // gate_cols if in_place else 0
    b_gates = row3(jnp.concatenate([rg_ba, rg_bx], axis=1))
    lamv = jnp.stack([lam_q1, lam_k1, lam_q2, lam_k2], axis=1)
    parts, ct_t, tab_t = jax.vmap(functools.partial(_s5_tables, n_steps=int(math.log2(s5_lanes))))(
        ssm_lambda_re, ssm_lambda_im, ssm_b_re, ssm_b_im, ssm_c_re, ssm_c_im, ssm_d, ssm_log_dt)
    n_ssm_groups = parts[0].shape[1]
    flat = lambda a: a.reshape((depth * n_ssm_groups,) + a.shape[2:])
    layered = lambda a: a.reshape((depth, n_ssm_groups) + a.shape[1:])
    kt, bt = _toeplitz(tuple(flat(a) for a in parts), math.gcd(n_ssm_groups, SUBLANES))
    tables = (layered(kt), layered(bt), ct_t, tab_t)
    rw = jnp.concatenate([router_coarse_w, router_fine_w], axis=2)
    rw = jnp.pad(rw, ((0, 0), (0, 0), (0, LANES - rw.shape[2])))
    rb = jnp.concatenate([router_coarse_b, router_fine_b], axis=1)
    rb = row3(jnp.pad(rb, ((0, 0), (0, LANES - rb.shape[1]))))
    merge_params = (row3(mix_norm_g), w_gate, ssm_glu_w.astype(BF16), row3(ssm_glu_b), proj_rnn.astype(BF16),
                    proj_attn.astype(BF16), proj_ssm.astype(BF16), w_out.astype(BF16), row3(ffn_norm_g), rw, rb)

    x2d = x.reshape(n, d_model)
    for l in range(depth):
        lambda_init = 0.8 - 0.6 * math.exp(-0.3 * l)
        q, k, vt, *slabs = _inproj(x2d, row3(mix_norm_g), pos_col, invf, w_in, l, splits, head_dim, tm_in, tq)
        slabs = [t.reshape(bsz, seq, LANES) for t in slabs]
        n_rs = r // LANES
        y_rnn = _rglru(slabs[:n_rs], slabs[n_rs:2 * n_rs], pos8, conv_w, row3(conv_b), rg_wa, rg_wx, b_gates,
                       row3(rg_lambda), l, tt_rnn)
        y_attn = _diff_attention(q.reshape(bsz, seq, qk), k.reshape(bsz, seq, qk), vt, lamv, row3(subln_g), l,
                                 heads, head_dim, lambda_init, tq, min(tiles["attn_heads"], heads))
        y_s5 = _s5(slabs[2 * n_rs:], tables, l, s5_lanes)
        x1, xn2, route, route_t, tile_tab, counts = _merge(
            x2d, y_attn.reshape(n, qk), [y.reshape(n, LANES) for y in y_rnn], [y.reshape(n, LANES) for y in y_s5],
            *merge_params, l, gate_block, n_groups, n_experts, tm_merge)

        cnt = counts[0, n_groups:n_groups + n_total].astype(jnp.int32)
        n_tiles = (cnt + tm_moe - 1) // tm_moe
        tile_end = jnp.cumsum(n_tiles)
        offsets = (tile_end - n_tiles) * tm_moe
        tab = tile_tab.reshape(n // tm_merge, SUBLANES, LANES)[:, :, n_groups:n_groups + n_total].astype(jnp.int32)
        tile_cnt = tab[:, 0].reshape(-1)
        tile_loc = tab[:, 1].reshape(-1)
        tile_dst = (tab[:, 2] + offsets[None, :]).reshape(-1)
        n_active = tile_end[-1:]
        gap_dst = jnp.concatenate([offsets + cnt, n_active * tm_moe])
        gap_cnt = jnp.concatenate([tile_end * tm_moe, jnp.full((1,), n_rows, jnp.int32)]) - gap_dst
        tile_ids = jnp.minimum(jnp.arange(n_rows // tm_moe, dtype=jnp.int32), n_active[0] - 1)
        tile_expert = jnp.sum((tile_ids[:, None] >= tile_end[None, :]).astype(jnp.int32), axis=1)
        e_ids = jnp.arange(n_total, dtype=jnp.int32)
        present = n_tiles > 0
        ordinal = jnp.cumsum(present.astype(jnp.int32)) - 1
        later = present[None, :] & (e_ids[None, :] > e_ids[:, None])
        succ = jnp.min(jnp.where(later, e_ids[None, :], n_total), axis=1)
        succ = jnp.where(succ < n_total, succ + l * n_total, -1)
        all_ids = jnp.arange(n_rows // tm_moe, dtype=jnp.int32)
        tile_first = ((all_ids < n_active[0]) & (all_ids == (tile_end - n_tiles)[tile_expert])).astype(jnp.int32)
        tile_slot = ordinal[tile_expert] % 2
        tile_next = succ[tile_expert]

        xs = _dispatch(tile_cnt, tile_loc, tile_dst, gap_cnt, gap_dst, xn2, route_t, n_rows, n_total, tm_merge)
        ys = _experts(tile_expert + l * n_total, tile_first, tile_slot, tile_next, n_active.astype(jnp.int32), xs,
                      expert_w1.reshape(depth * n_total, d_model, -1), expert_w3.reshape(depth * n_total, d_model, -1),
                      expert_w2.reshape(depth * n_total, -1, d_model), tm_moe)
        x2d = _combine(tile_cnt, tile_loc, tile_dst, x1, route, route_t, final_norm_g.reshape(1, d_model), ys,
                       n_total, tm_merge, l == depth - 1)
    return x2d.reshape(bsz, seq, d_model)
```

```python
import functools
import math

import jax
import jax.numpy as jnp
from jax import lax
from jax.experimental import pallas as pl
from jax.experimental.pallas import tpu as pltpu

F32 = jnp.float32
BF16 = jnp.bfloat16

RGLRU_C = 8.0
ROPE_THETA = 10000.0
TOP_K_FINE = 2
NEG_INF = -1e30
MIX_EPS = 1e-6
SUBLN_EPS = 1e-5

LANES = 128
SUBLANES = 8
VMEM_LIMIT_BYTES = 56 * 1024 * 1024

S5_CHUNK = 16


def _cparams(*sem, fusible=None):
    return pltpu.CompilerParams(dimension_semantics=sem, vmem_limit_bytes=VMEM_LIMIT_BYTES,
                                allow_input_fusion=fusible)


def _const_spec(shape):
    nd = len(shape)
    return pl.BlockSpec(shape, lambda *_: (0,) * nd, pipeline_mode=pl.Buffered(1))


def _layer_spec(shape, layer):
    nd = len(shape)
    return pl.BlockSpec((1,) + tuple(shape[1:]), lambda *_: (layer,) + (0,) * (nd - 1), pipeline_mode=pl.Buffered(1))


def _gelu_tanh(x):
    return 0.5 * x * (1.0 + jnp.tanh(math.sqrt(2.0 / math.pi) * (x + 0.044715 * (x * x * x))))


def _sigmoid(x):
    return 0.5 + 0.5 * jnp.tanh(0.5 * x)


def _rmsnorm(x, g, eps):
    return x * lax.rsqrt(jnp.mean(x * x, axis=-1, keepdims=True) + eps) * g


def _inproj_kernel(x_ref, g_ref, pos_ref, invf_ref, w_ref, *refs, splits, slab_cols, head_dim, q_scale):
    q_ref, k_ref, vt_ref = refs[:3]
    slab_refs, wb_ref = refs[3:-1], refs[-1]

    @pl.when(pl.program_id(0) == 0)
    def _():
        wb_ref[...] = w_ref[0].astype(BF16)

    x = x_ref[...]
    xn = _rmsnorm(x, g_ref[0], MIX_EPS)
    h = jnp.dot(xn.astype(BF16), wb_ref[...], preferred_element_type=F32)
    s0, s1, s2, s3, s4, s5 = splits
    for col, ref in zip(slab_cols, slab_refs):
        ref[...] = h[:, col:col + LANES]
    tkv = vt_ref.shape[2]
    for c in range(vt_ref.shape[0]):
        vt_ref[c] = jnp.transpose(h[c * tkv:(c + 1) * tkv, s3:s4]).astype(vt_ref.dtype)

    ang = pos_ref[...] * invf_ref[...]
    cos = jnp.cos(ang)
    sin = jnp.sin(ang)
    lane = lax.broadcasted_iota(jnp.int32, ang.shape, 1)
    first_half = (lane % head_dim) < (head_dim // 2)
    sin_signed = jnp.where(first_half, -sin, sin)

    def rope(t, scale):
        outs = []
        for a in range(t.shape[1] // LANES):
            xs = t[:, a * LANES:(a + 1) * LANES]
            fwd = pltpu.roll(xs, LANES - head_dim // 2, 1)
            bwd = pltpu.roll(xs, head_dim // 2, 1)
            rot = jnp.where(first_half, fwd, bwd)
            outs.append((xs * cos + rot * sin_signed) * scale)
        return jnp.concatenate(outs, axis=1)

    q_ref[...] = rope(h[:, s1:s2], q_scale).astype(q_ref.dtype)
    k_ref[...] = rope(h[:, s2:s3], 1.0).astype(k_ref.dtype)


def _inproj(x2d, g, posf, invf, w, layer, splits, head_dim, tm, tkv):
    n, d = x2d.shape
    mix_cols = splits[-1]
    widths = [splits[0]] + [splits[i] - splits[i - 1] for i in range(1, 6)]
    slab_cols = [start + a * LANES for start, width in ((0, widths[0]), (splits[0], widths[1]), (splits[4], widths[5]))
                 for a in range(width // LANES)]
    kern = functools.partial(_inproj_kernel, splits=splits, slab_cols=tuple(slab_cols), head_dim=head_dim,
                             q_scale=head_dim ** -0.5 * math.log2(math.e))
    row = lambda i: (i, 0)
    rows = lambda wd: pl.BlockSpec((tm, wd), row)
    out = lambda wd, dt: jax.ShapeDtypeStruct((n, wd), dt)
    return pl.pallas_call(
        kern,
        grid=(n // tm,),
        in_specs=[pl.BlockSpec((tm, d), row), _layer_spec(g.shape, layer), pl.BlockSpec((tm, 1), row),
                  _const_spec((1, LANES)), _layer_spec((w.shape[0], d, mix_cols), layer)],
        out_specs=[rows(widths[2]), rows(widths[3]),
                   pl.BlockSpec((tm // tkv, widths[4], tkv), lambda i: (i, 0, 0))] + [rows(LANES)] * len(slab_cols),
        out_shape=[out(widths[2], BF16), out(widths[3], BF16),
                   jax.ShapeDtypeStruct((n // tkv, widths[4], tkv), BF16)] + [out(LANES, F32)] * len(slab_cols),
        scratch_shapes=[pltpu.VMEM((d, mix_cols), BF16)],
        compiler_params=_cparams("arbitrary"),
        name="inproj",
    )(x2d, g, posf, invf, w)


def _rglru_kernel(*refs, n_slab):
    x_refs, g_refs = refs[:n_slab], refs[n_slab:2 * n_slab]
    pos_ref, cw_ref, cb_ref, wa_ref, wx_ref, b_ref, lam_ref = refs[2 * n_slab:2 * n_slab + 7]
    o_refs = refs[2 * n_slab + 7:3 * n_slab + 7]
    halo_ref, h_ref, w_ref = refs[3 * n_slab + 7:]
    j = pl.program_id(1)
    ph = SUBLANES
    m = x_refs[0].shape[1] // ph
    r = n_slab * LANES

    @pl.when(j == 0)
    def _():
        halo_ref[...] = jnp.zeros_like(halo_ref)
        h_ref[...] = jnp.zeros_like(h_ref)
        w_ref[...] = jnp.zeros_like(w_ref)
        blk = wa_ref.shape[2]
        for h in range(wa_ref.shape[1]):
            w_ref[h * blk:(h + 1) * blk, h * blk:(h + 1) * blk] = wa_ref[0, h].astype(BF16)
            w_ref[h * blk:(h + 1) * blk, r + h * blk:r + (h + 1) * blk] = wx_ref[0, h].astype(BF16)

    def phase(slabs, s):
        return jnp.concatenate([ref[0, pl.ds(s, m, stride=ph), :] for ref in slabs], axis=1)

    block = lax.broadcasted_iota(jnp.int32, (m, r), 0)

    def one_block_back(v, first):
        return jnp.where(block == 0, first, pltpu.roll(v, 1, 0))

    xs = [phase(x_refs, s) for s in range(ph)]
    halo = halo_ref[...]
    cw = cw_ref[0]
    taps = cw.shape[0]
    earlier = {s: one_block_back(xs[s], halo[s:s + 1]) for s in range(ph - taps + 1, ph)}
    xcs = []
    for s in range(ph):
        xc = cb_ref[0] + cw[0:1] * xs[s]
        for k in range(1, taps):
            xc = xc + cw[k:k + 1] * (xs[s - k] if s >= k else earlier[s - k + ph])
        xcs.append(xc)
    halo_ref[...] = jnp.concatenate([xs[s][m - 1:m] for s in range(ph)], axis=0)
    xc = jnp.concatenate(xcs, axis=0)

    gates = jnp.dot(xc.astype(BF16), w_ref[...], preferred_element_type=F32) + b_ref[0]
    rg = _sigmoid(gates[:, :r])
    ig = _sigmoid(gates[:, r:])
    z = -lam_ref[0]
    softplus = jnp.maximum(z, 0.0) + jnp.log(1.0 + jnp.exp(-jnp.abs(z)))
    a = jnp.exp((-RGLRU_C) * rg * softplus)
    mult = jnp.sqrt(1.0 - a * a)
    pos = pos_ref[0]
    reset = jnp.concatenate([pos[:, s:s + 1] for s in range(ph)], axis=0) == 0.0
    a = jnp.where(reset, 0.0, a)
    mult = jnp.where(reset, 1.0, mult)
    b = mult * ig * xc

    pa, pb = [a[0:m]], [b[0:m]]
    for s in range(1, ph):
        a_s = a[s * m:(s + 1) * m]
        pb.append(a_s * pb[-1] + b[s * m:(s + 1) * m])
        pa.append(a_s * pa[-1])
    ba, bb = pa[-1], pb[-1]
    d = 1
    while d < m:
        keep = block >= d
        a_sh = jnp.where(keep, pltpu.roll(ba, d, 0), 1.0)
        b_sh = jnp.where(keep, pltpu.roll(bb, d, 0), 0.0)
        bb = bb + ba * b_sh
        ba = ba * a_sh
        d *= 2
    h_prev = h_ref[...]
    h_end = bb + ba * h_prev
    h_in = one_block_back(h_end, h_prev)
    h_ref[...] = h_end[m - 1:m]
    for s in range(ph):
        out = (pb[s] + pa[s] * h_in) * _gelu_tanh(phase(g_refs, s))
        for c, o_ref in enumerate(o_refs):
            o_ref[0, pl.ds(s, m, stride=ph), :] = out[:, c * LANES:(c + 1) * LANES]


def _rglru(x_slabs, g_slabs, pos8, conv_w, conv_b, wa, wx, b_gates, lam, layer, tt):
    n_slab = len(x_slabs)
    bsz, seq, _ = x_slabs[0].shape
    r = n_slab * LANES
    slab = pl.BlockSpec((1, tt, LANES), lambda b, j: (b, j, 0))
    return pl.pallas_call(
        functools.partial(_rglru_kernel, n_slab=n_slab),
        grid=(bsz, seq // tt),
        in_specs=[slab] * (2 * n_slab) + [pl.BlockSpec((1, tt // SUBLANES, SUBLANES), lambda b, j: (b, j, 0))]
                 + [_layer_spec(a.shape, layer) for a in (conv_w, conv_b, wa, wx, b_gates, lam)],
        out_specs=[slab] * n_slab,
        out_shape=[jax.ShapeDtypeStruct((bsz, seq, LANES), F32)] * n_slab,
        scratch_shapes=[pltpu.VMEM((SUBLANES, r), F32), pltpu.VMEM((1, r), F32), pltpu.VMEM((r, 2 * r), BF16)],
        compiler_params=_cparams("parallel", "arbitrary", fusible=[False] * (2 * n_slab + 1)
                                 + [False, True, False, False, True, True]),
        name="rglru",
    )(*x_slabs, *g_slabs, pos8, conv_w, conv_b, wa, wx, b_gates, lam)


def _attn_kernel(q_ref, k_ref, vt_ref, lamv_ref, sg_ref, o_ref, *, tq, head_dim, lambda_init):
    i = pl.program_id(2)
    hw = 2 * head_dim
    hp = q_ref.shape[2] // hw
    vdim = vt_ref.shape[1] // hp
    lane = lax.broadcasted_iota(jnp.int32, (tq, hw), 1)
    zero = jnp.zeros((tq, hw), q_ref.dtype)
    ones = jnp.ones((2 * SUBLANES, tq), BF16)
    qqs = []
    for a in range(hp):
        q = q_ref[0, :, a * hw:(a + 1) * hw]
        qqs.append(jnp.concatenate([jnp.where(lane < head_dim, q, zero), jnp.where(lane >= head_dim, q, zero)],
                                   axis=0))

    def step(j, carry, masked):
        row0 = pl.multiple_of(j * tq, tq)
        out = []
        for a in range(hp):
            m, acc = carry[2 * a], carry[2 * a + 1]
            kb = k_ref[0, pl.ds(row0, tq), a * hw:(a + 1) * hw]
            s = lax.dot_general(kb, qqs[a], (((1,), (1,)), ((), ())), preferred_element_type=F32)
            if masked:
                key = lax.broadcasted_iota(jnp.int32, s.shape, 0)
                qry = lax.broadcasted_iota(jnp.int32, s.shape, 1)
                qry = jnp.where(qry >= tq, qry - tq, qry)
                s = jnp.where(key <= qry, s, NEG_INF)
            m_new = jnp.maximum(m, jnp.max(s, axis=0, keepdims=True))
            p = jnp.exp2(s - m_new)
            alpha = jnp.exp2(m - m_new)
            vt = jnp.concatenate([vt_ref[j, a * vdim:(a + 1) * vdim, :], ones], axis=0)
            out += [m_new, alpha * acc + jnp.dot(vt, p.astype(BF16), preferred_element_type=F32)]
        return tuple(out)

    init = (jnp.full((1, 2 * tq), NEG_INF, F32), jnp.zeros((vdim + 2 * SUBLANES, 2 * tq), F32)) * hp
    carry = lax.fori_loop(0, i, lambda j, c: step(j, c, False), init)
    carry = step(i, carry, True)

    lamv = lamv_ref[0]
    lam = (jnp.exp(jnp.sum(lamv[0:1] * lamv[1:2], axis=1, keepdims=True))
           - jnp.exp(jnp.sum(lamv[2:3] * lamv[3:4], axis=1, keepdims=True)) + lambda_init)
    for a in range(hp):
        acc = carry[2 * a + 1]
        ot = acc[:vdim] / acc[vdim:vdim + 1]
        o = jnp.transpose(ot[:, :tq] - lam * ot[:, tq:])
        o = _rmsnorm(o, sg_ref[0], SUBLN_EPS) * (1.0 - lambda_init)
        o_ref[0, :, a * vdim:(a + 1) * vdim] = o.astype(o_ref.dtype)


def _diff_attention(q, k, vt, lamv, subln_g, layer, heads, head_dim, lambda_init, tq, hp):
    bsz, seq, _ = q.shape
    vdim = vt.shape[1] // heads
    nkv = seq // tq
    kern = functools.partial(_attn_kernel, tq=tq, head_dim=head_dim, lambda_init=lambda_init)
    return pl.pallas_call(
        kern,
        grid=(bsz, heads // hp, seq // tq),
        in_specs=[pl.BlockSpec((1, tq, hp * 2 * head_dim), lambda b, h, i: (b, i, h)),
                  pl.BlockSpec((1, seq, hp * 2 * head_dim), lambda b, h, i: (b, 0, h)),
                  pl.BlockSpec((nkv, hp * vdim, tq), lambda b, h, i: (b, h, 0)),
                  _layer_spec(lamv.shape, layer), _layer_spec(subln_g.shape, layer)],
        out_specs=pl.BlockSpec((1, tq, hp * vdim), lambda b, h, i: (b, i, h)),
        out_shape=jax.ShapeDtypeStruct((bsz, seq, heads * vdim), BF16),
        compiler_params=_cparams("parallel", "parallel", "arbitrary"),
        name="diff_attn",
    )(q, k, vt, lamv, subln_g)


def _s5_tables(lam_re, lam_im, b_re, b_im, c_re, c_im, d_skip, log_dt, n_steps):
    tc = S5_CHUNK
    g, n, p = b_re.shape
    lr = lam_re.astype(F32)
    li = lam_im.astype(F32)
    dt = jnp.exp(log_dt.astype(F32))[:, None]
    mag = jnp.exp(lr * dt)
    ar = mag * jnp.cos(li * dt)
    ai = mag * jnp.sin(li * dt)
    den = lr * lr + li * li
    cr = ((ar - 1.0) * lr + ai * li) / den
    ci = (ai * lr - (ar - 1.0) * li) / den
    bb_re = cr[..., None] * b_re - ci[..., None] * b_im
    bb_im = cr[..., None] * b_im + ci[..., None] * b_re

    def apow(e):
        e = jnp.asarray(e, F32)[None, None, :]
        m = jnp.exp(e * (lr * dt)[..., None])
        ph = e * (li * dt)[..., None]
        return m * jnp.cos(ph), m * jnp.sin(ph)

    lags = jnp.arange(tc)
    pw_re, pw_im = apow(lags[::-1])
    cc = jnp.concatenate([c_re, -c_im], axis=2)
    skip = jnp.eye(p, dtype=F32)[None] * d_skip[:, None, :]
    p1_re, p1_im = apow(lags + 1)
    ca_re = jnp.einsum('gon,gnt->gton', c_re, p1_re) - jnp.einsum('gon,gnt->gton', c_im, p1_im)
    ca_im = jnp.einsum('gon,gnt->gton', c_re, p1_im) + jnp.einsum('gon,gnt->gton', c_im, p1_re)
    ct = jnp.concatenate([ca_re.reshape(g, tc * p, n), -ca_im.reshape(g, tc * p, n)], axis=2)
    st_re, st_im = apow(tc * (2 ** jnp.arange(n_steps)))
    tab = jnp.stack([st_re.transpose(0, 2, 1), st_im.transpose(0, 2, 1)], axis=2)
    tab = tab.reshape(g // 2, 2, 2 * n_steps, n).transpose(0, 2, 1, 3).reshape(g // 2, 2 * n_steps, 2 * n)
    tab = jnp.pad(tab, ((0, 0), (0, -(2 * n_steps) % SUBLANES), (0, 0)))
    return (pw_re, pw_im, bb_re, bb_im, cc, skip), ct.astype(BF16), tab.astype(F32)


def _toeplitz_kernel(pwr_ref, pwi_ref, bbr_ref, bbi_ref, cc_ref, skip_ref, o_ref, bt_ref):
    tc = S5_CHUNK
    p = cc_ref.shape[1]
    side = tc * p
    exact = functools.partial(jnp.dot, preferred_element_type=F32, precision=lax.Precision.HIGHEST)
    lane = lax.broadcasted_iota(jnp.int32, (tc, side), 1)
    row = lax.broadcasted_iota(jnp.int32, (tc, side), 0)
    spread = jnp.where((lane >= row * p) & (lane < (row + 1) * p), 1.0, 0.0).astype(BF16)
    lane = lax.broadcasted_iota(jnp.int32, (p, side), 1)
    row = lax.broadcasted_iota(jnp.int32, (p, side), 0)
    repeat = jnp.where(lane % p == row, 1.0, 0.0).astype(BF16)

    def place(re, im, sel):
        rest = jnp.concatenate([re, im], axis=0)
        out = None
        for _ in range(3):
            piece = rest.astype(BF16)
            rest = rest - piece.astype(F32)
            term = jnp.dot(piece, sel, preferred_element_type=F32)
            out = term if out is None else out + term
        return out[:re.shape[0]], out[re.shape[0]:]

    for gi in range(cc_ref.shape[0]):
        pr, pi = place(pwr_ref[gi], pwi_ref[gi], spread)
        br, bi = place(bbr_ref[gi], bbi_ref[gi], repeat)
        bt = jnp.concatenate([pr * br - pi * bi, pr * bi + pi * br], axis=0)
        bt_ref[gi] = bt.astype(bt_ref.dtype)
        k = exact(cc_ref[gi], bt)
        k = jnp.concatenate([k[:, :(tc - 1) * p], k[:, (tc - 1) * p:] + skip_ref[gi]], axis=1)
        s = jnp.concatenate([k, jnp.zeros((p, (tc - 1) * p), F32)], axis=1)
        o_ref[gi] = jnp.concatenate([s[:, (tc - 1 - t) * p:(tc - 1 - t) * p + tc * p] for t in range(tc)],
                                    axis=0).astype(o_ref.dtype)


def _toeplitz(parts, groups_per_step):
    cc = parts[4]
    m, p, n2 = cc.shape
    side = S5_CHUNK * p
    blk = lambda a: pl.BlockSpec((groups_per_step,) + a.shape[1:], lambda i: (i, 0, 0))
    return pl.pallas_call(
        _toeplitz_kernel,
        grid=(m // groups_per_step,),
        in_specs=[blk(a) for a in parts],
        out_specs=[pl.BlockSpec((groups_per_step, side, side), lambda i: (i, 0, 0)),
                   pl.BlockSpec((groups_per_step, n2, side), lambda i: (i, 0, 0))],
        out_shape=[jax.ShapeDtypeStruct((m, side, side), BF16), jax.ShapeDtypeStruct((m, n2, side), BF16)],
        compiler_params=_cparams("parallel"),
        name="s5_toeplitz",
    )(*parts)


def _s5_kernel(*refs, n_groups, p, n_slab):
    u_refs = refs[:n_slab]
    kt_ref, bt_ref, ct_ref, tab_ref = refs[n_slab:n_slab + 4]
    o_refs = refs[n_slab + 4:2 * n_slab + 4]
    ut_ref, y_ref, carry_ref, sr_ref, si_ref = refs[2 * n_slab + 4:]
    tc = S5_CHUNK
    c = u_refs[0].shape[1] // tc
    n2 = bt_ref.shape[2]
    half = n2 // 2
    n_steps = int(math.log2(c))

    @pl.when(pl.program_id(1) == 0)
    def _():
        carry_ref[...] = jnp.zeros_like(carry_ref)

    for k in range(tc):
        for a in range(n_slab):
            ut_ref[k, a * LANES:(a + 1) * LANES, :] = jnp.transpose(
                u_refs[a][0, pl.ds(k, c, stride=tc), :]).astype(BF16)

    n_pairs = n_groups // 2

    def local_states(gp, _):
        local = []
        for g in (2 * gp, 2 * gp + 1):
            ug = jnp.concatenate([ut_ref[k, pl.ds(pl.multiple_of(g * p, p), p), :] for k in range(tc)], axis=0)
            y_ref[g] = jnp.dot(kt_ref[0, g], ug, preferred_element_type=F32)
            local.append(jnp.dot(bt_ref[0, g], ug, preferred_element_type=F32))
        sr_ref[gp] = jnp.transpose(jnp.concatenate([local[0][:half], local[1][:half]], axis=0))
        si_ref[gp] = jnp.transpose(jnp.concatenate([local[0][half:], local[1][half:]], axis=0))
        return 0

    lax.fori_loop(0, n_pairs, local_states, 0, unroll=2)

    row = lax.broadcasted_iota(jnp.int32, (n_pairs, c, n2), 1)

    def shift(x, d, fill):
        return jnp.where(row >= d, pltpu.roll(x, d, 1), fill)

    tab = tab_ref[0]
    cin_r = carry_ref[:, 0:1, :]
    cin_i = carry_ref[:, 1:2, :]
    sr = sr_ref[...] + jnp.where(row == 0, tab[:, 0:1] * cin_r - tab[:, 1:2] * cin_i, 0.0)
    si = si_ref[...] + jnp.where(row == 0, tab[:, 0:1] * cin_i + tab[:, 1:2] * cin_r, 0.0)
    for s in range(n_steps):
        d = 1 << s
        ar, ai = tab[:, 2 * s:2 * s + 1], tab[:, 2 * s + 1:2 * s + 2]
        hr, hi = shift(sr, d, 0.0), shift(si, d, 0.0)
        sr, si = sr + ar * hr - ai * hi, si + ar * hi + ai * hr
    carry_ref[:, 0:1, :] = sr[:, c - 1:c]
    carry_ref[:, 1:2, :] = si[:, c - 1:c]
    sr_ref[...] = shift(sr, 1, cin_r)
    si_ref[...] = shift(si, 1, cin_i)

    def carried_response(gp, _):
        pr = jnp.transpose(sr_ref[gp])
        pi = jnp.transpose(si_ref[gp])
        for idx, g in enumerate((2 * gp, 2 * gp + 1)):
            prev = jnp.concatenate([pr[idx * half:(idx + 1) * half], pi[idx * half:(idx + 1) * half]], axis=0)
            y_ref[g] = y_ref[g] + jnp.dot(ct_ref[0, g], prev.astype(BF16), preferred_element_type=F32)
        return 0

    lax.fori_loop(0, n_pairs, carried_response, 0, unroll=2)

    groups_per_slab = LANES // p
    for t in range(tc):
        for a in range(n_slab):
            yt = jnp.concatenate([y_ref[g, t * p:(t + 1) * p, :]
                                  for g in range(a * groups_per_slab, (a + 1) * groups_per_slab)], axis=0)
            o_refs[a][0, pl.ds(t, c, stride=tc), :] = jnp.transpose(yt)


def _s5(u_slabs, tables, layer, c_lanes):
    kt, bt, ct, tab = tables
    n_slab = len(u_slabs)
    bsz, seq, _ = u_slabs[0].shape
    width = n_slab * LANES
    tc = S5_CHUNK
    n_groups, n2 = bt.shape[1], bt.shape[2]
    p = width // n_groups
    rows = c_lanes * tc
    kern = functools.partial(_s5_kernel, n_groups=n_groups, p=p, n_slab=n_slab)
    slab = pl.BlockSpec((1, rows, LANES), lambda b, j: (b, j, 0))
    return pl.pallas_call(
        kern,
        grid=(bsz, seq // rows),
        in_specs=[slab] * n_slab + [_layer_spec(t.shape, layer) for t in (kt, bt, ct, tab)],
        out_specs=[slab] * n_slab,
        out_shape=[jax.ShapeDtypeStruct((bsz, seq, LANES), F32)] * n_slab,
        scratch_shapes=[pltpu.VMEM((tc, width, c_lanes), BF16), pltpu.VMEM((n_groups, tc * p, c_lanes), F32),
                        pltpu.VMEM((n_groups // 2, SUBLANES, n2), F32),
                        pltpu.VMEM((n_groups // 2, c_lanes, n2), F32), pltpu.VMEM((n_groups // 2, c_lanes, n2), F32)],
        compiler_params=_cparams("parallel", "arbitrary", fusible=[False] * (n_slab + 2) + [True, True]),
        name="s5",
    )(*u_slabs, kt, bt, ct, tab)


def _merge_kernel(*refs, n_groups, n_experts, d_model, n_rslab, n_slab):
    x_ref, ya_ref = refs[:2]
    yr_refs = refs[2:2 + n_rslab]
    ys_refs = refs[2 + n_rslab:2 + n_rslab + n_slab]
    (mg_ref, wg_ref, gw_ref, gb_ref, pr_ref, pa_ref, ps_ref, wo_ref, fg_ref, rw_ref, rb_ref,
     x1_ref, xn_ref, route_ref, routet_ref, tab_ref, cnt_ref, run_ref, wgb_ref) = refs[2 + n_rslab + n_slab:]
    i = pl.program_id(0)

    @pl.when(i == 0)
    def _():
        run_ref[...] = jnp.zeros_like(run_ref)
        wgb_ref[...] = wg_ref[0].astype(BF16)

    x = x_ref[...]
    tm = x.shape[0]
    xn = _rmsnorm(x, mg_ref[0], MIX_EPS)
    gates = _sigmoid(jnp.dot(xn.astype(BF16), wgb_ref[...], preferred_element_type=F32))
    z = _gelu_tanh(jnp.concatenate([ref[...] for ref in ys_refs], axis=1))
    ys = z * _sigmoid(jnp.dot(z.astype(BF16), gw_ref[0], preferred_element_type=F32) + gb_ref[0])
    y_rnn = jnp.concatenate([ref[...] for ref in yr_refs], axis=1).astype(BF16)
    merged = (gates[:, :d_model] * jnp.dot(y_rnn, pr_ref[0], preferred_element_type=F32)
              + gates[:, d_model:2 * d_model] * jnp.dot(ya_ref[...], pa_ref[0], preferred_element_type=F32)
              + gates[:, 2 * d_model:] * jnp.dot(ys.astype(BF16), ps_ref[0], preferred_element_type=F32))
    x1 = x + jnp.dot(merged.astype(BF16), wo_ref[0], preferred_element_type=F32)
    x1_ref[...] = x1
    xn2 = _rmsnorm(x1, fg_ref[0], MIX_EPS)
    xn_ref[...] = xn2.astype(xn_ref.dtype)

    rw = rw_ref[0]
    x_hi = xn2.astype(BF16)
    x_lo = (xn2 - x_hi.astype(F32)).astype(BF16)
    w_hi = rw.astype(BF16)
    w_lo = (rw - w_hi.astype(F32)).astype(BF16)
    logits = (jnp.dot(x_hi, w_hi, preferred_element_type=F32) + jnp.dot(x_lo, w_hi, preferred_element_type=F32)
              + jnp.dot(x_hi, w_lo, preferred_element_type=F32) + rb_ref[0])
    lane = lax.broadcasted_iota(jnp.int32, logits.shape, 1).astype(F32)
    big = float(LANES)
    coarse = jnp.where(lane < n_groups, logits, NEG_INF)
    cmax = jnp.max(coarse, axis=1, keepdims=True)
    gsel = jnp.min(jnp.where(coarse == cmax, lane, big), axis=1, keepdims=True)
    p_sel = 1.0 / jnp.sum(jnp.where(lane < n_groups, jnp.exp(logits - cmax), 0.0), axis=1, keepdims=True)
    lo = n_groups + gsel * n_experts
    fine = jnp.where((lane >= lo) & (lane < lo + n_experts), logits, NEG_INF)
    m1 = jnp.max(fine, axis=1, keepdims=True)
    i1 = jnp.min(jnp.where(fine == m1, lane, big), axis=1, keepdims=True)
    fine2 = jnp.where(lane == i1, NEG_INF, fine)
    m2 = jnp.max(fine2, axis=1, keepdims=True)
    i2 = jnp.min(jnp.where(fine2 == m2, lane, big), axis=1, keepdims=True)
    e21 = jnp.exp(m2 - m1)
    w1 = p_sel / (1.0 + e21)
    w2 = p_sel * e21 / (1.0 + e21)
    oh1 = lane == i1
    oh2 = lane == i2
    onehot = jnp.where(oh1 | oh2, 1.0, 0.0)
    r_i = lax.broadcasted_iota(jnp.int32, (tm, tm), 0)
    c_i = lax.broadcasted_iota(jnp.int32, (tm, tm), 1)
    earlier = jnp.where(c_i < r_i, 1.0, 0.0).astype(BF16)
    rank = jnp.dot(earlier, onehot.astype(BF16), preferred_element_type=F32)
    cnt = jnp.sum(onehot, axis=0, keepdims=True)
    cnt = jnp.floor((cnt + (SUBLANES - 1)) * (1.0 / SUBLANES)) * SUBLANES
    k_i = lax.broadcasted_iota(jnp.int32, (LANES, LANES), 0)
    l_i = lax.broadcasted_iota(jnp.int32, (LANES, LANES), 1)
    lower = jnp.where(k_i < l_i, 1.0, 0.0)
    start = jnp.dot(jnp.broadcast_to(cnt, (SUBLANES, LANES)), lower, preferred_element_type=F32,
                    precision=lax.Precision.HIGHEST)[0:1]
    pos = rank + start
    lp1 = jnp.sum(jnp.where(oh1, pos, 0.0), axis=1, keepdims=True)
    lp2 = jnp.sum(jnp.where(oh2, pos, 0.0), axis=1, keepdims=True)
    route = jnp.where(lane == 0, w1, 0.0)
    route = jnp.where(lane == 1, w2, route)
    route = jnp.where(lane == 2, lp1, route)
    route = jnp.where(lane == 3, lp2, route)
    route_ref[...] = route
    routet_ref[...] = jnp.transpose(route)[:SUBLANES]
    sub = lax.broadcasted_iota(jnp.int32, (SUBLANES, LANES), 0)
    tab_ref[...] = jnp.where(sub == 0, cnt, jnp.where(sub == 1, start, jnp.where(sub == 2, run_ref[...], 0.0)))
    run_ref[...] = run_ref[...] + cnt
    cnt_ref[...] = run_ref[...]


def _merge(x2d, y_attn, y_rnn_slabs, y_s5_slabs, mix_g, w_gate, glu_w, glu_b, p_rnn, p_attn, p_ssm, w_out, ffn_g,
           rw, rb, layer, gate_block, n_groups, n_experts, tm):
    n, d = x2d.shape
    row = lambda i: (i, 0)
    n_rslab, n_slab = len(y_rnn_slabs), len(y_s5_slabs)
    n_branch_cols = 3 * d
    kern = functools.partial(_merge_kernel, n_groups=n_groups, n_experts=n_experts, d_model=d, n_rslab=n_rslab,
                             n_slab=n_slab)
    consts = [mix_g, w_gate, glu_w, glu_b, p_rnn, p_attn, p_ssm, w_out, ffn_g, rw, rb]
    const_specs = [_layer_spec(a.shape, layer) for a in consts]
    const_specs[1] = pl.BlockSpec((1, d, n_branch_cols), lambda *_: (layer, 0, gate_block),
                                  pipeline_mode=pl.Buffered(1))
    return pl.pallas_call(
        kern,
        grid=(n // tm,),
        in_specs=[pl.BlockSpec((tm, d), row), pl.BlockSpec((tm, y_attn.shape[1]), row)]
                 + [pl.BlockSpec((tm, LANES), row)] * (n_rslab + n_slab) + const_specs,
        out_specs=[pl.BlockSpec((tm, d), row), pl.BlockSpec((tm, d), row), pl.BlockSpec((tm, LANES), row),
                   pl.BlockSpec((SUBLANES, tm), lambda i: (0, i)), pl.BlockSpec((SUBLANES, LANES), row),
                   pl.BlockSpec((1, LANES), lambda i: (0, 0))],
        out_shape=[jax.ShapeDtypeStruct((n, d), F32), jax.ShapeDtypeStruct((n, d), BF16),
                   jax.ShapeDtypeStruct((n, LANES), F32), jax.ShapeDtypeStruct((SUBLANES, n), F32),
                   jax.ShapeDtypeStruct((n // tm * SUBLANES, LANES), F32), jax.ShapeDtypeStruct((1, LANES), F32)],
        scratch_shapes=[pltpu.VMEM((1, LANES), F32), pltpu.VMEM((d, n_branch_cols), BF16)],
        compiler_params=_cparams("arbitrary", fusible=[False] * (2 + n_rslab + n_slab)
                                 + [k != 1 for k in range(len(consts))]),
        name="merge_router",
    )(x2d, y_attn, *y_rnn_slabs, *y_s5_slabs, *consts)


def _sorted_rows(tm, n_total):
    rows = TOP_K_FINE * tm + n_total * (SUBLANES - 1)
    return -(-rows // LANES) * LANES


RUN_BITS = 6
TILE_BITS = 3


def _run_copies(base, n_total, cnt_ref, loc_ref, dst_ref, local_ref, remote_ref, sem, to_remote, wait):
    def copy(e, off, size):
        lstart = 0 if loc_ref is None else pl.multiple_of(loc_ref[base + e] + off, SUBLANES)
        local = local_ref.at[pl.ds(lstart, size)]
        remote = remote_ref.at[pl.ds(pl.multiple_of(dst_ref[base + e] + off, SUBLANES), size)]
        desc = pltpu.make_async_copy(local, remote, sem) if to_remote else pltpu.make_async_copy(remote, local, sem)
        if wait:
            desc.wait()
        else:
            desc.start()

    def expert(e, _):
        cnt = cnt_ref[base + e]
        big = 1 << RUN_BITS

        def chunk(c, _):
            copy(e,c * big, big)
            return 0

        n_big = cnt >> RUN_BITS
        lax.fori_loop(0, n_big, chunk, 0)
        off = n_big * big
        for b in reversed(range(TILE_BITS, RUN_BITS)):
            size = 1 << b

            @pl.when((cnt & size) != 0)
            def _(off=off, size=size):
                copy(e,off, size)

            off = off + (cnt & size)
        return 0

    lax.fori_loop(0, n_total, expert, 0, unroll=4 if n_total % 4 == 0 else 1)


def _wait_tile(base, n_total, min_rows, cnt_ref, loc_ref, local_ref, remote_ref, sem, to_remote):
    total = loc_ref[base + n_total - 1] + cnt_ref[base + n_total - 1]
    rest = total - min_rows

    def wait(size):
        local, remote = local_ref.at[pl.ds(0, size)], remote_ref.at[pl.ds(0, size)]
        (pltpu.make_async_copy(local, remote, sem) if to_remote else pltpu.make_async_copy(remote, local, sem)).wait()

    wait(min_rows)
    for b in range(TILE_BITS, (n_total * (SUBLANES - 1)).bit_length()):
        @pl.when((rest & (1 << b)) != 0)
        def _(b=b):
            wait(1 << b)


def _dispatch_kernel(cnt_ref, loc_ref, dst_ref, gap_cnt_ref, gap_dst_ref, x_ref, rt_ref, xs_ref, buf_ref, zero_ref,
                     sem, *, n_total):
    i = pl.program_id(0)
    last = pl.num_programs(0) - 1
    slot = i % 2
    tm = x_ref.shape[0]
    rows = buf_ref.shape[1]
    pos = rt_ref[...]
    j = lax.broadcasted_iota(jnp.int32, (rows, tm), 0).astype(F32)
    sel = jnp.where((j == pos[2:3]) | (j == pos[3:4]), 1.0, 0.0).astype(BF16)
    buf_ref[slot] = jnp.dot(sel, x_ref[...], preferred_element_type=F32)

    def copies(tile, tile_slot, wait):
        if wait:
            _wait_tile(tile * n_total, n_total, TOP_K_FINE * tm, cnt_ref, loc_ref, buf_ref.at[tile_slot], xs_ref,
                       sem.at[tile_slot], True)
        else:
            _run_copies(tile * n_total, n_total, cnt_ref, loc_ref, dst_ref, buf_ref.at[tile_slot], xs_ref,
                        sem.at[tile_slot], True, False)

    copies(i, slot, False)

    @pl.when(i > 0)
    def _():
        copies(i - 1, 1 - slot, True)

    @pl.when(i == last)
    def _():
        copies(i, slot, True)
        zero_ref[...] = jnp.zeros_like(zero_ref)
        for wait in (False, True):
            _run_copies(0, gap_cnt_ref.shape[0], gap_cnt_ref, None, gap_dst_ref, zero_ref, xs_ref, sem.at[0], True,
                        wait)


def _dispatch(tile_cnt, tile_loc, tile_dst, gap_cnt, gap_dst, xn2, route_t, n_rows, n_total, tm):
    n, d = xn2.shape
    grid_spec = pltpu.PrefetchScalarGridSpec(
        num_scalar_prefetch=5,
        grid=(n // tm,),
        in_specs=[pl.BlockSpec((tm, d), lambda i, *_: (i, 0)),
                  pl.BlockSpec((SUBLANES, tm), lambda i, *_: (0, i))],
        out_specs=pl.BlockSpec(memory_space=pl.ANY),
        scratch_shapes=[pltpu.VMEM((2, _sorted_rows(tm, n_total), d), F32), pltpu.VMEM((1 << RUN_BITS, d), F32),
                        pltpu.SemaphoreType.DMA((2,))],
    )
    return pl.pallas_call(
        functools.partial(_dispatch_kernel, n_total=n_total),
        grid_spec=grid_spec,
        out_shape=jax.ShapeDtypeStruct((n_rows, d), F32),
        compiler_params=_cparams("arbitrary"),
        name="moe_dispatch",
    )(tile_cnt, tile_loc, tile_dst, gap_cnt, gap_dst, xn2, route_t)


def _experts_kernel(te_ref, first_ref, slot_ref, next_ref, nact_ref, x_ref, w1_ref, w3_ref, w2_ref, o_ref,
                    w1f_ref, w3f_ref, w2f_ref, w1b_ref, w3b_ref, w2b_ref, sem):
    i = pl.program_id(0)

    def fetch(expert, slot, wait):
        for src, dst in ((w1_ref, w1f_ref), (w3_ref, w3f_ref), (w2_ref, w2f_ref)):
            copy = pltpu.make_async_copy(src.at[expert], dst.at[slot], sem.at[slot])
            if wait:
                copy.wait()
            else:
                copy.start()

    @pl.when(i == 0)
    def _():
        fetch(te_ref[0], slot_ref[0], False)

    @pl.when(jnp.logical_and(i < nact_ref[0], first_ref[i] == 1))
    def _():
        slot = slot_ref[i]
        fetch(te_ref[i], slot, True)
        w1b_ref[...] = w1f_ref[slot].astype(BF16)
        w3b_ref[...] = w3f_ref[slot].astype(BF16)
        w2b_ref[...] = w2f_ref[slot].astype(BF16)

        @pl.when(next_ref[i] >= 0)
        def _():
            fetch(next_ref[i], 1 - slot, False)

    @pl.when(i < nact_ref[0])
    def _():
        xb = x_ref[...].astype(BF16)
        h1 = jnp.dot(xb, w1b_ref[...], preferred_element_type=F32)
        h3 = jnp.dot(xb, w3b_ref[...], preferred_element_type=F32)
        hid = h1 * _sigmoid(h1) * h3
        o_ref[...] = jnp.dot(hid.astype(BF16), w2b_ref[...], preferred_element_type=F32).astype(o_ref.dtype)

    @pl.when(i >= nact_ref[0])
    def _():
        o_ref[...] = jnp.zeros_like(o_ref)


def _experts(tile_expert, tile_first, tile_slot, tile_next, n_active, xs, w1, w3, w2, tm):
    n_rows, d = xs.shape
    f = w1.shape[2]
    grid_spec = pltpu.PrefetchScalarGridSpec(
        num_scalar_prefetch=5,
        grid=(n_rows // tm,),
        in_specs=[pl.BlockSpec((tm, d), lambda i, te, fi, sl, nx, na: (jnp.minimum(i, na[0] - 1), 0)),
                  pl.BlockSpec(memory_space=pl.ANY), pl.BlockSpec(memory_space=pl.ANY),
                  pl.BlockSpec(memory_space=pl.ANY)],
        out_specs=pl.BlockSpec((tm, d), lambda i, *_: (i, 0)),
        scratch_shapes=[pltpu.VMEM((2, d, f), F32), pltpu.VMEM((2, d, f), F32), pltpu.VMEM((2, f, d), F32),
                        pltpu.VMEM((d, f), BF16), pltpu.VMEM((d, f), BF16), pltpu.VMEM((f, d), BF16),
                        pltpu.SemaphoreType.DMA((2,))],
    )
    return pl.pallas_call(
        _experts_kernel,
        grid_spec=grid_spec,
        out_shape=jax.ShapeDtypeStruct((n_rows, d), F32),
        compiler_params=_cparams("arbitrary"),
        name="moe_experts",
    )(tile_expert, tile_first, tile_slot, tile_next, n_active, xs, w1, w3, w2)


def _combine_kernel(cnt_ref, loc_ref, dst_ref, x_ref, route_ref, rt_ref, fg_ref, ys_ref, o_ref, buf_ref, sem,
                    *, n_total, final_norm):
    i = pl.program_id(0)
    slot = i % 2
    tm = x_ref.shape[0]
    rows = buf_ref.shape[1]

    def copies(tile, tile_slot, wait):
        if wait:
            _wait_tile(tile * n_total, n_total, TOP_K_FINE * tm, cnt_ref, loc_ref, buf_ref.at[tile_slot], ys_ref,
                       sem.at[tile_slot], False)
        else:
            _run_copies(tile * n_total, n_total, cnt_ref, loc_ref, dst_ref, buf_ref.at[tile_slot], ys_ref,
                        sem.at[tile_slot], False, False)

    @pl.when(i == 0)
    def _():
        buf_ref[...] = jnp.zeros_like(buf_ref)
        copies(0, 0, False)

    @pl.when(i + 1 < pl.num_programs(0))
    def _():
        copies(i + 1, 1 - slot, False)

    copies(i, slot, True)
    rt = rt_ref[...]
    jr = lax.broadcasted_iota(jnp.int32, (rows, tm), 0).astype(F32)
    gate = jnp.sum(jnp.where(jr == rt[2:3], rt[0:1], 0.0) + jnp.where(jr == rt[3:4], rt[1:2], 0.0),
                   axis=1, keepdims=True)
    yb = (buf_ref[slot] * gate).astype(BF16)
    route = route_ref[...]
    jc = lax.broadcasted_iota(jnp.int32, (tm, rows), 1).astype(F32)
    pick = jnp.where((jc == route[:, 2:3]) | (jc == route[:, 3:4]), 1.0, 0.0).astype(BF16)
    out = x_ref[...] + jnp.dot(pick, yb, preferred_element_type=F32)
    if final_norm:
        out = _rmsnorm(out, fg_ref[...], MIX_EPS)
    o_ref[...] = out


def _combine(tile_cnt, tile_loc, tile_dst, x1, route, route_t, final_g, ys, n_total, tm, final_norm):
    n, d = x1.shape
    kern = functools.partial(_combine_kernel, n_total=n_total, final_norm=final_norm)
    grid_spec = pltpu.PrefetchScalarGridSpec(
        num_scalar_prefetch=3,
        grid=(n // tm,),
        in_specs=[pl.BlockSpec((tm, d), lambda i, *_: (i, 0)),
                  pl.BlockSpec((tm, LANES), lambda i, *_: (i, 0)),
                  pl.BlockSpec((SUBLANES, tm), lambda i, *_: (0, i)),
                  pl.BlockSpec((1, d), lambda i, *_: (0, 0)),
                  pl.BlockSpec(memory_space=pl.ANY)],
        out_specs=pl.BlockSpec((tm, d), lambda i, *_: (i, 0)),
        scratch_shapes=[pltpu.VMEM((2, _sorted_rows(tm, n_total), d), F32), pltpu.SemaphoreType.DMA((2,))],
    )
    return pl.pallas_call(
        kern,
        grid_spec=grid_spec,
        out_shape=jax.ShapeDtypeStruct((n, d), F32),
        compiler_params=_cparams("arbitrary"),
        name="moe_combine",
    )(tile_cnt, tile_loc, tile_dst, x1, route, route_t, final_g, ys)


def _tile_plan(n, seq):
    return dict(inproj=min(512, n), rglru=min(256, seq), attn=min(512, seq), attn_heads=4,
                s5_lanes=min(LANES, seq // S5_CHUNK), merge=min(512, n), moe=min(512, n))


def kernel(x, positions, mix_norm_g, w_in, conv_w, conv_b, rg_wa, rg_ba, rg_wx, rg_bx, rg_lambda,
           lam_q1, lam_k1, lam_q2, lam_k2, subln_g,
           ssm_lambda_re, ssm_lambda_im, ssm_b_re, ssm_b_im, ssm_c_re, ssm_c_im, ssm_d, ssm_log_dt,
           ssm_glu_w, ssm_glu_b, proj_rnn, proj_attn, proj_ssm, w_out,
           ffn_norm_g, router_coarse_w, router_coarse_b, router_fine_w, router_fine_b,
           expert_w1, expert_w3, expert_w2, final_norm_g):
    return _forward(_tile_plan(x.shape[0] * x.shape[1], x.shape[1]),
                    x, positions, mix_norm_g, w_in, conv_w, conv_b, rg_wa, rg_ba, rg_wx, rg_bx, rg_lambda,
                    lam_q1, lam_k1, lam_q2, lam_k2, subln_g,
                    ssm_lambda_re, ssm_lambda_im, ssm_b_re, ssm_b_im, ssm_c_re, ssm_c_im, ssm_d, ssm_log_dt,
                    ssm_glu_w, ssm_glu_b, proj_rnn, proj_attn, proj_ssm, w_out,
                    ffn_norm_g, router_coarse_w, router_coarse_b, router_fine_w, router_fine_b,
                    expert_w1, expert_w3, expert_w2, final_norm_g)


def _forward(tiles, x, positions, mix_norm_g, w_in, conv_w, conv_b, rg_wa, rg_ba, rg_wx, rg_bx, rg_lambda,
             lam_q1, lam_k1, lam_q2, lam_k2, subln_g,
             ssm_lambda_re, ssm_lambda_im, ssm_b_re, ssm_b_im, ssm_c_re, ssm_c_im, ssm_d, ssm_log_dt,
             ssm_glu_w, ssm_glu_b, proj_rnn, proj_attn, proj_ssm, w_out,
             ffn_norm_g, router_coarse_w, router_coarse_b, router_fine_w, router_fine_b,
             expert_w1, expert_w3, expert_w2, final_norm_g):
    bsz, seq, d_model = x.shape
    depth = w_in.shape[0]
    n = bsz * seq
    r = conv_w.shape[2]
    sw = ssm_glu_w.shape[1]
    vdim = subln_g.shape[1]
    head_dim = vdim // 2
    in_cols = w_in.shape[2]
    qk = (in_cols - 2 * r - sw - 3 * d_model) // 3
    heads = qk // (2 * head_dim)
    splits = (r, 2 * r, 2 * r + qk, 2 * r + 2 * qk, 2 * r + 3 * qk, 2 * r + 3 * qk + sw)
    mix_cols = splits[-1]
    n_groups = router_coarse_w.shape[2]
    n_experts = expert_w1.shape[2]
    n_total = n_groups * n_experts

    tm_in, tt_rnn, tq = tiles["inproj"], tiles["rglru"], tiles["attn"]
    s5_lanes, tm_merge, tm_moe = tiles["s5_lanes"], tiles["merge"], tiles["moe"]
    n_rows = TOP_K_FINE * n + (n // tm_merge) * n_total * (SUBLANES - 1) + n_total * tm_moe
    n_rows = -(-n_rows // tm_moe) * tm_moe

    posf = positions.astype(F32)
    pos_col = posf.reshape(n, 1)
    pos8 = posf.reshape(bsz, seq // SUBLANES, SUBLANES)
    inv_freq = ROPE_THETA ** (-jnp.arange(0, head_dim, 2, dtype=F32) / head_dim)
    invf = jnp.tile(inv_freq, LANES // (head_dim // 2)).reshape(1, LANES)

    row3 = lambda a: a.reshape(depth, 1, a.shape[-1])
    gate_cols = in_cols - mix_cols
    in_place = mix_cols % gate_cols == 0
    w_gate = w_in if in_place else w_in[:, :, mix_cols:]
    gate_block = mix_cols // gate_cols if in_place else 0
    b_gates = row3(jnp.concatenate([rg_ba, rg_bx], axis=1))
    lamv = jnp.stack([lam_q1, lam_k1, lam_q2, lam_k2], axis=1)
    parts, ct_t, tab_t = jax.vmap(functools.partial(_s5_tables, n_steps=int(math.log2(s5_lanes))))(
        ssm_lambda_re, ssm_lambda_im, ssm_b_re, ssm_b_im, ssm_c_re, ssm_c_im, ssm_d, ssm_log_dt)
    n_ssm_groups = parts[0].shape[1]
    flat = lambda a: a.reshape((depth * n_ssm_groups,) + a.shape[2:])
    layered = lambda a: a.reshape((depth, n_ssm_groups) + a.shape[1:])
    kt, bt = _toeplitz(tuple(flat(a) for a in parts), math.gcd(n_ssm_groups, SUBLANES))
    tables = (layered(kt), layered(bt), ct_t, tab_t)
    rw = jnp.concatenate([router_coarse_w, router_fine_w], axis=2)
    rw = jnp.pad(rw, ((0, 0), (0, 0), (0, LANES - rw.shape[2])))
    rb = jnp.concatenate([router_coarse_b, router_fine_b], axis=1)
    rb = row3(jnp.pad(rb, ((0, 0), (0, LANES - rb.shape[1]))))
    merge_params = (row3(mix_norm_g), w_gate, ssm_glu_w.astype(BF16), row3(ssm_glu_b), proj_rnn.astype(BF16),
                    proj_attn.astype(BF16), proj_ssm.astype(BF16), w_out.astype(BF16), row3(ffn_norm_g), rw, rb)

    x2d = x.reshape(n, d_model)
    for l in range(depth):
        lambda_init = 0.8 - 0.6 * math.exp(-0.3 * l)
        q, k, vt, *slabs = _inproj(x2d, row3(mix_norm_g), pos_col, invf, w_in, l, splits, head_dim, tm_in, tq)
        slabs = [t.reshape(bsz, seq, LANES) for t in slabs]
        n_rs = r // LANES
        y_rnn = _rglru(slabs[:n_rs], slabs[n_rs:2 * n_rs], pos8, conv_w, row3(conv_b), rg_wa, rg_wx, b_gates,
                       row3(rg_lambda), l, tt_rnn)
        y_attn = _diff_attention(q.reshape(bsz, seq, qk), k.reshape(bsz, seq, qk), vt, lamv, row3(subln_g), l,
                                 heads, head_dim, lambda_init, tq, min(tiles["attn_heads"], heads))
        y_s5 = _s5(slabs[2 * n_rs:], tables, l, s5_lanes)
        x1, xn2, route, route_t, tile_tab, counts = _merge(
            x2d, y_attn.reshape(n, qk), [y.reshape(n, LANES) for y in y_rnn], [y.reshape(n, LANES) for y in y_s5],
            *merge_params, l, gate_block, n_groups, n_experts, tm_merge)

        cnt = counts[0, n_groups:n_groups + n_total].astype(jnp.int32)
        n_tiles = (cnt + tm_moe - 1) // tm_moe
        tile_end = jnp.cumsum(n_tiles)
        offsets = (tile_end - n_tiles) * tm_moe
        tab = tile_tab.reshape(n // tm_merge, SUBLANES, LANES)[:, :, n_groups:n_groups + n_total].astype(jnp.int32)
        tile_cnt = tab[:, 0].reshape(-1)
        tile_loc = tab[:, 1].reshape(-1)
        tile_dst = (tab[:, 2] + offsets[None, :]).reshape(-1)
        n_active = tile_end[-1:]
        gap_dst = jnp.concatenate([offsets + cnt, n_active * tm_moe])
        gap_cnt = jnp.concatenate([tile_end * tm_moe, jnp.full((1,), n_rows, jnp.int32)]) - gap_dst
        tile_ids = jnp.minimum(jnp.arange(n_rows // tm_moe, dtype=jnp.int32), n_active[0] - 1)
        tile_expert = jnp.sum((tile_ids[:, None] >= tile_end[None, :]).astype(jnp.int32), axis=1)
        e_ids = jnp.arange(n_total, dtype=jnp.int32)
        present = n_tiles > 0
        ordinal = jnp.cumsum(present.astype(jnp.int32)) - 1
        later = present[None, :] & (e_ids[None, :] > e_ids[:, None])
        succ = jnp.min(jnp.where(later, e_ids[None, :], n_total), axis=1)
        succ = jnp.where(succ < n_total, succ + l * n_total, -1)
        all_ids = jnp.arange(n_rows // tm_moe, dtype=jnp.int32)
        tile_first = ((all_ids < n_active[0]) & (all_ids == (tile_end - n_tiles)[tile_expert])).astype(jnp.int32)
        tile_slot = ordinal[tile_expert] % 2
        tile_next = succ[tile_expert]

        xs = _dispatch(tile_cnt, tile_loc, tile_dst, gap_cnt, gap_dst, xn2, route_t, n_rows, n_total, tm_merge)
        ys = _experts(tile_expert + l * n_total, tile_first, tile_slot, tile_next, n_active.astype(jnp.int32), xs,
                      expert_w1.reshape(depth * n_total, d_model, -1), expert_w3.reshape(depth * n_total, d_model, -1),
                      expert_w2.reshape(depth * n_total, -1, d_model), tm_moe)
        x2d = _combine(tile_cnt, tile_loc, tile_dst, x1, route, route_t, final_norm_g.reshape(1, d_model), ys,
                       n_total, tm_merge, l == depth - 1)
    return x2d.reshape(bsz, seq, d_model)
```
